```python
import math
import jax, jax.numpy as jnp
from jax import lax
import numpy as np

D_MODEL = 1024
BATCH = 2
SEQ = 8192
DEPTH = 4
DEC_BATCH = 128
DEC_SEQ = 8
PAST_LEN = 8192
PAGE_SIZE = 128

N_MIXERS = 3
EXPAND = 2
D_INNER = EXPAND * D_MODEL
NORM_EPS = 1e-6

GLA_HEADS = 4
GLA_KEY_DIM = D_MODEL // 2
GLA_HEAD_K = GLA_KEY_DIM // GLA_HEADS
GLA_HEAD_V = D_INNER // GLA_HEADS
GLA_GATE_RANK = 16
GLA_GATE_NORMALIZER = 16.0
GLA_CHUNK = 64
GLA_IN = 2 * GLA_KEY_DIM + 2 * D_INNER + GLA_GATE_RANK

SSD_HEAD_DIM = 64
SSD_HEADS = D_INNER // SSD_HEAD_DIM
SSD_GROUPS = 4
SSD_HEADS_PER_GROUP = SSD_HEADS // SSD_GROUPS
SSD_STATE = 128
SSD_CONV = 4
SSD_CONV_DIM = D_INNER + 2 * SSD_GROUPS * SSD_STATE
SSD_CHUNK = 64
SSD_IN = D_INNER + SSD_CONV_DIM + SSD_HEADS

SWA_HEAD_DIM = 64
SWA_Q_HEADS = D_INNER // SWA_HEAD_DIM
SWA_KV_HEADS = 4
SWA_GROUP = SWA_Q_HEADS // SWA_KV_HEADS
SWA_WINDOW = 128
SWA_KV_DIM = SWA_KV_HEADS * SWA_HEAD_DIM
SWA_IN = D_INNER + 2 * SWA_KV_DIM + D_INNER

kernel_name = 'hybrid_gla_ssd_swa_decode_step'


def rms_norm(x, w):
    x32 = x.astype(jnp.float32)
    y = x32 * lax.rsqrt(jnp.mean(x32 * x32, axis=-1, keepdims=True) + NORM_EPS)
    return (y * w.astype(jnp.float32)).astype(x.dtype)


def to_chunks(a, c):
    b, l = a.shape[0], a.shape[1]
    return jnp.moveaxis(a.reshape((b, l // c, c) + a.shape[2:]), 1, 0)


def from_chunks(a):
    a = jnp.moveaxis(a, 0, 1)
    return a.reshape((a.shape[0], a.shape[1] * a.shape[2]) + a.shape[3:])


def gla_chunk_scan(q, k, v, g, s0):
    L = q.shape[1]
    c = math.gcd(L, GLA_CHUNK)
    mask = jnp.tril(jnp.ones((c, c), dtype=bool))[None, :, :, None, None]

    def step(s, inp):
        qc, kc, vc, gc = inp
        cum = jnp.cumsum(gc, axis=1)
        decay = jnp.exp(jnp.where(mask, cum[:, :, None] - cum[:, None], -jnp.inf))
        att = jnp.einsum('bthk,bshk,btshk->btsh', qc, kc, decay)
        o = jnp.einsum('btsh,bshv->bthv', att, vc) + jnp.einsum('bthk,bhkv->bthv', qc * jnp.exp(cum), s)
        last = cum[:, -1]
        s_new = s * jnp.exp(last)[..., None] + jnp.einsum('bshk,bshv->bhkv', kc * jnp.exp(last[:, None] - cum), vc)
        return s_new.astype(s.dtype), o.astype(vc.dtype)

    s_fin, o = lax.scan(step, s0, (to_chunks(q, c), to_chunks(k, c), to_chunks(v, c), to_chunks(g, c)))
    return from_chunks(o), s_fin


def gla_mixer(h, w_in, w_gk2, b_gk, head_norm, w_out, s0):
    B, L, _ = h.shape
    proj = h @ w_in
    q, k, v, r, lr = jnp.split(proj, [GLA_KEY_DIM, 2 * GLA_KEY_DIM, 2 * GLA_KEY_DIM + D_INNER, 2 * GLA_KEY_DIM + 2 * D_INNER], axis=-1)
    q = q.reshape(B, L, GLA_HEADS, GLA_HEAD_K) * (GLA_HEAD_K ** -0.5)
    k = k.reshape(B, L, GLA_HEADS, GLA_HEAD_K)
    v = v.reshape(B, L, GLA_HEADS, GLA_HEAD_V)
    g = (jax.nn.log_sigmoid(lr @ w_gk2 + b_gk) / GLA_GATE_NORMALIZER).reshape(B, L, GLA_HEADS, GLA_HEAD_K)
    o, s_fin = gla_chunk_scan(q, k, v, g, s0)
    o = rms_norm(o, head_norm).reshape(B, L, D_INNER) * jax.nn.silu(r)
    return o @ w_out, s_fin


def ssd_chunk_scan(x, dt, la, bm, cm, s0):
    L = x.shape[1]
    c = math.gcd(L, SSD_CHUNK)
    mask = jnp.tril(jnp.ones((c, c), dtype=bool))[None, :, :, None, None]

    def step(s, inp):
        xc, dtc, lac, bc, cc = inp
        cum = jnp.cumsum(lac, axis=1)
        decay = jnp.exp(jnp.where(mask, cum[:, :, None] - cum[:, None], -jnp.inf))
        cb = jnp.einsum('btgn,bsgn->btsg', cc, bc)
        u = xc * dtc[..., None]
        y = jnp.einsum('btsge,btsg,bsgep->btgep', decay, cb, u) + jnp.einsum('btgn,bgepn->btgep', cc, s) * jnp.exp(cum)[..., None]
        last = cum[:, -1]
        s_new = s * jnp.exp(last)[..., None, None] + jnp.einsum('bsge,bsgn,bsgep->bgepn', jnp.exp(last[:, None] - cum), bc, u)
        return s_new.astype(s.dtype), y.astype(xc.dtype)

    s_fin, y = lax.scan(step, s0, (to_chunks(x, c), to_chunks(dt, c), to_chunks(la, c), to_chunks(bm, c), to_chunks(cm, c)))
    return from_chunks(y), s_fin


def ssd_mixer(h, w_in, conv_w, conv_b, dt_bias, a_log, d_skip, gate_norm, w_out, ssm0, conv0):
    B, L, _ = h.shape
    proj = h @ w_in
    z, xbc, dt_raw = jnp.split(proj, [D_INNER, D_INNER + SSD_CONV_DIM], axis=-1)
    ext = jnp.concatenate([conv0, xbc], axis=1)
    conv = conv_b + sum(ext[:, j:j + L] * conv_w[j] for j in range(SSD_CONV))
    xbc_c = jax.nn.silu(conv)
    xs, bm, cm = jnp.split(xbc_c, [D_INNER, D_INNER + SSD_GROUPS * SSD_STATE], axis=-1)
    xs = xs.reshape(B, L, SSD_GROUPS, SSD_HEADS_PER_GROUP, SSD_HEAD_DIM)
    bm = bm.reshape(B, L, SSD_GROUPS, SSD_STATE)
    cm = cm.reshape(B, L, SSD_GROUPS, SSD_STATE)
    dt = jax.nn.softplus(dt_raw + dt_bias).reshape(B, L, SSD_GROUPS, SSD_HEADS_PER_GROUP)
    a = -jnp.exp(a_log).reshape(SSD_GROUPS, SSD_HEADS_PER_GROUP)
    s0 = ssm0.reshape(B, SSD_GROUPS, SSD_HEADS_PER_GROUP, SSD_HEAD_DIM, SSD_STATE)
    y, s_fin = ssd_chunk_scan(xs, dt, dt * a, bm, cm, s0)
    y = y + xs * d_skip.reshape(SSD_GROUPS, SSD_HEADS_PER_GROUP, 1)
    y = y.reshape(B, L, D_INNER) * jax.nn.silu(z)
    y = rms_norm(y.reshape(B, L, SSD_GROUPS, D_INNER // SSD_GROUPS), gate_norm.reshape(SSD_GROUPS, D_INNER // SSD_GROUPS)).reshape(B, L, D_INNER)
    return y @ w_out, s_fin.reshape(B, SSD_HEADS, SSD_HEAD_DIM, SSD_STATE), ext[:, L:]


def swa_mixer(h, w_in, sinks, w_out, k0, v0, has_past):
    B, L, _ = h.shape
    W = SWA_WINDOW
    proj = h @ w_in
    q, k, v, gate = jnp.split(proj, [D_INNER, D_INNER + SWA_KV_DIM, D_INNER + 2 * SWA_KV_DIM], axis=-1)
    q = q.reshape(B, L, SWA_KV_HEADS, SWA_GROUP, SWA_HEAD_DIM) * (SWA_HEAD_DIM ** -0.5)
    k_ext = jnp.concatenate([k0, k.reshape(B, L, SWA_KV_HEADS, SWA_HEAD_DIM)], axis=1)
    v_ext = jnp.concatenate([v0, v.reshape(B, L, SWA_KV_HEADS, SWA_HEAD_DIM)], axis=1)
    bq = math.gcd(L, W)
    nb = L // bq
    idx = jnp.arange(nb)[:, None] * bq + jnp.arange(bq + W)[None, :]
    kb = k_ext[:, idx]
    vb = v_ext[:, idx]
    qb = q.reshape(B, nb, bq, SWA_KV_HEADS, SWA_GROUP, SWA_HEAD_DIM)
    scores = jnp.einsum('bnqkgd,bnskd->bnkgqs', qb, kb).astype(jnp.float32)
    diff = W + jnp.arange(bq)[:, None] - jnp.arange(bq + W)[None, :]
    band = (diff >= 0) & (diff < W)
    key_ok = (idx >= W) | has_past
    mask = band[None] & key_ok[:, None]
    scores = jnp.where(mask[None, :, None, None], scores, -jnp.inf)
    sink = sinks.astype(jnp.float32).reshape(1, 1, SWA_KV_HEADS, SWA_GROUP, 1, 1)
    m = jnp.maximum(jnp.max(scores, axis=-1, keepdims=True), sink)
    p = jnp.exp(scores - m)
    denom = jnp.sum(p, axis=-1, keepdims=True) + jnp.exp(sink - m)
    probs = (p / denom).astype(vb.dtype)
    o = jnp.einsum('bnkgqs,bnskd->bnqkgd', probs, vb).reshape(B, L, D_INNER)
    return (o * jax.nn.silu(gate)) @ w_out, k_ext[:, L:], v_ext[:, L:]


def init_states(batch, dtype):
    states = []
    for i in range(DEPTH):
        kind = i % N_MIXERS
        if kind == 0:
            states.append((jnp.zeros((batch, GLA_HEADS, GLA_HEAD_K, GLA_HEAD_V), dtype),))
        elif kind == 1:
            states.append((jnp.zeros((batch, SSD_HEADS, SSD_HEAD_DIM, SSD_STATE), dtype),
                           jnp.zeros((batch, SSD_CONV - 1, SSD_CONV_DIM), dtype)))
        else:
            states.append((jnp.zeros((batch, SWA_WINDOW, SWA_KV_HEADS, SWA_HEAD_DIM), dtype),
                           jnp.zeros((batch, SWA_WINDOW, SWA_KV_HEADS, SWA_HEAD_DIM), dtype)))
    return states


def run_trunk(x, layer_states, layer_params, final_norm, has_past):
    new_states = []
    for i in range(DEPTH):
        kind = i % N_MIXERS
        p = layer_params[i]
        st = layer_states[i]
        h = rms_norm(x, p[0])
        if kind == 0:
            out, s_new = gla_mixer(h, *p[1:], st[0])
            ns = (s_new,)
        elif kind == 1:
            out, s_new, c_new = ssd_mixer(h, *p[1:], st[0], st[1])
            ns = (s_new, c_new)
        else:
            out, k_new, v_new = swa_mixer(h, *p[1:], st[0], st[1], has_past)
            ns = (k_new, v_new)
        x = x + out
        new_states.append(ns)
    return rms_norm(x, final_norm), new_states


def setup_inputs(seed: int = 0) -> dict:
    key = jax.random.key(seed)
    ks = jax.random.split(key, 40)

    def nrm(k, shape, scale):
        return jax.random.normal(k, shape, jnp.float32) * scale

    def gain(k, n):
        return 1.0 + 0.02 * jax.random.normal(k, (n,), jnp.float32)

    dt0 = jnp.exp(jax.random.uniform(ks[20], (SSD_HEADS,), jnp.float32, math.log(1e-3), math.log(1e-1)))
    dt_bias = dt0 + jnp.log(-jnp.expm1(-dt0))
    return {
        'x_prompt': nrm(ks[0], (BATCH, SEQ, D_MODEL), 1.0),
        'x_sample': nrm(ks[1], (DEC_BATCH, DEC_SEQ, D_MODEL), 1.0),
        'state_gla_0': nrm(ks[2], (DEC_BATCH, GLA_HEADS, GLA_HEAD_K, GLA_HEAD_V), 0.5),
        'state_ssm_1': nrm(ks[3], (DEC_BATCH, SSD_HEADS, SSD_HEAD_DIM, SSD_STATE), 0.1),
        'state_conv_1': nrm(ks[4], (DEC_BATCH, SSD_CONV - 1, SSD_CONV_DIM), 1.0),
        'cache_swa_k_2': nrm(ks[5], (DEC_BATCH, SWA_WINDOW, SWA_KV_HEADS, SWA_HEAD_DIM), 1.0),
        'cache_swa_v_2': nrm(ks[6], (DEC_BATCH, SWA_WINDOW, SWA_KV_HEADS, SWA_HEAD_DIM), 1.0),
        'state_gla_3': nrm(ks[7], (DEC_BATCH, GLA_HEADS, GLA_HEAD_K, GLA_HEAD_V), 0.5),
        'l0_norm': gain(ks[8], D_MODEL),
        'l0_w_in': nrm(ks[9], (D_MODEL, GLA_IN), D_MODEL ** -0.5),
        'l0_w_gk2': nrm(ks[10], (GLA_GATE_RANK, GLA_KEY_DIM), GLA_GATE_RANK ** -0.5),
        'l0_b_gk': nrm(ks[11], (GLA_KEY_DIM,), 0.02),
        'l0_head_norm': gain(ks[12], GLA_HEAD_V),
        'l0_w_out': nrm(ks[13], (D_INNER, D_MODEL), D_INNER ** -0.5),
        'l1_norm': gain(ks[14], D_MODEL),
        'l1_w_in': nrm(ks[15], (D_MODEL, SSD_IN), D_MODEL ** -0.5),
        'l1_conv_w': nrm(ks[16], (SSD_CONV, SSD_CONV_DIM), SSD_CONV ** -0.5),
        'l1_conv_b': nrm(ks[17], (SSD_CONV_DIM,), 0.02),
        'l1_dt_bias': dt_bias,
        'l1_a_log': jnp.log(jax.random.uniform(ks[18], (SSD_HEADS,), jnp.float32, 1.0, 16.0)),
        'l1_d_skip': gain(ks[19], SSD_HEADS),
        'l1_gate_norm': gain(ks[21], D_INNER),
        'l1_w_out': nrm(ks[22], (D_INNER, D_MODEL), D_INNER ** -0.5),
        'l2_norm': gain(ks[23], D_MODEL),
        'l2_w_in': nrm(ks[24], (D_MODEL, SWA_IN), D_MODEL ** -0.5),
        'l2_sinks': nrm(ks[25], (SWA_Q_HEADS,), 1.0),
        'l2_w_out': nrm(ks[26], (D_INNER, D_MODEL), D_INNER ** -0.5),
        'l3_norm': gain(ks[27], D_MODEL),
        'l3_w_in': nrm(ks[28], (D_MODEL, GLA_IN), D_MODEL ** -0.5),
        'l3_w_gk2': nrm(ks[29], (GLA_GATE_RANK, GLA_KEY_DIM), GLA_GATE_RANK ** -0.5),
        'l3_b_gk': nrm(ks[30], (GLA_KEY_DIM,), 0.02),
        'l3_head_norm': gain(ks[31], GLA_HEAD_V),
        'l3_w_out': nrm(ks[32], (D_INNER, D_MODEL), D_INNER ** -0.5),
        'final_norm': gain(ks[33], D_MODEL),
    }


def reference(x_prompt, x_sample, state_gla_0, state_ssm_1, state_conv_1, cache_swa_k_2, cache_swa_v_2, state_gla_3,
              l0_norm, l0_w_in, l0_w_gk2, l0_b_gk, l0_head_norm, l0_w_out,
              l1_norm, l1_w_in, l1_conv_w, l1_conv_b, l1_dt_bias, l1_a_log, l1_d_skip, l1_gate_norm, l1_w_out,
              l2_norm, l2_w_in, l2_sinks, l2_w_out,
              l3_norm, l3_w_in, l3_w_gk2, l3_b_gk, l3_head_norm, l3_w_out,
              final_norm):
    layer_params = [
        (l0_norm, l0_w_in, l0_w_gk2, l0_b_gk, l0_head_norm, l0_w_out),
        (l1_norm, l1_w_in, l1_conv_w, l1_conv_b, l1_dt_bias, l1_a_log, l1_d_skip, l1_gate_norm, l1_w_out),
        (l2_norm, l2_w_in, l2_sinks, l2_w_out),
        (l3_norm, l3_w_in, l3_w_gk2, l3_b_gk, l3_head_norm, l3_w_out),
    ]
    y_prompt, ns_p = run_trunk(x_prompt, init_states(x_prompt.shape[0], x_prompt.dtype), layer_params, final_norm, False)
    sample_states = [(state_gla_0,), (state_ssm_1, state_conv_1), (cache_swa_k_2, cache_swa_v_2), (state_gla_3,)]
    y_sample, ns_s = run_trunk(x_sample, sample_states, layer_params, final_norm, True)
    return (y_prompt, y_sample,
            ns_p[0][0], ns_s[0][0],
            ns_p[1][0], ns_s[1][0],
            ns_p[1][1], ns_s[1][1],
            ns_p[2][0], ns_s[2][0],
            ns_p[2][1], ns_s[2][1],
            ns_p[3][0], ns_s[3][0])
```

```python
import functools
import math

import numpy as np
import jax
import jax.numpy as jnp
from jax import lax
from jax.experimental import pallas as pl
from jax.experimental.pallas import tpu as pltpu

F32 = jnp.float32
BF16 = jnp.bfloat16

NORM_EPS = 1e-6
D_MODEL = 1024
D_INNER = 2048
CHUNK = 64
LANE = 128
NEG_BIG = -1e30

GLA_HEADS = 4
GLA_HEAD_K = 128
GLA_HEAD_V = 512
GLA_KEY_DIM = 512
GLA_GATE_RANK = 16
GLA_GATE_NORMALIZER = 16.0

SSD_GROUPS = 4
SSD_HEADS = 32
SSD_HEAD_DIM = 64
SSD_STATE = 128
SSD_CONV = 4
SSD_CONV_DIM = 3072
SSD_GROUP_W = D_INNER // SSD_GROUPS

SWA_WINDOW = 128
SWA_KV_HEADS = 4
SWA_HEAD_DIM = 64
SWA_Q_HEADS = 32
SWA_KV_DIM = 256

VMEM_LIMIT = 52 * 1024 * 1024


def _cparams(sem):
    return pltpu.CompilerParams(dimension_semantics=sem, vmem_limit_bytes=VMEM_LIMIT)


def _dot(a, b):
    return jnp.dot(a.astype(BF16), b.astype(BF16), preferred_element_type=F32)


def _dot_nt(a, b):
    return lax.dot_general(a.astype(BF16), b.astype(BF16), (((1,), (1,)), ((), ())),
                           preferred_element_type=F32)


def _split3(x):
    hi = x.astype(BF16)
    r1 = x - hi.astype(F32)
    mid = r1.astype(BF16)
    lo = (r1 - mid.astype(F32)).astype(BF16)
    return hi, mid, lo


def _dot01_lhs(m01, x):
    hi, mid, lo = _split3(x)
    d = lambda p: jnp.dot(m01, p, preferred_element_type=F32)
    return d(hi) + d(mid) + d(lo)


def _dot01_rhs(x, m01):
    hi, mid, lo = _split3(x)
    d = lambda p: jnp.dot(p, m01, preferred_element_type=F32)
    return d(hi) + d(mid) + d(lo)


def _sigmoid(x):
    return 1.0 / (1.0 + jnp.exp(-x))


def _softplus(x):
    return jnp.maximum(x, 0.0) + jnp.log1p(jnp.exp(-jnp.abs(x)))


def _pad_rows(x, rows):
    if x.shape[0] == rows:
        return x
    return jnp.concatenate([x, jnp.zeros((rows - x.shape[0], x.shape[1]), x.dtype)], axis=0)


def _norm_proj_kernel(x_ref, nw_ref, wg_ref, wr_ref, gate_ref, rest_ref):
    x = x_ref[...]
    ms = jnp.mean(x * x, axis=-1, keepdims=True)
    h = (x * lax.rsqrt(ms + NORM_EPS) * nw_ref[...]).astype(BF16)
    gate_ref[...] = jnp.dot(h, wg_ref[...], preferred_element_type=F32)
    rest_ref[...] = jnp.dot(h, wr_ref[...], preferred_element_type=F32)


def _norm_proj(x2, nw, wg, wr, tm=256):
    t, d = x2.shape
    tm = min(tm, t)
    ng, nr = wg.shape[1], wr.shape[1]
    return pl.pallas_call(
        _norm_proj_kernel,
        grid=(t // tm,),
        in_specs=[
            pl.BlockSpec((tm, d), lambda i: (i, 0)),
            pl.BlockSpec((1, d), lambda i: (0, 0)),
            pl.BlockSpec((d, ng), lambda i: (0, 0)),
            pl.BlockSpec((d, nr), lambda i: (0, 0)),
        ],
        out_specs=[
            pl.BlockSpec((tm, ng), lambda i: (i, 0)),
            pl.BlockSpec((tm, nr), lambda i: (i, 0)),
        ],
        out_shape=[jax.ShapeDtypeStruct((t, ng), F32), jax.ShapeDtypeStruct((t, nr), F32)],
        compiler_params=_cparams(("parallel",)),
        name="norm_proj",
    )(x2, nw.reshape(1, d), wg, wr)


def _seg_rms(y, seg):
    parts = []
    for s in range(y.shape[1] // seg):
        p = y[:, s * seg:(s + 1) * seg]
        ms = jnp.mean(p * p, axis=-1, keepdims=True)
        parts.append(p * lax.rsqrt(ms + NORM_EPS))
    return jnp.concatenate(parts, axis=1)


def _out_proj_kernel(o_ref, gate_ref, nw_ref, w_ref, x_ref, fw_ref, out_ref, *, mode, final):
    o = o_ref[...]
    gt = gate_ref[...]
    act = gt * _sigmoid(gt)
    if mode == "gla":
        y = _seg_rms(o, GLA_HEAD_V) * nw_ref[...] * act
    elif mode == "ssd":
        y = _seg_rms(o * act, SSD_GROUP_W) * nw_ref[...]
    else:
        y = o * act
    out = x_ref[...] + jnp.dot(y.astype(BF16), w_ref[...], preferred_element_type=F32)
    if final:
        ms = jnp.mean(out * out, axis=-1, keepdims=True)
        out = out * lax.rsqrt(ms + NORM_EPS) * fw_ref[...]
    out_ref[...] = out


def _out_proj(mode, o2, gate2, nw, w_out, x2, fw, final, tm=256):
    t, di = o2.shape
    tm = min(tm, t)
    d = x2.shape[1]
    row = lambda i: (i, 0)
    fix = lambda i: (0, 0)
    return pl.pallas_call(
        functools.partial(_out_proj_kernel, mode=mode, final=final),
        grid=(t // tm,),
        in_specs=[
            pl.BlockSpec((tm, di), row),
            pl.BlockSpec((tm, di), row),
            pl.BlockSpec((1, di), fix),
            pl.BlockSpec((di, d), fix),
            pl.BlockSpec((tm, d), row),
            pl.BlockSpec((1, d), fix),
        ],
        out_specs=pl.BlockSpec((tm, d), row),
        out_shape=jax.ShapeDtypeStruct((t, d), F32),
        compiler_params=_cparams(("parallel",)),
        name="out_proj_" + mode,
    )(o2, gate2, nw.reshape(1, di), w_out, x2, fw.reshape(1, d))


_GLA_LEVELS = (32, 16, 8, 4, 2, 1)


def _gla_consts():
    c = CHUNK
    u = np.arange(c)[:, None]
    j = np.arange(c)[None, :]
    blocks = [(j <= u), (j > u)]
    lev = np.full((c, c), len(_GLA_LEVELS) + 1, np.int32)
    t = np.arange(c)[:, None]
    s = np.arange(c)[None, :]
    for li, h in enumerate(_GLA_LEVELS):
        b = (u // (2 * h)) * (2 * h) + h - 1
        blocks.append((j > np.minimum(u, b)) & (j <= np.maximum(u, b)))
        sib = (t // (2 * h) == s // (2 * h)) & (t % (2 * h) >= h) & (s % (2 * h) < h)
        lev[sib] = li
    lev[np.eye(c, dtype=bool)] = len(_GLA_LEVELS)
    m = np.concatenate(blocks, axis=0).astype(np.float32)
    return jnp.asarray(m, BF16), jnp.asarray(lev)


def _gla_scan_kernel(rest_ref, s0_ref, wgk_ref, bgk_ref, m_ref, lev_ref, o_ref, s_ref, *, valid):
    c = CHUNK
    t = pl.program_id(1)

    @pl.when(t == 0)
    def _():
        s_ref[...] = s0_ref[...]

    xc = _pad_rows(rest_ref[0], c)
    v = xc[:, 0:D_INNER]
    q = xc[:, D_INNER:D_INNER + GLA_KEY_DIM] * (GLA_HEAD_K ** -0.5)
    k = xc[:, D_INNER + GLA_KEY_DIM:D_INNER + 2 * GLA_KEY_DIM]
    lr = xc[:, D_INNER + 2 * GLA_KEY_DIM:]
    z = _dot(lr, wgk_ref[...]) + bgk_ref[...]
    g = (jnp.minimum(z, 0.0) - jnp.log1p(jnp.exp(-jnp.abs(z)))) / GLA_GATE_NORMALIZER
    if valid < c:
        row = lax.broadcasted_iota(jnp.int32, g.shape, 0)
        g = jnp.where(row < valid, g, 0.0)
    e_all = jnp.exp(_dot01_lhs(m_ref[...], g))
    lev = lev_ref[...]

    for h in range(GLA_HEADS):
        ks = slice(h * GLA_HEAD_K, (h + 1) * GLA_HEAD_K)
        qh, kh = q[:, ks], k[:, ks]
        vh = v[:, h * GLA_HEAD_V:(h + 1) * GLA_HEAD_V]
        e_in = e_all[0:c, ks]
        e_out = e_all[c:2 * c, ks]
        att = jnp.where(lev == len(_GLA_LEVELS), _dot_nt(qh, kh), 0.0)
        for li in range(len(_GLA_LEVELS)):
            el = e_all[(2 + li) * c:(3 + li) * c, ks]
            att = jnp.where(lev == li, _dot_nt(qh * el, kh * el), att)
        s_h = s_ref[0, h]
        o_h = _dot(att, vh) + _dot(qh * e_in, s_h)
        xt = jnp.concatenate([kh * e_out, e_in], axis=0).T
        ke_t = xt[:, 0:c]
        dcol = xt[:, 2 * c - 1:2 * c]
        s_ref[0, h] = s_h * dcol + _dot(ke_t, vh)
        o_ref[0, :, h * GLA_HEAD_V:(h + 1) * GLA_HEAD_V] = o_h[0:valid]


def _gla_scan(rest3, s0, w_gk2, b_gk):
    b, l, nr = rest3.shape
    valid = min(CHUNK, l)
    nt = l // valid
    m01, lev = _gla_consts()
    wgk = jnp.zeros((LANE, GLA_KEY_DIM), F32).at[:GLA_GATE_RANK].set(w_gk2).astype(BF16)
    sblk = (1, GLA_HEADS, GLA_HEAD_K, GLA_HEAD_V)
    return pl.pallas_call(
        functools.partial(_gla_scan_kernel, valid=valid),
        grid=(b, nt),
        in_specs=[
            pl.BlockSpec((1, valid, nr), lambda i, t: (i, t, 0)),
            pl.BlockSpec(sblk, lambda i, t: (i, 0, 0, 0)),
            pl.BlockSpec((LANE, GLA_KEY_DIM), lambda i, t: (0, 0)),
            pl.BlockSpec((1, GLA_KEY_DIM), lambda i, t: (0, 0)),
            pl.BlockSpec(m01.shape, lambda i, t: (0, 0)),
            pl.BlockSpec(lev.shape, lambda i, t: (0, 0)),
        ],
        out_specs=[
            pl.BlockSpec((1, valid, D_INNER), lambda i, t: (i, t, 0)),
            pl.BlockSpec(sblk, lambda i, t: (i, 0, 0, 0)),
        ],
        out_shape=[jax.ShapeDtypeStruct((b, l, D_INNER), F32), jax.ShapeDtypeStruct(s0.shape, F32)],
        compiler_params=_cparams(("parallel", "arbitrary")),
        name="gla_scan",
    )(rest3, s0, wgk, b_gk.reshape(1, GLA_KEY_DIM), m01, lev)


def _ssd_consts():
    c = CHUNK
    hp = np.arange(D_INNER)
    expand = (np.arange(LANE)[:, None] == (hp // SSD_HEAD_DIM)[None, :])
    j = np.arange(c)[:, None]
    s_of = (np.arange(SSD_HEADS * c) % c)[None, :]
    tle = (j <= s_of).astype(np.float32)
    tri = np.where(s_of <= j, 0.0, NEG_BIG).astype(np.float32)
    tl = (np.arange(c)[None, :] <= np.arange(c)[:, None])
    r = np.arange(2 * c)[:, None] // c
    cc = np.arange(LANE)[None, :] // SSD_HEAD_DIM
    bd = (r == cc).astype(np.float32)
    return (jnp.asarray(expand, BF16), jnp.asarray(tle), jnp.asarray(tri),
            jnp.asarray(tl, BF16), jnp.asarray(bd))


def _ssd_scan_kernel(rest_ref, s0_ref, conv0_ref, cw_ref, cb_ref, dtb_ref, alog_ref, dsk_ref,
                     ex_ref, tle_ref, tri_ref, tl_ref, bd_ref,
                     y_ref, sfin_ref, ext_ref, st_ref, *, valid):
    c = CHUNK
    t = pl.program_id(1)
    nt = pl.num_programs(1)

    @pl.when(t == 0)
    def _():
        ext_ref[0:8, :] = conv0_ref[0]
        for g in range(SSD_GROUPS):
            st_ref[g] = s0_ref[0, g].T

    raw = _pad_rows(rest_ref[0], c)
    ext_ref[8:8 + c, :] = raw[:, 0:SSD_CONV_DIM]
    conv = cb_ref[...] + (cw_ref[0:1, :] * ext_ref[5:5 + c, :] + cw_ref[1:2, :] * ext_ref[6:6 + c, :]
                          + cw_ref[2:3, :] * ext_ref[7:7 + c, :] + cw_ref[3:4, :] * ext_ref[8:8 + c, :])
    if valid == c:
        ext_ref[0:8, :] = ext_ref[c:c + 8, :]
    xbc = conv * _sigmoid(conv)
    xs = xbc[:, 0:D_INNER]
    bm = xbc[:, D_INNER:D_INNER + SSD_GROUPS * SSD_STATE]
    cm = xbc[:, D_INNER + SSD_GROUPS * SSD_STATE:]

    dt = _softplus(raw[:, SSD_CONV_DIM:] + dtb_ref[...])
    if valid < c:
        row = lax.broadcasted_iota(jnp.int32, dt.shape, 0)
        dt = jnp.where(row < valid, dt, 0.0)
    la = dt * (-jnp.exp(alog_ref[...]))
    rep = _dot01_rhs(jnp.concatenate([dt, la], axis=0), ex_ref[...])
    dt_rep, la_rep = rep[0:c], rep[c:2 * c]
    cum_rep = _dot01_lhs(tl_ref[...], la_rep)
    cum_t = _dot01_lhs(jnp.ones((8, c), BF16), la_rep * tle_ref[...])[0:1]
    decay = jnp.exp(cum_rep - cum_t + tri_ref[...])
    e_cum = jnp.exp(cum_rep)
    last = cum_rep[c - 1:c, :]
    e_last = jnp.exp(last)
    u = xs * dt_rep
    uw = u * jnp.exp(last - cum_rep)
    bd = bd_ref[...]

    for g in range(SSD_GROUPS):
        gs = slice(g * SSD_GROUP_W, (g + 1) * SSD_GROUP_W)
        ns = slice(g * SSD_STATE, (g + 1) * SSD_STATE)
        cg, bg = cm[:, ns], bm[:, ns]
        cb_rep = _dot_nt(cg, jnp.concatenate([bg] * (SSD_GROUP_W // c), axis=0))
        wg = decay[:, gs] * cb_rep
        ug = u[:, gs]
        parts = []
        for p in range(SSD_GROUP_W // LANE):
            ps = slice(p * LANE, (p + 1) * LANE)
            up = ug[:, ps]
            parts.append(_dot(wg[:, ps], jnp.concatenate([up, up], axis=0) * bd))
        s_g = st_ref[g]
        y_g = (jnp.concatenate(parts, axis=1) + _dot(cg, s_g) * e_cum[:, gs]
               + xs[:, gs] * dsk_ref[:, gs])
        bg_t = jnp.concatenate([bg, jnp.zeros_like(bg)], axis=0).T[:, 0:c]
        st_ref[g] = s_g * e_last[:, gs] + _dot(bg_t, uw[:, gs])
        y_ref[0, :, gs] = y_g[0:valid]

    @pl.when(t == nt - 1)
    def _():
        for g in range(SSD_GROUPS):
            sfin_ref[0, g] = st_ref[g].T


def _ssd_scan(rest3, s0, conv0, conv_w, conv_b, dt_bias, a_log, d_skip):
    b, l, nr = rest3.shape
    valid = min(CHUNK, l)
    nt = l // valid
    ex, tle, tri, tl, bd = _ssd_consts()
    s0g = s0.reshape(b, SSD_GROUPS, SSD_GROUP_W, SSD_STATE)
    conv0p = jnp.concatenate([jnp.zeros((b, 8 - (SSD_CONV - 1), SSD_CONV_DIM), F32), conv0], axis=1)
    pad = lambda a: jnp.zeros((1, LANE), F32).at[0, :SSD_HEADS].set(a)
    dsk = jnp.repeat(d_skip, SSD_HEAD_DIM).reshape(1, D_INNER)
    sblk = (1, SSD_GROUPS, SSD_GROUP_W, SSD_STATE)
    fix2 = lambda i, t: (0, 0)
    y, sfin = pl.pallas_call(
        functools.partial(_ssd_scan_kernel, valid=valid),
        grid=(b, nt),
        in_specs=[
            pl.BlockSpec((1, valid, nr), lambda i, t: (i, t, 0)),
            pl.BlockSpec(sblk, lambda i, t: (i, 0, 0, 0)),
            pl.BlockSpec((1, 8, SSD_CONV_DIM), lambda i, t: (i, 0, 0)),
            pl.BlockSpec((SSD_CONV, SSD_CONV_DIM), fix2),
            pl.BlockSpec((1, SSD_CONV_DIM), fix2),
            pl.BlockSpec((1, LANE), fix2),
            pl.BlockSpec((1, LANE), fix2),
            pl.BlockSpec((1, D_INNER), fix2),
            pl.BlockSpec(ex.shape, fix2),
            pl.BlockSpec(tle.shape, fix2),
            pl.BlockSpec(tri.shape, fix2),
            pl.BlockSpec(tl.shape, fix2),
            pl.BlockSpec(bd.shape, fix2),
        ],
        out_specs=[
            pl.BlockSpec((1, valid, D_INNER), lambda i, t: (i, t, 0)),
            pl.BlockSpec(sblk, lambda i, t: (i, 0, 0, 0)),
        ],
        out_shape=[jax.ShapeDtypeStruct((b, l, D_INNER), F32), jax.ShapeDtypeStruct(s0g.shape, F32)],
        scratch_shapes=[
            pltpu.VMEM((CHUNK + 8, SSD_CONV_DIM), F32),
            pltpu.VMEM((SSD_GROUPS, SSD_STATE, SSD_GROUP_W), F32),
        ],
        compiler_params=_cparams(("parallel", "arbitrary")),
        name="ssd_scan",
    )(rest3, s0g, conv0p, conv_w, conv_b.reshape(1, -1), pad(dt_bias), pad(a_log), dsk,
      ex, tle, tri, tl, bd)
    return y, sfin.reshape(s0.shape)


def _swa_mask(bq, first_block_has_no_past):
    rows = 4 * bq
    tq = (np.arange(rows) % bq)[:, None]
    s = (np.arange(4 * SWA_WINDOW) % (2 * SWA_WINDOW))[None, :]
    ok = (s > tq) & (s <= tq + SWA_WINDOW)
    if first_block_has_no_past:
        ok = ok & (s >= SWA_WINDOW)
    return np.where(ok, 0.0, NEG_BIG).astype(np.float32)


def _block_diag_pair(col, rolled, odd):
    lane = lax.broadcasted_iota(jnp.int32, col.shape, 1)
    lo = lane < SWA_HEAD_DIM
    if odd:
        top = jnp.where(lo, rolled, 0.0)
        bot = jnp.where(lo, 0.0, col)
    else:
        top = jnp.where(lo, col, 0.0)
        bot = jnp.where(lo, 0.0, rolled)
    return jnp.concatenate([top, bot], axis=0)


def _swa_kernel(sink_ref, rest_ref, kprev_ref, vprev_ref, mask_ref, o_ref, kbuf_ref, vbuf_ref, *, bq, carry):
    w = SWA_WINDOW
    n = pl.program_id(1)

    @pl.when(n == 0)
    def _():
        kbuf_ref[...] = jnp.zeros_like(kbuf_ref)
        vbuf_ref[...] = jnp.zeros_like(vbuf_ref)
        kbuf_ref[0:w, :] = kprev_ref[0]
        vbuf_ref[0:w, :] = vprev_ref[0]

    x = rest_ref[0]
    kbuf_ref[w:w + bq, :] = x[:, D_INNER:D_INNER + SWA_KV_DIM]
    vbuf_ref[w:w + bq, :] = x[:, D_INNER + SWA_KV_DIM:]
    kall = kbuf_ref[...]
    vall = vbuf_ref[...]
    amask = mask_ref[jnp.where(n == 0, 0, 1)]
    npair = (SWA_Q_HEADS // SWA_KV_HEADS) // 2

    for j in range(SWA_KV_HEADS):
        cs = slice((j // 2) * LANE, (j // 2 + 1) * LANE)
        kcol, vcol = kall[:, cs], vall[:, cs]
        k2 = _block_diag_pair(kcol, pltpu.roll(kcol, SWA_HEAD_DIM, axis=1), j % 2)
        v2 = _block_diag_pair(vcol, pltpu.roll(vcol, SWA_HEAD_DIM, axis=1), j % 2)
        qbase = j * npair * LANE
        qs = jnp.concatenate([x[:, qbase + p * LANE:qbase + (p + 1) * LANE] for p in range(npair)],
                             axis=0) * (SWA_HEAD_DIM ** -0.5)
        sc = _dot_nt(qs, k2) + amask
        probs = []
        for half in range(2):
            sh = sc[:, half * 2 * w:(half + 1) * 2 * w]
            sink = jnp.concatenate(
                [jnp.full((bq, 1), sink_ref[j * 2 * npair + 2 * p + half], F32) for p in range(npair)], axis=0)
            m = jnp.maximum(jnp.max(sh, axis=-1, keepdims=True), sink)
            pe = jnp.exp(sh - m)
            denom = jnp.sum(pe, axis=-1, keepdims=True) + jnp.exp(sink - m)
            probs.append(pe / denom)
        o = _dot(jnp.concatenate(probs, axis=1), v2)
        for p in range(npair):
            o_ref[0, :, qbase + p * LANE:qbase + (p + 1) * LANE] = o[p * bq:(p + 1) * bq]

    if carry:
        kbuf_ref[0:w, :] = kbuf_ref[w:2 * w, :]
        vbuf_ref[0:w, :] = vbuf_ref[w:2 * w, :]


def _swa_attn(rest3, kprev, vprev, sinks, has_past):
    b, l, nr = rest3.shape
    w = SWA_WINDOW
    bq = math.gcd(l, w)
    nb = l // bq
    assert nb == 1 or bq == w
    m_later = _swa_mask(bq, False)
    m_first = m_later if has_past else _swa_mask(bq, True)
    masks = jnp.asarray(np.stack([m_first, m_later]))
    return pl.pallas_call(
        functools.partial(_swa_kernel, bq=bq, carry=nb > 1),
        grid=(b, nb),
        in_specs=[
            pl.BlockSpec(memory_space=pltpu.SMEM),
            pl.BlockSpec((1, bq, nr), lambda i, n: (i, n, 0)),
            pl.BlockSpec((1, w, SWA_KV_DIM), lambda i, n: (i, 0, 0)),
            pl.BlockSpec((1, w, SWA_KV_DIM), lambda i, n: (i, 0, 0)),
            pl.BlockSpec(masks.shape, lambda i, n: (0, 0, 0)),
        ],
        out_specs=pl.BlockSpec((1, bq, D_INNER), lambda i, n: (i, n, 0)),
        out_shape=jax.ShapeDtypeStruct((b, l, D_INNER), F32),
        scratch_shapes=[pltpu.VMEM((2 * w, SWA_KV_DIM), F32), pltpu.VMEM((2 * w, SWA_KV_DIM), F32)],
        compiler_params=_cparams(("parallel", "arbitrary")),
        name="swa_attn",
    )(sinks, rest3, kprev, vprev, masks)


def _pad_cols(w, n):
    return jnp.concatenate([w, jnp.zeros((w.shape[0], n - w.shape[1]), w.dtype)], axis=1)


def _prep_weights(p):
    kd, di = GLA_KEY_DIM, D_INNER
    out = {}
    for i in (0, 3):
        w = p[f"l{i}_w_in"]
        q, k, v, r, lr = (w[:, 0:kd], w[:, kd:2 * kd], w[:, 2 * kd:2 * kd + di],
                          w[:, 2 * kd + di:2 * kd + 2 * di], w[:, 2 * kd + 2 * di:])
        out[f"l{i}_wg"] = r.astype(BF16)
        out[f"l{i}_wr"] = jnp.concatenate([v, q, k, _pad_cols(lr, LANE)], axis=1).astype(BF16)
    w = p["l1_w_in"]
    out["l1_wg"] = w[:, 0:di].astype(BF16)
    out["l1_wr"] = _pad_cols(w[:, di:], SSD_CONV_DIM + LANE).astype(BF16)
    w = p["l2_w_in"]
    out["l2_wg"] = w[:, di + 2 * SWA_KV_DIM:].astype(BF16)
    out["l2_wr"] = w[:, 0:di + 2 * SWA_KV_DIM].astype(BF16)
    for i in range(4):
        out[f"l{i}_wo"] = p[f"l{i}_w_out"].astype(BF16)
    return out


def _trunk(x, st, p, wts, has_past):
    b, l, d = x.shape
    x2 = x.reshape(b * l, d)
    ones_inner = jnp.ones((D_INNER,), F32)
    new = {}

    def gla(i, x2, s0):
        gate, rest = _norm_proj(x2, p[f"l{i}_norm"], wts[f"l{i}_wg"], wts[f"l{i}_wr"])
        o, s_fin = _gla_scan(rest.reshape(b, l, -1), s0, p[f"l{i}_w_gk2"], p[f"l{i}_b_gk"])
        hn = jnp.tile(p[f"l{i}_head_norm"], GLA_HEADS)
        fin = i == 3
        x2 = _out_proj("gla", o.reshape(b * l, D_INNER), gate, hn, wts[f"l{i}_wo"], x2,
                       p["final_norm"], fin)
        return x2, s_fin

    x2, new["gla0"] = gla(0, x2, st["gla0"])

    gate, rest = _norm_proj(x2, p["l1_norm"], wts["l1_wg"], wts["l1_wr"])
    rest3 = rest.reshape(b, l, -1)
    y, new["ssm"] = _ssd_scan(rest3, st["ssm"], st["conv"], p["l1_conv_w"], p["l1_conv_b"],
                              p["l1_dt_bias"], p["l1_a_log"], p["l1_d_skip"])
    ext_tail = jnp.concatenate([st["conv"], rest3[:, max(l - (SSD_CONV - 1), 0):, 0:SSD_CONV_DIM]], axis=1)
    new["conv"] = ext_tail[:, -(SSD_CONV - 1):]
    x2 = _out_proj("ssd", y.reshape(b * l, D_INNER), gate, p["l1_gate_norm"], wts["l1_wo"], x2,
                   p["final_norm"], False)

    gate, rest = _norm_proj(x2, p["l2_norm"], wts["l2_wg"], wts["l2_wr"])
    rest3 = rest.reshape(b, l, -1)
    kprev = st["swa_k"].reshape(b, SWA_WINDOW, SWA_KV_DIM)
    vprev = st["swa_v"].reshape(b, SWA_WINDOW, SWA_KV_DIM)
    o = _swa_attn(rest3, kprev, vprev, p["l2_sinks"], has_past)
    tail = max(l - SWA_WINDOW, 0)
    k_new = rest3[:, tail:, D_INNER:D_INNER + SWA_KV_DIM]
    v_new = rest3[:, tail:, D_INNER + SWA_KV_DIM:]
    new["swa_k"] = jnp.concatenate([kprev, k_new], axis=1)[:, -SWA_WINDOW:].reshape(st["swa_k"].shape)
    new["swa_v"] = jnp.concatenate([vprev, v_new], axis=1)[:, -SWA_WINDOW:].reshape(st["swa_v"].shape)
    x2 = _out_proj("swa", o.reshape(b * l, D_INNER), gate, ones_inner, wts["l2_wo"], x2,
                   p["final_norm"], False)

    x2, new["gla3"] = gla(3, x2, st["gla3"])
    return x2.reshape(b, l, d), new


def kernel(x_prompt, x_sample, state_gla_0, state_ssm_1, state_conv_1, cache_swa_k_2, cache_swa_v_2, state_gla_3, l0_norm, l0_w_in, l0_w_gk2, l0_b_gk, l0_head_norm, l0_w_out, l1_norm, l1_w_in, l1_conv_w, l1_conv_b, l1_dt_bias, l1_a_log, l1_d_skip, l1_gate_norm, l1_w_out, l2_norm, l2_w_in, l2_sinks, l2_w_out, l3_norm, l3_w_in, l3_w_gk2, l3_b_gk, l3_head_norm, l3_w_out, final_norm):
    p = dict(l0_norm=l0_norm, l0_w_in=l0_w_in, l0_w_gk2=l0_w_gk2, l0_b_gk=l0_b_gk,
             l0_head_norm=l0_head_norm, l0_w_out=l0_w_out,
             l1_norm=l1_norm, l1_w_in=l1_w_in, l1_conv_w=l1_conv_w, l1_conv_b=l1_conv_b,
             l1_dt_bias=l1_dt_bias, l1_a_log=l1_a_log, l1_d_skip=l1_d_skip,
             l1_gate_norm=l1_gate_norm, l1_w_out=l1_w_out,
             l2_norm=l2_norm, l2_w_in=l2_w_in, l2_sinks=l2_sinks, l2_w_out=l2_w_out,
             l3_norm=l3_norm, l3_w_in=l3_w_in, l3_w_gk2=l3_w_gk2, l3_b_gk=l3_b_gk,
             l3_head_norm=l3_head_norm, l3_w_out=l3_w_out, final_norm=final_norm)
    wts = _prep_weights(p)

    bp = x_prompt.shape[0]
    z = lambda a: jnp.zeros((bp,) + a.shape[1:], a.dtype)
    st_p = dict(gla0=z(state_gla_0), ssm=z(state_ssm_1), conv=z(state_conv_1),
                swa_k=z(cache_swa_k_2), swa_v=z(cache_swa_v_2), gla3=z(state_gla_3))
    st_s = dict(gla0=state_gla_0, ssm=state_ssm_1, conv=state_conv_1,
                swa_k=cache_swa_k_2, swa_v=cache_swa_v_2, gla3=state_gla_3)
    y_p, n_p = _trunk(x_prompt, st_p, p, wts, False)
    y_s, n_s = _trunk(x_sample, st_s, p, wts, True)
    return (y_p, y_s,
            n_p["gla0"], n_s["gla0"],
            n_p["ssm"], n_s["ssm"],
            n_p["conv"], n_s["conv"],
            n_p["swa_k"], n_s["swa_k"],
            n_p["swa_v"], n_s["swa_v"],
            n_p["gla3"], n_s["gla3"])
```

```python
import functools
import math

import numpy as np
import jax
import jax.numpy as jnp
from jax import lax
from jax.experimental import pallas as pl
from jax.experimental.pallas import tpu as pltpu

F32 = jnp.float32
BF16 = jnp.bfloat16

NORM_EPS = 1e-6
D_MODEL = 1024
D_INNER = 2048
CHUNK = 64
LANE = 128
NEG_BIG = -1e30

GLA_HEADS = 4
GLA_HEAD_K = 128
GLA_HEAD_V = 512
GLA_KEY_DIM = 512
GLA_GATE_RANK = 16
GLA_GATE_NORMALIZER = 16.0

SSD_GROUPS = 4
SSD_HEADS = 32
SSD_HEAD_DIM = 64
SSD_STATE = 128
SSD_CONV = 4
SSD_CONV_DIM = 3072
SSD_GROUP_W = D_INNER // SSD_GROUPS

SWA_WINDOW = 128
SWA_KV_HEADS = 4
SWA_HEAD_DIM = 64
SWA_Q_HEADS = 32
SWA_KV_DIM = 256

VMEM_LIMIT = 52 * 1024 * 1024


def _cparams(sem):
    return pltpu.CompilerParams(dimension_semantics=sem, vmem_limit_bytes=VMEM_LIMIT)


def _dot(a, b):
    return jnp.dot(a.astype(BF16), b.astype(BF16), preferred_element_type=F32)


def _dot_nt(a, b):
    return lax.dot_general(a.astype(BF16), b.astype(BF16), (((1,), (1,)), ((), ())),
                           preferred_element_type=F32)


def _split3(x):
    hi = x.astype(BF16)
    r1 = x - hi.astype(F32)
    mid = r1.astype(BF16)
    lo = (r1 - mid.astype(F32)).astype(BF16)
    return hi, mid, lo


def _dot01_lhs(m01, x):
    hi, mid, lo = _split3(x)
    d = lambda p: jnp.dot(m01, p, preferred_element_type=F32)
    return d(hi) + d(mid) + d(lo)


def _dot01_rhs(x, m01):
    hi, mid, lo = _split3(x)
    d = lambda p: jnp.dot(p, m01, preferred_element_type=F32)
    return d(hi) + d(mid) + d(lo)


def _sigmoid(x):
    return 1.0 / (1.0 + jnp.exp(-x))


def _softplus(x):
    return jnp.maximum(x, 0.0) + jnp.log1p(jnp.exp(-jnp.abs(x)))


def _pad_rows(x, rows):
    if x.shape[0] == rows:
        return x
    return jnp.concatenate([x, jnp.zeros((rows - x.shape[0], x.shape[1]), x.dtype)], axis=0)


def _norm_proj_kernel(x_ref, nw_ref, wg_ref, wr_ref, gate_ref, rest_ref):
    x = x_ref[...]
    ms = jnp.mean(x * x, axis=-1, keepdims=True)
    h = (x * lax.rsqrt(ms + NORM_EPS) * nw_ref[...]).astype(BF16)
    gate_ref[...] = jnp.dot(h, wg_ref[...], preferred_element_type=F32)
    rest_ref[...] = jnp.dot(h, wr_ref[...], preferred_element_type=F32)


def _norm_proj(x2, nw, wg, wr, tm=256):
    t, d = x2.shape
    tm = min(tm, t)
    ng, nr = wg.shape[1], wr.shape[1]
    return pl.pallas_call(
        _norm_proj_kernel,
        grid=(t // tm,),
        in_specs=[
            pl.BlockSpec((tm, d), lambda i: (i, 0)),
            pl.BlockSpec((1, d), lambda i: (0, 0)),
            pl.BlockSpec((d, ng), lambda i: (0, 0)),
            pl.BlockSpec((d, nr), lambda i: (0, 0)),
        ],
        out_specs=[
            pl.BlockSpec((tm, ng), lambda i: (i, 0)),
            pl.BlockSpec((tm, nr), lambda i: (i, 0)),
        ],
        out_shape=[jax.ShapeDtypeStruct((t, ng), F32), jax.ShapeDtypeStruct((t, nr), F32)],
        compiler_params=_cparams(("parallel",)),
        name="norm_proj",
    )(x2, nw.reshape(1, d), wg, wr)


def _seg_rms(y, seg):
    parts = []
    for s in range(y.shape[1] // seg):
        p = y[:, s * seg:(s + 1) * seg]
        ms = jnp.mean(p * p, axis=-1, keepdims=True)
        parts.append(p * lax.rsqrt(ms + NORM_EPS))
    return jnp.concatenate(parts, axis=1)


def _out_proj_kernel(o_ref, gate_ref, nw_ref, w_ref, x_ref, fw_ref, out_ref, *, mode, final):
    o = o_ref[...]
    gt = gate_ref[...]
    act = gt * _sigmoid(gt)
    if mode == "gla":
        y = _seg_rms(o, GLA_HEAD_V) * nw_ref[...] * act
    elif mode == "ssd":
        y = _seg_rms(o * act, SSD_GROUP_W) * nw_ref[...]
    else:
        y = o * act
    out = x_ref[...] + jnp.dot(y.astype(BF16), w_ref[...], preferred_element_type=F32)
    if final:
        ms = jnp.mean(out * out, axis=-1, keepdims=True)
        out = out * lax.rsqrt(ms + NORM_EPS) * fw_ref[...]
    out_ref[...] = out


def _out_proj(mode, o2, gate2, nw, w_out, x2, fw, final, tm=256):
    t, di = o2.shape
    tm = min(tm, t)
    d = x2.shape[1]
    row = lambda i: (i, 0)
    fix = lambda i: (0, 0)
    return pl.pallas_call(
        functools.partial(_out_proj_kernel, mode=mode, final=final),
        grid=(t // tm,),
        in_specs=[
            pl.BlockSpec((tm, di), row),
            pl.BlockSpec((tm, di), row),
            pl.BlockSpec((1, di), fix),
            pl.BlockSpec((di, d), fix),
            pl.BlockSpec((tm, d), row),
            pl.BlockSpec((1, d), fix),
        ],
        out_specs=pl.BlockSpec((tm, d), row),
        out_shape=jax.ShapeDtypeStruct((t, d), F32),
        compiler_params=_cparams(("parallel",)),
        name="out_proj_" + mode,
    )(o2, gate2, nw.reshape(1, di), w_out, x2, fw.reshape(1, d))


BF16_SUBLANES = 16
GLA_CHUNKS_PER_STEP = 2
GLA_SEQS_PER_STEP = 2


def _gla_levels(c):
    return tuple(c >> (i + 1) for i in range(int(math.log2(c))))


def _gla_consts(c):
    levels = _gla_levels(c)
    u = np.arange(c)[:, None]
    j = np.arange(c)[None, :]
    blocks = [(j <= u), (j > u)]
    lev = np.full((c, c), len(levels) + 1, np.int32)
    t = np.arange(c)[:, None]
    s = np.arange(c)[None, :]
    for li, h in enumerate(levels):
        b = (u // (2 * h)) * (2 * h) + h - 1
        blocks.append((j > np.minimum(u, b)) & (j <= np.maximum(u, b)))
        sib = (t // (2 * h) == s // (2 * h)) & (t % (2 * h) >= h) & (s % (2 * h) < h)
        lev[sib] = li
    lev[np.eye(c, dtype=bool)] = len(levels)
    m = np.concatenate(blocks, axis=0).astype(np.float32)
    kpad = -(-3 * c // BF16_SUBLANES) * BF16_SUBLANES
    m3 = np.zeros((m.shape[0], kpad), np.float32)
    m3[:, 0:3 * c] = np.concatenate([m, m, m], axis=1)
    if 2 * c == LANE:
        lev = np.concatenate([lev, lev], axis=1)
    return jnp.asarray(m3, BF16), jnp.asarray(lev)


def _gla_att_pair(qp, kp, e_lv, lev, c):
    nl = len(e_lv)
    z = jnp.zeros((c, GLA_HEAD_K), BF16)

    def nt(qe, ke):
        kb = ke.astype(BF16)
        rhs = jnp.concatenate([jnp.concatenate([kb[:, 0:GLA_HEAD_K], z], axis=1),
                               jnp.concatenate([z, kb[:, GLA_HEAD_K:]], axis=1)], axis=0)
        return lax.dot_general(qe.astype(BF16), rhs, (((1,), (1,)), ((), ())), preferred_element_type=F32)

    att = jnp.where(lev == nl, nt(qp, kp), 0.0)
    for li in range(nl):
        att = jnp.where(lev == li, nt(qp * e_lv[li], kp * e_lv[li]), att)
    return att


def _gla_att_head(qh, kh, e_lv, lev):
    nl = len(e_lv)
    att = jnp.where(lev == nl, _dot_nt(qh, kh), 0.0)
    for li in range(nl):
        att = jnp.where(lev == li, _dot_nt(qh * e_lv[li], kh * e_lv[li]), att)
    return att


def _gla_chunk(xc, s_ref, it, wgk, bgk, m3, lev, c):
    levels = _gla_levels(c)
    nl = len(levels)
    v = xc[:, 0:D_INNER]
    q = xc[:, D_INNER:D_INNER + GLA_KEY_DIM] * (GLA_HEAD_K ** -0.5)
    k = xc[:, D_INNER + GLA_KEY_DIM:D_INNER + 2 * GLA_KEY_DIM]
    lr = xc[:, D_INNER + 2 * GLA_KEY_DIM:]
    z = _dot(lr, wgk) + bgk
    g = (jnp.minimum(z, 0.0) - jnp.log1p(jnp.exp(-jnp.abs(z)))) / GLA_GATE_NORMALIZER
    parts = list(_split3(g))
    if m3.shape[1] > 3 * c:
        parts.append(jnp.zeros((m3.shape[1] - 3 * c, g.shape[1]), BF16))
    e_all = jnp.exp(jnp.dot(m3, jnp.concatenate(parts, axis=0), preferred_element_type=F32))
    e_in, e_out = e_all[0:c], e_all[c:2 * c]
    e_lv = [e_all[(2 + li) * c:(3 + li) * c] for li in range(nl)]
    qe_in = q * e_in
    ke_out = k * e_out
    kc = max(c, BF16_SUBLANES)
    outs = []

    def head_tail(h, att_h, vcat_rows):
        ks = slice(h * GLA_HEAD_K, (h + 1) * GLA_HEAD_K)
        vh = v[:, h * GLA_HEAD_V:(h + 1) * GLA_HEAD_V]
        s_h = s_ref[it, h]
        o_h = _dot(att_h, vcat_rows) + _dot(qe_in[:, ks], s_h)
        fill = [] if 2 * c == LANE else [jnp.zeros((LANE - 2 * c, GLA_HEAD_K), F32)]
        xt = jnp.concatenate([ke_out[:, ks], e_in[:, ks]] + fill, axis=0).T
        dcol = xt[:, 2 * c - 1:2 * c]
        vpad = vh if kc == c else jnp.concatenate([vh, jnp.zeros((kc - c, GLA_HEAD_V), F32)], axis=0)
        s_ref[it, h] = s_h * dcol + _dot(xt[:, 0:kc], vpad)
        return o_h

    if 2 * c == LANE:
        lane_lo = lax.broadcasted_iota(jnp.int32, (c, LANE), 1) < c
        for pr in range(GLA_HEADS // 2):
            ls = slice(2 * pr * GLA_HEAD_K, (2 * pr + 2) * GLA_HEAD_K)
            att = _gla_att_pair(q[:, ls], k[:, ls], [e[:, ls] for e in e_lv], lev, c)
            vcat = v[:, 2 * pr * GLA_HEAD_V:(2 * pr + 2) * GLA_HEAD_V]
            vcat = jnp.concatenate([vcat[:, 0:GLA_HEAD_V], vcat[:, GLA_HEAD_V:]], axis=0)
            outs.append(head_tail(2 * pr, jnp.where(lane_lo, att, 0.0), vcat))
            outs.append(head_tail(2 * pr + 1, jnp.where(lane_lo, 0.0, att), vcat))
    else:
        for h in range(GLA_HEADS):
            ks = slice(h * GLA_HEAD_K, (h + 1) * GLA_HEAD_K)
            att = _gla_att_head(q[:, ks], k[:, ks], [e[:, ks] for e in e_lv], lev)
            att = jnp.concatenate([att, jnp.zeros((c, kc - c), F32)], axis=1)
            vh = v[:, h * GLA_HEAD_V:(h + 1) * GLA_HEAD_V]
            vpad = jnp.concatenate([vh, jnp.zeros((kc - c, GLA_HEAD_V), F32)], axis=0)
            outs.append(head_tail(h, att, vpad))
    return jnp.concatenate(outs, axis=1)


def _gla_scan_kernel(rest_ref, s0_ref, wgk_ref, bgk_ref, m_ref, lev_ref, o_ref, s_ref, *, c, nch, items):
    t = pl.program_id(1)

    @pl.when(t == 0)
    def _():
        s_ref[...] = s0_ref[...]

    wgk, bgk, m3, lev = wgk_ref[...], bgk_ref[...], m_ref[...], lev_ref[...]
    for it in range(items):
        for ci in range(nch):
            rows = slice(ci * c, (ci + 1) * c)
            o_ref[it, rows, :] = _gla_chunk(rest_ref[it, rows, :], s_ref, it, wgk, bgk, m3, lev, c)


def _gla_scan(rest3, s0, w_gk2, b_gk):
    b, l, nr = rest3.shape
    c = math.gcd(l, CHUNK)
    nch = math.gcd(l // c, GLA_CHUNKS_PER_STEP)
    nt = l // (c * nch)
    items = math.gcd(b, GLA_SEQS_PER_STEP) if nt == 1 else 1
    m01, lev = _gla_consts(c)
    wgk = jnp.zeros((LANE, GLA_KEY_DIM), F32).at[:GLA_GATE_RANK].set(w_gk2).astype(BF16)
    sblk = (items, GLA_HEADS, GLA_HEAD_K, GLA_HEAD_V)
    return pl.pallas_call(
        functools.partial(_gla_scan_kernel, c=c, nch=nch, items=items),
        grid=(b // items, nt),
        in_specs=[
            pl.BlockSpec((items, c * nch, nr), lambda i, t: (i, t, 0)),
            pl.BlockSpec(sblk, lambda i, t: (i, 0, 0, 0)),
            pl.BlockSpec((LANE, GLA_KEY_DIM), lambda i, t: (0, 0)),
            pl.BlockSpec((1, GLA_KEY_DIM), lambda i, t: (0, 0)),
            pl.BlockSpec(m01.shape, lambda i, t: (0, 0)),
            pl.BlockSpec(lev.shape, lambda i, t: (0, 0)),
        ],
        out_specs=[
            pl.BlockSpec((items, c * nch, D_INNER), lambda i, t: (i, t, 0)),
            pl.BlockSpec(sblk, lambda i, t: (i, 0, 0, 0)),
        ],
        out_shape=[jax.ShapeDtypeStruct((b, l, D_INNER), F32), jax.ShapeDtypeStruct(s0.shape, F32)],
        compiler_params=_cparams(("parallel", "arbitrary")),
        name="gla_scan",
    )(rest3, s0, wgk, b_gk.reshape(1, GLA_KEY_DIM), m01, lev)


def _ssd_consts():
    c = CHUNK
    hp = np.arange(D_INNER)
    expand = (np.arange(LANE)[:, None] == (hp // SSD_HEAD_DIM)[None, :])
    j = np.arange(c)[:, None]
    s_of = (np.arange(SSD_HEADS * c) % c)[None, :]
    tle = (j <= s_of).astype(np.float32)
    tri = np.where(s_of <= j, 0.0, NEG_BIG).astype(np.float32)
    tl = (np.arange(c)[None, :] <= np.arange(c)[:, None])
    r = np.arange(2 * c)[:, None] // c
    cc = np.arange(LANE)[None, :] // SSD_HEAD_DIM
    bd = (r == cc).astype(np.float32)
    return (jnp.asarray(expand, BF16), jnp.asarray(tle), jnp.asarray(tri),
            jnp.asarray(tl, BF16), jnp.asarray(bd))


def _ssd_scan_kernel(rest_ref, s0_ref, conv0_ref, cw_ref, cb_ref, dtb_ref, alog_ref, dsk_ref,
                     ex_ref, tle_ref, tri_ref, tl_ref, bd_ref,
                     y_ref, sfin_ref, ext_ref, st_ref, *, valid):
    c = CHUNK
    t = pl.program_id(1)
    nt = pl.num_programs(1)

    @pl.when(t == 0)
    def _():
        ext_ref[0:8, :] = conv0_ref[0]
        for g in range(SSD_GROUPS):
            st_ref[g] = s0_ref[0, g].T

    raw = _pad_rows(rest_ref[0], c)
    ext_ref[8:8 + c, :] = raw[:, 0:SSD_CONV_DIM]
    conv = cb_ref[...] + (cw_ref[0:1, :] * ext_ref[5:5 + c, :] + cw_ref[1:2, :] * ext_ref[6:6 + c, :]
                          + cw_ref[2:3, :] * ext_ref[7:7 + c, :] + cw_ref[3:4, :] * ext_ref[8:8 + c, :])
    if valid == c:
        ext_ref[0:8, :] = ext_ref[c:c + 8, :]
    xbc = conv * _sigmoid(conv)
    xs = xbc[:, 0:D_INNER]
    bm = xbc[:, D_INNER:D_INNER + SSD_GROUPS * SSD_STATE]
    cm = xbc[:, D_INNER + SSD_GROUPS * SSD_STATE:]

    dt = _softplus(raw[:, SSD_CONV_DIM:] + dtb_ref[...])
    if valid < c:
        row = lax.broadcasted_iota(jnp.int32, dt.shape, 0)
        dt = jnp.where(row < valid, dt, 0.0)
    la = dt * (-jnp.exp(alog_ref[...]))
    rep = _dot01_rhs(jnp.concatenate([dt, la], axis=0), ex_ref[...])
    dt_rep, la_rep = rep[0:c], rep[c:2 * c]
    cum_rep = _dot01_lhs(tl_ref[...], la_rep)
    cum_t = _dot01_lhs(jnp.ones((8, c), BF16), la_rep * tle_ref[...])[0:1]
    decay = jnp.exp(cum_rep - cum_t + tri_ref[...])
    e_cum = jnp.exp(cum_rep)
    last = cum_rep[c - 1:c, :]
    e_last = jnp.exp(last)
    u = xs * dt_rep
    uw = u * jnp.exp(last - cum_rep)
    bd = bd_ref[...]

    for g in range(SSD_GROUPS):
        gs = slice(g * SSD_GROUP_W, (g + 1) * SSD_GROUP_W)
        ns = slice(g * SSD_STATE, (g + 1) * SSD_STATE)
        cg, bg = cm[:, ns], bm[:, ns]
        cb_rep = _dot_nt(cg, jnp.concatenate([bg] * (SSD_GROUP_W // c), axis=0))
        wg = decay[:, gs] * cb_rep
        ug = u[:, gs]
        parts = []
        for p in range(SSD_GROUP_W // LANE):
            ps = slice(p * LANE, (p + 1) * LANE)
            up = ug[:, ps]
            parts.append(_dot(wg[:, ps], jnp.concatenate([up, up], axis=0) * bd))
        s_g = st_ref[g]
        y_g = (jnp.concatenate(parts, axis=1) + _dot(cg, s_g) * e_cum[:, gs]
               + xs[:, gs] * dsk_ref[:, gs])
        bg_t = jnp.concatenate([bg, jnp.zeros_like(bg)], axis=0).T[:, 0:c]
        st_ref[g] = s_g * e_last[:, gs] + _dot(bg_t, uw[:, gs])
        y_ref[0, :, gs] = y_g[0:valid]

    @pl.when(t == nt - 1)
    def _():
        for g in range(SSD_GROUPS):
            sfin_ref[0, g] = st_ref[g].T


def _ssd_scan(rest3, s0, conv0, conv_w, conv_b, dt_bias, a_log, d_skip):
    b, l, nr = rest3.shape
    valid = min(CHUNK, l)
    nt = l // valid
    ex, tle, tri, tl, bd = _ssd_consts()
    s0g = s0.reshape(b, SSD_GROUPS, SSD_GROUP_W, SSD_STATE)
    conv0p = jnp.concatenate([jnp.zeros((b, 8 - (SSD_CONV - 1), SSD_CONV_DIM), F32), conv0], axis=1)
    pad = lambda a: jnp.zeros((1, LANE), F32).at[0, :SSD_HEADS].set(a)
    dsk = jnp.repeat(d_skip, SSD_HEAD_DIM).reshape(1, D_INNER)
    sblk = (1, SSD_GROUPS, SSD_GROUP_W, SSD_STATE)
    fix2 = lambda i, t: (0, 0)
    y, sfin = pl.pallas_call(
        functools.partial(_ssd_scan_kernel, valid=valid),
        grid=(b, nt),
        in_specs=[
            pl.BlockSpec((1, valid, nr), lambda i, t: (i, t, 0)),
            pl.BlockSpec(sblk, lambda i, t: (i, 0, 0, 0)),
            pl.BlockSpec((1, 8, SSD_CONV_DIM), lambda i, t: (i, 0, 0)),
            pl.BlockSpec((SSD_CONV, SSD_CONV_DIM), fix2),
            pl.BlockSpec((1, SSD_CONV_DIM), fix2),
            pl.BlockSpec((1, LANE), fix2),
            pl.BlockSpec((1, LANE), fix2),
            pl.BlockSpec((1, D_INNER), fix2),
            pl.BlockSpec(ex.shape, fix2),
            pl.BlockSpec(tle.shape, fix2),
            pl.BlockSpec(tri.shape, fix2),
            pl.BlockSpec(tl.shape, fix2),
            pl.BlockSpec(bd.shape, fix2),
        ],
        out_specs=[
            pl.BlockSpec((1, valid, D_INNER), lambda i, t: (i, t, 0)),
            pl.BlockSpec(sblk, lambda i, t: (i, 0, 0, 0)),
        ],
        out_shape=[jax.ShapeDtypeStruct((b, l, D_INNER), F32), jax.ShapeDtypeStruct(s0g.shape, F32)],
        scratch_shapes=[
            pltpu.VMEM((CHUNK + 8, SSD_CONV_DIM), F32),
            pltpu.VMEM((SSD_GROUPS, SSD_STATE, SSD_GROUP_W), F32),
        ],
        compiler_params=_cparams(("parallel", "arbitrary")),
        name="ssd_scan",
    )(rest3, s0g, conv0p, conv_w, conv_b.reshape(1, -1), pad(dt_bias), pad(a_log), dsk,
      ex, tle, tri, tl, bd)
    return y, sfin.reshape(s0.shape)


def _swa_mask(bq, first_block_has_no_past):
    rows = 4 * bq
    tq = (np.arange(rows) % bq)[:, None]
    s = (np.arange(4 * SWA_WINDOW) % (2 * SWA_WINDOW))[None, :]
    ok = (s > tq) & (s <= tq + SWA_WINDOW)
    if first_block_has_no_past:
        ok = ok & (s >= SWA_WINDOW)
    return np.where(ok, 0.0, NEG_BIG).astype(np.float32)


def _block_diag_pair(col, rolled, odd):
    lane = lax.broadcasted_iota(jnp.int32, col.shape, 1)
    lo = lane < SWA_HEAD_DIM
    if odd:
        top = jnp.where(lo, rolled, 0.0)
        bot = jnp.where(lo, 0.0, col)
    else:
        top = jnp.where(lo, col, 0.0)
        bot = jnp.where(lo, 0.0, rolled)
    return jnp.concatenate([top, bot], axis=0)


def _swa_kernel(sink_ref, x_ref, kprev_ref, vprev_ref, mask_ref, ones_ref, o_ref, *, bq, items):
    w = SWA_WINDOW
    n = pl.program_id(1)
    amask = mask_ref[jnp.where(n == 0, 0, 1)]
    npair = (SWA_Q_HEADS // SWA_KV_HEADS) // 2
    lane_lo = lax.broadcasted_iota(jnp.int32, (npair * bq, LANE), 1) < SWA_HEAD_DIM
    sinks = []
    for j in range(SWA_KV_HEADS):
        sinks.append([jnp.concatenate(
            [jnp.full((bq, 1), sink_ref[j * 2 * npair + 2 * p + half], F32) for p in range(npair)], axis=0)
            for half in range(2)])

    for i in range(items):
        x = x_ref[i]
        fill = [] if bq == w else [jnp.zeros((w - bq, SWA_KV_DIM), F32)]
        kall = jnp.concatenate([kprev_ref[i], x[:, D_INNER:D_INNER + SWA_KV_DIM]] + fill, axis=0)
        vall = jnp.concatenate([vprev_ref[i], x[:, D_INNER + SWA_KV_DIM:]] + fill, axis=0)
        for j in range(SWA_KV_HEADS):
            cs = slice((j // 2) * LANE, (j // 2 + 1) * LANE)
            kcol, vcol = kall[:, cs], vall[:, cs]
            k2 = _block_diag_pair(kcol, pltpu.roll(kcol, SWA_HEAD_DIM, axis=1), j % 2)
            v2 = _block_diag_pair(vcol, pltpu.roll(vcol, SWA_HEAD_DIM, axis=1), j % 2)
            v2e = jnp.concatenate([v2.astype(BF16), ones_ref[...]], axis=1)
            qbase = j * npair * LANE
            qs = jnp.concatenate([x[:, qbase + p * LANE:qbase + (p + 1) * LANE] for p in range(npair)],
                                 axis=0) * (SWA_HEAD_DIM ** -0.5)
            sc = _dot_nt(qs, k2) + amask
            pes, ms = [], []
            for half in range(2):
                sh = sc[:, half * 2 * w:(half + 1) * 2 * w]
                m = jnp.maximum(jnp.max(sh, axis=-1, keepdims=True), sinks[j][half])
                pes.append(jnp.exp(sh - m).astype(BF16))
                ms.append(m)
            o = jnp.dot(jnp.concatenate(pes, axis=1), v2e, preferred_element_type=F32)
            esink = jnp.where(lane_lo, jnp.exp(sinks[j][0] - ms[0]), jnp.exp(sinks[j][1] - ms[1]))
            res = o[:, 0:LANE] / (o[:, LANE:] + esink)
            for p in range(npair):
                o_ref[i, :, qbase + p * LANE:qbase + (p + 1) * LANE] = res[p * bq:(p + 1) * bq]


def _swa_attn(rest3, kprev, vprev, sinks, has_past):
    b, l, nr = rest3.shape
    w = SWA_WINDOW
    bq = math.gcd(l, w)
    nb = l // bq
    assert nb == 1 or bq == w
    m_later = _swa_mask(bq, False)
    m_first = m_later if has_past else _swa_mask(bq, True)
    masks = jnp.asarray(np.stack([m_first, m_later]))
    ones2 = jnp.asarray((np.arange(4 * w)[:, None] // (2 * w)) == (np.arange(LANE)[None, :] // SWA_HEAD_DIM), BF16)
    if nb == 1:
        items = math.gcd(b, 8)
        kspec = pl.BlockSpec((items, w, SWA_KV_DIM), lambda i, n: (i, 0, 0))
        vspec = kspec
        kin, vin = kprev, vprev
    else:
        assert not has_past
        items = 1
        kcol = D_INNER // SWA_KV_DIM
        kspec = pl.BlockSpec((1, w, SWA_KV_DIM), lambda i, n: (i, jnp.maximum(n - 1, 0), kcol))
        vspec = pl.BlockSpec((1, w, SWA_KV_DIM), lambda i, n: (i, jnp.maximum(n - 1, 0), kcol + 1))
        kin, vin = rest3, rest3
    return pl.pallas_call(
        functools.partial(_swa_kernel, bq=bq, items=items),
        grid=(b // items, nb),
        in_specs=[
            pl.BlockSpec(memory_space=pltpu.SMEM),
            pl.BlockSpec((items, bq, nr), lambda i, n: (i, n, 0)),
            kspec,
            vspec,
            pl.BlockSpec(masks.shape, lambda i, n: (0, 0, 0)),
            pl.BlockSpec(ones2.shape, lambda i, n: (0, 0)),
        ],
        out_specs=pl.BlockSpec((items, bq, D_INNER), lambda i, n: (i, n, 0)),
        out_shape=jax.ShapeDtypeStruct((b, l, D_INNER), F32),
        compiler_params=_cparams(("parallel", "parallel")),
        name="swa_attn",
    )(sinks, rest3, kin, vin, masks, ones2)


def _pad_cols(w, n):
    return jnp.concatenate([w, jnp.zeros((w.shape[0], n - w.shape[1]), w.dtype)], axis=1)


def _prep_weights(p):
    kd, di = GLA_KEY_DIM, D_INNER
    out = {}
    for i in (0, 3):
        w = p[f"l{i}_w_in"]
        q, k, v, r, lr = (w[:, 0:kd], w[:, kd:2 * kd], w[:, 2 * kd:2 * kd + di],
                          w[:, 2 * kd + di:2 * kd + 2 * di], w[:, 2 * kd + 2 * di:])
        out[f"l{i}_wg"] = r.astype(BF16)
        out[f"l{i}_wr"] = jnp.concatenate([v, q, k, _pad_cols(lr, LANE)], axis=1).astype(BF16)
    w = p["l1_w_in"]
    out["l1_wg"] = w[:, 0:di].astype(BF16)
    out["l1_wr"] = _pad_cols(w[:, di:], SSD_CONV_DIM + LANE).astype(BF16)
    w = p["l2_w_in"]
    out["l2_wg"] = w[:, di + 2 * SWA_KV_DIM:].astype(BF16)
    out["l2_wr"] = w[:, 0:di + 2 * SWA_KV_DIM].astype(BF16)
    for i in range(4):
        out[f"l{i}_wo"] = p[f"l{i}_w_out"].astype(BF16)
    return out


def _trunk(x, st, p, wts, has_past):
    b, l, d = x.shape
    x2 = x.reshape(b * l, d)
    ones_inner = jnp.ones((D_INNER,), F32)
    new = {}

    def gla(i, x2, s0):
        gate, rest = _norm_proj(x2, p[f"l{i}_norm"], wts[f"l{i}_wg"], wts[f"l{i}_wr"])
        o, s_fin = _gla_scan(rest.reshape(b, l, -1), s0, p[f"l{i}_w_gk2"], p[f"l{i}_b_gk"])
        hn = jnp.tile(p[f"l{i}_head_norm"], GLA_HEADS)
        fin = i == 3
        x2 = _out_proj("gla", o.reshape(b * l, D_INNER), gate, hn, wts[f"l{i}_wo"], x2,
                       p["final_norm"], fin)
        return x2, s_fin

    x2, new["gla0"] = gla(0, x2, st["gla0"])

    gate, rest = _norm_proj(x2, p["l1_norm"], wts["l1_wg"], wts["l1_wr"])
    rest3 = rest.reshape(b, l, -1)
    y, new["ssm"] = _ssd_scan(rest3, st["ssm"], st["conv"], p["l1_conv_w"], p["l1_conv_b"],
                              p["l1_dt_bias"], p["l1_a_log"], p["l1_d_skip"])
    ext_tail = jnp.concatenate([st["conv"], rest3[:, max(l - (SSD_CONV - 1), 0):, 0:SSD_CONV_DIM]], axis=1)
    new["conv"] = ext_tail[:, -(SSD_CONV - 1):]
    x2 = _out_proj("ssd", y.reshape(b * l, D_INNER), gate, p["l1_gate_norm"], wts["l1_wo"], x2,
                   p["final_norm"], False)

    gate, rest = _norm_proj(x2, p["l2_norm"], wts["l2_wg"], wts["l2_wr"])
    rest3 = rest.reshape(b, l, -1)
    kprev = st["swa_k"].reshape(b, SWA_WINDOW, SWA_KV_DIM)
    vprev = st["swa_v"].reshape(b, SWA_WINDOW, SWA_KV_DIM)
    o = _swa_attn(rest3, kprev, vprev, p["l2_sinks"], has_past)
    tail = max(l - SWA_WINDOW, 0)
    k_new = rest3[:, tail:, D_INNER:D_INNER + SWA_KV_DIM]
    v_new = rest3[:, tail:, D_INNER + SWA_KV_DIM:]
    new["swa_k"] = jnp.concatenate([kprev, k_new], axis=1)[:, -SWA_WINDOW:].reshape(st["swa_k"].shape)
    new["swa_v"] = jnp.concatenate([vprev, v_new], axis=1)[:, -SWA_WINDOW:].reshape(st["swa_v"].shape)
    x2 = _out_proj("swa", o.reshape(b * l, D_INNER), gate, ones_inner, wts["l2_wo"], x2,
                   p["final_norm"], False)

    x2, new["gla3"] = gla(3, x2, st["gla3"])
    return x2.reshape(b, l, d), new


def kernel(x_prompt, x_sample, state_gla_0, state_ssm_1, state_conv_1, cache_swa_k_2, cache_swa_v_2, state_gla_3, l0_norm, l0_w_in, l0_w_gk2, l0_b_gk, l0_head_norm, l0_w_out, l1_norm, l1_w_in, l1_conv_w, l1_conv_b, l1_dt_bias, l1_a_log, l1_d_skip, l1_gate_norm, l1_w_out, l2_norm, l2_w_in, l2_sinks, l2_w_out, l3_norm, l3_w_in, l3_w_gk2, l3_b_gk, l3_head_norm, l3_w_out, final_norm):
    p = dict(l0_norm=l0_norm, l0_w_in=l0_w_in, l0_w_gk2=l0_w_gk2, l0_b_gk=l0_b_gk,
             l0_head_norm=l0_head_norm, l0_w_out=l0_w_out,
             l1_norm=l1_norm, l1_w_in=l1_w_in, l1_conv_w=l1_conv_w, l1_conv_b=l1_conv_b,
             l1_dt_bias=l1_dt_bias, l1_a_log=l1_a_log, l1_d_skip=l1_d_skip,
             l1_gate_norm=l1_gate_norm, l1_w_out=l1_w_out,
             l2_norm=l2_norm, l2_w_in=l2_w_in, l2_sinks=l2_sinks, l2_w_out=l2_w_out,
             l3_norm=l3_norm, l3_w_in=l3_w_in, l3_w_gk2=l3_w_gk2, l3_b_gk=l3_b_gk,
             l3_head_norm=l3_head_norm, l3_w_out=l3_w_out, final_norm=final_norm)
    wts = _prep_weights(p)

    bp = x_prompt.shape[0]
    z = lambda a: jnp.zeros((bp,) + a.shape[1:], a.dtype)
    st_p = dict(gla0=z(state_gla_0), ssm=z(state_ssm_1), conv=z(state_conv_1),
                swa_k=z(cache_swa_k_2), swa_v=z(cache_swa_v_2), gla3=z(state_gla_3))
    st_s = dict(gla0=state_gla_0, ssm=state_ssm_1, conv=state_conv_1,
                swa_k=cache_swa_k_2, swa_v=cache_swa_v_2, gla3=state_gla_3)
    y_p, n_p = _trunk(x_prompt, st_p, p, wts, False)
    y_s, n_s = _trunk(x_sample, st_s, p, wts, True)
    return (y_p, y_s,
            n_p["gla0"], n_s["gla0"],
            n_p["ssm"], n_s["ssm"],
            n_p["conv"], n_s["conv"],
            n_p["swa_k"], n_s["swa_k"],
            n_p["swa_v"], n_s["swa_v"],
            n_p["gla3"], n_s["gla3"])
```

```python
import functools
import math

import numpy as np
import jax
import jax.numpy as jnp
from jax import lax
from jax.experimental import pallas as pl
from jax.experimental.pallas import tpu as pltpu

F32 = jnp.float32
BF16 = jnp.bfloat16

NORM_EPS = 1e-6
D_MODEL = 1024
D_INNER = 2048
CHUNK = 64
LANE = 128
BF16_SUBLANES = 16
NEG_BIG = -1e30

GLA_HEADS = 4
GLA_HEAD_K = 128
GLA_HEAD_V = 512
GLA_KEY_DIM = 512
GLA_GATE_RANK = 16
GLA_GATE_NORMALIZER = 16.0

SSD_GROUPS = 4
SSD_HEADS = 32
SSD_HEAD_DIM = 64
SSD_STATE = 128
SSD_CONV = 4
SSD_CONV_DIM = 3072
SSD_GROUP_W = D_INNER // SSD_GROUPS

SWA_WINDOW = 128
SWA_KV_HEADS = 4
SWA_HEAD_DIM = 64
SWA_Q_HEADS = 32
SWA_KV_DIM = 256

VMEM_LIMIT = 52 * 1024 * 1024


def _cparams(sem):
    return pltpu.CompilerParams(dimension_semantics=sem, vmem_limit_bytes=VMEM_LIMIT)


def _dot(a, b):
    return jnp.dot(a.astype(BF16), b.astype(BF16), preferred_element_type=F32)


def _dot_nt(a, b):
    return lax.dot_general(a.astype(BF16), b.astype(BF16), (((1,), (1,)), ((), ())),
                           preferred_element_type=F32)


def _split3(x):
    hi = x.astype(BF16)
    r1 = x - hi.astype(F32)
    mid = r1.astype(BF16)
    lo = (r1 - mid.astype(F32)).astype(BF16)
    return hi, mid, lo


def _dot01_lhs(m01, x):
    hi, mid, lo = _split3(x)
    d = lambda p: jnp.dot(m01, p, preferred_element_type=F32)
    return d(hi) + d(mid) + d(lo)


def _dot01_rhs(x, m01):
    hi, mid, lo = _split3(x)
    d = lambda p: jnp.dot(p, m01, preferred_element_type=F32)
    return d(hi) + d(mid) + d(lo)


def _sigmoid(x):
    return 1.0 / (1.0 + jnp.exp(-x))


def _softplus(x):
    return jnp.maximum(x, 0.0) + jnp.log1p(jnp.exp(-jnp.abs(x)))


def _mixer_out_dtype(block_rows):
    return BF16 if block_rows % BF16_SUBLANES == 0 else F32


def _pad_rows(x, rows):
    if x.shape[0] == rows:
        return x
    return jnp.concatenate([x, jnp.zeros((rows - x.shape[0], x.shape[1]), x.dtype)], axis=0)


def _norm_proj_kernel(x_ref, nw_ref, wg_ref, wr_ref, gate_ref, rest_ref):
    x = x_ref[...]
    ms = jnp.mean(x * x, axis=-1, keepdims=True)
    h = (x * lax.rsqrt(ms + NORM_EPS) * nw_ref[...]).astype(BF16)
    gate_ref[...] = jnp.dot(h, wg_ref[...], preferred_element_type=F32).astype(gate_ref.dtype)
    rest_ref[...] = jnp.dot(h, wr_ref[...], preferred_element_type=F32)


def _norm_proj(x2, nw, wg, wr, tm=256):
    t, d = x2.shape
    tm = min(tm, t)
    ng, nr = wg.shape[1], wr.shape[1]
    return pl.pallas_call(
        _norm_proj_kernel,
        grid=(t // tm,),
        in_specs=[
            pl.BlockSpec((tm, d), lambda i: (i, 0)),
            pl.BlockSpec((1, d), lambda i: (0, 0)),
            pl.BlockSpec((d, ng), lambda i: (0, 0)),
            pl.BlockSpec((d, nr), lambda i: (0, 0)),
        ],
        out_specs=[
            pl.BlockSpec((tm, ng), lambda i: (i, 0)),
            pl.BlockSpec((tm, nr), lambda i: (i, 0)),
        ],
        out_shape=[jax.ShapeDtypeStruct((t, ng), BF16), jax.ShapeDtypeStruct((t, nr), F32)],
        compiler_params=_cparams(("parallel",)),
        name="norm_proj",
    )(x2, nw.reshape(1, d), wg, wr)


def _seg_rms(y, seg):
    parts = []
    for s in range(y.shape[1] // seg):
        p = y[:, s * seg:(s + 1) * seg]
        ms = jnp.mean(p * p, axis=-1, keepdims=True)
        parts.append(p * lax.rsqrt(ms + NORM_EPS))
    return jnp.concatenate(parts, axis=1)


def _out_proj_kernel(o_ref, gate_ref, nw_ref, w_ref, x_ref, fw_ref, out_ref, *, mode, final):
    o = o_ref[...].astype(F32)
    gt = gate_ref[...].astype(F32)
    act = gt * _sigmoid(gt)
    if mode == "gla":
        y = _seg_rms(o, GLA_HEAD_V) * nw_ref[...] * act
    elif mode == "ssd":
        y = _seg_rms(o * act, SSD_GROUP_W) * nw_ref[...]
    else:
        y = o * act
    out = x_ref[...] + jnp.dot(y.astype(BF16), w_ref[...], preferred_element_type=F32)
    if final:
        ms = jnp.mean(out * out, axis=-1, keepdims=True)
        out = out * lax.rsqrt(ms + NORM_EPS) * fw_ref[...]
    out_ref[...] = out


def _out_proj(mode, o2, gate2, nw, w_out, x2, fw, final, tm=256):
    t, di = o2.shape
    tm = min(tm, t)
    d = x2.shape[1]
    row = lambda i: (i, 0)
    fix = lambda i: (0, 0)
    return pl.pallas_call(
        functools.partial(_out_proj_kernel, mode=mode, final=final),
        grid=(t // tm,),
        in_specs=[
            pl.BlockSpec((tm, di), row),
            pl.BlockSpec((tm, di), row),
            pl.BlockSpec((1, di), fix),
            pl.BlockSpec((di, d), fix),
            pl.BlockSpec((tm, d), row),
            pl.BlockSpec((1, d), fix),
        ],
        out_specs=pl.BlockSpec((tm, d), row),
        out_shape=jax.ShapeDtypeStruct((t, d), F32),
        compiler_params=_cparams(("parallel",)),
        name="out_proj_" + mode,
    )(o2, gate2, nw.reshape(1, di), w_out, x2, fw.reshape(1, d))


GLA_CHUNKS_PER_STEP = 4
GLA_SEQS_PER_STEP = 4
GLA_SINGLE_ANCHOR_MAX_DECAY = 60.0


def _gla_levels(c):
    return tuple(c >> (i + 1) for i in range(int(math.log2(c))))


def _gla_consts(c):
    levels = _gla_levels(c)
    u = np.arange(c)[:, None]
    j = np.arange(c)[None, :]
    blocks = [(j <= u), (j > u)]
    lev = np.full((c, c), len(levels) + 1, np.int32)
    t = np.arange(c)[:, None]
    s = np.arange(c)[None, :]
    for li, h in enumerate(levels):
        b = (u // (2 * h)) * (2 * h) + h - 1
        blocks.append((j > np.minimum(u, b)) & (j <= np.maximum(u, b)))
        sib = (t // (2 * h) == s // (2 * h)) & (t % (2 * h) >= h) & (s % (2 * h) < h)
        lev[sib] = li
    lev[np.eye(c, dtype=bool)] = len(levels)
    m = np.concatenate(blocks, axis=0).astype(np.float32)
    kpad = -(-3 * c // BF16_SUBLANES) * BF16_SUBLANES
    m3 = np.zeros((m.shape[0], kpad), np.float32)
    m3[:, 0:3 * c] = np.concatenate([m, m, m], axis=1)
    if 2 * c == LANE:
        lev = np.concatenate([lev, lev], axis=1)
    return jnp.asarray(m3, BF16), jnp.asarray(m3[0:c], BF16), jnp.asarray(lev)


def _gla_scores(qx, kx, e_q, e_k, lev, nt):
    att = None
    for li, (eq, ek) in enumerate(zip(e_q, e_k)):
        a = nt(qx if eq is None else qx * eq, kx if ek is None else kx * ek)
        att = jnp.where(lev == li, a, 0.0 if att is None else att)
    return att


def _nt_pair(qe, ke):
    kb = ke.astype(BF16)
    z = jnp.zeros((kb.shape[0], GLA_HEAD_K), BF16)
    rhs = jnp.concatenate([jnp.concatenate([kb[:, 0:GLA_HEAD_K], z], axis=1),
                           jnp.concatenate([z, kb[:, GLA_HEAD_K:]], axis=1)], axis=0)
    return lax.dot_general(qe.astype(BF16), rhs, (((1,), (1,)), ((), ())), preferred_element_type=F32)


def _gla_prep(xc, wgk, bgk):
    v = xc[:, 0:D_INNER]
    q = xc[:, D_INNER:D_INNER + GLA_KEY_DIM] * (GLA_HEAD_K ** -0.5)
    k = xc[:, D_INNER + GLA_KEY_DIM:D_INNER + 2 * GLA_KEY_DIM]
    lr = xc[:, D_INNER + 2 * GLA_KEY_DIM:]
    z = _dot(lr, wgk) + bgk
    g = (jnp.minimum(z, 0.0) - jnp.log1p(jnp.exp(-jnp.abs(z)))) / GLA_GATE_NORMALIZER
    return v, q, k, g


def _gla_chunk(v, q, k, g, s_in_ref, s_ref, it, m3, mc, lev, c, single_anchor):
    levels = _gla_levels(c)
    nl = len(levels)
    parts = list(_split3(g))
    if m3.shape[1] > 3 * c:
        parts.append(jnp.zeros((m3.shape[1] - 3 * c, g.shape[1]), BF16))
    g3 = jnp.concatenate(parts, axis=0)
    if single_anchor:
        cum = jnp.dot(mc, g3, preferred_element_type=F32)
        e_in, e_out = jnp.exp(cum), jnp.exp(cum[c - 1:c] - cum)
        e_q, e_k = [e_in], [jnp.exp(-cum)]
        lev = jnp.where(lev <= nl, 0, 1)
    else:
        e_all = jnp.exp(jnp.dot(m3, g3, preferred_element_type=F32))
        e_in, e_out = e_all[0:c], e_all[c:2 * c]
        e_q = [e_all[(2 + li) * c:(3 + li) * c] for li in range(nl)] + [None]
        e_k = e_q
    qe_in = q * e_in
    ke_out = k * e_out
    kc = max(c, BF16_SUBLANES)
    outs = []

    def head_tail(h, att_h, vcat_rows):
        ks = slice(h * GLA_HEAD_K, (h + 1) * GLA_HEAD_K)
        vh = v[:, h * GLA_HEAD_V:(h + 1) * GLA_HEAD_V]
        s_h = s_in_ref[it, h]
        o_h = _dot(att_h, vcat_rows) + _dot(qe_in[:, ks], s_h)
        fill = [] if 2 * c == LANE else [jnp.zeros((LANE - 2 * c, GLA_HEAD_K), F32)]
        xt = jnp.concatenate([ke_out[:, ks], e_in[:, ks]] + fill, axis=0).T
        dcol = xt[:, 2 * c - 1:2 * c]
        vpad = vh if kc == c else jnp.concatenate([vh, jnp.zeros((kc - c, GLA_HEAD_V), F32)], axis=0)
        s_ref[it, h] = s_h * dcol + _dot(xt[:, 0:kc], vpad)
        return o_h

    if 2 * c == LANE:
        lane_lo = lax.broadcasted_iota(jnp.int32, (c, LANE), 1) < c
        for pr in range(GLA_HEADS // 2):
            ls = slice(2 * pr * GLA_HEAD_K, (2 * pr + 2) * GLA_HEAD_K)
            cut = lambda es: [None if e is None else e[:, ls] for e in es]
            att = _gla_scores(q[:, ls], k[:, ls], cut(e_q), cut(e_k), lev, _nt_pair)
            vcat = v[:, 2 * pr * GLA_HEAD_V:(2 * pr + 2) * GLA_HEAD_V]
            vcat = jnp.concatenate([vcat[:, 0:GLA_HEAD_V], vcat[:, GLA_HEAD_V:]], axis=0)
            outs.append(head_tail(2 * pr, jnp.where(lane_lo, att, 0.0), vcat))
            outs.append(head_tail(2 * pr + 1, jnp.where(lane_lo, 0.0, att), vcat))
    else:
        for h in range(GLA_HEADS):
            ks = slice(h * GLA_HEAD_K, (h + 1) * GLA_HEAD_K)
            cut = lambda es: [None if e is None else e[:, ks] for e in es]
            att = _gla_scores(q[:, ks], k[:, ks], cut(e_q), cut(e_k), lev, _dot_nt)
            att = jnp.concatenate([att, jnp.zeros((c, kc - c), F32)], axis=1)
            vh = v[:, h * GLA_HEAD_V:(h + 1) * GLA_HEAD_V]
            vpad = jnp.concatenate([vh, jnp.zeros((kc - c, GLA_HEAD_V), F32)], axis=0)
            outs.append(head_tail(h, att, vpad))
    return jnp.concatenate(outs, axis=1)


def _gla_scan_kernel(rest_ref, s0_ref, wgk_ref, bgk_ref, m_ref, mc_ref, lev_ref, o_ref, s_ref,
                     *, c, nch, items, single_step):
    if not single_step:
        @pl.when(pl.program_id(1) == 0)
        def _():
            s_ref[...] = s0_ref[...]

    wgk, bgk = wgk_ref[...], bgk_ref[...]
    for it in range(items):
        preps = [_gla_prep(rest_ref[it, ci * c:(ci + 1) * c, :], wgk, bgk) for ci in range(nch)]

        def run(single_anchor, it=it, preps=preps):
            for ci, (v, q, k, g) in enumerate(preps):
                s_in = s0_ref if single_step and ci == 0 else s_ref
                o = _gla_chunk(v, q, k, g, s_in, s_ref, it, m_ref[...], mc_ref[...], lev_ref[...], c, single_anchor)
                o_ref[it, ci * c:(ci + 1) * c, :] = o.astype(o_ref.dtype)

        if c < CHUNK:
            run(False)
            continue
        low = None
        for _, _, _, g in preps:
            tot = jnp.min(jnp.sum(g, axis=0, keepdims=True))
            low = tot if low is None else jnp.minimum(low, tot)
        single_anchor_ok = low >= -GLA_SINGLE_ANCHOR_MAX_DECAY
        pl.when(single_anchor_ok)(functools.partial(run, True))
        pl.when(jnp.logical_not(single_anchor_ok))(functools.partial(run, False))


def _gla_scan(rest3, s0, w_gk2, b_gk):
    b, l, nr = rest3.shape
    c = math.gcd(l, CHUNK)
    nch = math.gcd(l // c, GLA_CHUNKS_PER_STEP)
    nt = l // (c * nch)
    items = math.gcd(b, GLA_SEQS_PER_STEP) if nt == 1 else 1
    m01, mc, lev = _gla_consts(c)
    wgk = jnp.zeros((LANE, GLA_KEY_DIM), F32).at[:GLA_GATE_RANK].set(w_gk2).astype(BF16)
    sblk = (items, GLA_HEADS, GLA_HEAD_K, GLA_HEAD_V)
    return pl.pallas_call(
        functools.partial(_gla_scan_kernel, c=c, nch=nch, items=items, single_step=nt == 1),
        grid=(b // items, nt),
        in_specs=[
            pl.BlockSpec((items, c * nch, nr), lambda i, t: (i, t, 0)),
            pl.BlockSpec(sblk, lambda i, t: (i, 0, 0, 0)),
            pl.BlockSpec((LANE, GLA_KEY_DIM), lambda i, t: (0, 0)),
            pl.BlockSpec((1, GLA_KEY_DIM), lambda i, t: (0, 0)),
            pl.BlockSpec(m01.shape, lambda i, t: (0, 0)),
            pl.BlockSpec(mc.shape, lambda i, t: (0, 0)),
            pl.BlockSpec(lev.shape, lambda i, t: (0, 0)),
        ],
        out_specs=[
            pl.BlockSpec((items, c * nch, D_INNER), lambda i, t: (i, t, 0)),
            pl.BlockSpec(sblk, lambda i, t: (i, 0, 0, 0)),
        ],
        out_shape=[jax.ShapeDtypeStruct((b, l, D_INNER), _mixer_out_dtype(c)),
                   jax.ShapeDtypeStruct(s0.shape, F32)],
        compiler_params=_cparams(("parallel", "arbitrary")),
        name="gla_scan",
    )(rest3, s0, wgk, b_gk.reshape(1, GLA_KEY_DIM), m01, mc, lev)


def _ssd_consts():
    c = CHUNK
    hp = np.arange(D_INNER)
    expand = (np.arange(LANE)[:, None] == (hp // SSD_HEAD_DIM)[None, :])
    j = np.arange(c)[:, None]
    s_of = (np.arange(SSD_HEADS * c) % c)[None, :]
    tle = (j <= s_of).astype(np.float32)
    tri = np.where(s_of <= j, 0.0, NEG_BIG).astype(np.float32)
    tl = (np.arange(c)[None, :] <= np.arange(c)[:, None])
    r = np.arange(2 * c)[:, None] // c
    cc = np.arange(LANE)[None, :] // SSD_HEAD_DIM
    bd = (r == cc).astype(np.float32)
    return (jnp.asarray(expand, BF16), jnp.asarray(tle), jnp.asarray(tri),
            jnp.asarray(tl, BF16), jnp.asarray(bd))


def _ssd_scan_kernel(rest_ref, s0_ref, conv0_ref, cw_ref, cb_ref, dtb_ref, alog_ref, dsk_ref,
                     ex_ref, tle_ref, tri_ref, tl_ref, bd_ref,
                     y_ref, sfin_ref, ext_ref, st_ref, *, valid):
    c = CHUNK
    t = pl.program_id(1)
    nt = pl.num_programs(1)

    @pl.when(t == 0)
    def _():
        ext_ref[0:8, :] = conv0_ref[0]
        for g in range(SSD_GROUPS):
            st_ref[g] = s0_ref[0, g].T

    raw = _pad_rows(rest_ref[0], c)
    ext_ref[8:8 + c, :] = raw[:, 0:SSD_CONV_DIM]
    conv = cb_ref[...] + (cw_ref[0:1, :] * ext_ref[5:5 + c, :] + cw_ref[1:2, :] * ext_ref[6:6 + c, :]
                          + cw_ref[2:3, :] * ext_ref[7:7 + c, :] + cw_ref[3:4, :] * ext_ref[8:8 + c, :])
    if valid == c:
        ext_ref[0:8, :] = ext_ref[c:c + 8, :]
    xbc = conv * _sigmoid(conv)
    xs = xbc[:, 0:D_INNER]
    bm = xbc[:, D_INNER:D_INNER + SSD_GROUPS * SSD_STATE]
    cm = xbc[:, D_INNER + SSD_GROUPS * SSD_STATE:]

    dt = _softplus(raw[:, SSD_CONV_DIM:] + dtb_ref[...])
    if valid < c:
        row = lax.broadcasted_iota(jnp.int32, dt.shape, 0)
        dt = jnp.where(row < valid, dt, 0.0)
    la = dt * (-jnp.exp(alog_ref[...]))
    rep = _dot01_rhs(jnp.concatenate([dt, la], axis=0), ex_ref[...])
    dt_rep, la_rep = rep[0:c], rep[c:2 * c]
    cum_rep = _dot01_lhs(tl_ref[...], la_rep)
    cum_t = _dot01_lhs(jnp.ones((8, c), BF16), la_rep * tle_ref[...])[0:1]
    decay = jnp.exp(cum_rep - cum_t + tri_ref[...])
    e_cum = jnp.exp(cum_rep)
    last = cum_rep[c - 1:c, :]
    e_last = jnp.exp(last)
    u = xs * dt_rep
    uw = u * jnp.exp(last - cum_rep)
    bd = bd_ref[...]

    for g in range(SSD_GROUPS):
        gs = slice(g * SSD_GROUP_W, (g + 1) * SSD_GROUP_W)
        ns = slice(g * SSD_STATE, (g + 1) * SSD_STATE)
        cg, bg = cm[:, ns], bm[:, ns]
        cb_rep = _dot_nt(cg, jnp.concatenate([bg] * (SSD_GROUP_W // c), axis=0))
        wg = decay[:, gs] * cb_rep
        ug = u[:, gs]
        parts = []
        for p in range(SSD_GROUP_W // LANE):
            ps = slice(p * LANE, (p + 1) * LANE)
            up = ug[:, ps]
            parts.append(_dot(wg[:, ps], jnp.concatenate([up, up], axis=0) * bd))
        s_g = st_ref[g]
        y_g = (jnp.concatenate(parts, axis=1) + _dot(cg, s_g) * e_cum[:, gs]
               + xs[:, gs] * dsk_ref[:, gs])
        bg_t = jnp.concatenate([bg, jnp.zeros_like(bg)], axis=0).T[:, 0:c]
        st_ref[g] = s_g * e_last[:, gs] + _dot(bg_t, uw[:, gs])
        y_ref[0, :, gs] = y_g[0:valid].astype(y_ref.dtype)

    @pl.when(t == nt - 1)
    def _():
        for g in range(SSD_GROUPS):
            sfin_ref[0, g] = st_ref[g].T


def _ssd_scan(rest3, s0, conv0, conv_w, conv_b, dt_bias, a_log, d_skip):
    b, l, nr = rest3.shape
    valid = min(CHUNK, l)
    nt = l // valid
    ex, tle, tri, tl, bd = _ssd_consts()
    s0g = s0.reshape(b, SSD_GROUPS, SSD_GROUP_W, SSD_STATE)
    conv0p = jnp.concatenate([jnp.zeros((b, 8 - (SSD_CONV - 1), SSD_CONV_DIM), F32), conv0], axis=1)
    pad = lambda a: jnp.zeros((1, LANE), F32).at[0, :SSD_HEADS].set(a)
    dsk = jnp.repeat(d_skip, SSD_HEAD_DIM).reshape(1, D_INNER)
    sblk = (1, SSD_GROUPS, SSD_GROUP_W, SSD_STATE)
    fix2 = lambda i, t: (0, 0)
    y, sfin = pl.pallas_call(
        functools.partial(_ssd_scan_kernel, valid=valid),
        grid=(b, nt),
        in_specs=[
            pl.BlockSpec((1, valid, nr), lambda i, t: (i, t, 0)),
            pl.BlockSpec(sblk, lambda i, t: (i, 0, 0, 0)),
            pl.BlockSpec((1, 8, SSD_CONV_DIM), lambda i, t: (i, 0, 0)),
            pl.BlockSpec((SSD_CONV, SSD_CONV_DIM), fix2),
            pl.BlockSpec((1, SSD_CONV_DIM), fix2),
            pl.BlockSpec((1, LANE), fix2),
            pl.BlockSpec((1, LANE), fix2),
            pl.BlockSpec((1, D_INNER), fix2),
            pl.BlockSpec(ex.shape, fix2),
            pl.BlockSpec(tle.shape, fix2),
            pl.BlockSpec(tri.shape, fix2),
            pl.BlockSpec(tl.shape, fix2),
            pl.BlockSpec(bd.shape, fix2),
        ],
        out_specs=[
            pl.BlockSpec((1, valid, D_INNER), lambda i, t: (i, t, 0)),
            pl.BlockSpec(sblk, lambda i, t: (i, 0, 0, 0)),
        ],
        out_shape=[jax.ShapeDtypeStruct((b, l, D_INNER), _mixer_out_dtype(valid)),
                   jax.ShapeDtypeStruct(s0g.shape, F32)],
        scratch_shapes=[
            pltpu.VMEM((CHUNK + 8, SSD_CONV_DIM), F32),
            pltpu.VMEM((SSD_GROUPS, SSD_STATE, SSD_GROUP_W), F32),
        ],
        compiler_params=_cparams(("parallel", "arbitrary")),
        name="ssd_scan",
    )(rest3, s0g, conv0p, conv_w, conv_b.reshape(1, -1), pad(dt_bias), pad(a_log), dsk,
      ex, tle, tri, tl, bd)
    return y, sfin.reshape(s0.shape)


def _swa_mask(bq, first_block_has_no_past):
    rows = 4 * bq
    tq = (np.arange(rows) % bq)[:, None]
    s = (np.arange(4 * SWA_WINDOW) % (2 * SWA_WINDOW))[None, :]
    ok = (s > tq) & (s <= tq + SWA_WINDOW)
    if first_block_has_no_past:
        ok = ok & (s >= SWA_WINDOW)
    return np.where(ok, 0.0, NEG_BIG).astype(np.float32)


def _block_diag_pair(col, rolled, odd):
    lane = lax.broadcasted_iota(jnp.int32, col.shape, 1)
    lo = lane < SWA_HEAD_DIM
    if odd:
        top = jnp.where(lo, rolled, 0.0)
        bot = jnp.where(lo, 0.0, col)
    else:
        top = jnp.where(lo, col, 0.0)
        bot = jnp.where(lo, 0.0, rolled)
    return jnp.concatenate([top, bot], axis=0)


def _swa_kernel(sink_ref, x_ref, kprev_ref, vprev_ref, mask_ref, ones_ref, o_ref, *, bq, items):
    w = SWA_WINDOW
    n = pl.program_id(1)
    amask = mask_ref[jnp.where(n == 0, 0, 1)]
    npair = (SWA_Q_HEADS // SWA_KV_HEADS) // 2
    lane_lo = lax.broadcasted_iota(jnp.int32, (npair * bq, LANE), 1) < SWA_HEAD_DIM
    sinks = []
    for j in range(SWA_KV_HEADS):
        sinks.append([jnp.concatenate(
            [jnp.full((bq, 1), sink_ref[j * 2 * npair + 2 * p + half], F32) for p in range(npair)], axis=0)
            for half in range(2)])

    for i in range(items):
        x = x_ref[i]
        fill = [] if bq == w else [jnp.zeros((w - bq, SWA_KV_DIM), F32)]
        kall = jnp.concatenate([kprev_ref[i], x[:, D_INNER:D_INNER + SWA_KV_DIM]] + fill, axis=0)
        vall = jnp.concatenate([vprev_ref[i], x[:, D_INNER + SWA_KV_DIM:]] + fill, axis=0)
        for j in range(SWA_KV_HEADS):
            cs = slice((j // 2) * LANE, (j // 2 + 1) * LANE)
            kcol, vcol = kall[:, cs], vall[:, cs]
            k2 = _block_diag_pair(kcol, pltpu.roll(kcol, SWA_HEAD_DIM, axis=1), j % 2)
            v2 = _block_diag_pair(vcol, pltpu.roll(vcol, SWA_HEAD_DIM, axis=1), j % 2)
            v2e = jnp.concatenate([v2.astype(BF16), ones_ref[...]], axis=1)
            qbase = j * npair * LANE
            qs = jnp.concatenate([x[:, qbase + p * LANE:qbase + (p + 1) * LANE] for p in range(npair)],
                                 axis=0) * (SWA_HEAD_DIM ** -0.5)
            sc = _dot_nt(qs, k2) + amask
            pes, ms = [], []
            for half in range(2):
                sh = sc[:, half * 2 * w:(half + 1) * 2 * w]
                m = jnp.maximum(jnp.max(sh, axis=-1, keepdims=True), sinks[j][half])
                pes.append(jnp.exp(sh - m).astype(BF16))
                ms.append(m)
            o = jnp.dot(jnp.concatenate(pes, axis=1), v2e, preferred_element_type=F32)
            esink = jnp.exp(jnp.where(lane_lo, sinks[j][0] - ms[0], sinks[j][1] - ms[1]))
            res = o[:, 0:LANE] / (o[:, LANE:] + esink)
            for p in range(npair):
                o_ref[i, :, qbase + p * LANE:qbase + (p + 1) * LANE] = res[p * bq:(p + 1) * bq].astype(o_ref.dtype)


def _swa_attn(rest3, kprev, vprev, sinks, has_past):
    b, l, nr = rest3.shape
    w = SWA_WINDOW
    bq = math.gcd(l, w)
    nb = l // bq
    assert nb == 1 or bq == w
    m_later = _swa_mask(bq, False)
    m_first = m_later if has_past else _swa_mask(bq, True)
    masks = jnp.asarray(np.stack([m_first, m_later]))
    ones2 = jnp.asarray((np.arange(4 * w)[:, None] // (2 * w)) == (np.arange(LANE)[None, :] // SWA_HEAD_DIM), BF16)
    if nb == 1:
        items = math.gcd(b, 8)
        kspec = pl.BlockSpec((items, w, SWA_KV_DIM), lambda i, n: (i, 0, 0))
        vspec = kspec
        kin, vin = kprev, vprev
    else:
        assert not has_past
        items = 1
        kcol = D_INNER // SWA_KV_DIM
        kspec = pl.BlockSpec((1, w, SWA_KV_DIM), lambda i, n: (i, jnp.maximum(n - 1, 0), kcol))
        vspec = pl.BlockSpec((1, w, SWA_KV_DIM), lambda i, n: (i, jnp.maximum(n - 1, 0), kcol + 1))
        kin, vin = rest3, rest3
    return pl.pallas_call(
        functools.partial(_swa_kernel, bq=bq, items=items),
        grid=(b // items, nb),
        in_specs=[
            pl.BlockSpec(memory_space=pltpu.SMEM),
            pl.BlockSpec((items, bq, nr), lambda i, n: (i, n, 0)),
            kspec,
            vspec,
            pl.BlockSpec(masks.shape, lambda i, n: (0, 0, 0)),
            pl.BlockSpec(ones2.shape, lambda i, n: (0, 0)),
        ],
        out_specs=pl.BlockSpec((items, bq, D_INNER), lambda i, n: (i, n, 0)),
        out_shape=jax.ShapeDtypeStruct((b, l, D_INNER), _mixer_out_dtype(bq)),
        compiler_params=_cparams(("parallel", "parallel")),
        name="swa_attn",
    )(sinks, rest3, kin, vin, masks, ones2)


def _pad_cols(w, n):
    return jnp.concatenate([w, jnp.zeros((w.shape[0], n - w.shape[1]), w.dtype)], axis=1)


def _prep_weights(p):
    kd, di = GLA_KEY_DIM, D_INNER
    out = {}
    for i in (0, 3):
        w = p[f"l{i}_w_in"]
        q, k, v, r, lr = (w[:, 0:kd], w[:, kd:2 * kd], w[:, 2 * kd:2 * kd + di],
                          w[:, 2 * kd + di:2 * kd + 2 * di], w[:, 2 * kd + 2 * di:])
        out[f"l{i}_wg"] = r.astype(BF16)
        out[f"l{i}_wr"] = jnp.concatenate([v, q, k, _pad_cols(lr, LANE)], axis=1).astype(BF16)
    w = p["l1_w_in"]
    out["l1_wg"] = w[:, 0:di].astype(BF16)
    out["l1_wr"] = _pad_cols(w[:, di:], SSD_CONV_DIM + LANE).astype(BF16)
    w = p["l2_w_in"]
    out["l2_wg"] = w[:, di + 2 * SWA_KV_DIM:].astype(BF16)
    out["l2_wr"] = w[:, 0:di + 2 * SWA_KV_DIM].astype(BF16)
    for i in range(4):
        out[f"l{i}_wo"] = p[f"l{i}_w_out"].astype(BF16)
    return out


def _trunk(x, st, p, wts, has_past):
    b, l, d = x.shape
    x2 = x.reshape(b * l, d)
    ones_inner = jnp.ones((D_INNER,), F32)
    new = {}

    def gla(i, x2, s0):
        gate, rest = _norm_proj(x2, p[f"l{i}_norm"], wts[f"l{i}_wg"], wts[f"l{i}_wr"])
        o, s_fin = _gla_scan(rest.reshape(b, l, -1), s0, p[f"l{i}_w_gk2"], p[f"l{i}_b_gk"])
        hn = jnp.tile(p[f"l{i}_head_norm"], GLA_HEADS)
        fin = i == 3
        x2 = _out_proj("gla", o.reshape(b * l, D_INNER), gate, hn, wts[f"l{i}_wo"], x2,
                       p["final_norm"], fin)
        return x2, s_fin

    x2, new["gla0"] = gla(0, x2, st["gla0"])

    gate, rest = _norm_proj(x2, p["l1_norm"], wts["l1_wg"], wts["l1_wr"])
    rest3 = rest.reshape(b, l, -1)
    y, new["ssm"] = _ssd_scan(rest3, st["ssm"], st["conv"], p["l1_conv_w"], p["l1_conv_b"],
                              p["l1_dt_bias"], p["l1_a_log"], p["l1_d_skip"])
    ext_tail = jnp.concatenate([st["conv"], rest3[:, max(l - (SSD_CONV - 1), 0):, 0:SSD_CONV_DIM]], axis=1)
    new["conv"] = ext_tail[:, -(SSD_CONV - 1):]
    x2 = _out_proj("ssd", y.reshape(b * l, D_INNER), gate, p["l1_gate_norm"], wts["l1_wo"], x2,
                   p["final_norm"], False)

    gate, rest = _norm_proj(x2, p["l2_norm"], wts["l2_wg"], wts["l2_wr"])
    rest3 = rest.reshape(b, l, -1)
    kprev = st["swa_k"].reshape(b, SWA_WINDOW, SWA_KV_DIM)
    vprev = st["swa_v"].reshape(b, SWA_WINDOW, SWA_KV_DIM)
    o = _swa_attn(rest3, kprev, vprev, p["l2_sinks"], has_past)
    tail = max(l - SWA_WINDOW, 0)
    k_new = rest3[:, tail:, D_INNER:D_INNER + SWA_KV_DIM]
    v_new = rest3[:, tail:, D_INNER + SWA_KV_DIM:]
    new["swa_k"] = jnp.concatenate([kprev, k_new], axis=1)[:, -SWA_WINDOW:].reshape(st["swa_k"].shape)
    new["swa_v"] = jnp.concatenate([vprev, v_new], axis=1)[:, -SWA_WINDOW:].reshape(st["swa_v"].shape)
    x2 = _out_proj("swa", o.reshape(b * l, D_INNER), gate, ones_inner, wts["l2_wo"], x2,
                   p["final_norm"], False)

    x2, new["gla3"] = gla(3, x2, st["gla3"])
    return x2.reshape(b, l, d), new


def kernel(x_prompt, x_sample, state_gla_0, state_ssm_1, state_conv_1, cache_swa_k_2, cache_swa_v_2, state_gla_3, l0_norm, l0_w_in, l0_w_gk2, l0_b_gk, l0_head_norm, l0_w_out, l1_norm, l1_w_in, l1_conv_w, l1_conv_b, l1_dt_bias, l1_a_log, l1_d_skip, l1_gate_norm, l1_w_out, l2_norm, l2_w_in, l2_sinks, l2_w_out, l3_norm, l3_w_in, l3_w_gk2, l3_b_gk, l3_head_norm, l3_w_out, final_norm):
    p = dict(l0_norm=l0_norm, l0_w_in=l0_w_in, l0_w_gk2=l0_w_gk2, l0_b_gk=l0_b_gk,
             l0_head_norm=l0_head_norm, l0_w_out=l0_w_out,
             l1_norm=l1_norm, l1_w_in=l1_w_in, l1_conv_w=l1_conv_w, l1_conv_b=l1_conv_b,
             l1_dt_bias=l1_dt_bias, l1_a_log=l1_a_log, l1_d_skip=l1_d_skip,
             l1_gate_norm=l1_gate_norm, l1_w_out=l1_w_out,
             l2_norm=l2_norm, l2_w_in=l2_w_in, l2_sinks=l2_sinks, l2_w_out=l2_w_out,
             l3_norm=l3_norm, l3_w_in=l3_w_in, l3_w_gk2=l3_w_gk2, l3_b_gk=l3_b_gk,
             l3_head_norm=l3_head_norm, l3_w_out=l3_w_out, final_norm=final_norm)
    wts = _prep_weights(p)

    bp = x_prompt.shape[0]
    z = lambda a: jnp.zeros((bp,) + a.shape[1:], a.dtype)
    st_p = dict(gla0=z(state_gla_0), ssm=z(state_ssm_1), conv=z(state_conv_1),
                swa_k=z(cache_swa_k_2), swa_v=z(cache_swa_v_2), gla3=z(state_gla_3))
    st_s = dict(gla0=state_gla_0, ssm=state_ssm_1, conv=state_conv_1,
                swa_k=cache_swa_k_2, swa_v=cache_swa_v_2, gla3=state_gla_3)
    y_p, n_p = _trunk(x_prompt, st_p, p, wts, False)
    y_s, n_s = _trunk(x_sample, st_s, p, wts, True)
    return (y_p, y_s,
            n_p["gla0"], n_s["gla0"],
            n_p["ssm"], n_s["ssm"],
            n_p["conv"], n_s["conv"],
            n_p["swa_k"], n_s["swa_k"],
            n_p["swa_v"], n_s["swa_v"],
            n_p["gla3"], n_s["gla3"])
```

```python
import functools
import math

import numpy as np
import jax
import jax.numpy as jnp
from jax import lax
from jax.experimental import pallas as pl
from jax.experimental.pallas import tpu as pltpu

F32 = jnp.float32
BF16 = jnp.bfloat16

NORM_EPS = 1e-6
D_MODEL = 1024
D_INNER = 2048
CHUNK = 64
LANE = 128
BF16_SUBLANES = 16
NEG_BIG = -1e30

GLA_HEADS = 4
GLA_HEAD_K = 128
GLA_HEAD_V = 512
GLA_KEY_DIM = 512
GLA_GATE_RANK = 16
GLA_GATE_NORMALIZER = 16.0

SSD_GROUPS = 4
SSD_HEADS = 32
SSD_HEAD_DIM = 64
SSD_STATE = 128
SSD_CONV = 4
SSD_CONV_DIM = 3072
SSD_GROUP_W = D_INNER // SSD_GROUPS

SWA_WINDOW = 128
SWA_KV_HEADS = 4
SWA_HEAD_DIM = 64
SWA_Q_HEADS = 32
SWA_KV_DIM = 256

VMEM_LIMIT = 52 * 1024 * 1024


def _cparams(sem):
    return pltpu.CompilerParams(dimension_semantics=sem, vmem_limit_bytes=VMEM_LIMIT)


def _dot(a, b):
    return jnp.dot(a.astype(BF16), b.astype(BF16), preferred_element_type=F32)


def _dot_nt(a, b):
    return lax.dot_general(a.astype(BF16), b.astype(BF16), (((1,), (1,)), ((), ())),
                           preferred_element_type=F32)


def _split3(x):
    hi = x.astype(BF16)
    r1 = x - hi.astype(F32)
    mid = r1.astype(BF16)
    lo = (r1 - mid.astype(F32)).astype(BF16)
    return hi, mid, lo


def _sigmoid(x):
    return 1.0 / (1.0 + jnp.exp(-x))


def _softplus(x):
    return jnp.maximum(x, 0.0) + jnp.log1p(jnp.exp(-jnp.abs(x)))


def _mixer_out_dtype(block_rows):
    return BF16 if block_rows % BF16_SUBLANES == 0 else F32


def _norm_proj_kernel(x_ref, nw_ref, wg_ref, wr_ref, gate_ref, rest_ref):
    x = x_ref[...]
    ms = jnp.mean(x * x, axis=-1, keepdims=True)
    h = (x * lax.rsqrt(ms + NORM_EPS) * nw_ref[...]).astype(BF16)
    gate_ref[...] = jnp.dot(h, wg_ref[...], preferred_element_type=F32).astype(gate_ref.dtype)
    rest_ref[...] = jnp.dot(h, wr_ref[...], preferred_element_type=F32)


def _norm_proj(x2, nw, wg, wr, tm=256):
    t, d = x2.shape
    tm = min(tm, t)
    ng, nr = wg.shape[1], wr.shape[1]
    return pl.pallas_call(
        _norm_proj_kernel,
        grid=(t // tm,),
        in_specs=[
            pl.BlockSpec((tm, d), lambda i: (i, 0)),
            pl.BlockSpec((1, d), lambda i: (0, 0)),
            pl.BlockSpec((d, ng), lambda i: (0, 0)),
            pl.BlockSpec((d, nr), lambda i: (0, 0)),
        ],
        out_specs=[
            pl.BlockSpec((tm, ng), lambda i: (i, 0)),
            pl.BlockSpec((tm, nr), lambda i: (i, 0)),
        ],
        out_shape=[jax.ShapeDtypeStruct((t, ng), BF16), jax.ShapeDtypeStruct((t, nr), F32)],
        compiler_params=_cparams(("parallel",)),
        name="norm_proj",
    )(x2, nw.reshape(1, d), wg, wr)


def _seg_rms(y, seg):
    parts = []
    for s in range(y.shape[1] // seg):
        p = y[:, s * seg:(s + 1) * seg]
        ms = jnp.mean(p * p, axis=-1, keepdims=True)
        parts.append(p * lax.rsqrt(ms + NORM_EPS))
    return jnp.concatenate(parts, axis=1)


def _out_proj_kernel(o_ref, gate_ref, nw_ref, w_ref, x_ref, fw_ref, out_ref, *, mode, final):
    o = o_ref[...].astype(F32)
    gt = gate_ref[...].astype(F32)
    act = gt * _sigmoid(gt)
    if mode == "gla":
        y = _seg_rms(o, GLA_HEAD_V) * nw_ref[...] * act
    elif mode == "ssd":
        y = _seg_rms(o * act, SSD_GROUP_W) * nw_ref[...]
    else:
        y = o * act
    out = x_ref[...] + jnp.dot(y.astype(BF16), w_ref[...], preferred_element_type=F32)
    if final:
        ms = jnp.mean(out * out, axis=-1, keepdims=True)
        out = out * lax.rsqrt(ms + NORM_EPS) * fw_ref[...]
    out_ref[...] = out


def _out_proj(mode, o2, gate2, nw, w_out, x2, fw, final, tm=256):
    t, di = o2.shape
    tm = min(tm, t)
    d = x2.shape[1]
    row = lambda i: (i, 0)
    fix = lambda i: (0, 0)
    return pl.pallas_call(
        functools.partial(_out_proj_kernel, mode=mode, final=final),
        grid=(t // tm,),
        in_specs=[
            pl.BlockSpec((tm, di), row),
            pl.BlockSpec((tm, di), row),
            pl.BlockSpec((1, di), fix),
            pl.BlockSpec((di, d), fix),
            pl.BlockSpec((tm, d), row),
            pl.BlockSpec((1, d), fix),
        ],
        out_specs=pl.BlockSpec((tm, d), row),
        out_shape=jax.ShapeDtypeStruct((t, d), F32),
        compiler_params=_cparams(("parallel",)),
        name="out_proj_" + mode,
    )(o2, gate2, nw.reshape(1, di), w_out, x2, fw.reshape(1, d))


GLA_CHUNKS_PER_STEP = 4
GLA_SINGLE_ANCHOR_MAX_DECAY = 60.0


def _gla_levels(seglen):
    return tuple(seglen >> (i + 1) for i in range(int(math.log2(seglen))))


def _gla_consts(seglen):
    c = CHUNK
    levels = _gla_levels(seglen)
    u = np.arange(c)[:, None]
    j = np.arange(c)[None, :]
    same = (u // seglen) == (j // seglen)
    blocks = [same & (j <= u), same & (j > u)]
    lev = np.full((c, c), len(levels) + 1, np.int32)
    for li, h in enumerate(levels):
        b = (u // (2 * h)) * (2 * h) + h - 1
        blocks.append((j > np.minimum(u, b)) & (j <= np.maximum(u, b)))
        sib = (u // (2 * h) == j // (2 * h)) & (u % (2 * h) >= h) & (j % (2 * h) < h)
        lev[sib] = li
    lev[np.eye(c, dtype=bool)] = len(levels)
    m = np.concatenate(blocks, axis=0).astype(np.float32)
    m3 = np.concatenate([m, m, m], axis=1)
    lev = np.concatenate([lev, lev], axis=1)
    return jnp.asarray(m3, BF16), jnp.asarray(lev)


def _gla_scores(qx, kx, e_q, e_k, lev, nt):
    att = None
    for li, (eq, ek) in enumerate(zip(e_q, e_k)):
        a = nt(qx if eq is None else qx * eq, kx if ek is None else kx * ek)
        att = jnp.where(lev == li, a, 0.0 if att is None else att)
    return att


def _nt_pair(qe, ke):
    kb = ke.astype(BF16)
    z = jnp.zeros((kb.shape[0], GLA_HEAD_K), BF16)
    rhs = jnp.concatenate([jnp.concatenate([kb[:, 0:GLA_HEAD_K], z], axis=1),
                           jnp.concatenate([z, kb[:, GLA_HEAD_K:]], axis=1)], axis=0)
    return lax.dot_general(qe.astype(BF16), rhs, (((1,), (1,)), ((), ())), preferred_element_type=F32)


def _gla_prep(xc, wgk, bgk):
    v = xc[:, 0:D_INNER]
    q = xc[:, D_INNER:D_INNER + GLA_KEY_DIM] * (GLA_HEAD_K ** -0.5)
    k = xc[:, D_INNER + GLA_KEY_DIM:D_INNER + 2 * GLA_KEY_DIM]
    lr = xc[:, D_INNER + 2 * GLA_KEY_DIM:]
    z = _dot(lr, wgk) + bgk
    g = (jnp.minimum(z, 0.0) - jnp.log1p(jnp.exp(-jnp.abs(z)))) / GLA_GATE_NORMALIZER
    return v, q, k, g


def _gla_chunk(v, q, k, g, state_io, m3, lev, nl, single_anchor):
    c = CHUNK
    g3 = jnp.concatenate(_split3(g), axis=0)
    if single_anchor:
        rs = jnp.dot(m3[0:2 * c], g3, preferred_element_type=F32)
        cum = rs[0:c]
        e_in, e_out = jnp.exp(cum), jnp.exp(rs[c:2 * c])
        e_q, e_k = [e_in], [jnp.exp(-cum)]
        lev = jnp.where(lev <= nl, 0, 1)
    else:
        e_all = jnp.exp(jnp.dot(m3, g3, preferred_element_type=F32))
        e_in, e_out = e_all[0:c], e_all[c:2 * c]
        e_q = [e_all[(2 + li) * c:(3 + li) * c] for li in range(nl)] + [None]
        e_k = e_q
    qe_in = q * e_in
    ke_out = k * e_out
    lane_lo = lax.broadcasted_iota(jnp.int32, (c, LANE), 1) < c
    outs = []
    for pr in range(GLA_HEADS // 2):
        ls = slice(2 * pr * GLA_HEAD_K, (2 * pr + 2) * GLA_HEAD_K)
        cut = lambda es: [None if e is None else e[:, ls] for e in es]
        att = _gla_scores(q[:, ls], k[:, ls], cut(e_q), cut(e_k), lev, _nt_pair)
        vpair = v[:, 2 * pr * GLA_HEAD_V:(2 * pr + 2) * GLA_HEAD_V]
        vcat = jnp.concatenate([vpair[:, 0:GLA_HEAD_V], vpair[:, GLA_HEAD_V:]], axis=0)
        for half, att_h in enumerate((jnp.where(lane_lo, att, 0.0), jnp.where(lane_lo, 0.0, att))):
            h = 2 * pr + half
            ks = slice(h * GLA_HEAD_K, (h + 1) * GLA_HEAD_K)
            xt = jnp.concatenate([ke_out[:, ks], e_in[:, ks]], axis=0).T
            o_inter = state_io(h, qe_in[:, ks], xt, v[:, h * GLA_HEAD_V:(h + 1) * GLA_HEAD_V])
            outs.append(_dot(att_h, vcat) + o_inter)
    return jnp.concatenate(outs, axis=1)


def _gla_dispatch(preps, run):
    low = None
    for _, _, _, g in preps:
        tot = jnp.min(jnp.sum(g, axis=0, keepdims=True))
        low = tot if low is None else jnp.minimum(low, tot)
    single_anchor_ok = low >= -GLA_SINGLE_ANCHOR_MAX_DECAY
    pl.when(single_anchor_ok)(functools.partial(run, True))
    pl.when(jnp.logical_not(single_anchor_ok))(functools.partial(run, False))


def _gla_carry_kernel(rest_ref, s0_ref, wgk_ref, bgk_ref, m_ref, lev_ref, o_ref, s_ref, *, nch, nl):
    c = CHUNK

    @pl.when(pl.program_id(1) == 0)
    def _():
        s_ref[...] = s0_ref[...]

    def state_io(h, qe_in_h, xt, v_h):
        s_h = s_ref[0, h]
        s_ref[0, h] = s_h * xt[:, 2 * c - 1:2 * c] + _dot(xt[:, 0:c], v_h)
        return _dot(qe_in_h, s_h)

    wgk, bgk = wgk_ref[...], bgk_ref[...]
    preps = [_gla_prep(rest_ref[0, ci * c:(ci + 1) * c, :], wgk, bgk) for ci in range(nch)]

    def run(single_anchor):
        for ci, (v, q, k, g) in enumerate(preps):
            o = _gla_chunk(v, q, k, g, state_io, m_ref[...], lev_ref[...], nl, single_anchor)
            o_ref[0, ci * c:(ci + 1) * c, :] = o.astype(o_ref.dtype)

    _gla_dispatch(preps, run)


def _gla_packed_kernel(rest_ref, s0_ref, wgk_ref, bgk_ref, m_ref, lev_ref, o_ref, s_ref, *, seglen, nl):
    c = CHUNK
    nseg = c // seglen
    seg_of_lane = lax.broadcasted_iota(jnp.int32, (GLA_HEAD_K, c), 1) // seglen

    def state_io(h, qe_in_h, xt, v_h):
        ke_t = xt[:, 0:c]
        outs = []
        for sg in range(nseg):
            s_h = s0_ref[sg, h]
            outs.append(_dot(qe_in_h[sg * seglen:(sg + 1) * seglen], s_h))
            end = c + (sg + 1) * seglen - 1
            s_ref[sg, h] = s_h * xt[:, end:end + 1] + _dot(jnp.where(seg_of_lane == sg, ke_t, 0.0), v_h)
        return jnp.concatenate(outs, axis=0)

    preps = [_gla_prep(rest_ref[...], wgk_ref[...], bgk_ref[...])]

    def run(single_anchor):
        v, q, k, g = preps[0]
        o = _gla_chunk(v, q, k, g, state_io, m_ref[...], lev_ref[...], nl, single_anchor)
        o_ref[...] = o.astype(o_ref.dtype)

    _gla_dispatch(preps, run)


def _gla_scan(rest3, s0, w_gk2, b_gk):
    b, l, nr = rest3.shape
    c = CHUNK
    packed = l < c
    seglen = l if packed else c
    assert c % seglen == 0 and seglen % 8 == 0 and (l % c == 0 or packed)
    nl = len(_gla_levels(seglen))
    m01, lev = _gla_consts(seglen)
    wgk = jnp.zeros((LANE, GLA_KEY_DIM), F32).at[:GLA_GATE_RANK].set(w_gk2).astype(BF16)
    consts = (wgk, b_gk.reshape(1, GLA_KEY_DIM), m01, lev)
    if packed:
        nseq = c // seglen
        assert b % nseq == 0
        sblk = (nseq, GLA_HEADS, GLA_HEAD_K, GLA_HEAD_V)
        o, s_fin = pl.pallas_call(
            functools.partial(_gla_packed_kernel, seglen=seglen, nl=nl),
            grid=(b // nseq,),
            in_specs=[pl.BlockSpec((c, nr), lambda i: (i, 0)),
                      pl.BlockSpec(sblk, lambda i: (i, 0, 0, 0))]
                     + [pl.BlockSpec(a.shape, lambda i: (0, 0)) for a in consts],
            out_specs=[pl.BlockSpec((c, D_INNER), lambda i: (i, 0)),
                       pl.BlockSpec(sblk, lambda i: (i, 0, 0, 0))],
            out_shape=[jax.ShapeDtypeStruct((b * l, D_INNER), BF16), jax.ShapeDtypeStruct(s0.shape, F32)],
            compiler_params=_cparams(("parallel",)),
            name="gla_scan_packed",
        )(rest3.reshape(b * l, nr), s0, *consts)
        return o.reshape(b, l, D_INNER), s_fin
    nch = math.gcd(l // c, GLA_CHUNKS_PER_STEP)
    sblk = (1, GLA_HEADS, GLA_HEAD_K, GLA_HEAD_V)
    return pl.pallas_call(
        functools.partial(_gla_carry_kernel, nch=nch, nl=nl),
        grid=(b, l // (c * nch)),
        in_specs=[pl.BlockSpec((1, c * nch, nr), lambda i, t: (i, t, 0)),
                  pl.BlockSpec(sblk, lambda i, t: (i, 0, 0, 0))]
                 + [pl.BlockSpec(a.shape, lambda i, t: (0, 0)) for a in consts],
        out_specs=[pl.BlockSpec((1, c * nch, D_INNER), lambda i, t: (i, t, 0)),
                   pl.BlockSpec(sblk, lambda i, t: (i, 0, 0, 0))],
        out_shape=[jax.ShapeDtypeStruct((b, l, D_INNER), BF16), jax.ShapeDtypeStruct(s0.shape, F32)],
        compiler_params=_cparams(("parallel", "arbitrary")),
        name="gla_scan",
    )(rest3, s0, *consts)


SSD_CHUNKS_PER_STEP = 2


def _ssd2_consts(seglen):
    c = CHUNK
    u = np.arange(c)
    same = (u[:, None] // seglen) == (u[None, :] // seglen)
    tl = same & (u[None, :] <= u[:, None])
    m = np.concatenate([tl, same], axis=0).astype(np.float32)
    m3 = np.concatenate([m, m, m], axis=1)
    expand = (np.arange(LANE)[:, None] == (np.arange(D_INNER) // SSD_HEAD_DIM)[None, :]).astype(np.float32)
    ex3 = np.concatenate([expand] * 3, axis=0)
    s_of = np.arange(LANE) % c
    tri = np.where(same[:, s_of] & (s_of[None, :] <= u[:, None]), 0.0, NEG_BIG).astype(np.float32)
    bd = ((np.arange(2 * c)[:, None] // c) == (np.arange(LANE)[None, :] // SSD_HEAD_DIM)).astype(np.float32)
    return jnp.asarray(m3, BF16), jnp.asarray(ex3, BF16), jnp.asarray(tri), jnp.asarray(bd)


def _ssd2_conv(ext_ref, rows, cw_ref, cb_ref):
    e = ext_ref[0:8 + rows, :]
    acc = cw_ref[SSD_CONV - 1:SSD_CONV, :] * e[8:]
    for k in range(1, SSD_CONV):
        acc = acc + cw_ref[SSD_CONV - 1 - k:SSD_CONV - k, :] * pltpu.roll(e, k, axis=0)[8:]
    return cb_ref[...] + acc


def _ssd2_chunk(raw, conv, dtb, a_neg, dsk, m3, ex3, tri, bd, state_io, seglen):
    c = CHUNK
    xbc = conv * _sigmoid(conv)
    xs = xbc[:, 0:D_INNER]
    bm = xbc[:, D_INNER:D_INNER + SSD_GROUPS * SSD_STATE]
    cm = xbc[:, D_INNER + SSD_GROUPS * SSD_STATE:]
    dt = _softplus(raw[:, SSD_CONV_DIM:] + dtb)
    la = dt * a_neg
    rs = jnp.dot(m3, jnp.concatenate(_split3(la), axis=0), preferred_element_type=F32)
    cum, tot = rs[0:c], rs[c:2 * c]
    ecum = jnp.exp(cum)
    stack = jnp.concatenate([dt, dt * jnp.exp(tot - cum), cum, ecum], axis=0)
    rep = jnp.dot(jnp.concatenate(_split3(stack), axis=1), ex3, preferred_element_type=F32)
    dt_rep, dtw_rep, cum_rep, ecum_rep = (rep[i * c:(i + 1) * c] for i in range(4))
    u = xs * dt_rep
    uw = xs * dtw_rep
    at = jnp.concatenate([cum, cum], axis=0).T
    lane_lo = lax.broadcasted_iota(jnp.int32, (1, LANE), 1) < c
    ys = []
    for g in range(SSD_GROUPS):
        gs = slice(g * SSD_GROUP_W, (g + 1) * SSD_GROUP_W)
        ns = slice(g * SSD_STATE, (g + 1) * SSD_STATE)
        cg, bg = cm[:, ns], bm[:, ns]
        cb_rep = _dot_nt(cg, jnp.concatenate([bg] * (SSD_GROUP_W // c), axis=0))
        parts = []
        for p in range(SSD_GROUP_W // LANE):
            col = g * (SSD_GROUP_W // LANE) + p
            ps = slice(col * LANE, (col + 1) * LANE)
            cum_s = jnp.where(lane_lo, at[2 * col:2 * col + 1, :], at[2 * col + 1:2 * col + 2, :])
            w = jnp.exp(cum_rep[:, ps] - cum_s + tri) * cb_rep[:, p * LANE:(p + 1) * LANE]
            up = u[:, ps]
            parts.append(_dot(w, jnp.concatenate([up, up], axis=0) * bd))
        y_inter = state_io(g, cg, bg, uw[:, gs], ecum_rep[:, gs])
        ys.append(jnp.concatenate(parts, axis=1) + y_inter + xs[:, gs] * dsk[:, gs])
    return jnp.concatenate(ys, axis=1)


def _ssd2_carry_kernel(rest_ref, s0_ref, conv0_ref, cw_ref, cb_ref, dtb_ref, alog_ref, dsk_ref,
                       m3_ref, ex3_ref, tri_ref, bd_ref, y_ref, sfin_ref, ext_ref, st_ref, *, nch):
    c = CHUNK
    t = pl.program_id(1)

    @pl.when(t == 0)
    def _():
        ext_ref[0:8, :] = conv0_ref[0]
        for g in range(SSD_GROUPS):
            st_ref[g] = s0_ref[0, g].T

    a_neg = -jnp.exp(alog_ref[...])

    def state_io(g, cg, bg, uw_g, ecum_g):
        s_g = st_ref[g]
        y_inter = _dot(cg, s_g) * ecum_g
        bg_t = jnp.concatenate([bg, jnp.zeros_like(bg)], axis=0).T[:, 0:c]
        st_ref[g] = s_g * ecum_g[c - 1:c, :] + _dot(bg_t, uw_g)
        return y_inter

    for ci in range(nch):
        raw = rest_ref[0, ci * c:(ci + 1) * c, :]
        ext_ref[8:8 + c, :] = raw[:, 0:SSD_CONV_DIM]
        conv = _ssd2_conv(ext_ref, c, cw_ref, cb_ref)
        ext_ref[0:8, :] = ext_ref[c:c + 8, :]
        y = _ssd2_chunk(raw, conv, dtb_ref[...], a_neg, dsk_ref[...], m3_ref[...], ex3_ref[...],
                        tri_ref[...], bd_ref[...], state_io, c)
        y_ref[0, ci * c:(ci + 1) * c, :] = y.astype(y_ref.dtype)

    @pl.when(t == pl.num_programs(1) - 1)
    def _():
        for g in range(SSD_GROUPS):
            sfin_ref[0, g] = st_ref[g].T


def _ssd2_packed_kernel(rest_ref, s0_ref, conv0_ref, cw_ref, cb_ref, dtb_ref, alog_ref, dsk_ref,
                        m3_ref, ex3_ref, tri_ref, bd_ref, y_ref, sfin_ref, ext_ref, *, seglen):
    c = CHUNK
    nseg = c // seglen
    ext_ref[0:8, :] = jnp.zeros((8, SSD_CONV_DIM), F32)
    for sg in range(nseg):
        ext_ref[8 + 16 * sg:16 + 16 * sg, :] = conv0_ref[sg]
        ext_ref[16 + 16 * sg:24 + 16 * sg, :] = rest_ref[sg * seglen:(sg + 1) * seglen, 0:SSD_CONV_DIM]
    conv2 = _ssd2_conv(ext_ref, 2 * c, cw_ref, cb_ref)
    conv = jnp.concatenate([conv2[16 * sg + 8:16 * sg + 16] for sg in range(nseg)], axis=0)
    a_neg = -jnp.exp(alog_ref[...])
    seg_of_row = lax.broadcasted_iota(jnp.int32, (c, SSD_STATE), 0) // seglen

    def state_io(g, cg, bg, uw_g, ecum_g):
        ends = jnp.concatenate([ecum_g[(sg + 1) * seglen - 1:(sg + 1) * seglen, :] for sg in range(nseg)], axis=0)
        fill = jnp.zeros((LANE - c - nseg, SSD_GROUP_W), F32)
        tg = jnp.concatenate([uw_g, ends, fill], axis=0).T
        uw_t = tg[:, 0:c]
        y_parts = []
        for sg in range(nseg):
            s_sg = s0_ref[sg, g]
            y_parts.append(_dot_nt(cg[sg * seglen:(sg + 1) * seglen], s_sg))
            b_sg = jnp.where(seg_of_row == sg, bg, 0.0)
            sfin_ref[sg, g] = s_sg * tg[:, c + sg:c + sg + 1] + _dot(uw_t, b_sg)
        return jnp.concatenate(y_parts, axis=0) * ecum_g

    y = _ssd2_chunk(rest_ref[...], conv, dtb_ref[...], a_neg, dsk_ref[...], m3_ref[...], ex3_ref[...],
                    tri_ref[...], bd_ref[...], state_io, seglen)
    y_ref[...] = y.astype(y_ref.dtype)


def _ssd2_scan(rest3, s0, conv0, conv_w, conv_b, dt_bias, a_log, d_skip):
    b, l, nr = rest3.shape
    c = CHUNK
    packed = l < c
    seglen = l if packed else c
    assert c % seglen == 0 and seglen % 8 == 0 and (l % c == 0 or packed)
    m3, ex3, tri, bd = _ssd2_consts(seglen)
    s0g = s0.reshape(b, SSD_GROUPS, SSD_GROUP_W, SSD_STATE)
    conv0p = jnp.concatenate([jnp.zeros((b, 8 - (SSD_CONV - 1), SSD_CONV_DIM), F32), conv0], axis=1)
    pad = lambda a: jnp.zeros((1, LANE), F32).at[0, :SSD_HEADS].set(a)
    dsk = jnp.repeat(d_skip, SSD_HEAD_DIM).reshape(1, D_INNER)
    consts = (conv_w, conv_b.reshape(1, -1), pad(dt_bias), pad(a_log), dsk, m3, ex3, tri, bd)
    if packed:
        nseq = c // seglen
        assert b % nseq == 0
        fix = lambda i: (0, 0)
        const_specs = [pl.BlockSpec(a.shape, fix) for a in consts]
        sblk = (nseq, SSD_GROUPS, SSD_GROUP_W, SSD_STATE)
        y, sfin = pl.pallas_call(
            functools.partial(_ssd2_packed_kernel, seglen=seglen),
            grid=(b // nseq,),
            in_specs=[pl.BlockSpec((c, nr), lambda i: (i, 0)),
                      pl.BlockSpec(sblk, lambda i: (i, 0, 0, 0)),
                      pl.BlockSpec((nseq, 8, SSD_CONV_DIM), lambda i: (i, 0, 0))] + const_specs,
            out_specs=[pl.BlockSpec((c, D_INNER), lambda i: (i, 0)),
                       pl.BlockSpec(sblk, lambda i: (i, 0, 0, 0))],
            out_shape=[jax.ShapeDtypeStruct((b * l, D_INNER), BF16), jax.ShapeDtypeStruct(s0g.shape, F32)],
            scratch_shapes=[pltpu.VMEM((2 * c + 8, SSD_CONV_DIM), F32)],
            compiler_params=_cparams(("parallel",)),
            name="ssd_scan_packed",
        )(rest3.reshape(b * l, nr), s0g, conv0p, *consts)
        return y.reshape(b, l, D_INNER), sfin.reshape(s0.shape)
    nch = math.gcd(l // c, SSD_CHUNKS_PER_STEP)
    fix2 = lambda i, t: (0, 0)
    const_specs = [pl.BlockSpec(a.shape, fix2) for a in consts]
    sblk = (1, SSD_GROUPS, SSD_GROUP_W, SSD_STATE)
    y, sfin = pl.pallas_call(
        functools.partial(_ssd2_carry_kernel, nch=nch),
        grid=(b, l // (c * nch)),
        in_specs=[pl.BlockSpec((1, c * nch, nr), lambda i, t: (i, t, 0)),
                  pl.BlockSpec(sblk, lambda i, t: (i, 0, 0, 0)),
                  pl.BlockSpec((1, 8, SSD_CONV_DIM), lambda i, t: (i, 0, 0))] + const_specs,
        out_specs=[pl.BlockSpec((1, c * nch, D_INNER), lambda i, t: (i, t, 0)),
                   pl.BlockSpec(sblk, lambda i, t: (i, 0, 0, 0))],
        out_shape=[jax.ShapeDtypeStruct((b, l, D_INNER), BF16), jax.ShapeDtypeStruct(s0g.shape, F32)],
        scratch_shapes=[pltpu.VMEM((c + 8, SSD_CONV_DIM), F32),
                        pltpu.VMEM((SSD_GROUPS, SSD_STATE, SSD_GROUP_W), F32)],
        compiler_params=_cparams(("parallel", "arbitrary")),
        name="ssd_scan",
    )(rest3, s0g, conv0p, *consts)
    return y, sfin.reshape(s0.shape)


def _swa_mask(bq, first_block_has_no_past):
    rows = 4 * bq
    tq = (np.arange(rows) % bq)[:, None]
    s = (np.arange(4 * SWA_WINDOW) % (2 * SWA_WINDOW))[None, :]
    ok = (s > tq) & (s <= tq + SWA_WINDOW)
    if first_block_has_no_past:
        ok = ok & (s >= SWA_WINDOW)
    return np.where(ok, 0.0, NEG_BIG).astype(np.float32)


def _block_diag_pair(col, rolled, odd):
    lane = lax.broadcasted_iota(jnp.int32, col.shape, 1)
    lo = lane < SWA_HEAD_DIM
    if odd:
        top = jnp.where(lo, rolled, 0.0)
        bot = jnp.where(lo, 0.0, col)
    else:
        top = jnp.where(lo, col, 0.0)
        bot = jnp.where(lo, 0.0, rolled)
    return jnp.concatenate([top, bot], axis=0)


def _swa_kernel(sink_ref, x_ref, kprev_ref, vprev_ref, mask_ref, ones_ref, o_ref, *, bq, items):
    w = SWA_WINDOW
    n = pl.program_id(1)
    amask = mask_ref[jnp.where(n == 0, 0, 1)]
    npair = (SWA_Q_HEADS // SWA_KV_HEADS) // 2
    lane_lo = lax.broadcasted_iota(jnp.int32, (npair * bq, LANE), 1) < SWA_HEAD_DIM
    sinks = []
    for j in range(SWA_KV_HEADS):
        sinks.append([jnp.concatenate(
            [jnp.full((bq, 1), sink_ref[j * 2 * npair + 2 * p + half], F32) for p in range(npair)], axis=0)
            for half in range(2)])

    for i in range(items):
        x = x_ref[i]
        fill = [] if bq == w else [jnp.zeros((w - bq, SWA_KV_DIM), F32)]
        kall = jnp.concatenate([kprev_ref[i], x[:, D_INNER:D_INNER + SWA_KV_DIM]] + fill, axis=0)
        vall = jnp.concatenate([vprev_ref[i], x[:, D_INNER + SWA_KV_DIM:]] + fill, axis=0)
        for j in range(SWA_KV_HEADS):
            cs = slice((j // 2) * LANE, (j // 2 + 1) * LANE)
            kcol, vcol = kall[:, cs], vall[:, cs]
            k2 = _block_diag_pair(kcol, pltpu.roll(kcol, SWA_HEAD_DIM, axis=1), j % 2)
            v2 = _block_diag_pair(vcol, pltpu.roll(vcol, SWA_HEAD_DIM, axis=1), j % 2)
            v2e = jnp.concatenate([v2.astype(BF16), ones_ref[...]], axis=1)
            qbase = j * npair * LANE
            qs = jnp.concatenate([x[:, qbase + p * LANE:qbase + (p + 1) * LANE] for p in range(npair)],
                                 axis=0) * (SWA_HEAD_DIM ** -0.5)
            sc = _dot_nt(qs, k2) + amask
            pes, ms = [], []
            for half in range(2):
                sh = sc[:, half * 2 * w:(half + 1) * 2 * w]
                m = jnp.maximum(jnp.max(sh, axis=-1, keepdims=True), sinks[j][half])
                pes.append(jnp.exp(sh - m).astype(BF16))
                ms.append(m)
            o = jnp.dot(jnp.concatenate(pes, axis=1), v2e, preferred_element_type=F32)
            esink = jnp.exp(jnp.where(lane_lo, sinks[j][0] - ms[0], sinks[j][1] - ms[1]))
            res = o[:, 0:LANE] / (o[:, LANE:] + esink)
            for p in range(npair):
                o_ref[i, :, qbase + p * LANE:qbase + (p + 1) * LANE] = res[p * bq:(p + 1) * bq].astype(o_ref.dtype)


def _swa_attn(rest3, kprev, vprev, sinks, has_past):
    b, l, nr = rest3.shape
    w = SWA_WINDOW
    bq = math.gcd(l, w)
    nb = l // bq
    assert nb == 1 or bq == w
    m_later = _swa_mask(bq, False)
    m_first = m_later if has_past else _swa_mask(bq, True)
    masks = jnp.asarray(np.stack([m_first, m_later]))
    ones2 = jnp.asarray((np.arange(4 * w)[:, None] // (2 * w)) == (np.arange(LANE)[None, :] // SWA_HEAD_DIM), BF16)
    if nb == 1:
        items = math.gcd(b, 8)
        kspec = pl.BlockSpec((items, w, SWA_KV_DIM), lambda i, n: (i, 0, 0))
        vspec = kspec
        kin, vin = kprev, vprev
    else:
        assert not has_past
        items = 1
        kcol = D_INNER // SWA_KV_DIM
        kspec = pl.BlockSpec((1, w, SWA_KV_DIM), lambda i, n: (i, jnp.maximum(n - 1, 0), kcol))
        vspec = pl.BlockSpec((1, w, SWA_KV_DIM), lambda i, n: (i, jnp.maximum(n - 1, 0), kcol + 1))
        kin, vin = rest3, rest3
    return pl.pallas_call(
        functools.partial(_swa_kernel, bq=bq, items=items),
        grid=(b // items, nb),
        in_specs=[
            pl.BlockSpec(memory_space=pltpu.SMEM),
            pl.BlockSpec((items, bq, nr), lambda i, n: (i, n, 0)),
            kspec,
            vspec,
            pl.BlockSpec(masks.shape, lambda i, n: (0, 0, 0)),
            pl.BlockSpec(ones2.shape, lambda i, n: (0, 0)),
        ],
        out_specs=pl.BlockSpec((items, bq, D_INNER), lambda i, n: (i, n, 0)),
        out_shape=jax.ShapeDtypeStruct((b, l, D_INNER), _mixer_out_dtype(bq)),
        compiler_params=_cparams(("parallel", "parallel")),
        name="swa_attn",
    )(sinks, rest3, kin, vin, masks, ones2)


def _pad_cols(w, n):
    return jnp.concatenate([w, jnp.zeros((w.shape[0], n - w.shape[1]), w.dtype)], axis=1)


def _prep_weights(p):
    kd, di = GLA_KEY_DIM, D_INNER
    out = {}
    for i in (0, 3):
        w = p[f"l{i}_w_in"]
        q, k, v, r, lr = (w[:, 0:kd], w[:, kd:2 * kd], w[:, 2 * kd:2 * kd + di],
                          w[:, 2 * kd + di:2 * kd + 2 * di], w[:, 2 * kd + 2 * di:])
        out[f"l{i}_wg"] = r.astype(BF16)
        out[f"l{i}_wr"] = jnp.concatenate([v, q, k, _pad_cols(lr, LANE)], axis=1).astype(BF16)
    w = p["l1_w_in"]
    out["l1_wg"] = w[:, 0:di].astype(BF16)
    out["l1_wr"] = _pad_cols(w[:, di:], SSD_CONV_DIM + LANE).astype(BF16)
    w = p["l2_w_in"]
    out["l2_wg"] = w[:, di + 2 * SWA_KV_DIM:].astype(BF16)
    out["l2_wr"] = w[:, 0:di + 2 * SWA_KV_DIM].astype(BF16)
    for i in range(4):
        out[f"l{i}_wo"] = p[f"l{i}_w_out"].astype(BF16)
    return out


def _trunk(x, st, p, wts, has_past):
    b, l, d = x.shape
    x2 = x.reshape(b * l, d)
    ones_inner = jnp.ones((D_INNER,), F32)
    new = {}

    def gla(i, x2, s0):
        gate, rest = _norm_proj(x2, p[f"l{i}_norm"], wts[f"l{i}_wg"], wts[f"l{i}_wr"])
        o, s_fin = _gla_scan(rest.reshape(b, l, -1), s0, p[f"l{i}_w_gk2"], p[f"l{i}_b_gk"])
        hn = jnp.tile(p[f"l{i}_head_norm"], GLA_HEADS)
        fin = i == 3
        x2 = _out_proj("gla", o.reshape(b * l, D_INNER), gate, hn, wts[f"l{i}_wo"], x2,
                       p["final_norm"], fin)
        return x2, s_fin

    x2, new["gla0"] = gla(0, x2, st["gla0"])

    gate, rest = _norm_proj(x2, p["l1_norm"], wts["l1_wg"], wts["l1_wr"])
    rest3 = rest.reshape(b, l, -1)
    y, new["ssm"] = _ssd2_scan(rest3, st["ssm"], st["conv"], p["l1_conv_w"], p["l1_conv_b"],
                              p["l1_dt_bias"], p["l1_a_log"], p["l1_d_skip"])
    ext_tail = jnp.concatenate([st["conv"], rest3[:, max(l - (SSD_CONV - 1), 0):, 0:SSD_CONV_DIM]], axis=1)
    new["conv"] = ext_tail[:, -(SSD_CONV - 1):]
    x2 = _out_proj("ssd", y.reshape(b * l, D_INNER), gate, p["l1_gate_norm"], wts["l1_wo"], x2,
                   p["final_norm"], False)

    gate, rest = _norm_proj(x2, p["l2_norm"], wts["l2_wg"], wts["l2_wr"])
    rest3 = rest.reshape(b, l, -1)
    kprev = st["swa_k"].reshape(b, SWA_WINDOW, SWA_KV_DIM)
    vprev = st["swa_v"].reshape(b, SWA_WINDOW, SWA_KV_DIM)
    o = _swa_attn(rest3, kprev, vprev, p["l2_sinks"], has_past)
    tail = max(l - SWA_WINDOW, 0)
    k_new = rest3[:, tail:, D_INNER:D_INNER + SWA_KV_DIM]
    v_new = rest3[:, tail:, D_INNER + SWA_KV_DIM:]
    new["swa_k"] = jnp.concatenate([kprev, k_new], axis=1)[:, -SWA_WINDOW:].reshape(st["swa_k"].shape)
    new["swa_v"] = jnp.concatenate([vprev, v_new], axis=1)[:, -SWA_WINDOW:].reshape(st["swa_v"].shape)
    x2 = _out_proj("swa", o.reshape(b * l, D_INNER), gate, ones_inner, wts["l2_wo"], x2,
                   p["final_norm"], False)

    x2, new["gla3"] = gla(3, x2, st["gla3"])
    return x2.reshape(b, l, d), new


def kernel(x_prompt, x_sample, state_gla_0, state_ssm_1, state_conv_1, cache_swa_k_2, cache_swa_v_2, state_gla_3, l0_norm, l0_w_in, l0_w_gk2, l0_b_gk, l0_head_norm, l0_w_out, l1_norm, l1_w_in, l1_conv_w, l1_conv_b, l1_dt_bias, l1_a_log, l1_d_skip, l1_gate_norm, l1_w_out, l2_norm, l2_w_in, l2_sinks, l2_w_out, l3_norm, l3_w_in, l3_w_gk2, l3_b_gk, l3_head_norm, l3_w_out, final_norm):
    p = dict(l0_norm=l0_norm, l0_w_in=l0_w_in, l0_w_gk2=l0_w_gk2, l0_b_gk=l0_b_gk,
             l0_head_norm=l0_head_norm, l0_w_out=l0_w_out,
             l1_norm=l1_norm, l1_w_in=l1_w_in, l1_conv_w=l1_conv_w, l1_conv_b=l1_conv_b,
             l1_dt_bias=l1_dt_bias, l1_a_log=l1_a_log, l1_d_skip=l1_d_skip,
             l1_gate_norm=l1_gate_norm, l1_w_out=l1_w_out,
             l2_norm=l2_norm, l2_w_in=l2_w_in, l2_sinks=l2_sinks, l2_w_out=l2_w_out,
             l3_norm=l3_norm, l3_w_in=l3_w_in, l3_w_gk2=l3_w_gk2, l3_b_gk=l3_b_gk,
             l3_head_norm=l3_head_norm, l3_w_out=l3_w_out, final_norm=final_norm)
    wts = _prep_weights(p)

    bp = x_prompt.shape[0]
    z = lambda a: jnp.zeros((bp,) + a.shape[1:], a.dtype)
    st_p = dict(gla0=z(state_gla_0), ssm=z(state_ssm_1), conv=z(state_conv_1),
                swa_k=z(cache_swa_k_2), swa_v=z(cache_swa_v_2), gla3=z(state_gla_3))
    st_s = dict(gla0=state_gla_0, ssm=state_ssm_1, conv=state_conv_1,
                swa_k=cache_swa_k_2, swa_v=cache_swa_v_2, gla3=state_gla_3)
    y_p, n_p = _trunk(x_prompt, st_p, p, wts, False)
    y_s, n_s = _trunk(x_sample, st_s, p, wts, True)
    return (y_p, y_s,
            n_p["gla0"], n_s["gla0"],
            n_p["ssm"], n_s["ssm"],
            n_p["conv"], n_s["conv"],
            n_p["swa_k"], n_s["swa_k"],
            n_p["swa_v"], n_s["swa_v"],
            n_p["gla3"], n_s["gla3"])
```

```python
import functools
import math

import numpy as np
import jax
import jax.numpy as jnp
from jax import lax
from jax.experimental import pallas as pl
from jax.experimental.pallas import tpu as pltpu

F32 = jnp.float32
BF16 = jnp.bfloat16

NORM_EPS = 1e-6
D_MODEL = 1024
D_INNER = 2048
CHUNK = 64
LANE = 128
BF16_SUBLANES = 16
NEG_BIG = -1e30

GLA_HEADS = 4
GLA_HEAD_K = 128
GLA_HEAD_V = 512
GLA_KEY_DIM = 512
GLA_GATE_RANK = 16
GLA_GATE_NORMALIZER = 16.0

SSD_GROUPS = 4
SSD_HEADS = 32
SSD_HEAD_DIM = 64
SSD_STATE = 128
SSD_CONV = 4
SSD_CONV_DIM = 3072
SSD_GROUP_W = D_INNER // SSD_GROUPS

SWA_WINDOW = 128
SWA_KV_HEADS = 4
SWA_HEAD_DIM = 64
SWA_Q_HEADS = 32
SWA_KV_DIM = 256

VMEM_LIMIT = 52 * 1024 * 1024


def _cparams(sem):
    return pltpu.CompilerParams(dimension_semantics=sem, vmem_limit_bytes=VMEM_LIMIT)


def _dot(a, b):
    return jnp.dot(a.astype(BF16), b.astype(BF16), preferred_element_type=F32)


def _dot_nt(a, b):
    return lax.dot_general(a.astype(BF16), b.astype(BF16), (((1,), (1,)), ((), ())),
                           preferred_element_type=F32)


def _split3(x):
    hi = x.astype(BF16)
    r1 = x - hi.astype(F32)
    mid = r1.astype(BF16)
    lo = (r1 - mid.astype(F32)).astype(BF16)
    return hi, mid, lo


def _sigmoid(x):
    return 1.0 / (1.0 + jnp.exp(-x))


def _softplus(x):
    return jnp.maximum(x, 0.0) + jnp.log1p(jnp.exp(-jnp.abs(x)))


def _mixer_out_dtype(block_rows):
    return BF16 if block_rows % BF16_SUBLANES == 0 else F32


PROJ_ROWS_PER_STEP = 512
PROJ_CAST_ROWS = 256


def _round_up(n, m):
    return -(-n // m) * m


def _norm_proj_kernel(x_ref, nw_ref, w_ref, *refs, layout):
    step = pl.program_id(0)
    aux_piece = layout["aux"]
    if aux_piece is None:
        gate_ref, rest_ref, wg_scr, wr_scr = refs
    else:
        gate_ref, rest_ref, aux_ref, wg_scr, wr_scr, wa_scr = refs
    d = w_ref.shape[0]

    def cast_piece(dst, dst_off, src_off, width):
        wpad = _round_up(width, LANE)
        for r in range(0, d, PROJ_CAST_ROWS):
            piece = w_ref[r:r + PROJ_CAST_ROWS, src_off:src_off + width]
            if wpad > width:
                piece = jnp.concatenate([piece, jnp.zeros((PROJ_CAST_ROWS, wpad - width), F32)], axis=1)
            dst[r:r + PROJ_CAST_ROWS, dst_off:dst_off + wpad] = piece.astype(BF16)
        return wpad

    @pl.when(step == 0)
    def _():
        cast_piece(wg_scr, 0, *layout["gate"])
        off = 0
        for src_off, width in layout["rest"]:
            off += cast_piece(wr_scr, off, src_off, width)
        if aux_piece is not None:
            cast_piece(wa_scr, 0, *aux_piece)

    x = x_ref[...]
    ms = jnp.mean(x * x, axis=-1, keepdims=True)
    h = (x * lax.rsqrt(ms + NORM_EPS) * nw_ref[...]).astype(BF16)
    gate_ref[...] = jnp.dot(h, wg_scr[...], preferred_element_type=F32).astype(gate_ref.dtype)
    rest_ref[...] = jnp.dot(h, wr_scr[...], preferred_element_type=F32).astype(rest_ref.dtype)
    if aux_piece is not None:
        aux_ref[...] = jnp.dot(h, wa_scr[...], preferred_element_type=F32)


def _proj_layout(kind):
    kd, di = GLA_KEY_DIM, D_INNER
    if kind == "gla":
        return dict(gate=(2 * kd + di, di),
                    rest=[(2 * kd, di), (0, kd), (kd, kd), (2 * kd + 2 * di, GLA_GATE_RANK)], aux=None)
    if kind == "ssd":
        return dict(gate=(0, di), rest=[(di, SSD_CONV_DIM)], aux=(di + SSD_CONV_DIM, SSD_HEADS))
    return dict(gate=(di + 2 * SWA_KV_DIM, di), rest=[(0, di + 2 * SWA_KV_DIM)], aux=None)


def _norm_proj(kind, x2, nw, w_in):
    t, d = x2.shape
    tm = min(PROJ_ROWS_PER_STEP, t)
    layout = _proj_layout(kind)
    ng = layout["gate"][1]
    nr = sum(_round_up(w, LANE) for _, w in layout["rest"])
    has_aux = layout["aux"] is not None
    row = lambda i: (i, 0)
    fix = lambda i: (0, 0)
    out_specs = [pl.BlockSpec((tm, ng), row), pl.BlockSpec((tm, nr), row)]
    out_shape = [jax.ShapeDtypeStruct((t, ng), BF16), jax.ShapeDtypeStruct((t, nr), BF16)]
    scratch = [pltpu.VMEM((d, ng), BF16), pltpu.VMEM((d, nr), BF16)]
    if has_aux:
        out_specs.append(pl.BlockSpec((tm, LANE), row))
        out_shape.append(jax.ShapeDtypeStruct((t, LANE), F32))
        scratch.append(pltpu.VMEM((d, LANE), BF16))
    return pl.pallas_call(
        functools.partial(_norm_proj_kernel, layout=layout),
        grid=(t // tm,),
        in_specs=[
            pl.BlockSpec((tm, d), row),
            pl.BlockSpec((1, d), fix),
            pl.BlockSpec(w_in.shape, fix, pipeline_mode=pl.Buffered(1)),
        ],
        out_specs=out_specs,
        out_shape=out_shape,
        scratch_shapes=scratch,
        compiler_params=_cparams(("arbitrary",)),
        name="norm_proj_" + kind,
    )(x2, nw.reshape(1, d), w_in)


def _seg_rms(y, seg):
    parts = []
    for s in range(y.shape[1] // seg):
        p = y[:, s * seg:(s + 1) * seg]
        ms = jnp.mean(p * p, axis=-1, keepdims=True)
        parts.append(p * lax.rsqrt(ms + NORM_EPS))
    return jnp.concatenate(parts, axis=1)


def _out_proj_kernel(o_ref, gate_ref, nw_ref, w_ref, x_ref, fw_ref, out_ref, w_scr, *, mode, final):
    @pl.when(pl.program_id(0) == 0)
    def _():
        for r in range(0, w_ref.shape[0], PROJ_CAST_ROWS):
            w_scr[r:r + PROJ_CAST_ROWS, :] = w_ref[r:r + PROJ_CAST_ROWS, :].astype(BF16)

    o = o_ref[...].astype(F32)
    gt = gate_ref[...].astype(F32)
    act = gt * _sigmoid(gt)
    if mode == "gla":
        y = _seg_rms(o, GLA_HEAD_V) * nw_ref[...] * act
    elif mode == "ssd":
        y = _seg_rms(o * act, SSD_GROUP_W) * nw_ref[...]
    else:
        y = o * act
    out = x_ref[...] + jnp.dot(y.astype(BF16), w_scr[...], preferred_element_type=F32)
    if final:
        ms = jnp.mean(out * out, axis=-1, keepdims=True)
        out = out * lax.rsqrt(ms + NORM_EPS) * fw_ref[...]
    out_ref[...] = out


def _out_proj(mode, o2, gate2, nw, w_out, x2, fw, final):
    t, di = o2.shape
    tm = min(PROJ_ROWS_PER_STEP, t)
    d = x2.shape[1]
    row = lambda i: (i, 0)
    fix = lambda i: (0, 0)
    return pl.pallas_call(
        functools.partial(_out_proj_kernel, mode=mode, final=final),
        grid=(t // tm,),
        in_specs=[
            pl.BlockSpec((tm, di), row),
            pl.BlockSpec((tm, di), row),
            pl.BlockSpec((1, di), fix),
            pl.BlockSpec((di, d), fix),
            pl.BlockSpec((tm, d), row),
            pl.BlockSpec((1, d), fix),
        ],
        out_specs=pl.BlockSpec((tm, d), row),
        out_shape=jax.ShapeDtypeStruct((t, d), F32),
        scratch_shapes=[pltpu.VMEM((di, d), BF16)],
        compiler_params=_cparams(("arbitrary",)),
        name="out_proj_" + mode,
    )(o2, gate2, nw.reshape(1, di), w_out, x2, fw.reshape(1, d))


GLA_CHUNKS_PER_STEP = 4
GLA_SINGLE_ANCHOR_MAX_DECAY = 60.0


def _gla_levels(seglen):
    return tuple(seglen >> (i + 1) for i in range(int(math.log2(seglen))))


def _gla_consts(seglen):
    c = CHUNK
    levels = _gla_levels(seglen)
    u = np.arange(c)[:, None]
    j = np.arange(c)[None, :]
    same = (u // seglen) == (j // seglen)
    blocks = [same & (j <= u), same & (j > u)]
    lev = np.full((c, c), len(levels) + 1, np.int32)
    for li, h in enumerate(levels):
        b = (u // (2 * h)) * (2 * h) + h - 1
        blocks.append((j > np.minimum(u, b)) & (j <= np.maximum(u, b)))
        sib = (u // (2 * h) == j // (2 * h)) & (u % (2 * h) >= h) & (j % (2 * h) < h)
        lev[sib] = li
    lev[np.eye(c, dtype=bool)] = len(levels)
    m = np.concatenate(blocks, axis=0).astype(np.float32)
    m3 = np.concatenate([m, m, m], axis=1)
    lev = np.concatenate([lev, lev], axis=1)
    return jnp.asarray(m3, BF16), jnp.asarray(lev)


def _gla_scores(qx, kx, e_q, e_k, lev, nt):
    att = None
    for li, (eq, ek) in enumerate(zip(e_q, e_k)):
        a = nt(qx if eq is None else qx * eq, kx if ek is None else kx * ek)
        att = jnp.where(lev == li, a, 0.0 if att is None else att)
    return att


def _nt_pair(qe, ke):
    kb = ke.astype(BF16)
    z = jnp.zeros((kb.shape[0], GLA_HEAD_K), BF16)
    rhs = jnp.concatenate([jnp.concatenate([kb[:, 0:GLA_HEAD_K], z], axis=1),
                           jnp.concatenate([z, kb[:, GLA_HEAD_K:]], axis=1)], axis=0)
    return lax.dot_general(qe.astype(BF16), rhs, (((1,), (1,)), ((), ())), preferred_element_type=F32)


def _gla_prep(xc, wgk, bgk):
    v = xc[:, 0:D_INNER]
    q = xc[:, D_INNER:D_INNER + GLA_KEY_DIM].astype(F32) * (GLA_HEAD_K ** -0.5)
    k = xc[:, D_INNER + GLA_KEY_DIM:D_INNER + 2 * GLA_KEY_DIM].astype(F32)
    lr = xc[:, D_INNER + 2 * GLA_KEY_DIM:]
    z = _dot(lr, wgk) + bgk
    g = (jnp.minimum(z, 0.0) - jnp.log1p(jnp.exp(-jnp.abs(z)))) / GLA_GATE_NORMALIZER
    return v, q, k, g


def _gla_chunk(v, q, k, g, state_io, m3, lev, nl, single_anchor):
    c = CHUNK
    g3 = jnp.concatenate(_split3(g), axis=0)
    if single_anchor:
        rs = jnp.dot(m3[0:2 * c], g3, preferred_element_type=F32)
        cum = rs[0:c]
        e_in, e_out = jnp.exp(cum), jnp.exp(rs[c:2 * c])
        e_q, e_k = [e_in], [jnp.exp(-cum)]
        lev = jnp.where(lev <= nl, 0, 1)
    else:
        e_all = jnp.exp(jnp.dot(m3, g3, preferred_element_type=F32))
        e_in, e_out = e_all[0:c], e_all[c:2 * c]
        e_q = [e_all[(2 + li) * c:(3 + li) * c] for li in range(nl)] + [None]
        e_k = e_q
    qe_in = q * e_in
    ke_out = k * e_out
    lane_lo = lax.broadcasted_iota(jnp.int32, (c, LANE), 1) < c
    outs = []
    for pr in range(GLA_HEADS // 2):
        ls = slice(2 * pr * GLA_HEAD_K, (2 * pr + 2) * GLA_HEAD_K)
        cut = lambda es: [None if e is None else e[:, ls] for e in es]
        att = _gla_scores(q[:, ls], k[:, ls], cut(e_q), cut(e_k), lev, _nt_pair)
        vpair = v[:, 2 * pr * GLA_HEAD_V:(2 * pr + 2) * GLA_HEAD_V]
        vcat = jnp.concatenate([vpair[:, 0:GLA_HEAD_V], vpair[:, GLA_HEAD_V:]], axis=0)
        for half, att_h in enumerate((jnp.where(lane_lo, att, 0.0), jnp.where(lane_lo, 0.0, att))):
            h = 2 * pr + half
            ks = slice(h * GLA_HEAD_K, (h + 1) * GLA_HEAD_K)
            xt = jnp.concatenate([ke_out[:, ks], e_in[:, ks]], axis=0).T
            o_inter = state_io(h, qe_in[:, ks], xt, v[:, h * GLA_HEAD_V:(h + 1) * GLA_HEAD_V])
            outs.append(_dot(att_h, vcat) + o_inter)
    return jnp.concatenate(outs, axis=1)


def _gla_dispatch(preps, run):
    low = None
    for _, _, _, g in preps:
        tot = jnp.min(jnp.sum(g, axis=0, keepdims=True))
        low = tot if low is None else jnp.minimum(low, tot)
    single_anchor_ok = low >= -GLA_SINGLE_ANCHOR_MAX_DECAY
    pl.when(single_anchor_ok)(functools.partial(run, True))
    pl.when(jnp.logical_not(single_anchor_ok))(functools.partial(run, False))


def _gla_carry_kernel(rest_ref, s0_ref, wgk_ref, bgk_ref, m_ref, lev_ref, o_ref, s_ref, *, nch, nl):
    c = CHUNK

    @pl.when(pl.program_id(1) == 0)
    def _():
        s_ref[...] = s0_ref[...]

    def state_io(h, qe_in_h, xt, v_h):
        s_h = s_ref[0, h]
        s_ref[0, h] = s_h * xt[:, 2 * c - 1:2 * c] + _dot(xt[:, 0:c], v_h)
        return _dot(qe_in_h, s_h)

    wgk, bgk = wgk_ref[...], bgk_ref[...]
    preps = [_gla_prep(rest_ref[0, ci * c:(ci + 1) * c, :], wgk, bgk) for ci in range(nch)]

    def run(single_anchor):
        for ci, (v, q, k, g) in enumerate(preps):
            o = _gla_chunk(v, q, k, g, state_io, m_ref[...], lev_ref[...], nl, single_anchor)
            o_ref[0, ci * c:(ci + 1) * c, :] = o.astype(o_ref.dtype)

    _gla_dispatch(preps, run)


def _gla_packed_kernel(rest_ref, s0_ref, wgk_ref, bgk_ref, m_ref, lev_ref, o_ref, s_ref, *, seglen, nl):
    c = CHUNK
    nseg = c // seglen
    seg_of_lane = lax.broadcasted_iota(jnp.int32, (GLA_HEAD_K, c), 1) // seglen

    def state_io(h, qe_in_h, xt, v_h):
        ke_t = xt[:, 0:c]
        outs = []
        for sg in range(nseg):
            s_h = s0_ref[sg, h]
            outs.append(_dot(qe_in_h[sg * seglen:(sg + 1) * seglen], s_h))
            end = c + (sg + 1) * seglen - 1
            s_ref[sg, h] = s_h * xt[:, end:end + 1] + _dot(jnp.where(seg_of_lane == sg, ke_t, 0.0), v_h)
        return jnp.concatenate(outs, axis=0)

    preps = [_gla_prep(rest_ref[...], wgk_ref[...], bgk_ref[...])]

    def run(single_anchor):
        v, q, k, g = preps[0]
        o = _gla_chunk(v, q, k, g, state_io, m_ref[...], lev_ref[...], nl, single_anchor)
        o_ref[...] = o.astype(o_ref.dtype)

    _gla_dispatch(preps, run)


def _gla_scan(rest3, s0, w_gk2, b_gk):
    b, l, nr = rest3.shape
    c = CHUNK
    packed = l < c
    seglen = l if packed else c
    assert c % seglen == 0 and seglen % 8 == 0 and (l % c == 0 or packed)
    nl = len(_gla_levels(seglen))
    m01, lev = _gla_consts(seglen)
    wgk = jnp.zeros((LANE, GLA_KEY_DIM), F32).at[:GLA_GATE_RANK].set(w_gk2).astype(BF16)
    consts = (wgk, b_gk.reshape(1, GLA_KEY_DIM), m01, lev)
    if packed:
        nseq = c // seglen
        assert b % nseq == 0
        sblk = (nseq, GLA_HEADS, GLA_HEAD_K, GLA_HEAD_V)
        o, s_fin = pl.pallas_call(
            functools.partial(_gla_packed_kernel, seglen=seglen, nl=nl),
            grid=(b // nseq,),
            in_specs=[pl.BlockSpec((c, nr), lambda i: (i, 0)),
                      pl.BlockSpec(sblk, lambda i: (i, 0, 0, 0))]
                     + [pl.BlockSpec(a.shape, lambda i: (0, 0)) for a in consts],
            out_specs=[pl.BlockSpec((c, D_INNER), lambda i: (i, 0)),
                       pl.BlockSpec(sblk, lambda i: (i, 0, 0, 0))],
            out_shape=[jax.ShapeDtypeStruct((b * l, D_INNER), BF16), jax.ShapeDtypeStruct(s0.shape, F32)],
            compiler_params=_cparams(("parallel",)),
            name="gla_scan_packed",
        )(rest3.reshape(b * l, nr), s0, *consts)
        return o.reshape(b, l, D_INNER), s_fin
    nch = math.gcd(l // c, GLA_CHUNKS_PER_STEP)
    sblk = (1, GLA_HEADS, GLA_HEAD_K, GLA_HEAD_V)
    return pl.pallas_call(
        functools.partial(_gla_carry_kernel, nch=nch, nl=nl),
        grid=(b, l // (c * nch)),
        in_specs=[pl.BlockSpec((1, c * nch, nr), lambda i, t: (i, t, 0)),
                  pl.BlockSpec(sblk, lambda i, t: (i, 0, 0, 0))]
                 + [pl.BlockSpec(a.shape, lambda i, t: (0, 0)) for a in consts],
        out_specs=[pl.BlockSpec((1, c * nch, D_INNER), lambda i, t: (i, t, 0)),
                   pl.BlockSpec(sblk, lambda i, t: (i, 0, 0, 0))],
        out_shape=[jax.ShapeDtypeStruct((b, l, D_INNER), BF16), jax.ShapeDtypeStruct(s0.shape, F32)],
        compiler_params=_cparams(("parallel", "arbitrary")),
        name="gla_scan",
    )(rest3, s0, *consts)


SSD_CHUNKS_PER_STEP = 2


def _ssd2_consts(seglen):
    c = CHUNK
    u = np.arange(c)
    same = (u[:, None] // seglen) == (u[None, :] // seglen)
    tl = same & (u[None, :] <= u[:, None])
    m = np.concatenate([tl, same], axis=0).astype(np.float32)
    m3 = np.concatenate([m, m, m], axis=1)
    expand = (np.arange(LANE)[:, None] == (np.arange(D_INNER) // SSD_HEAD_DIM)[None, :]).astype(np.float32)
    ex3 = np.concatenate([expand] * 3, axis=0)
    s_of = np.arange(LANE) % c
    tri = np.where(same[:, s_of] & (s_of[None, :] <= u[:, None]), 0.0, NEG_BIG).astype(np.float32)
    bd = ((np.arange(2 * c)[:, None] // c) == (np.arange(LANE)[None, :] // SSD_HEAD_DIM)).astype(np.float32)
    return jnp.asarray(m3, BF16), jnp.asarray(ex3, BF16), jnp.asarray(tri), jnp.asarray(bd)


def _ssd2_conv(ext_ref, rows, cw_ref, cb_ref):
    e = ext_ref[0:8 + rows, :]
    acc = cw_ref[SSD_CONV - 1:SSD_CONV, :] * e[8:]
    for k in range(1, SSD_CONV):
        acc = acc + cw_ref[SSD_CONV - 1 - k:SSD_CONV - k, :] * pltpu.roll(e, k, axis=0)[8:]
    return cb_ref[...] + acc


def _ssd2_chunk(dt_raw, conv, dtb, a_neg, dsk, m3, ex3, tri, bd, state_io):
    c = CHUNK
    xbc = conv * _sigmoid(conv)
    xs = xbc[:, 0:D_INNER]
    bm = xbc[:, D_INNER:D_INNER + SSD_GROUPS * SSD_STATE]
    cm = xbc[:, D_INNER + SSD_GROUPS * SSD_STATE:]
    dt = _softplus(dt_raw + dtb)
    la = dt * a_neg
    rs = jnp.dot(m3, jnp.concatenate(_split3(la), axis=0), preferred_element_type=F32)
    cum, tot = rs[0:c], rs[c:2 * c]
    ecum = jnp.exp(cum)
    stack = jnp.concatenate([dt, dt * jnp.exp(tot - cum), cum, ecum], axis=0)
    rep = jnp.dot(jnp.concatenate(_split3(stack), axis=1), ex3, preferred_element_type=F32)
    dt_rep, dtw_rep, cum_rep, ecum_rep = (rep[i * c:(i + 1) * c] for i in range(4))
    u = xs * dt_rep
    uw = xs * dtw_rep
    at = jnp.concatenate([cum, cum], axis=0).T
    lane_lo = lax.broadcasted_iota(jnp.int32, (1, LANE), 1) < c
    ys = []
    for g in range(SSD_GROUPS):
        gs = slice(g * SSD_GROUP_W, (g + 1) * SSD_GROUP_W)
        ns = slice(g * SSD_STATE, (g + 1) * SSD_STATE)
        cg, bg = cm[:, ns], bm[:, ns]
        cb_rep = _dot_nt(cg, jnp.concatenate([bg] * (SSD_GROUP_W // c), axis=0))
        parts = []
        for p in range(SSD_GROUP_W // LANE):
            col = g * (SSD_GROUP_W // LANE) + p
            ps = slice(col * LANE, (col + 1) * LANE)
            cum_s = jnp.where(lane_lo, at[2 * col:2 * col + 1, :], at[2 * col + 1:2 * col + 2, :])
            w = jnp.exp(cum_rep[:, ps] - cum_s + tri) * cb_rep[:, p * LANE:(p + 1) * LANE]
            up = u[:, ps]
            parts.append(_dot(w, jnp.concatenate([up, up], axis=0) * bd))
        y_inter = state_io(g, cg, bg, uw[:, gs], ecum_rep[:, gs])
        ys.append(jnp.concatenate(parts, axis=1) + y_inter + xs[:, gs] * dsk[:, gs])
    return jnp.concatenate(ys, axis=1)


def _ssd2_carry_kernel(rest_ref, dtr_ref, s0_ref, conv0_ref, cw_ref, cb_ref, dtb_ref, alog_ref, dsk_ref,
                       m3_ref, ex3_ref, tri_ref, bd_ref, y_ref, sfin_ref, ext_ref, st_ref, *, nch):
    c = CHUNK
    t = pl.program_id(1)

    @pl.when(t == 0)
    def _():
        ext_ref[0:8, :] = conv0_ref[0]
        for g in range(SSD_GROUPS):
            st_ref[g] = s0_ref[0, g].T

    a_neg = -jnp.exp(alog_ref[...])

    def state_io(g, cg, bg, uw_g, ecum_g):
        s_g = st_ref[g]
        y_inter = _dot(cg, s_g) * ecum_g
        bg_t = jnp.concatenate([bg, jnp.zeros_like(bg)], axis=0).T[:, 0:c]
        st_ref[g] = s_g * ecum_g[c - 1:c, :] + _dot(bg_t, uw_g)
        return y_inter

    for ci in range(nch):
        ext_ref[8:8 + c, :] = rest_ref[0, ci * c:(ci + 1) * c, :].astype(F32)
        conv = _ssd2_conv(ext_ref, c, cw_ref, cb_ref)
        ext_ref[0:8, :] = ext_ref[c:c + 8, :]
        y = _ssd2_chunk(dtr_ref[0, ci * c:(ci + 1) * c, :], conv, dtb_ref[...], a_neg, dsk_ref[...],
                        m3_ref[...], ex3_ref[...], tri_ref[...], bd_ref[...], state_io)
        y_ref[0, ci * c:(ci + 1) * c, :] = y.astype(y_ref.dtype)

    @pl.when(t == pl.num_programs(1) - 1)
    def _():
        for g in range(SSD_GROUPS):
            sfin_ref[0, g] = st_ref[g].T


def _ssd2_packed_kernel(rest_ref, dtr_ref, s0_ref, conv0_ref, cw_ref, cb_ref, dtb_ref, alog_ref, dsk_ref,
                        m3_ref, ex3_ref, tri_ref, bd_ref, y_ref, sfin_ref, ext_ref, *, seglen):
    c = CHUNK
    nseg = c // seglen
    ext_ref[0:8, :] = jnp.zeros((8, SSD_CONV_DIM), F32)
    xbc_raw = rest_ref[...].astype(F32)
    for sg in range(nseg):
        ext_ref[8 + 16 * sg:16 + 16 * sg, :] = conv0_ref[sg]
        ext_ref[16 + 16 * sg:24 + 16 * sg, :] = xbc_raw[sg * seglen:(sg + 1) * seglen]
    conv2 = _ssd2_conv(ext_ref, 2 * c, cw_ref, cb_ref)
    conv = jnp.concatenate([conv2[16 * sg + 8:16 * sg + 16] for sg in range(nseg)], axis=0)
    a_neg = -jnp.exp(alog_ref[...])
    seg_of_row = lax.broadcasted_iota(jnp.int32, (c, SSD_STATE), 0) // seglen

    def state_io(g, cg, bg, uw_g, ecum_g):
        ends = jnp.concatenate([ecum_g[(sg + 1) * seglen - 1:(sg + 1) * seglen, :] for sg in range(nseg)], axis=0)
        fill = jnp.zeros((LANE - c - nseg, SSD_GROUP_W), F32)
        tg = jnp.concatenate([uw_g, ends, fill], axis=0).T
        uw_t = tg[:, 0:c]
        y_parts = []
        for sg in range(nseg):
            s_sg = s0_ref[sg, g]
            y_parts.append(_dot_nt(cg[sg * seglen:(sg + 1) * seglen], s_sg))
            b_sg = jnp.where(seg_of_row == sg, bg, 0.0)
            sfin_ref[sg, g] = s_sg * tg[:, c + sg:c + sg + 1] + _dot(uw_t, b_sg)
        return jnp.concatenate(y_parts, axis=0) * ecum_g

    y = _ssd2_chunk(dtr_ref[...], conv, dtb_ref[...], a_neg, dsk_ref[...], m3_ref[...], ex3_ref[...],
                    tri_ref[...], bd_ref[...], state_io)
    y_ref[...] = y.astype(y_ref.dtype)


def _ssd2_scan(rest3, dt_raw3, s0, conv0, conv_w, conv_b, dt_bias, a_log, d_skip):
    b, l, nr = rest3.shape
    c = CHUNK
    packed = l < c
    seglen = l if packed else c
    assert c % seglen == 0 and seglen % 8 == 0 and (l % c == 0 or packed)
    m3, ex3, tri, bd = _ssd2_consts(seglen)
    s0g = s0.reshape(b, SSD_GROUPS, SSD_GROUP_W, SSD_STATE)
    conv0p = jnp.concatenate([jnp.zeros((b, 8 - (SSD_CONV - 1), SSD_CONV_DIM), F32), conv0], axis=1)
    pad = lambda a: jnp.zeros((1, LANE), F32).at[0, :SSD_HEADS].set(a)
    dsk = jnp.repeat(d_skip, SSD_HEAD_DIM).reshape(1, D_INNER)
    consts = (conv_w, conv_b.reshape(1, -1), pad(dt_bias), pad(a_log), dsk, m3, ex3, tri, bd)
    if packed:
        nseq = c // seglen
        assert b % nseq == 0
        fix = lambda i: (0, 0)
        const_specs = [pl.BlockSpec(a.shape, fix) for a in consts]
        sblk = (nseq, SSD_GROUPS, SSD_GROUP_W, SSD_STATE)
        y, sfin = pl.pallas_call(
            functools.partial(_ssd2_packed_kernel, seglen=seglen),
            grid=(b // nseq,),
            in_specs=[pl.BlockSpec((c, nr), lambda i: (i, 0)),
                      pl.BlockSpec((c, LANE), lambda i: (i, 0)),
                      pl.BlockSpec(sblk, lambda i: (i, 0, 0, 0)),
                      pl.BlockSpec((nseq, 8, SSD_CONV_DIM), lambda i: (i, 0, 0))] + const_specs,
            out_specs=[pl.BlockSpec((c, D_INNER), lambda i: (i, 0)),
                       pl.BlockSpec(sblk, lambda i: (i, 0, 0, 0))],
            out_shape=[jax.ShapeDtypeStruct((b * l, D_INNER), BF16), jax.ShapeDtypeStruct(s0g.shape, F32)],
            scratch_shapes=[pltpu.VMEM((2 * c + 8, SSD_CONV_DIM), F32)],
            compiler_params=_cparams(("parallel",)),
            name="ssd_scan_packed",
        )(rest3.reshape(b * l, nr), dt_raw3.reshape(b * l, LANE), s0g, conv0p, *consts)
        return y.reshape(b, l, D_INNER), sfin.reshape(s0.shape)
    nch = math.gcd(l // c, SSD_CHUNKS_PER_STEP)
    fix2 = lambda i, t: (0, 0)
    const_specs = [pl.BlockSpec(a.shape, fix2) for a in consts]
    sblk = (1, SSD_GROUPS, SSD_GROUP_W, SSD_STATE)
    y, sfin = pl.pallas_call(
        functools.partial(_ssd2_carry_kernel, nch=nch),
        grid=(b, l // (c * nch)),
        in_specs=[pl.BlockSpec((1, c * nch, nr), lambda i, t: (i, t, 0)),
                  pl.BlockSpec((1, c * nch, LANE), lambda i, t: (i, t, 0)),
                  pl.BlockSpec(sblk, lambda i, t: (i, 0, 0, 0)),
                  pl.BlockSpec((1, 8, SSD_CONV_DIM), lambda i, t: (i, 0, 0))] + const_specs,
        out_specs=[pl.BlockSpec((1, c * nch, D_INNER), lambda i, t: (i, t, 0)),
                   pl.BlockSpec(sblk, lambda i, t: (i, 0, 0, 0))],
        out_shape=[jax.ShapeDtypeStruct((b, l, D_INNER), BF16), jax.ShapeDtypeStruct(s0g.shape, F32)],
        scratch_shapes=[pltpu.VMEM((c + 8, SSD_CONV_DIM), F32),
                        pltpu.VMEM((SSD_GROUPS, SSD_STATE, SSD_GROUP_W), F32)],
        compiler_params=_cparams(("parallel", "arbitrary")),
        name="ssd_scan",
    )(rest3, dt_raw3, s0g, conv0p, *consts)
    return y, sfin.reshape(s0.shape)


def _swa_mask(bq, first_block_has_no_past):
    rows = 4 * bq
    tq = (np.arange(rows) % bq)[:, None]
    s = (np.arange(4 * SWA_WINDOW) % (2 * SWA_WINDOW))[None, :]
    ok = (s > tq) & (s <= tq + SWA_WINDOW)
    if first_block_has_no_past:
        ok = ok & (s >= SWA_WINDOW)
    return np.where(ok, 0.0, NEG_BIG).astype(np.float32)


def _block_diag_pair(col, rolled, odd):
    lane = lax.broadcasted_iota(jnp.int32, col.shape, 1)
    lo = lane < SWA_HEAD_DIM
    if odd:
        top = jnp.where(lo, rolled, 0.0)
        bot = jnp.where(lo, 0.0, col)
    else:
        top = jnp.where(lo, col, 0.0)
        bot = jnp.where(lo, 0.0, rolled)
    return jnp.concatenate([top, bot], axis=0)


def _swa_kernel(sink_ref, x_ref, kprev_ref, vprev_ref, mask_ref, ones_ref, o_ref, *, bq, items):
    w = SWA_WINDOW
    n = pl.program_id(1)
    amask = mask_ref[jnp.where(n == 0, 0, 1)]
    npair = (SWA_Q_HEADS // SWA_KV_HEADS) // 2
    lane_lo = lax.broadcasted_iota(jnp.int32, (npair * bq, LANE), 1) < SWA_HEAD_DIM
    sinks = []
    for j in range(SWA_KV_HEADS):
        sinks.append([jnp.concatenate(
            [jnp.full((bq, 1), sink_ref[j * 2 * npair + 2 * p + half], F32) for p in range(npair)], axis=0)
            for half in range(2)])

    for i in range(items):
        x = x_ref[i]
        fill = [] if bq == w else [jnp.zeros((w - bq, SWA_KV_DIM), F32)]
        kall = jnp.concatenate([kprev_ref[i], x[:, D_INNER:D_INNER + SWA_KV_DIM]] + fill, axis=0)
        vall = jnp.concatenate([vprev_ref[i], x[:, D_INNER + SWA_KV_DIM:]] + fill, axis=0)
        for j in range(SWA_KV_HEADS):
            cs = slice((j // 2) * LANE, (j // 2 + 1) * LANE)
            kcol, vcol = kall[:, cs], vall[:, cs]
            swap = lambda a: jnp.concatenate([a[:, SWA_HEAD_DIM:], a[:, 0:SWA_HEAD_DIM]], axis=1)
            k2 = _block_diag_pair(kcol, swap(kcol), j % 2)
            v2 = _block_diag_pair(vcol, swap(vcol), j % 2)
            v2e = jnp.concatenate([v2.astype(BF16), ones_ref[...]], axis=1)
            qbase = j * npair * LANE
            qs = jnp.concatenate([x[:, qbase + p * LANE:qbase + (p + 1) * LANE] for p in range(npair)],
                                 axis=0) * (SWA_HEAD_DIM ** -0.5)
            sc = _dot_nt(qs, k2) + amask
            pes, ms = [], []
            for half in range(2):
                sh = sc[:, half * 2 * w:(half + 1) * 2 * w]
                m = jnp.maximum(jnp.max(sh, axis=-1, keepdims=True), sinks[j][half])
                pes.append(jnp.exp(sh - m).astype(BF16))
                ms.append(m)
            o = jnp.dot(jnp.concatenate(pes, axis=1), v2e, preferred_element_type=F32)
            esink = jnp.exp(jnp.where(lane_lo, sinks[j][0] - ms[0], sinks[j][1] - ms[1]))
            res = o[:, 0:LANE] / (o[:, LANE:] + esink)
            for p in range(npair):
                o_ref[i, :, qbase + p * LANE:qbase + (p + 1) * LANE] = res[p * bq:(p + 1) * bq].astype(o_ref.dtype)


def _swa_attn(rest3, kprev, vprev, sinks, has_past):
    b, l, nr = rest3.shape
    w = SWA_WINDOW
    bq = math.gcd(l, w)
    nb = l // bq
    assert nb == 1 or bq == w
    m_later = _swa_mask(bq, False)
    m_first = m_later if has_past else _swa_mask(bq, True)
    masks = jnp.asarray(np.stack([m_first, m_later]))
    ones2 = jnp.asarray((np.arange(4 * w)[:, None] // (2 * w)) == (np.arange(LANE)[None, :] // SWA_HEAD_DIM), BF16)
    if nb == 1:
        rest3 = rest3.astype(F32)
        items = math.gcd(b, 8)
        kspec = pl.BlockSpec((items, w, SWA_KV_DIM), lambda i, n: (i, 0, 0))
        vspec = kspec
        kin, vin = kprev, vprev
    else:
        assert not has_past
        items = 1
        kcol = D_INNER // SWA_KV_DIM
        kspec = pl.BlockSpec((1, w, SWA_KV_DIM), lambda i, n: (i, jnp.maximum(n - 1, 0), kcol))
        vspec = pl.BlockSpec((1, w, SWA_KV_DIM), lambda i, n: (i, jnp.maximum(n - 1, 0), kcol + 1))
        kin, vin = rest3, rest3
    return pl.pallas_call(
        functools.partial(_swa_kernel, bq=bq, items=items),
        grid=(b // items, nb),
        in_specs=[
            pl.BlockSpec(memory_space=pltpu.SMEM),
            pl.BlockSpec((items, bq, nr), lambda i, n: (i, n, 0)),
            kspec,
            vspec,
            pl.BlockSpec(masks.shape, lambda i, n: (0, 0, 0)),
            pl.BlockSpec(ones2.shape, lambda i, n: (0, 0)),
        ],
        out_specs=pl.BlockSpec((items, bq, D_INNER), lambda i, n: (i, n, 0)),
        out_shape=jax.ShapeDtypeStruct((b, l, D_INNER), _mixer_out_dtype(bq)),
        compiler_params=_cparams(("parallel", "parallel")),
        name="swa_attn",
    )(sinks, rest3, kin, vin, masks, ones2)


def _trunk(x, st, p, has_past):
    b, l, d = x.shape
    x2 = x.reshape(b * l, d)
    ones_inner = jnp.ones((D_INNER,), F32)
    new = {}

    def gla(i, x2, s0):
        gate, rest = _norm_proj("gla", x2, p[f"l{i}_norm"], p[f"l{i}_w_in"])
        o, s_fin = _gla_scan(rest.reshape(b, l, -1), s0, p[f"l{i}_w_gk2"], p[f"l{i}_b_gk"])
        hn = jnp.tile(p[f"l{i}_head_norm"], GLA_HEADS)
        fin = i == 3
        x2 = _out_proj("gla", o.reshape(b * l, D_INNER), gate, hn, p[f"l{i}_w_out"], x2,
                       p["final_norm"], fin)
        return x2, s_fin

    x2, new["gla0"] = gla(0, x2, st["gla0"])

    gate, rest, dt_raw = _norm_proj("ssd", x2, p["l1_norm"], p["l1_w_in"])
    rest3 = rest.reshape(b, l, -1)
    y, new["ssm"] = _ssd2_scan(rest3, dt_raw.reshape(b, l, -1), st["ssm"], st["conv"], p["l1_conv_w"],
                               p["l1_conv_b"], p["l1_dt_bias"], p["l1_a_log"], p["l1_d_skip"])
    ext_tail = jnp.concatenate([st["conv"], rest3[:, max(l - (SSD_CONV - 1), 0):, :].astype(F32)], axis=1)
    new["conv"] = ext_tail[:, -(SSD_CONV - 1):]
    x2 = _out_proj("ssd", y.reshape(b * l, D_INNER), gate, p["l1_gate_norm"], p["l1_w_out"], x2,
                   p["final_norm"], False)

    gate, rest = _norm_proj("swa", x2, p["l2_norm"], p["l2_w_in"])
    rest3 = rest.reshape(b, l, -1)
    kprev = st["swa_k"].reshape(b, SWA_WINDOW, SWA_KV_DIM)
    vprev = st["swa_v"].reshape(b, SWA_WINDOW, SWA_KV_DIM)
    o = _swa_attn(rest3, kprev, vprev, p["l2_sinks"], has_past)
    tail = max(l - SWA_WINDOW, 0)
    k_new = rest3[:, tail:, D_INNER:D_INNER + SWA_KV_DIM].astype(F32)
    v_new = rest3[:, tail:, D_INNER + SWA_KV_DIM:].astype(F32)
    new["swa_k"] = jnp.concatenate([kprev, k_new], axis=1)[:, -SWA_WINDOW:].reshape(st["swa_k"].shape)
    new["swa_v"] = jnp.concatenate([vprev, v_new], axis=1)[:, -SWA_WINDOW:].reshape(st["swa_v"].shape)
    x2 = _out_proj("swa", o.reshape(b * l, D_INNER), gate, ones_inner, p["l2_w_out"], x2,
                   p["final_norm"], False)

    x2, new["gla3"] = gla(3, x2, st["gla3"])
    return x2.reshape(b, l, d), new


def kernel(x_prompt, x_sample, state_gla_0, state_ssm_1, state_conv_1, cache_swa_k_2, cache_swa_v_2, state_gla_3, l0_norm, l0_w_in, l0_w_gk2, l0_b_gk, l0_head_norm, l0_w_out, l1_norm, l1_w_in, l1_conv_w, l1_conv_b, l1_dt_bias, l1_a_log, l1_d_skip, l1_gate_norm, l1_w_out, l2_norm, l2_w_in, l2_sinks, l2_w_out, l3_norm, l3_w_in, l3_w_gk2, l3_b_gk, l3_head_norm, l3_w_out, final_norm):
    p = dict(l0_norm=l0_norm, l0_w_in=l0_w_in, l0_w_gk2=l0_w_gk2, l0_b_gk=l0_b_gk,
             l0_head_norm=l0_head_norm, l0_w_out=l0_w_out,
             l1_norm=l1_norm, l1_w_in=l1_w_in, l1_conv_w=l1_conv_w, l1_conv_b=l1_conv_b,
             l1_dt_bias=l1_dt_bias, l1_a_log=l1_a_log, l1_d_skip=l1_d_skip,
             l1_gate_norm=l1_gate_norm, l1_w_out=l1_w_out,
             l2_norm=l2_norm, l2_w_in=l2_w_in, l2_sinks=l2_sinks, l2_w_out=l2_w_out,
             l3_norm=l3_norm, l3_w_in=l3_w_in, l3_w_gk2=l3_w_gk2, l3_b_gk=l3_b_gk,
             l3_head_norm=l3_head_norm, l3_w_out=l3_w_out, final_norm=final_norm)

    bp = x_prompt.shape[0]
    z = lambda a: jnp.zeros((bp,) + a.shape[1:], a.dtype)
    st_p = dict(gla0=z(state_gla_0), ssm=z(state_ssm_1), conv=z(state_conv_1),
                swa_k=z(cache_swa_k_2), swa_v=z(cache_swa_v_2), gla3=z(state_gla_3))
    st_s = dict(gla0=state_gla_0, ssm=state_ssm_1, conv=state_conv_1,
                swa_k=cache_swa_k_2, swa_v=cache_swa_v_2, gla3=state_gla_3)
    y_p, n_p = _trunk(x_prompt, st_p, p, False)
    y_s, n_s = _trunk(x_sample, st_s, p, True)
    return (y_p, y_s,
            n_p["gla0"], n_s["gla0"],
            n_p["ssm"], n_s["ssm"],
            n_p["conv"], n_s["conv"],
            n_p["swa_k"], n_s["swa_k"],
            n_p["swa_v"], n_s["swa_v"],
            n_p["gla3"], n_s["gla3"])
```

```python
import functools
import math

import numpy as np
import jax
import jax.numpy as jnp
from jax import lax
from jax.experimental import pallas as pl
from jax.experimental.pallas import tpu as pltpu

F32 = jnp.float32
BF16 = jnp.bfloat16

NORM_EPS = 1e-6
D_MODEL = 1024
D_INNER = 2048
CHUNK = 64
LANE = 128
BF16_SUBLANES = 16
NEG_BIG = -1e30

GLA_HEADS = 4
GLA_HEAD_K = 128
GLA_HEAD_V = 512
GLA_KEY_DIM = 512
GLA_GATE_RANK = 16
GLA_GATE_NORMALIZER = 16.0

SSD_GROUPS = 4
SSD_HEADS = 32
SSD_HEAD_DIM = 64
SSD_STATE = 128
SSD_CONV = 4
SSD_CONV_DIM = 3072
SSD_GROUP_W = D_INNER // SSD_GROUPS

SWA_WINDOW = 128
SWA_KV_HEADS = 4
SWA_HEAD_DIM = 64
SWA_Q_HEADS = 32
SWA_KV_DIM = 256

VMEM_LIMIT = 52 * 1024 * 1024


def _cparams(sem):
    return pltpu.CompilerParams(dimension_semantics=sem, vmem_limit_bytes=VMEM_LIMIT)


def _dot(a, b):
    return jnp.dot(a.astype(BF16), b.astype(BF16), preferred_element_type=F32)


def _dot_nt(a, b):
    return lax.dot_general(a.astype(BF16), b.astype(BF16), (((1,), (1,)), ((), ())),
                           preferred_element_type=F32)


def _split3(x):
    hi = x.astype(BF16)
    r1 = x - hi.astype(F32)
    mid = r1.astype(BF16)
    lo = (r1 - mid.astype(F32)).astype(BF16)
    return hi, mid, lo


def _sigmoid(x):
    return 1.0 / (1.0 + jnp.exp(-x))


def _softplus(x):
    return jnp.maximum(x, 0.0) + jnp.log(1.0 + jnp.exp(-jnp.abs(x)))


def _mixer_out_dtype(block_rows):
    return BF16 if block_rows % BF16_SUBLANES == 0 else F32


PROJ_ROWS_PER_STEP = 512
PROJ_CAST_ROWS = 256


def _round_up(n, m):
    return -(-n // m) * m


def _norm_proj_kernel(x_ref, nw_ref, w_ref, *refs, layout):
    step = pl.program_id(0)
    aux_piece = layout["aux"]
    if aux_piece is None:
        gate_ref, rest_ref, wg_scr, wr_scr = refs
    else:
        gate_ref, rest_ref, aux_ref, wg_scr, wr_scr, wa_scr = refs
    d = w_ref.shape[0]

    def cast_piece(dst, dst_off, src_off, width):
        wpad = _round_up(width, LANE)
        for r in range(0, d, PROJ_CAST_ROWS):
            piece = w_ref[r:r + PROJ_CAST_ROWS, src_off:src_off + width]
            if wpad > width:
                piece = jnp.concatenate([piece, jnp.zeros((PROJ_CAST_ROWS, wpad - width), F32)], axis=1)
            dst[r:r + PROJ_CAST_ROWS, dst_off:dst_off + wpad] = piece.astype(BF16)
        return wpad

    @pl.when(step == 0)
    def _():
        cast_piece(wg_scr, 0, *layout["gate"])
        off = 0
        for src_off, width in layout["rest"]:
            off += cast_piece(wr_scr, off, src_off, width)
        if aux_piece is not None:
            cast_piece(wa_scr, 0, *aux_piece)

    x = x_ref[...]
    ms = jnp.mean(x * x, axis=-1, keepdims=True)
    h = (x * lax.rsqrt(ms + NORM_EPS) * nw_ref[...]).astype(BF16)
    gate_ref[...] = jnp.dot(h, wg_scr[...], preferred_element_type=F32).astype(gate_ref.dtype)
    rest_ref[...] = jnp.dot(h, wr_scr[...], preferred_element_type=F32).astype(rest_ref.dtype)
    if aux_piece is not None:
        aux_ref[...] = jnp.dot(h, wa_scr[...], preferred_element_type=F32)


def _proj_layout(kind):
    kd, di = GLA_KEY_DIM, D_INNER
    if kind == "gla":
        return dict(gate=(2 * kd + di, di),
                    rest=[(2 * kd, di), (0, kd), (kd, kd), (2 * kd + 2 * di, GLA_GATE_RANK)], aux=None)
    if kind == "ssd":
        return dict(gate=(0, di), rest=[(di, SSD_CONV_DIM)], aux=(di + SSD_CONV_DIM, SSD_HEADS))
    return dict(gate=(di + 2 * SWA_KV_DIM, di), rest=[(0, di + 2 * SWA_KV_DIM)], aux=None)


def _norm_proj(kind, x2, nw, w_in):
    t, d = x2.shape
    tm = min(PROJ_ROWS_PER_STEP, t)
    layout = _proj_layout(kind)
    ng = layout["gate"][1]
    nr = sum(_round_up(w, LANE) for _, w in layout["rest"])
    has_aux = layout["aux"] is not None
    row = lambda i: (i, 0)
    fix = lambda i: (0, 0)
    out_specs = [pl.BlockSpec((tm, ng), row), pl.BlockSpec((tm, nr), row)]
    out_shape = [jax.ShapeDtypeStruct((t, ng), BF16), jax.ShapeDtypeStruct((t, nr), BF16)]
    scratch = [pltpu.VMEM((d, ng), BF16), pltpu.VMEM((d, nr), BF16)]
    if has_aux:
        out_specs.append(pl.BlockSpec((tm, LANE), row))
        out_shape.append(jax.ShapeDtypeStruct((t, LANE), F32))
        scratch.append(pltpu.VMEM((d, LANE), BF16))
    return pl.pallas_call(
        functools.partial(_norm_proj_kernel, layout=layout),
        grid=(t // tm,),
        in_specs=[
            pl.BlockSpec((tm, d), row),
            pl.BlockSpec((1, d), fix),
            pl.BlockSpec(w_in.shape, fix, pipeline_mode=pl.Buffered(1)),
        ],
        out_specs=out_specs,
        out_shape=out_shape,
        scratch_shapes=scratch,
        compiler_params=_cparams(("arbitrary",)),
        name="norm_proj_" + kind,
    )(x2, nw.reshape(1, d), w_in)


def _seg_rms(y, seg):
    parts = []
    for s in range(y.shape[1] // seg):
        p = y[:, s * seg:(s + 1) * seg]
        ms = jnp.mean(p * p, axis=-1, keepdims=True)
        parts.append(p * lax.rsqrt(ms + NORM_EPS))
    return jnp.concatenate(parts, axis=1)


def _out_proj_kernel(o_ref, gate_ref, nw_ref, w_ref, x_ref, fw_ref, out_ref, w_scr, *, mode, final):
    @pl.when(pl.program_id(0) == 0)
    def _():
        for r in range(0, w_ref.shape[0], PROJ_CAST_ROWS):
            w_scr[r:r + PROJ_CAST_ROWS, :] = w_ref[r:r + PROJ_CAST_ROWS, :].astype(BF16)

    o = o_ref[...].astype(F32)
    gt = gate_ref[...].astype(F32)
    act = gt * _sigmoid(gt)
    if mode == "gla":
        y = _seg_rms(o, GLA_HEAD_V) * nw_ref[...] * act
    elif mode == "ssd":
        y = _seg_rms(o * act, SSD_GROUP_W) * nw_ref[...]
    else:
        y = o * act
    out = x_ref[...] + jnp.dot(y.astype(BF16), w_scr[...], preferred_element_type=F32)
    if final:
        ms = jnp.mean(out * out, axis=-1, keepdims=True)
        out = out * lax.rsqrt(ms + NORM_EPS) * fw_ref[...]
    out_ref[...] = out


def _out_proj(mode, o2, gate2, nw, w_out, x2, fw, final):
    t, di = o2.shape
    tm = min(PROJ_ROWS_PER_STEP, t)
    d = x2.shape[1]
    row = lambda i: (i, 0)
    fix = lambda i: (0, 0)
    return pl.pallas_call(
        functools.partial(_out_proj_kernel, mode=mode, final=final),
        grid=(t // tm,),
        in_specs=[
            pl.BlockSpec((tm, di), row),
            pl.BlockSpec((tm, di), row),
            pl.BlockSpec((1, di), fix),
            pl.BlockSpec((di, d), fix),
            pl.BlockSpec((tm, d), row),
            pl.BlockSpec((1, d), fix),
        ],
        out_specs=pl.BlockSpec((tm, d), row),
        out_shape=jax.ShapeDtypeStruct((t, d), F32),
        scratch_shapes=[pltpu.VMEM((di, d), BF16)],
        compiler_params=_cparams(("arbitrary",)),
        name="out_proj_" + mode,
    )(o2, gate2, nw.reshape(1, di), w_out, x2, fw.reshape(1, d))


GLA_CHUNKS_PER_STEP = 4
GLA_LONG_CHUNK = 128
GLA_SINGLE_ANCHOR_MAX_DECAY = 60.0


def _gla_levels(seglen):
    return tuple(seglen >> (i + 1) for i in range(int(math.log2(seglen))))


def _gla_consts(seglen, c):
    levels = _gla_levels(seglen)
    u = np.arange(c)[:, None]
    j = np.arange(c)[None, :]
    same = (u // seglen) == (j // seglen)
    blocks = [same & (j <= u), same & (j > u)]
    lev = np.full((c, c), len(levels) + 1, np.int32)
    for li, h in enumerate(levels):
        b = (u // (2 * h)) * (2 * h) + h - 1
        blocks.append((j > np.minimum(u, b)) & (j <= np.maximum(u, b)))
        sib = (u // (2 * h) == j // (2 * h)) & (u % (2 * h) >= h) & (j % (2 * h) < h)
        lev[sib] = li
    lev[np.eye(c, dtype=bool)] = len(levels)
    m = np.concatenate(blocks, axis=0).astype(np.float32)
    m3 = np.concatenate([m, m, m], axis=1)
    if 2 * c == LANE:
        lev = np.concatenate([lev, lev], axis=1)
    return jnp.asarray(m3, BF16), jnp.asarray(lev)


def _gla_scores(qx, kx, e_q, e_k, lev, nt):
    att = None
    for li, (eq, ek) in enumerate(zip(e_q, e_k)):
        a = nt(qx if eq is None else qx * eq, kx if ek is None else kx * ek)
        att = jnp.where(lev == li, a, 0.0 if att is None else att)
    return att


def _nt_pair(qe, ke):
    kb = ke.astype(BF16)
    z = jnp.zeros((kb.shape[0], GLA_HEAD_K), BF16)
    rhs = jnp.concatenate([jnp.concatenate([kb[:, 0:GLA_HEAD_K], z], axis=1),
                           jnp.concatenate([z, kb[:, GLA_HEAD_K:]], axis=1)], axis=0)
    return lax.dot_general(qe.astype(BF16), rhs, (((1,), (1,)), ((), ())), preferred_element_type=F32)


def _gla_prep(xc, wgk, bgk):
    v = xc[:, 0:D_INNER]
    q = xc[:, D_INNER:D_INNER + GLA_KEY_DIM].astype(F32) * (GLA_HEAD_K ** -0.5)
    k = xc[:, D_INNER + GLA_KEY_DIM:D_INNER + 2 * GLA_KEY_DIM].astype(F32)
    lr = xc[:, D_INNER + 2 * GLA_KEY_DIM:]
    z = _dot(lr, wgk) + bgk
    g = (jnp.minimum(z, 0.0) - jnp.log(1.0 + jnp.exp(-jnp.abs(z)))) * (1.0 / GLA_GATE_NORMALIZER)
    return v, q, k, g


def _gla_chunk(v, q, k, g, state_io, m3, lev, nl, single_anchor):
    c = g.shape[0]
    g3 = jnp.concatenate(_split3(g), axis=0)
    if single_anchor:
        rs = jnp.dot(m3[0:2 * c], g3, preferred_element_type=F32)
        cum = rs[0:c]
        e_in, e_out = jnp.exp(cum), jnp.exp(rs[c:2 * c])
        e_q, e_k = [e_in], [jnp.exp(-cum)]
        lev = jnp.where(lev <= nl, 0, 1)
    else:
        e_all = jnp.exp(jnp.dot(m3, g3, preferred_element_type=F32))
        e_in, e_out = e_all[0:c], e_all[c:2 * c]
        e_q = [e_all[(2 + li) * c:(3 + li) * c] for li in range(nl)] + [None]
        e_k = e_q
    qe_in = q * e_in
    ke_out = k * e_out
    outs = []
    if c == LANE:
        for h in range(GLA_HEADS):
            ks = slice(h * GLA_HEAD_K, (h + 1) * GLA_HEAD_K)
            cut = lambda es: [None if e is None else e[:, ks] for e in es]
            att = _gla_scores(q[:, ks], k[:, ks], cut(e_q), cut(e_k), lev, _dot_nt)
            v_h = v[:, h * GLA_HEAD_V:(h + 1) * GLA_HEAD_V]
            xt = jnp.concatenate([ke_out[:, ks], e_in[:, ks]], axis=0).T
            outs.append(_dot(att, v_h) + state_io(h, qe_in[:, ks], xt, v_h))
        return jnp.concatenate(outs, axis=1)
    lane_lo = lax.broadcasted_iota(jnp.int32, (c, LANE), 1) < c
    for pr in range(GLA_HEADS // 2):
        ls = slice(2 * pr * GLA_HEAD_K, (2 * pr + 2) * GLA_HEAD_K)
        cut = lambda es: [None if e is None else e[:, ls] for e in es]
        att = _gla_scores(q[:, ls], k[:, ls], cut(e_q), cut(e_k), lev, _nt_pair)
        vpair = v[:, 2 * pr * GLA_HEAD_V:(2 * pr + 2) * GLA_HEAD_V]
        vcat = jnp.concatenate([vpair[:, 0:GLA_HEAD_V], vpair[:, GLA_HEAD_V:]], axis=0)
        for half, att_h in enumerate((jnp.where(lane_lo, att, 0.0), jnp.where(lane_lo, 0.0, att))):
            h = 2 * pr + half
            ks = slice(h * GLA_HEAD_K, (h + 1) * GLA_HEAD_K)
            xt = jnp.concatenate([ke_out[:, ks], e_in[:, ks]], axis=0).T
            o_inter = state_io(h, qe_in[:, ks], xt, v[:, h * GLA_HEAD_V:(h + 1) * GLA_HEAD_V])
            outs.append(_dot(att_h, vcat) + o_inter)
    return jnp.concatenate(outs, axis=1)


def _gla_dispatch(preps, run):
    low = None
    for _, _, _, g in preps:
        tot = jnp.min(jnp.sum(g, axis=0, keepdims=True))
        low = tot if low is None else jnp.minimum(low, tot)
    single_anchor_ok = low >= -GLA_SINGLE_ANCHOR_MAX_DECAY
    pl.when(single_anchor_ok)(functools.partial(run, True))
    pl.when(jnp.logical_not(single_anchor_ok))(functools.partial(run, False))


def _gla_carry_kernel(rest_ref, s0_ref, wgk_ref, bgk_ref, m_ref, lev_ref, o_ref, s_ref, *, c, nch, nl):
    @pl.when(pl.program_id(1) == 0)
    def _():
        s_ref[...] = s0_ref[...]

    def state_io(h, qe_in_h, xt, v_h):
        s_h = s_ref[0, h]
        s_ref[0, h] = s_h * xt[:, 2 * c - 1:2 * c] + _dot(xt[:, 0:c], v_h)
        return _dot(qe_in_h, s_h)

    wgk, bgk = wgk_ref[...], bgk_ref[...]
    preps = [_gla_prep(rest_ref[0, ci * c:(ci + 1) * c, :], wgk, bgk) for ci in range(nch)]

    def run(single_anchor):
        for ci, (v, q, k, g) in enumerate(preps):
            o = _gla_chunk(v, q, k, g, state_io, m_ref[...], lev_ref[...], nl, single_anchor)
            o_ref[0, ci * c:(ci + 1) * c, :] = o.astype(o_ref.dtype)

    _gla_dispatch(preps, run)


def _gla_packed_kernel(rest_ref, s0_ref, wgk_ref, bgk_ref, m_ref, lev_ref, o_ref, s_ref, *, seglen, nl):
    c = CHUNK
    nseg = c // seglen
    seg_of_lane = lax.broadcasted_iota(jnp.int32, (GLA_HEAD_K, c), 1) // seglen

    def state_io(h, qe_in_h, xt, v_h):
        ke_t = xt[:, 0:c]
        outs = []
        for sg in range(nseg):
            s_h = s0_ref[sg, h]
            outs.append(_dot(qe_in_h[sg * seglen:(sg + 1) * seglen], s_h))
            end = c + (sg + 1) * seglen - 1
            s_ref[sg, h] = s_h * xt[:, end:end + 1] + _dot(jnp.where(seg_of_lane == sg, ke_t, 0.0), v_h)
        return jnp.concatenate(outs, axis=0)

    preps = [_gla_prep(rest_ref[...], wgk_ref[...], bgk_ref[...])]

    def run(single_anchor):
        v, q, k, g = preps[0]
        o = _gla_chunk(v, q, k, g, state_io, m_ref[...], lev_ref[...], nl, single_anchor)
        o_ref[...] = o.astype(o_ref.dtype)

    _gla_dispatch(preps, run)


def _gla_scan(rest3, s0, w_gk2, b_gk):
    b, l, nr = rest3.shape
    packed = l < CHUNK
    c = CHUNK if packed or l % GLA_LONG_CHUNK else GLA_LONG_CHUNK
    seglen = l if packed else c
    assert c % seglen == 0 and seglen % 8 == 0 and (l % c == 0 or packed)
    nl = len(_gla_levels(seglen))
    m01, lev = _gla_consts(seglen, c)
    wgk = jnp.zeros((LANE, GLA_KEY_DIM), F32).at[:GLA_GATE_RANK].set(w_gk2).astype(BF16)
    consts = (wgk, b_gk.reshape(1, GLA_KEY_DIM), m01, lev)
    if packed:
        nseq = c // seglen
        assert b % nseq == 0
        sblk = (nseq, GLA_HEADS, GLA_HEAD_K, GLA_HEAD_V)
        o, s_fin = pl.pallas_call(
            functools.partial(_gla_packed_kernel, seglen=seglen, nl=nl),
            grid=(b // nseq,),
            in_specs=[pl.BlockSpec((c, nr), lambda i: (i, 0)),
                      pl.BlockSpec(sblk, lambda i: (i, 0, 0, 0))]
                     + [pl.BlockSpec(a.shape, lambda i: (0, 0)) for a in consts],
            out_specs=[pl.BlockSpec((c, D_INNER), lambda i: (i, 0)),
                       pl.BlockSpec(sblk, lambda i: (i, 0, 0, 0))],
            out_shape=[jax.ShapeDtypeStruct((b * l, D_INNER), BF16), jax.ShapeDtypeStruct(s0.shape, F32)],
            compiler_params=_cparams(("parallel",)),
            name="gla_scan_packed",
        )(rest3.reshape(b * l, nr), s0, *consts)
        return o.reshape(b, l, D_INNER), s_fin
    nch = math.gcd(l // c, GLA_CHUNKS_PER_STEP)
    sblk = (1, GLA_HEADS, GLA_HEAD_K, GLA_HEAD_V)
    return pl.pallas_call(
        functools.partial(_gla_carry_kernel, c=c, nch=nch, nl=nl),
        grid=(b, l // (c * nch)),
        in_specs=[pl.BlockSpec((1, c * nch, nr), lambda i, t: (i, t, 0)),
                  pl.BlockSpec(sblk, lambda i, t: (i, 0, 0, 0))]
                 + [pl.BlockSpec(a.shape, lambda i, t: (0, 0)) for a in consts],
        out_specs=[pl.BlockSpec((1, c * nch, D_INNER), lambda i, t: (i, t, 0)),
                   pl.BlockSpec(sblk, lambda i, t: (i, 0, 0, 0))],
        out_shape=[jax.ShapeDtypeStruct((b, l, D_INNER), BF16), jax.ShapeDtypeStruct(s0.shape, F32)],
        compiler_params=_cparams(("parallel", "arbitrary")),
        name="gla_scan",
    )(rest3, s0, *consts)


SSD_CHUNKS_PER_STEP = 4


def _ssd2_consts(seglen):
    c = CHUNK
    u = np.arange(c)
    same = (u[:, None] // seglen) == (u[None, :] // seglen)
    tl = same & (u[None, :] <= u[:, None])
    m = np.concatenate([tl, same], axis=0).astype(np.float32)
    m3 = np.concatenate([m, m, m], axis=1)
    expand = (np.arange(LANE)[:, None] == (np.arange(D_INNER) // SSD_HEAD_DIM)[None, :]).astype(np.float32)
    ex3 = np.concatenate([expand] * 3, axis=0)
    s_of = np.arange(LANE) % c
    tri = np.where(same[:, s_of] & (s_of[None, :] <= u[:, None]), 0.0, NEG_BIG).astype(np.float32)
    bd = ((np.arange(2 * c)[:, None] // c) == (np.arange(LANE)[None, :] // SSD_HEAD_DIM)).astype(np.float32)
    return jnp.asarray(m3, BF16), jnp.asarray(ex3, BF16), jnp.asarray(tri), jnp.asarray(bd)


def _ssd2_conv(ext_ref, rows, cw_ref, cb_ref):
    e = ext_ref[0:8 + rows, :]
    acc = cw_ref[SSD_CONV - 1:SSD_CONV, :] * e[8:]
    for k in range(1, SSD_CONV):
        acc = acc + cw_ref[SSD_CONV - 1 - k:SSD_CONV - k, :] * pltpu.roll(e, k, axis=0)[8:]
    return cb_ref[...] + acc


def _ssd2_chunk(dt_raw, conv, dtb, a_neg, dsk, m3, ex3, tri, bd, state_io):
    c = CHUNK
    xbc = conv * _sigmoid(conv)
    xs = xbc[:, 0:D_INNER]
    bm = xbc[:, D_INNER:D_INNER + SSD_GROUPS * SSD_STATE]
    cm = xbc[:, D_INNER + SSD_GROUPS * SSD_STATE:]
    dt = _softplus(dt_raw + dtb)
    la = dt * a_neg
    rs = jnp.dot(m3, jnp.concatenate(_split3(la), axis=0), preferred_element_type=F32)
    cum, tot = rs[0:c], rs[c:2 * c]
    ecum = jnp.exp(cum)
    stack = jnp.concatenate([dt, dt * jnp.exp(tot - cum), cum, ecum], axis=0)
    rep = jnp.dot(jnp.concatenate(_split3(stack), axis=1), ex3, preferred_element_type=F32)
    dt_rep, dtw_rep, cum_rep, ecum_rep = (rep[i * c:(i + 1) * c] for i in range(4))
    u = xs * dt_rep
    uw = xs * dtw_rep
    at = jnp.concatenate([cum, cum], axis=0).T
    lane_lo = lax.broadcasted_iota(jnp.int32, (1, LANE), 1) < c
    ys = []
    for g in range(SSD_GROUPS):
        gs = slice(g * SSD_GROUP_W, (g + 1) * SSD_GROUP_W)
        ns = slice(g * SSD_STATE, (g + 1) * SSD_STATE)
        cg, bg = cm[:, ns], bm[:, ns]
        cb_rep = _dot_nt(cg, jnp.concatenate([bg] * (SSD_GROUP_W // c), axis=0))
        parts = []
        for p in range(SSD_GROUP_W // LANE):
            col = g * (SSD_GROUP_W // LANE) + p
            ps = slice(col * LANE, (col + 1) * LANE)
            cum_s = jnp.where(lane_lo, at[2 * col:2 * col + 1, :], at[2 * col + 1:2 * col + 2, :])
            w = jnp.exp(cum_rep[:, ps] - cum_s + tri) * cb_rep[:, p * LANE:(p + 1) * LANE]
            up = u[:, ps]
            parts.append(_dot(w, jnp.concatenate([up, up], axis=0) * bd))
        y_inter = state_io(g, cg, bg, uw[:, gs], ecum_rep[:, gs])
        ys.append(jnp.concatenate(parts, axis=1) + y_inter + xs[:, gs] * dsk[:, gs])
    return jnp.concatenate(ys, axis=1)


def _ssd2_carry_kernel(rest_ref, dtr_ref, s0_ref, conv0_ref, cw_ref, cb_ref, dtb_ref, alog_ref, dsk_ref,
                       m3_ref, ex3_ref, tri_ref, bd_ref, y_ref, sfin_ref, ext_ref, st_ref, *, nch):
    c = CHUNK
    t = pl.program_id(1)

    @pl.when(t == 0)
    def _():
        ext_ref[0:8, :] = conv0_ref[0]
        for g in range(SSD_GROUPS):
            st_ref[g] = s0_ref[0, g].T

    a_neg = -jnp.exp(alog_ref[...])

    def state_io(g, cg, bg, uw_g, ecum_g):
        s_g = st_ref[g]
        y_inter = _dot(cg, s_g) * ecum_g
        bg_t = jnp.concatenate([bg, jnp.zeros_like(bg)], axis=0).T[:, 0:c]
        st_ref[g] = s_g * ecum_g[c - 1:c, :] + _dot(bg_t, uw_g)
        return y_inter

    for ci in range(nch):
        ext_ref[8:8 + c, :] = rest_ref[0, ci * c:(ci + 1) * c, :].astype(F32)
        conv = _ssd2_conv(ext_ref, c, cw_ref, cb_ref)
        ext_ref[0:8, :] = ext_ref[c:c + 8, :]
        y = _ssd2_chunk(dtr_ref[0, ci * c:(ci + 1) * c, :], conv, dtb_ref[...], a_neg, dsk_ref[...],
                        m3_ref[...], ex3_ref[...], tri_ref[...], bd_ref[...], state_io)
        y_ref[0, ci * c:(ci + 1) * c, :] = y.astype(y_ref.dtype)

    @pl.when(t == pl.num_programs(1) - 1)
    def _():
        for g in range(SSD_GROUPS):
            sfin_ref[0, g] = st_ref[g].T


def _ssd2_packed_kernel(rest_ref, dtr_ref, s0_ref, conv0_ref, cw_ref, cb_ref, dtb_ref, alog_ref, dsk_ref,
                        m3_ref, ex3_ref, tri_ref, bd_ref, y_ref, sfin_ref, ext_ref, *, seglen):
    c = CHUNK
    nseg = c // seglen
    ext_ref[0:8, :] = jnp.zeros((8, SSD_CONV_DIM), F32)
    xbc_raw = rest_ref[...].astype(F32)
    for sg in range(nseg):
        ext_ref[8 + 16 * sg:16 + 16 * sg, :] = conv0_ref[sg]
        ext_ref[16 + 16 * sg:24 + 16 * sg, :] = xbc_raw[sg * seglen:(sg + 1) * seglen]
    conv2 = _ssd2_conv(ext_ref, 2 * c, cw_ref, cb_ref)
    conv = jnp.concatenate([conv2[16 * sg + 8:16 * sg + 16] for sg in range(nseg)], axis=0)
    a_neg = -jnp.exp(alog_ref[...])
    seg_of_row = lax.broadcasted_iota(jnp.int32, (c, SSD_STATE), 0) // seglen

    def state_io(g, cg, bg, uw_g, ecum_g):
        ends = jnp.concatenate([ecum_g[(sg + 1) * seglen - 1:(sg + 1) * seglen, :] for sg in range(nseg)], axis=0)
        fill = jnp.zeros((LANE - c - nseg, SSD_GROUP_W), F32)
        tg = jnp.concatenate([uw_g, ends, fill], axis=0).T
        uw_t = tg[:, 0:c]
        y_parts = []
        for sg in range(nseg):
            s_sg = s0_ref[sg, g]
            y_parts.append(_dot_nt(cg[sg * seglen:(sg + 1) * seglen], s_sg))
            b_sg = jnp.where(seg_of_row == sg, bg, 0.0)
            sfin_ref[sg, g] = s_sg * tg[:, c + sg:c + sg + 1] + _dot(uw_t, b_sg)
        return jnp.concatenate(y_parts, axis=0) * ecum_g

    y = _ssd2_chunk(dtr_ref[...], conv, dtb_ref[...], a_neg, dsk_ref[...], m3_ref[...], ex3_ref[...],
                    tri_ref[...], bd_ref[...], state_io)
    y_ref[...] = y.astype(y_ref.dtype)


def _ssd2_scan(rest3, dt_raw3, s0, conv0, conv_w, conv_b, dt_bias, a_log, d_skip):
    b, l, nr = rest3.shape
    c = CHUNK
    packed = l < c
    seglen = l if packed else c
    assert c % seglen == 0 and seglen % 8 == 0 and (l % c == 0 or packed)
    m3, ex3, tri, bd = _ssd2_consts(seglen)
    s0g = s0.reshape(b, SSD_GROUPS, SSD_GROUP_W, SSD_STATE)
    conv0p = jnp.concatenate([jnp.zeros((b, 8 - (SSD_CONV - 1), SSD_CONV_DIM), F32), conv0], axis=1)
    pad = lambda a: jnp.zeros((1, LANE), F32).at[0, :SSD_HEADS].set(a)
    dsk = jnp.repeat(d_skip, SSD_HEAD_DIM).reshape(1, D_INNER)
    consts = (conv_w, conv_b.reshape(1, -1), pad(dt_bias), pad(a_log), dsk, m3, ex3, tri, bd)
    if packed:
        nseq = c // seglen
        assert b % nseq == 0
        fix = lambda i: (0, 0)
        const_specs = [pl.BlockSpec(a.shape, fix) for a in consts]
        sblk = (nseq, SSD_GROUPS, SSD_GROUP_W, SSD_STATE)
        y, sfin = pl.pallas_call(
            functools.partial(_ssd2_packed_kernel, seglen=seglen),
            grid=(b // nseq,),
            in_specs=[pl.BlockSpec((c, nr), lambda i: (i, 0)),
                      pl.BlockSpec((c, LANE), lambda i: (i, 0)),
                      pl.BlockSpec(sblk, lambda i: (i, 0, 0, 0)),
                      pl.BlockSpec((nseq, 8, SSD_CONV_DIM), lambda i: (i, 0, 0))] + const_specs,
            out_specs=[pl.BlockSpec((c, D_INNER), lambda i: (i, 0)),
                       pl.BlockSpec(sblk, lambda i: (i, 0, 0, 0))],
            out_shape=[jax.ShapeDtypeStruct((b * l, D_INNER), BF16), jax.ShapeDtypeStruct(s0g.shape, F32)],
            scratch_shapes=[pltpu.VMEM((2 * c + 8, SSD_CONV_DIM), F32)],
            compiler_params=_cparams(("parallel",)),
            name="ssd_scan_packed",
        )(rest3.reshape(b * l, nr), dt_raw3.reshape(b * l, LANE), s0g, conv0p, *consts)
        return y.reshape(b, l, D_INNER), sfin.reshape(s0.shape)
    nch = math.gcd(l // c, SSD_CHUNKS_PER_STEP)
    fix2 = lambda i, t: (0, 0)
    const_specs = [pl.BlockSpec(a.shape, fix2) for a in consts]
    sblk = (1, SSD_GROUPS, SSD_GROUP_W, SSD_STATE)
    y, sfin = pl.pallas_call(
        functools.partial(_ssd2_carry_kernel, nch=nch),
        grid=(b, l // (c * nch)),
        in_specs=[pl.BlockSpec((1, c * nch, nr), lambda i, t: (i, t, 0)),
                  pl.BlockSpec((1, c * nch, LANE), lambda i, t: (i, t, 0)),
                  pl.BlockSpec(sblk, lambda i, t: (i, 0, 0, 0)),
                  pl.BlockSpec((1, 8, SSD_CONV_DIM), lambda i, t: (i, 0, 0))] + const_specs,
        out_specs=[pl.BlockSpec((1, c * nch, D_INNER), lambda i, t: (i, t, 0)),
                   pl.BlockSpec(sblk, lambda i, t: (i, 0, 0, 0))],
        out_shape=[jax.ShapeDtypeStruct((b, l, D_INNER), BF16), jax.ShapeDtypeStruct(s0g.shape, F32)],
        scratch_shapes=[pltpu.VMEM((c + 8, SSD_CONV_DIM), F32),
                        pltpu.VMEM((SSD_GROUPS, SSD_STATE, SSD_GROUP_W), F32)],
        compiler_params=_cparams(("parallel", "arbitrary")),
        name="ssd_scan",
    )(rest3, dt_raw3, s0g, conv0p, *consts)
    return y, sfin.reshape(s0.shape)


def _swa_mask(bq, first_block_has_no_past):
    rows = 4 * bq
    tq = (np.arange(rows) % bq)[:, None]
    s = (np.arange(4 * SWA_WINDOW) % (2 * SWA_WINDOW))[None, :]
    ok = (s > tq) & (s <= tq + SWA_WINDOW)
    if first_block_has_no_past:
        ok = ok & (s >= SWA_WINDOW)
    return np.where(ok, 0.0, NEG_BIG).astype(np.float32)


def _block_diag_pair(col, rolled, odd):
    lane = lax.broadcasted_iota(jnp.int32, col.shape, 1)
    lo = lane < SWA_HEAD_DIM
    if odd:
        top = jnp.where(lo, rolled, 0.0)
        bot = jnp.where(lo, 0.0, col)
    else:
        top = jnp.where(lo, col, 0.0)
        bot = jnp.where(lo, 0.0, rolled)
    return jnp.concatenate([top, bot], axis=0)


def _swa_kernel(sink_ref, x_ref, kprev_ref, vprev_ref, mask_ref, ones_ref, o_ref, *cache_refs, bq, items):
    w = SWA_WINDOW
    n = pl.program_id(1)
    amask = mask_ref[jnp.where(n == 0, 0, 1)]
    npair = (SWA_Q_HEADS // SWA_KV_HEADS) // 2
    rows = npair * bq
    lane_lo = lax.broadcasted_iota(jnp.int32, (rows, LANE), 1) < SWA_HEAD_DIM
    sinks = []
    for j in range(SWA_KV_HEADS):
        sinks.append([jnp.concatenate(
            [jnp.full((bq, 1), sink_ref[j * 2 * npair + 2 * p + half], F32) for p in range(npair)], axis=0)
            for half in range(2)])

    for i in range(items):
        x = x_ref[i]
        fill = [] if bq == w else [jnp.zeros((w - bq, SWA_KV_DIM), F32)]
        kall = jnp.concatenate([kprev_ref[i], x[:, D_INNER:D_INNER + SWA_KV_DIM]] + fill, axis=0)
        vall = jnp.concatenate([vprev_ref[i], x[:, D_INNER + SWA_KV_DIM:]] + fill, axis=0)
        if cache_refs:
            cache_refs[0][i] = kall[bq:bq + w]
            cache_refs[1][i] = vall[bq:bq + w]
        for j in range(SWA_KV_HEADS):
            cs = slice((j // 2) * LANE, (j // 2 + 1) * LANE)
            kcol, vcol = kall[:, cs], vall[:, cs]
            swap = lambda a: jnp.concatenate([a[:, SWA_HEAD_DIM:], a[:, 0:SWA_HEAD_DIM]], axis=1)
            k2 = _block_diag_pair(kcol, swap(kcol), j % 2)
            v2 = _block_diag_pair(vcol, swap(vcol), j % 2)
            v2e = jnp.concatenate([v2.astype(BF16), ones_ref[...]], axis=1)
            qbase = j * npair * LANE
            qs = jnp.concatenate([x[:, qbase + p * LANE:qbase + (p + 1) * LANE] for p in range(npair)],
                                 axis=0) * (SWA_HEAD_DIM ** -0.5)
            sc = _dot_nt(qs, k2) + amask
            pes, ms = [], []
            for half in range(2):
                sh = sc[:, half * 2 * w:(half + 1) * 2 * w]
                m = jnp.maximum(jnp.max(sh, axis=-1, keepdims=True), sinks[j][half])
                pes.append(jnp.exp(sh - m).astype(BF16))
                ms.append(m)
            o = jnp.dot(jnp.concatenate(pes, axis=1), v2e, preferred_element_type=F32)
            esink = jnp.exp(jnp.where(lane_lo, sinks[j][0] - ms[0], sinks[j][1] - ms[1]))
            res = o[:, 0:LANE] / (o[:, LANE:] + esink)
            for p in range(npair):
                o_ref[i, :, qbase + p * LANE:qbase + (p + 1) * LANE] = res[p * bq:(p + 1) * bq].astype(o_ref.dtype)


def _swa_attn(rest3, kprev, vprev, sinks, has_past):
    b, l, nr = rest3.shape
    w = SWA_WINDOW
    bq = math.gcd(l, w)
    nb = l // bq
    assert nb == 1 or bq == w
    m_later = _swa_mask(bq, False)
    m_first = m_later if has_past else _swa_mask(bq, True)
    masks = jnp.asarray(np.stack([m_first, m_later]))
    ones2 = jnp.asarray((np.arange(4 * w)[:, None] // (2 * w)) == (np.arange(LANE)[None, :] // SWA_HEAD_DIM), BF16)
    if nb == 1:
        rest3 = rest3.astype(F32)
        items = math.gcd(b, 8)
        kspec = pl.BlockSpec((items, w, SWA_KV_DIM), lambda i, n: (i, 0, 0))
        vspec = kspec
        kin, vin = kprev, vprev
    else:
        assert not has_past
        items = 1
        kcol = D_INNER // SWA_KV_DIM
        kspec = pl.BlockSpec((1, w, SWA_KV_DIM), lambda i, n: (i, jnp.maximum(n - 1, 0), kcol))
        vspec = pl.BlockSpec((1, w, SWA_KV_DIM), lambda i, n: (i, jnp.maximum(n - 1, 0), kcol + 1))
        kin, vin = rest3, rest3
    out_specs = [pl.BlockSpec((items, bq, D_INNER), lambda i, n: (i, n, 0))]
    out_shape = [jax.ShapeDtypeStruct((b, l, D_INNER), _mixer_out_dtype(bq))]
    if nb == 1:
        out_specs += [pl.BlockSpec((items, w, SWA_KV_DIM), lambda i, n: (i, 0, 0))] * 2
        out_shape += [jax.ShapeDtypeStruct((b, w, SWA_KV_DIM), F32)] * 2
    return pl.pallas_call(
        functools.partial(_swa_kernel, bq=bq, items=items),
        grid=(b // items, nb),
        in_specs=[
            pl.BlockSpec(memory_space=pltpu.SMEM),
            pl.BlockSpec((items, bq, nr), lambda i, n: (i, n, 0)),
            kspec,
            vspec,
            pl.BlockSpec(masks.shape, lambda i, n: (0, 0, 0)),
            pl.BlockSpec(ones2.shape, lambda i, n: (0, 0)),
        ],
        out_specs=out_specs,
        out_shape=out_shape,
        compiler_params=_cparams(("parallel", "parallel")),
        name="swa_attn",
    )(sinks, rest3, kin, vin, masks, ones2)


def _trunk(x, st, p, has_past):
    b, l, d = x.shape
    x2 = x.reshape(b * l, d)
    ones_inner = jnp.ones((D_INNER,), F32)
    new = {}

    def gla(i, x2, s0):
        gate, rest = _norm_proj("gla", x2, p[f"l{i}_norm"], p[f"l{i}_w_in"])
        o, s_fin = _gla_scan(rest.reshape(b, l, -1), s0, p[f"l{i}_w_gk2"], p[f"l{i}_b_gk"])
        hn = jnp.tile(p[f"l{i}_head_norm"], GLA_HEADS)
        fin = i == 3
        x2 = _out_proj("gla", o.reshape(b * l, D_INNER), gate, hn, p[f"l{i}_w_out"], x2,
                       p["final_norm"], fin)
        return x2, s_fin

    x2, new["gla0"] = gla(0, x2, st["gla0"])

    gate, rest, dt_raw = _norm_proj("ssd", x2, p["l1_norm"], p["l1_w_in"])
    rest3 = rest.reshape(b, l, -1)
    y, new["ssm"] = _ssd2_scan(rest3, dt_raw.reshape(b, l, -1), st["ssm"], st["conv"], p["l1_conv_w"],
                               p["l1_conv_b"], p["l1_dt_bias"], p["l1_a_log"], p["l1_d_skip"])
    ext_tail = jnp.concatenate([st["conv"], rest3[:, max(l - (SSD_CONV - 1), 0):, :].astype(F32)], axis=1)
    new["conv"] = ext_tail[:, -(SSD_CONV - 1):]
    x2 = _out_proj("ssd", y.reshape(b * l, D_INNER), gate, p["l1_gate_norm"], p["l1_w_out"], x2,
                   p["final_norm"], False)

    gate, rest = _norm_proj("swa", x2, p["l2_norm"], p["l2_w_in"])
    rest3 = rest.reshape(b, l, -1)
    kprev = st["swa_k"].reshape(b, SWA_WINDOW, SWA_KV_DIM)
    vprev = st["swa_v"].reshape(b, SWA_WINDOW, SWA_KV_DIM)
    res = _swa_attn(rest3, kprev, vprev, p["l2_sinks"], has_past)
    o = res[0]
    if len(res) == 3:
        k_win, v_win = res[1], res[2]
    else:
        k_win = rest3[:, l - SWA_WINDOW:, D_INNER:D_INNER + SWA_KV_DIM].astype(F32)
        v_win = rest3[:, l - SWA_WINDOW:, D_INNER + SWA_KV_DIM:].astype(F32)
    new["swa_k"] = k_win.reshape(st["swa_k"].shape)
    new["swa_v"] = v_win.reshape(st["swa_v"].shape)
    x2 = _out_proj("swa", o.reshape(b * l, D_INNER), gate, ones_inner, p["l2_w_out"], x2,
                   p["final_norm"], False)

    x2, new["gla3"] = gla(3, x2, st["gla3"])
    return x2.reshape(b, l, d), new


def kernel(x_prompt, x_sample, state_gla_0, state_ssm_1, state_conv_1, cache_swa_k_2, cache_swa_v_2, state_gla_3, l0_norm, l0_w_in, l0_w_gk2, l0_b_gk, l0_head_norm, l0_w_out, l1_norm, l1_w_in, l1_conv_w, l1_conv_b, l1_dt_bias, l1_a_log, l1_d_skip, l1_gate_norm, l1_w_out, l2_norm, l2_w_in, l2_sinks, l2_w_out, l3_norm, l3_w_in, l3_w_gk2, l3_b_gk, l3_head_norm, l3_w_out, final_norm):
    p = dict(l0_norm=l0_norm, l0_w_in=l0_w_in, l0_w_gk2=l0_w_gk2, l0_b_gk=l0_b_gk,
             l0_head_norm=l0_head_norm, l0_w_out=l0_w_out,
             l1_norm=l1_norm, l1_w_in=l1_w_in, l1_conv_w=l1_conv_w, l1_conv_b=l1_conv_b,
             l1_dt_bias=l1_dt_bias, l1_a_log=l1_a_log, l1_d_skip=l1_d_skip,
             l1_gate_norm=l1_gate_norm, l1_w_out=l1_w_out,
             l2_norm=l2_norm, l2_w_in=l2_w_in, l2_sinks=l2_sinks, l2_w_out=l2_w_out,
             l3_norm=l3_norm, l3_w_in=l3_w_in, l3_w_gk2=l3_w_gk2, l3_b_gk=l3_b_gk,
             l3_head_norm=l3_head_norm, l3_w_out=l3_w_out, final_norm=final_norm)

    bp = x_prompt.shape[0]
    z = lambda a: jnp.zeros((bp,) + a.shape[1:], a.dtype)
    st_p = dict(gla0=z(state_gla_0), ssm=z(state_ssm_1), conv=z(state_conv_1),
                swa_k=z(cache_swa_k_2), swa_v=z(cache_swa_v_2), gla3=z(state_gla_3))
    st_s = dict(gla0=state_gla_0, ssm=state_ssm_1, conv=state_conv_1,
                swa_k=cache_swa_k_2, swa_v=cache_swa_v_2, gla3=state_gla_3)
    y_p, n_p = _trunk(x_prompt, st_p, p, False)
    y_s, n_s = _trunk(x_sample, st_s, p, True)
    return (y_p, y_s,
            n_p["gla0"], n_s["gla0"],
            n_p["ssm"], n_s["ssm"],
            n_p["conv"], n_s["conv"],
            n_p["swa_k"], n_s["swa_k"],
            n_p["swa_v"], n_s["swa_v"],
            n_p["gla3"], n_s["gla3"])
```

```python
import functools
import math

import numpy as np
import jax
import jax.numpy as jnp
from jax import lax
from jax.experimental import pallas as pl
from jax.experimental.pallas import tpu as pltpu

F32 = jnp.float32
BF16 = jnp.bfloat16

NORM_EPS = 1e-6
D_MODEL = 1024
D_INNER = 2048
CHUNK = 64
LANE = 128
BF16_SUBLANES = 16
NEG_BIG = -1e30

GLA_HEADS = 4
GLA_HEAD_K = 128
GLA_HEAD_V = 512
GLA_KEY_DIM = 512
GLA_GATE_RANK = 16
GLA_GATE_NORMALIZER = 16.0

SSD_GROUPS = 4
SSD_HEADS = 32
SSD_HEAD_DIM = 64
SSD_STATE = 128
SSD_CONV = 4
SSD_CONV_DIM = 3072
SSD_GROUP_W = D_INNER // SSD_GROUPS

SWA_WINDOW = 128
SWA_KV_HEADS = 4
SWA_HEAD_DIM = 64
SWA_Q_HEADS = 32
SWA_KV_DIM = 256

VMEM_LIMIT = 52 * 1024 * 1024


def _cparams(sem):
    return pltpu.CompilerParams(dimension_semantics=sem, vmem_limit_bytes=VMEM_LIMIT)


def _dot(a, b):
    return jnp.dot(a.astype(BF16), b.astype(BF16), preferred_element_type=F32)


def _dot_nt(a, b):
    return lax.dot_general(a.astype(BF16), b.astype(BF16), (((1,), (1,)), ((), ())),
                           preferred_element_type=F32)


def _split3(x):
    hi = x.astype(BF16)
    r1 = x - hi.astype(F32)
    mid = r1.astype(BF16)
    lo = (r1 - mid.astype(F32)).astype(BF16)
    return hi, mid, lo


def _sigmoid(x):
    return 1.0 / (1.0 + jnp.exp(-x))


def _softplus(x):
    return jnp.maximum(x, 0.0) + jnp.log(1.0 + jnp.exp(-jnp.abs(x)))


def _mixer_out_dtype(block_rows):
    return BF16 if block_rows % BF16_SUBLANES == 0 else F32


PROJ_ROWS_PER_STEP = 256
PROJ_CAST_ROWS = 256


def _round_up(n, m):
    return -(-n // m) * m


def _group_steps(row_counts, tm):
    n = [t // min(tm, t) for t in row_counts]
    starts = [sum(n[:g]) for g in range(len(n))]
    return n, starts


def _group_row_map(start, nsteps):
    return lambda i: (jnp.clip(i - start, 0, nsteps - 1), 0)


def _norm_proj_kernel(*refs, layout, group_steps):
    step = pl.program_id(0)
    aux_piece = layout["aux"]
    ngroups = len(group_steps)
    nout = 2 if aux_piece is None else 3
    x_refs = refs[0:ngroups]
    nw_ref, w_ref = refs[ngroups], refs[ngroups + 1]
    out_refs = refs[ngroups + 2:ngroups + 2 + nout * ngroups]
    scr = refs[ngroups + 2 + nout * ngroups:]
    wg_scr, wr_scr = scr[0], scr[1]
    wa_scr = scr[2] if aux_piece is not None else None
    d = w_ref.shape[0]

    def cast_piece(dst, dst_off, src_off, width):
        wpad = _round_up(width, LANE)
        for r in range(0, d, PROJ_CAST_ROWS):
            piece = w_ref[r:r + PROJ_CAST_ROWS, src_off:src_off + width]
            if wpad > width:
                piece = jnp.concatenate([piece, jnp.zeros((PROJ_CAST_ROWS, wpad - width), F32)], axis=1)
            dst[r:r + PROJ_CAST_ROWS, dst_off:dst_off + wpad] = piece.astype(BF16)
        return wpad

    @pl.when(step == 0)
    def _():
        cast_piece(wg_scr, 0, *layout["gate"])
        off = 0
        for src_off, width in layout["rest"]:
            off += cast_piece(wr_scr, off, src_off, width)
        if aux_piece is not None:
            cast_piece(wa_scr, 0, *aux_piece)

    def project(x_ref, outs):
        x = x_ref[...]
        ms = jnp.mean(x * x, axis=-1, keepdims=True)
        h = (x * lax.rsqrt(ms + NORM_EPS) * nw_ref[...]).astype(BF16)
        outs[0][...] = jnp.dot(h, wg_scr[...], preferred_element_type=F32).astype(outs[0].dtype)
        outs[1][...] = jnp.dot(h, wr_scr[...], preferred_element_type=F32).astype(outs[1].dtype)
        if aux_piece is not None:
            outs[2][...] = jnp.dot(h, wa_scr[...], preferred_element_type=F32)

    for g, (start, nsteps) in enumerate(group_steps):
        pl.when((step >= start) & (step < start + nsteps))(
            functools.partial(project, x_refs[g], out_refs[g * nout:(g + 1) * nout]))


def _proj_layout(kind):
    kd, di = GLA_KEY_DIM, D_INNER
    if kind == "gla":
        return dict(gate=(2 * kd + di, di),
                    rest=[(2 * kd, di), (0, kd), (kd, kd), (2 * kd + 2 * di, GLA_GATE_RANK)], aux=None)
    if kind == "ssd":
        return dict(gate=(0, di), rest=[(di, SSD_CONV_DIM)], aux=(di + SSD_CONV_DIM, SSD_HEADS))
    return dict(gate=(di + 2 * SWA_KV_DIM, di), rest=[(0, di + 2 * SWA_KV_DIM)], aux=None)


def _norm_proj(kind, xs, nw, w_in):
    d = xs[0].shape[1]
    layout = _proj_layout(kind)
    ng = layout["gate"][1]
    nr = sum(_round_up(w, LANE) for _, w in layout["rest"])
    has_aux = layout["aux"] is not None
    nsteps, starts = _group_steps([x.shape[0] for x in xs], PROJ_ROWS_PER_STEP)
    fix = lambda i: (0, 0)
    in_specs, out_specs, out_shape = [], [], []
    for x, n, s in zip(xs, nsteps, starts):
        t = x.shape[0]
        tm = t // n
        row = _group_row_map(s, n)
        in_specs.append(pl.BlockSpec((tm, d), row))
        out_specs += [pl.BlockSpec((tm, ng), row), pl.BlockSpec((tm, nr), row)]
        out_shape += [jax.ShapeDtypeStruct((t, ng), BF16), jax.ShapeDtypeStruct((t, nr), BF16)]
        if has_aux:
            out_specs.append(pl.BlockSpec((tm, LANE), row))
            out_shape.append(jax.ShapeDtypeStruct((t, LANE), F32))
    in_specs += [pl.BlockSpec((1, d), fix), pl.BlockSpec(w_in.shape, fix, pipeline_mode=pl.Buffered(1))]
    scratch = [pltpu.VMEM((d, ng), BF16), pltpu.VMEM((d, nr), BF16)]
    if has_aux:
        scratch.append(pltpu.VMEM((d, LANE), BF16))
    outs = pl.pallas_call(
        functools.partial(_norm_proj_kernel, layout=layout, group_steps=tuple(zip(starts, nsteps))),
        grid=(sum(nsteps),),
        in_specs=in_specs,
        out_specs=out_specs,
        out_shape=out_shape,
        scratch_shapes=scratch,
        compiler_params=_cparams(("arbitrary",)),
        name="norm_proj_" + kind,
    )(*xs, nw.reshape(1, d), w_in)
    nout = 3 if has_aux else 2
    return [outs[g * nout:(g + 1) * nout] for g in range(len(xs))]


def _seg_rms(y, seg):
    parts = []
    for s in range(y.shape[1] // seg):
        p = y[:, s * seg:(s + 1) * seg]
        ms = jnp.mean(p * p, axis=-1, keepdims=True)
        parts.append(p * lax.rsqrt(ms + NORM_EPS))
    return jnp.concatenate(parts, axis=1)


def _out_proj_kernel(*refs, mode, final, group_steps):
    ngroups = len(group_steps)
    nw_ref, w_ref, fw_ref = refs[3 * ngroups:3 * ngroups + 3]
    out_refs = refs[3 * ngroups + 3:4 * ngroups + 3]
    w_scr = refs[4 * ngroups + 3]
    step = pl.program_id(0)

    @pl.when(step == 0)
    def _():
        for r in range(0, w_ref.shape[0], PROJ_CAST_ROWS):
            w_scr[r:r + PROJ_CAST_ROWS, :] = w_ref[r:r + PROJ_CAST_ROWS, :].astype(BF16)

    def project(o_ref, gate_ref, x_ref, out_ref):
        o = o_ref[...].astype(F32)
        gt = gate_ref[...].astype(F32)
        act = gt * _sigmoid(gt)
        if mode == "gla":
            y = _seg_rms(o, GLA_HEAD_V) * nw_ref[...] * act
        elif mode == "ssd":
            y = _seg_rms(o * act, SSD_GROUP_W) * nw_ref[...]
        else:
            y = o * act
        out = x_ref[...] + jnp.dot(y.astype(BF16), w_scr[...], preferred_element_type=F32)
        if final:
            ms = jnp.mean(out * out, axis=-1, keepdims=True)
            out = out * lax.rsqrt(ms + NORM_EPS) * fw_ref[...]
        out_ref[...] = out

    for g, (start, nsteps) in enumerate(group_steps):
        pl.when((step >= start) & (step < start + nsteps))(
            functools.partial(project, refs[3 * g], refs[3 * g + 1], refs[3 * g + 2], out_refs[g]))


def _out_proj(mode, os_, gates, nw, w_out, xs, fw, final):
    di = os_[0].shape[1]
    d = xs[0].shape[1]
    nsteps, starts = _group_steps([x.shape[0] for x in xs], PROJ_ROWS_PER_STEP)
    fix = lambda i: (0, 0)
    in_specs, out_specs, out_shape, operands = [], [], [], []
    for o2, g2, x2, n, s in zip(os_, gates, xs, nsteps, starts):
        t = x2.shape[0]
        tm = t // n
        row = _group_row_map(s, n)
        in_specs += [pl.BlockSpec((tm, di), row), pl.BlockSpec((tm, di), row), pl.BlockSpec((tm, d), row)]
        operands += [o2, g2, x2]
        out_specs.append(pl.BlockSpec((tm, d), row))
        out_shape.append(jax.ShapeDtypeStruct((t, d), F32))
    in_specs += [pl.BlockSpec((1, di), fix), pl.BlockSpec((di, d), fix), pl.BlockSpec((1, d), fix)]
    return pl.pallas_call(
        functools.partial(_out_proj_kernel, mode=mode, final=final, group_steps=tuple(zip(starts, nsteps))),
        grid=(sum(nsteps),),
        in_specs=in_specs,
        out_specs=out_specs,
        out_shape=out_shape,
        scratch_shapes=[pltpu.VMEM((di, d), BF16)],
        compiler_params=_cparams(("arbitrary",)),
        name="out_proj_" + mode,
    )(*operands, nw.reshape(1, di), w_out, fw.reshape(1, d))


GLA_CHUNKS_PER_STEP = 4
GLA_LONG_CHUNK = 128
GLA_SINGLE_ANCHOR_MAX_DECAY = 60.0


def _gla_levels(seglen):
    return tuple(seglen >> (i + 1) for i in range(int(math.log2(seglen))))


def _gla_consts(seglen, c):
    levels = _gla_levels(seglen)
    u = np.arange(c)[:, None]
    j = np.arange(c)[None, :]
    same = (u // seglen) == (j // seglen)
    blocks = [same & (j <= u), same & (j > u)]
    lev = np.full((c, c), len(levels) + 1, np.int32)
    for li, h in enumerate(levels):
        b = (u // (2 * h)) * (2 * h) + h - 1
        blocks.append((j > np.minimum(u, b)) & (j <= np.maximum(u, b)))
        sib = (u // (2 * h) == j // (2 * h)) & (u % (2 * h) >= h) & (j % (2 * h) < h)
        lev[sib] = li
    lev[np.eye(c, dtype=bool)] = len(levels)
    m = np.concatenate(blocks, axis=0).astype(np.float32)
    m3 = np.concatenate([m, m, m], axis=1)
    if 2 * c == LANE:
        lev = np.concatenate([lev, lev], axis=1)
    return jnp.asarray(m3, BF16), jnp.asarray(lev)


def _gla_scores(qx, kx, e_q, e_k, lev, nt):
    att = None
    for li, (eq, ek) in enumerate(zip(e_q, e_k)):
        a = nt(qx if eq is None else qx * eq, kx if ek is None else kx * ek)
        att = jnp.where(lev == li, a, 0.0 if att is None else att)
    return att


def _nt_pair(qe, ke):
    kb = ke.astype(BF16)
    z = jnp.zeros((kb.shape[0], GLA_HEAD_K), BF16)
    rhs = jnp.concatenate([jnp.concatenate([kb[:, 0:GLA_HEAD_K], z], axis=1),
                           jnp.concatenate([z, kb[:, GLA_HEAD_K:]], axis=1)], axis=0)
    return lax.dot_general(qe.astype(BF16), rhs, (((1,), (1,)), ((), ())), preferred_element_type=F32)


def _gla_prep(xc, wgk, bgk):
    v = xc[:, 0:D_INNER]
    q = xc[:, D_INNER:D_INNER + GLA_KEY_DIM].astype(F32) * (GLA_HEAD_K ** -0.5)
    k = xc[:, D_INNER + GLA_KEY_DIM:D_INNER + 2 * GLA_KEY_DIM].astype(F32)
    lr = xc[:, D_INNER + 2 * GLA_KEY_DIM:]
    z = _dot(lr, wgk) + bgk
    g = (jnp.minimum(z, 0.0) - jnp.log(1.0 + jnp.exp(-jnp.abs(z)))) * (1.0 / GLA_GATE_NORMALIZER)
    return v, q, k, g


def _gla_chunk(v, q, k, g, state_io, m3, lev, nl, single_anchor):
    c = g.shape[0]
    g3 = jnp.concatenate(_split3(g), axis=0)
    if single_anchor:
        rs = jnp.dot(m3[0:2 * c], g3, preferred_element_type=F32)
        cum = rs[0:c]
        e_in, e_out = jnp.exp(cum), jnp.exp(rs[c:2 * c])
        e_q, e_k = [e_in], [jnp.exp(-cum)]
        lev = jnp.where(lev <= nl, 0, 1)
    else:
        e_all = jnp.exp(jnp.dot(m3, g3, preferred_element_type=F32))
        e_in, e_out = e_all[0:c], e_all[c:2 * c]
        e_q = [e_all[(2 + li) * c:(3 + li) * c] for li in range(nl)] + [None]
        e_k = e_q
    qe_in = q * e_in
    ke_out = k * e_out
    outs = []
    if c == LANE:
        for h in range(GLA_HEADS):
            ks = slice(h * GLA_HEAD_K, (h + 1) * GLA_HEAD_K)
            cut = lambda es: [None if e is None else e[:, ks] for e in es]
            att = _gla_scores(q[:, ks], k[:, ks], cut(e_q), cut(e_k), lev, _dot_nt)
            v_h = v[:, h * GLA_HEAD_V:(h + 1) * GLA_HEAD_V]
            xt = jnp.concatenate([ke_out[:, ks], e_in[:, ks]], axis=0).T
            outs.append(_dot(att, v_h) + state_io(h, qe_in[:, ks], xt, v_h))
        return jnp.concatenate(outs, axis=1)
    lane_lo = lax.broadcasted_iota(jnp.int32, (c, LANE), 1) < c
    for pr in range(GLA_HEADS // 2):
        ls = slice(2 * pr * GLA_HEAD_K, (2 * pr + 2) * GLA_HEAD_K)
        cut = lambda es: [None if e is None else e[:, ls] for e in es]
        att = _gla_scores(q[:, ls], k[:, ls], cut(e_q), cut(e_k), lev, _nt_pair)
        vpair = v[:, 2 * pr * GLA_HEAD_V:(2 * pr + 2) * GLA_HEAD_V]
        vcat = jnp.concatenate([vpair[:, 0:GLA_HEAD_V], vpair[:, GLA_HEAD_V:]], axis=0)
        for half, att_h in enumerate((jnp.where(lane_lo, att, 0.0), jnp.where(lane_lo, 0.0, att))):
            h = 2 * pr + half
            ks = slice(h * GLA_HEAD_K, (h + 1) * GLA_HEAD_K)
            xt = jnp.concatenate([ke_out[:, ks], e_in[:, ks]], axis=0).T
            o_inter = state_io(h, qe_in[:, ks], xt, v[:, h * GLA_HEAD_V:(h + 1) * GLA_HEAD_V])
            outs.append(_dot(att_h, vcat) + o_inter)
    return jnp.concatenate(outs, axis=1)


def _gla_dispatch(preps, run):
    low = None
    for _, _, _, g in preps:
        tot = jnp.min(jnp.sum(g, axis=0, keepdims=True))
        low = tot if low is None else jnp.minimum(low, tot)
    single_anchor_ok = low >= -GLA_SINGLE_ANCHOR_MAX_DECAY
    pl.when(single_anchor_ok)(functools.partial(run, True))
    pl.when(jnp.logical_not(single_anchor_ok))(functools.partial(run, False))


def _gla_carry_kernel(rest_ref, s0_ref, wgk_ref, bgk_ref, m_ref, lev_ref, o_ref, s_ref, *, c, nch, nl):
    @pl.when(pl.program_id(1) == 0)
    def _():
        s_ref[...] = s0_ref[...]

    def state_io(h, qe_in_h, xt, v_h):
        s_h = s_ref[0, h]
        s_ref[0, h] = s_h * xt[:, 2 * c - 1:2 * c] + _dot(xt[:, 0:c], v_h)
        return _dot(qe_in_h, s_h)

    wgk, bgk = wgk_ref[...], bgk_ref[...]
    preps = [_gla_prep(rest_ref[0, ci * c:(ci + 1) * c, :], wgk, bgk) for ci in range(nch)]

    def run(single_anchor):
        for ci, (v, q, k, g) in enumerate(preps):
            o = _gla_chunk(v, q, k, g, state_io, m_ref[...], lev_ref[...], nl, single_anchor)
            o_ref[0, ci * c:(ci + 1) * c, :] = o.astype(o_ref.dtype)

    _gla_dispatch(preps, run)


def _gla_packed_kernel(rest_ref, s0_ref, wgk_ref, bgk_ref, m_ref, lev_ref, o_ref, s_ref, *, seglen, nl):
    c = CHUNK
    nseg = c // seglen
    seg_of_lane = lax.broadcasted_iota(jnp.int32, (GLA_HEAD_K, c), 1) // seglen

    def state_io(h, qe_in_h, xt, v_h):
        ke_t = xt[:, 0:c]
        outs = []
        for sg in range(nseg):
            s_h = s0_ref[sg, h]
            outs.append(_dot(qe_in_h[sg * seglen:(sg + 1) * seglen], s_h))
            end = c + (sg + 1) * seglen - 1
            s_ref[sg, h] = s_h * xt[:, end:end + 1] + _dot(jnp.where(seg_of_lane == sg, ke_t, 0.0), v_h)
        return jnp.concatenate(outs, axis=0)

    preps = [_gla_prep(rest_ref[...], wgk_ref[...], bgk_ref[...])]

    def run(single_anchor):
        v, q, k, g = preps[0]
        o = _gla_chunk(v, q, k, g, state_io, m_ref[...], lev_ref[...], nl, single_anchor)
        o_ref[...] = o.astype(o_ref.dtype)

    _gla_dispatch(preps, run)


def _gla_scan(rest3, s0, w_gk2, b_gk):
    b, l, nr = rest3.shape
    packed = l < CHUNK
    c = CHUNK if packed or l % GLA_LONG_CHUNK else GLA_LONG_CHUNK
    seglen = l if packed else c
    assert c % seglen == 0 and seglen % 8 == 0 and (l % c == 0 or packed)
    nl = len(_gla_levels(seglen))
    m01, lev = _gla_consts(seglen, c)
    wgk = jnp.zeros((LANE, GLA_KEY_DIM), F32).at[:GLA_GATE_RANK].set(w_gk2).astype(BF16)
    consts = (wgk, b_gk.reshape(1, GLA_KEY_DIM), m01, lev)
    if packed:
        nseq = c // seglen
        assert b % nseq == 0
        sblk = (nseq, GLA_HEADS, GLA_HEAD_K, GLA_HEAD_V)
        o, s_fin = pl.pallas_call(
            functools.partial(_gla_packed_kernel, seglen=seglen, nl=nl),
            grid=(b // nseq,),
            in_specs=[pl.BlockSpec((c, nr), lambda i: (i, 0)),
                      pl.BlockSpec(sblk, lambda i: (i, 0, 0, 0))]
                     + [pl.BlockSpec(a.shape, lambda i: (0, 0)) for a in consts],
            out_specs=[pl.BlockSpec((c, D_INNER), lambda i: (i, 0)),
                       pl.BlockSpec(sblk, lambda i: (i, 0, 0, 0))],
            out_shape=[jax.ShapeDtypeStruct((b * l, D_INNER), BF16), jax.ShapeDtypeStruct(s0.shape, F32)],
            compiler_params=_cparams(("parallel",)),
            name="gla_scan_packed",
        )(rest3.reshape(b * l, nr), s0, *consts)
        return o.reshape(b, l, D_INNER), s_fin
    nch = math.gcd(l // c, GLA_CHUNKS_PER_STEP)
    sblk = (1, GLA_HEADS, GLA_HEAD_K, GLA_HEAD_V)
    return pl.pallas_call(
        functools.partial(_gla_carry_kernel, c=c, nch=nch, nl=nl),
        grid=(b, l // (c * nch)),
        in_specs=[pl.BlockSpec((1, c * nch, nr), lambda i, t: (i, t, 0)),
                  pl.BlockSpec(sblk, lambda i, t: (i, 0, 0, 0))]
                 + [pl.BlockSpec(a.shape, lambda i, t: (0, 0)) for a in consts],
        out_specs=[pl.BlockSpec((1, c * nch, D_INNER), lambda i, t: (i, t, 0)),
                   pl.BlockSpec(sblk, lambda i, t: (i, 0, 0, 0))],
        out_shape=[jax.ShapeDtypeStruct((b, l, D_INNER), BF16), jax.ShapeDtypeStruct(s0.shape, F32)],
        compiler_params=_cparams(("parallel", "arbitrary")),
        name="gla_scan",
    )(rest3, s0, *consts)


SSD_CHUNKS_PER_STEP = 4


def _ssd2_consts(seglen):
    c = CHUNK
    u = np.arange(c)
    same = (u[:, None] // seglen) == (u[None, :] // seglen)
    tl = same & (u[None, :] <= u[:, None])
    m = np.concatenate([tl, same], axis=0).astype(np.float32)
    m3 = np.concatenate([m, m, m], axis=1)
    expand = (np.arange(LANE)[:, None] == (np.arange(D_INNER) // SSD_HEAD_DIM)[None, :]).astype(np.float32)
    ex3 = np.concatenate([expand] * 3, axis=0)
    s_of = np.arange(LANE) % c
    tri = np.where(same[:, s_of] & (s_of[None, :] <= u[:, None]), 0.0, NEG_BIG).astype(np.float32)
    bd = ((np.arange(2 * c)[:, None] // c) == (np.arange(LANE)[None, :] // SSD_HEAD_DIM)).astype(np.float32)
    return jnp.asarray(m3, BF16), jnp.asarray(ex3, BF16), jnp.asarray(tri), jnp.asarray(bd)


def _ssd2_conv(ext_ref, rows, cw_ref, cb_ref):
    e = ext_ref[0:8 + rows, :]
    acc = cw_ref[SSD_CONV - 1:SSD_CONV, :] * e[8:]
    for k in range(1, SSD_CONV):
        acc = acc + cw_ref[SSD_CONV - 1 - k:SSD_CONV - k, :] * pltpu.roll(e, k, axis=0)[8:]
    return cb_ref[...] + acc


def _ssd2_chunk(dt_raw, conv, dtb, a_neg, dsk, m3, ex3, tri, bd, state_io):
    c = CHUNK
    xbc = conv * _sigmoid(conv)
    xs = xbc[:, 0:D_INNER]
    bm = xbc[:, D_INNER:D_INNER + SSD_GROUPS * SSD_STATE]
    cm = xbc[:, D_INNER + SSD_GROUPS * SSD_STATE:]
    dt = _softplus(dt_raw + dtb)
    la = dt * a_neg
    rs = jnp.dot(m3, jnp.concatenate(_split3(la), axis=0), preferred_element_type=F32)
    cum, tot = rs[0:c], rs[c:2 * c]
    ecum = jnp.exp(cum)
    stack = jnp.concatenate([dt, dt * jnp.exp(tot - cum), cum, ecum], axis=0)
    rep = jnp.dot(jnp.concatenate(_split3(stack), axis=1), ex3, preferred_element_type=F32)
    dt_rep, dtw_rep, cum_rep, ecum_rep = (rep[i * c:(i + 1) * c] for i in range(4))
    u = xs * dt_rep
    uw = xs * dtw_rep
    at = jnp.concatenate([cum, cum], axis=0).T
    lane_lo = lax.broadcasted_iota(jnp.int32, (1, LANE), 1) < c
    ys = []
    for g in range(SSD_GROUPS):
        gs = slice(g * SSD_GROUP_W, (g + 1) * SSD_GROUP_W)
        ns = slice(g * SSD_STATE, (g + 1) * SSD_STATE)
        cg, bg = cm[:, ns], bm[:, ns]
        cb_rep = _dot_nt(cg, jnp.concatenate([bg] * (SSD_GROUP_W // c), axis=0))
        parts = []
        for p in range(SSD_GROUP_W // LANE):
            col = g * (SSD_GROUP_W // LANE) + p
            ps = slice(col * LANE, (col + 1) * LANE)
            cum_s = jnp.where(lane_lo, at[2 * col:2 * col + 1, :], at[2 * col + 1:2 * col + 2, :])
            w = jnp.exp(cum_rep[:, ps] - cum_s + tri) * cb_rep[:, p * LANE:(p + 1) * LANE]
            up = u[:, ps]
            parts.append(_dot(w, jnp.concatenate([up, up], axis=0) * bd))
        y_inter = state_io(g, cg, bg, uw[:, gs], ecum_rep[:, gs])
        ys.append(jnp.concatenate(parts, axis=1) + y_inter + xs[:, gs] * dsk[:, gs])
    return jnp.concatenate(ys, axis=1)


def _ssd2_carry_kernel(rest_ref, dtr_ref, s0_ref, conv0_ref, cw_ref, cb_ref, dtb_ref, alog_ref, dsk_ref,
                       m3_ref, ex3_ref, tri_ref, bd_ref, y_ref, sfin_ref, ext_ref, st_ref, *, nch):
    c = CHUNK
    t = pl.program_id(1)

    @pl.when(t == 0)
    def _():
        ext_ref[0:8, :] = conv0_ref[0]
        for g in range(SSD_GROUPS):
            st_ref[g] = s0_ref[0, g].T

    a_neg = -jnp.exp(alog_ref[...])

    def state_io(g, cg, bg, uw_g, ecum_g):
        s_g = st_ref[g]
        y_inter = _dot(cg, s_g) * ecum_g
        bg_t = jnp.concatenate([bg, jnp.zeros_like(bg)], axis=0).T[:, 0:c]
        st_ref[g] = s_g * ecum_g[c - 1:c, :] + _dot(bg_t, uw_g)
        return y_inter

    for ci in range(nch):
        ext_ref[8:8 + c, :] = rest_ref[0, ci * c:(ci + 1) * c, :].astype(F32)
        conv = _ssd2_conv(ext_ref, c, cw_ref, cb_ref)
        ext_ref[0:8, :] = ext_ref[c:c + 8, :]
        y = _ssd2_chunk(dtr_ref[0, ci * c:(ci + 1) * c, :], conv, dtb_ref[...], a_neg, dsk_ref[...],
                        m3_ref[...], ex3_ref[...], tri_ref[...], bd_ref[...], state_io)
        y_ref[0, ci * c:(ci + 1) * c, :] = y.astype(y_ref.dtype)

    @pl.when(t == pl.num_programs(1) - 1)
    def _():
        for g in range(SSD_GROUPS):
            sfin_ref[0, g] = st_ref[g].T


def _ssd2_packed_kernel(rest_ref, dtr_ref, s0_ref, conv0_ref, cw_ref, cb_ref, dtb_ref, alog_ref, dsk_ref,
                        m3_ref, ex3_ref, tri_ref, bd_ref, y_ref, sfin_ref, ext_ref, *, seglen):
    c = CHUNK
    nseg = c // seglen
    ext_ref[0:8, :] = jnp.zeros((8, SSD_CONV_DIM), F32)
    xbc_raw = rest_ref[...].astype(F32)
    for sg in range(nseg):
        ext_ref[8 + 16 * sg:16 + 16 * sg, :] = conv0_ref[sg]
        ext_ref[16 + 16 * sg:24 + 16 * sg, :] = xbc_raw[sg * seglen:(sg + 1) * seglen]
    conv2 = _ssd2_conv(ext_ref, 2 * c, cw_ref, cb_ref)
    conv = jnp.concatenate([conv2[16 * sg + 8:16 * sg + 16] for sg in range(nseg)], axis=0)
    a_neg = -jnp.exp(alog_ref[...])
    seg_of_row = lax.broadcasted_iota(jnp.int32, (c, SSD_STATE), 0) // seglen

    def state_io(g, cg, bg, uw_g, ecum_g):
        ends = jnp.concatenate([ecum_g[(sg + 1) * seglen - 1:(sg + 1) * seglen, :] for sg in range(nseg)], axis=0)
        fill = jnp.zeros((LANE - c - nseg, SSD_GROUP_W), F32)
        tg = jnp.concatenate([uw_g, ends, fill], axis=0).T
        uw_t = tg[:, 0:c]
        y_parts = []
        for sg in range(nseg):
            s_sg = s0_ref[sg, g]
            y_parts.append(_dot_nt(cg[sg * seglen:(sg + 1) * seglen], s_sg))
            b_sg = jnp.where(seg_of_row == sg, bg, 0.0)
            sfin_ref[sg, g] = s_sg * tg[:, c + sg:c + sg + 1] + _dot(uw_t, b_sg)
        return jnp.concatenate(y_parts, axis=0) * ecum_g

    y = _ssd2_chunk(dtr_ref[...], conv, dtb_ref[...], a_neg, dsk_ref[...], m3_ref[...], ex3_ref[...],
                    tri_ref[...], bd_ref[...], state_io)
    y_ref[...] = y.astype(y_ref.dtype)


def _ssd2_scan(rest3, dt_raw3, s0, conv0, conv_w, conv_b, dt_bias, a_log, d_skip):
    b, l, nr = rest3.shape
    c = CHUNK
    packed = l < c
    seglen = l if packed else c
    assert c % seglen == 0 and seglen % 8 == 0 and (l % c == 0 or packed)
    m3, ex3, tri, bd = _ssd2_consts(seglen)
    s0g = s0.reshape(b, SSD_GROUPS, SSD_GROUP_W, SSD_STATE)
    conv0p = jnp.concatenate([jnp.zeros((b, 8 - (SSD_CONV - 1), SSD_CONV_DIM), F32), conv0], axis=1)
    pad = lambda a: jnp.zeros((1, LANE), F32).at[0, :SSD_HEADS].set(a)
    dsk = jnp.repeat(d_skip, SSD_HEAD_DIM).reshape(1, D_INNER)
    consts = (conv_w, conv_b.reshape(1, -1), pad(dt_bias), pad(a_log), dsk, m3, ex3, tri, bd)
    if packed:
        nseq = c // seglen
        assert b % nseq == 0
        fix = lambda i: (0, 0)
        const_specs = [pl.BlockSpec(a.shape, fix) for a in consts]
        sblk = (nseq, SSD_GROUPS, SSD_GROUP_W, SSD_STATE)
        y, sfin = pl.pallas_call(
            functools.partial(_ssd2_packed_kernel, seglen=seglen),
            grid=(b // nseq,),
            in_specs=[pl.BlockSpec((c, nr), lambda i: (i, 0)),
                      pl.BlockSpec((c, LANE), lambda i: (i, 0)),
                      pl.BlockSpec(sblk, lambda i: (i, 0, 0, 0)),
                      pl.BlockSpec((nseq, 8, SSD_CONV_DIM), lambda i: (i, 0, 0))] + const_specs,
            out_specs=[pl.BlockSpec((c, D_INNER), lambda i: (i, 0)),
                       pl.BlockSpec(sblk, lambda i: (i, 0, 0, 0))],
            out_shape=[jax.ShapeDtypeStruct((b * l, D_INNER), BF16), jax.ShapeDtypeStruct(s0g.shape, F32)],
            scratch_shapes=[pltpu.VMEM((2 * c + 8, SSD_CONV_DIM), F32)],
            compiler_params=_cparams(("parallel",)),
            name="ssd_scan_packed",
        )(rest3.reshape(b * l, nr), dt_raw3.reshape(b * l, LANE), s0g, conv0p, *consts)
        return y.reshape(b, l, D_INNER), sfin.reshape(s0.shape)
    nch = math.gcd(l // c, SSD_CHUNKS_PER_STEP)
    fix2 = lambda i, t: (0, 0)
    const_specs = [pl.BlockSpec(a.shape, fix2) for a in consts]
    sblk = (1, SSD_GROUPS, SSD_GROUP_W, SSD_STATE)
    y, sfin = pl.pallas_call(
        functools.partial(_ssd2_carry_kernel, nch=nch),
        grid=(b, l // (c * nch)),
        in_specs=[pl.BlockSpec((1, c * nch, nr), lambda i, t: (i, t, 0)),
                  pl.BlockSpec((1, c * nch, LANE), lambda i, t: (i, t, 0)),
                  pl.BlockSpec(sblk, lambda i, t: (i, 0, 0, 0)),
                  pl.BlockSpec((1, 8, SSD_CONV_DIM), lambda i, t: (i, 0, 0))] + const_specs,
        out_specs=[pl.BlockSpec((1, c * nch, D_INNER), lambda i, t: (i, t, 0)),
                   pl.BlockSpec(sblk, lambda i, t: (i, 0, 0, 0))],
        out_shape=[jax.ShapeDtypeStruct((b, l, D_INNER), BF16), jax.ShapeDtypeStruct(s0g.shape, F32)],
        scratch_shapes=[pltpu.VMEM((c + 8, SSD_CONV_DIM), F32),
                        pltpu.VMEM((SSD_GROUPS, SSD_STATE, SSD_GROUP_W), F32)],
        compiler_params=_cparams(("parallel", "arbitrary")),
        name="ssd_scan",
    )(rest3, dt_raw3, s0g, conv0p, *consts)
    return y, sfin.reshape(s0.shape)


SWA_BLOCKS_PER_STEP = 2


def _swa_mask(bq, first_block_has_no_past):
    rows = 4 * bq
    tq = (np.arange(rows) % bq)[:, None]
    s = (np.arange(4 * SWA_WINDOW) % (2 * SWA_WINDOW))[None, :]
    ok = (s > tq) & (s <= tq + SWA_WINDOW)
    if first_block_has_no_past:
        ok = ok & (s >= SWA_WINDOW)
    return np.where(ok, 0.0, NEG_BIG).astype(np.float32)


def _block_diag_pair(col, rolled, odd):
    lane = lax.broadcasted_iota(jnp.int32, col.shape, 1)
    lo = lane < SWA_HEAD_DIM
    if odd:
        top = jnp.where(lo, rolled, 0.0)
        bot = jnp.where(lo, 0.0, col)
    else:
        top = jnp.where(lo, col, 0.0)
        bot = jnp.where(lo, 0.0, rolled)
    return jnp.concatenate([top, bot], axis=0)


def _swa_kernel(sink_ref, x_ref, kprev_ref, vprev_ref, mask_ref, ones_ref, o_ref, *cache_refs, bq, items, nsub):
    w = SWA_WINDOW
    n = pl.program_id(1)
    npair = (SWA_Q_HEADS // SWA_KV_HEADS) // 2
    rows = npair * bq
    lane_lo = lax.broadcasted_iota(jnp.int32, (rows, LANE), 1) < SWA_HEAD_DIM
    sinks = []
    for j in range(SWA_KV_HEADS):
        sinks.append([jnp.concatenate(
            [jnp.full((bq, 1), sink_ref[j * 2 * npair + 2 * p + half], F32) for p in range(npair)], axis=0)
            for half in range(2)])

    kcols = slice(D_INNER, D_INNER + SWA_KV_DIM)
    vcols = slice(D_INNER + SWA_KV_DIM, D_INNER + 2 * SWA_KV_DIM)
    for i in range(items * nsub):
        if nsub == 1:
            x = x_ref[i]
            kprev, vprev = kprev_ref[i], vprev_ref[i]
            amask = mask_ref[jnp.where(n == 0, 0, 1)]
            out_rows = (i, slice(None))
        else:
            x = x_ref[0, i * bq:(i + 1) * bq, :]
            if i == 0:
                kprev, vprev = kprev_ref[0], vprev_ref[0]
                amask = mask_ref[jnp.where(n == 0, 0, 1)]
            else:
                kprev = x_ref[0, (i - 1) * bq:i * bq, kcols]
                vprev = x_ref[0, (i - 1) * bq:i * bq, vcols]
                amask = mask_ref[1]
            out_rows = (0, slice(i * bq, (i + 1) * bq))
        fill = [] if bq == w else [jnp.zeros((w - bq, SWA_KV_DIM), F32)]
        kall = jnp.concatenate([kprev, x[:, kcols]] + fill, axis=0)
        vall = jnp.concatenate([vprev, x[:, vcols]] + fill, axis=0)
        if cache_refs:
            cache_refs[0][i] = kall[bq:bq + w]
            cache_refs[1][i] = vall[bq:bq + w]
        for j in range(SWA_KV_HEADS):
            cs = slice((j // 2) * LANE, (j // 2 + 1) * LANE)
            kcol, vcol = kall[:, cs], vall[:, cs]
            swap = lambda a: jnp.concatenate([a[:, SWA_HEAD_DIM:], a[:, 0:SWA_HEAD_DIM]], axis=1)
            k2 = _block_diag_pair(kcol, swap(kcol), j % 2)
            v2 = _block_diag_pair(vcol, swap(vcol), j % 2)
            v2e = jnp.concatenate([v2.astype(BF16), ones_ref[...]], axis=1)
            qbase = j * npair * LANE
            qs = jnp.concatenate([x[:, qbase + p * LANE:qbase + (p + 1) * LANE] for p in range(npair)],
                                 axis=0) * (SWA_HEAD_DIM ** -0.5)
            sc = _dot_nt(qs, k2) + amask
            pes, ms = [], []
            for half in range(2):
                sh = sc[:, half * 2 * w:(half + 1) * 2 * w]
                m = jnp.maximum(jnp.max(sh, axis=-1, keepdims=True), sinks[j][half])
                pes.append(jnp.exp(sh - m).astype(BF16))
                ms.append(m)
            o = jnp.dot(jnp.concatenate(pes, axis=1), v2e, preferred_element_type=F32)
            esink = jnp.exp(jnp.where(lane_lo, sinks[j][0] - ms[0], sinks[j][1] - ms[1]))
            res = o[:, 0:LANE] / (o[:, LANE:] + esink)
            for p in range(npair):
                o_ref[out_rows[0], out_rows[1], qbase + p * LANE:qbase + (p + 1) * LANE] = (
                    res[p * bq:(p + 1) * bq].astype(o_ref.dtype))


def _swa_attn(rest3, kprev, vprev, sinks, has_past):
    b, l, nr = rest3.shape
    w = SWA_WINDOW
    bq = math.gcd(l, w)
    nb = l // bq
    assert nb == 1 or bq == w
    m_later = _swa_mask(bq, False)
    m_first = m_later if has_past else _swa_mask(bq, True)
    masks = jnp.asarray(np.stack([m_first, m_later]))
    ones2 = jnp.asarray((np.arange(4 * w)[:, None] // (2 * w)) == (np.arange(LANE)[None, :] // SWA_HEAD_DIM), BF16)
    if nb == 1:
        rest3 = rest3.astype(F32)
        items, nsub = math.gcd(b, 8), 1
        kspec = pl.BlockSpec((items, w, SWA_KV_DIM), lambda i, n: (i, 0, 0))
        vspec = kspec
        kin, vin = kprev, vprev
    else:
        assert not has_past
        items, nsub = 1, math.gcd(nb, SWA_BLOCKS_PER_STEP)
        kcol = D_INNER // SWA_KV_DIM
        kspec = pl.BlockSpec((1, w, SWA_KV_DIM), lambda i, n: (i, jnp.maximum(n * nsub - 1, 0), kcol))
        vspec = pl.BlockSpec((1, w, SWA_KV_DIM), lambda i, n: (i, jnp.maximum(n * nsub - 1, 0), kcol + 1))
        kin, vin = rest3, rest3
    out_specs = [pl.BlockSpec((items, bq * nsub, D_INNER), lambda i, n: (i, n, 0))]
    out_shape = [jax.ShapeDtypeStruct((b, l, D_INNER), _mixer_out_dtype(bq))]
    if nb == 1:
        out_specs += [pl.BlockSpec((items, w, SWA_KV_DIM), lambda i, n: (i, 0, 0))] * 2
        out_shape += [jax.ShapeDtypeStruct((b, w, SWA_KV_DIM), F32)] * 2
    return pl.pallas_call(
        functools.partial(_swa_kernel, bq=bq, items=items, nsub=nsub),
        grid=(b // items, nb // nsub),
        in_specs=[
            pl.BlockSpec(memory_space=pltpu.SMEM),
            pl.BlockSpec((items, bq * nsub, nr), lambda i, n: (i, n, 0)),
            kspec,
            vspec,
            pl.BlockSpec(masks.shape, lambda i, n: (0, 0, 0)),
            pl.BlockSpec(ones2.shape, lambda i, n: (0, 0)),
        ],
        out_specs=out_specs,
        out_shape=out_shape,
        compiler_params=_cparams(("parallel", "parallel")),
        name="swa_attn",
    )(sinks, rest3, kin, vin, masks, ones2)


def _trunk(xs, sts, p, has_pasts):
    shapes = [x.shape for x in xs]
    x2s = [x.reshape(-1, x.shape[-1]) for x in xs]
    ones_inner = jnp.ones((D_INNER,), F32)
    news = [{} for _ in xs]
    flat = lambda a: a.reshape(-1, a.shape[-1])

    def gla(i, x2s, key):
        proj = _norm_proj("gla", x2s, p[f"l{i}_norm"], p[f"l{i}_w_in"])
        os_ = []
        for g, ((b, l, _), (gate, rest)) in enumerate(zip(shapes, proj)):
            o, news[g][key] = _gla_scan(rest.reshape(b, l, -1), sts[g][key], p[f"l{i}_w_gk2"], p[f"l{i}_b_gk"])
            os_.append(flat(o))
        hn = jnp.tile(p[f"l{i}_head_norm"], GLA_HEADS)
        return _out_proj("gla", os_, [pr[0] for pr in proj], hn, p[f"l{i}_w_out"], x2s, p["final_norm"], i == 3)

    x2s = gla(0, x2s, "gla0")

    proj = _norm_proj("ssd", x2s, p["l1_norm"], p["l1_w_in"])
    os_ = []
    for g, ((b, l, _), (gate, rest, dt_raw)) in enumerate(zip(shapes, proj)):
        st = sts[g]
        rest3 = rest.reshape(b, l, -1)
        y, news[g]["ssm"] = _ssd2_scan(rest3, dt_raw.reshape(b, l, -1), st["ssm"], st["conv"], p["l1_conv_w"],
                                       p["l1_conv_b"], p["l1_dt_bias"], p["l1_a_log"], p["l1_d_skip"])
        ext_tail = jnp.concatenate([st["conv"], rest3[:, max(l - (SSD_CONV - 1), 0):, :].astype(F32)], axis=1)
        news[g]["conv"] = ext_tail[:, -(SSD_CONV - 1):]
        os_.append(flat(y))
    x2s = _out_proj("ssd", os_, [pr[0] for pr in proj], p["l1_gate_norm"], p["l1_w_out"], x2s,
                    p["final_norm"], False)

    proj = _norm_proj("swa", x2s, p["l2_norm"], p["l2_w_in"])
    os_ = []
    for g, ((b, l, _), (gate, rest)) in enumerate(zip(shapes, proj)):
        st = sts[g]
        rest3 = rest.reshape(b, l, -1)
        kprev = st["swa_k"].reshape(b, SWA_WINDOW, SWA_KV_DIM)
        vprev = st["swa_v"].reshape(b, SWA_WINDOW, SWA_KV_DIM)
        res = _swa_attn(rest3, kprev, vprev, p["l2_sinks"], has_pasts[g])
        if len(res) == 3:
            k_win, v_win = res[1], res[2]
        else:
            k_win = rest3[:, l - SWA_WINDOW:, D_INNER:D_INNER + SWA_KV_DIM].astype(F32)
            v_win = rest3[:, l - SWA_WINDOW:, D_INNER + SWA_KV_DIM:].astype(F32)
        news[g]["swa_k"] = k_win.reshape(st["swa_k"].shape)
        news[g]["swa_v"] = v_win.reshape(st["swa_v"].shape)
        os_.append(flat(res[0]))
    x2s = _out_proj("swa", os_, [pr[0] for pr in proj], ones_inner, p["l2_w_out"], x2s, p["final_norm"], False)

    x2s = gla(3, x2s, "gla3")
    return [x2.reshape(s) for x2, s in zip(x2s, shapes)], news


def kernel(x_prompt, x_sample, state_gla_0, state_ssm_1, state_conv_1, cache_swa_k_2, cache_swa_v_2, state_gla_3, l0_norm, l0_w_in, l0_w_gk2, l0_b_gk, l0_head_norm, l0_w_out, l1_norm, l1_w_in, l1_conv_w, l1_conv_b, l1_dt_bias, l1_a_log, l1_d_skip, l1_gate_norm, l1_w_out, l2_norm, l2_w_in, l2_sinks, l2_w_out, l3_norm, l3_w_in, l3_w_gk2, l3_b_gk, l3_head_norm, l3_w_out, final_norm):
    p = dict(l0_norm=l0_norm, l0_w_in=l0_w_in, l0_w_gk2=l0_w_gk2, l0_b_gk=l0_b_gk,
             l0_head_norm=l0_head_norm, l0_w_out=l0_w_out,
             l1_norm=l1_norm, l1_w_in=l1_w_in, l1_conv_w=l1_conv_w, l1_conv_b=l1_conv_b,
             l1_dt_bias=l1_dt_bias, l1_a_log=l1_a_log, l1_d_skip=l1_d_skip,
             l1_gate_norm=l1_gate_norm, l1_w_out=l1_w_out,
             l2_norm=l2_norm, l2_w_in=l2_w_in, l2_sinks=l2_sinks, l2_w_out=l2_w_out,
             l3_norm=l3_norm, l3_w_in=l3_w_in, l3_w_gk2=l3_w_gk2, l3_b_gk=l3_b_gk,
             l3_head_norm=l3_head_norm, l3_w_out=l3_w_out, final_norm=final_norm)

    bp = x_prompt.shape[0]
    z = lambda a: jnp.zeros((bp,) + a.shape[1:], a.dtype)
    st_p = dict(gla0=z(state_gla_0), ssm=z(state_ssm_1), conv=z(state_conv_1),
                swa_k=z(cache_swa_k_2), swa_v=z(cache_swa_v_2), gla3=z(state_gla_3))
    st_s = dict(gla0=state_gla_0, ssm=state_ssm_1, conv=state_conv_1,
                swa_k=cache_swa_k_2, swa_v=cache_swa_v_2, gla3=state_gla_3)
    (y_p, y_s), (n_p, n_s) = _trunk([x_prompt, x_sample], [st_p, st_s], p, [False, True])
    return (y_p, y_s,
            n_p["gla0"], n_s["gla0"],
            n_p["ssm"], n_s["ssm"],
            n_p["conv"], n_s["conv"],
            n_p["swa_k"], n_s["swa_k"],
            n_p["swa_v"], n_s["swa_v"],
            n_p["gla3"], n_s["gla3"])
```

```python
import functools
import math

import numpy as np
import jax
import jax.numpy as jnp
from jax import lax
from jax.experimental import pallas as pl
from jax.experimental.pallas import tpu as pltpu

F32 = jnp.float32
BF16 = jnp.bfloat16

NORM_EPS = 1e-6
D_MODEL = 1024
D_INNER = 2048
CHUNK = 64
LANE = 128
BF16_SUBLANES = 16
NEG_BIG = -1e30

GLA_HEADS = 4
GLA_HEAD_K = 128
GLA_HEAD_V = 512
GLA_KEY_DIM = 512
GLA_GATE_RANK = 16
GLA_GATE_NORMALIZER = 16.0

SSD_GROUPS = 4
SSD_HEADS = 32
SSD_HEAD_DIM = 64
SSD_STATE = 128
SSD_CONV = 4
SSD_CONV_DIM = 3072
SSD_GROUP_W = D_INNER // SSD_GROUPS

SWA_WINDOW = 128
SWA_KV_HEADS = 4
SWA_HEAD_DIM = 64
SWA_Q_HEADS = 32
SWA_KV_DIM = 256

VMEM_LIMIT = 52 * 1024 * 1024


def _cparams(sem):
    return pltpu.CompilerParams(dimension_semantics=sem, vmem_limit_bytes=VMEM_LIMIT)


def _dot(a, b):
    return jnp.dot(a.astype(BF16), b.astype(BF16), preferred_element_type=F32)


def _dot_nt(a, b):
    return lax.dot_general(a.astype(BF16), b.astype(BF16), (((1,), (1,)), ((), ())),
                           preferred_element_type=F32)


def _split3(x):
    hi = x.astype(BF16)
    r1 = x - hi.astype(F32)
    mid = r1.astype(BF16)
    lo = (r1 - mid.astype(F32)).astype(BF16)
    return hi, mid, lo


def _sigmoid(x):
    return 1.0 / (1.0 + jnp.exp(-x))


def _softplus(x):
    return jnp.maximum(x, 0.0) + jnp.log(1.0 + jnp.exp(-jnp.abs(x)))


def _mixer_out_dtype(block_rows):
    return BF16 if block_rows % BF16_SUBLANES == 0 else F32


PROJ_ROWS_PER_STEP = 512
PROJ_CAST_ROWS = 256


def _round_up(n, m):
    return -(-n // m) * m


def _group_steps(row_counts, tm):
    n = [t // min(tm, t) for t in row_counts]
    starts = [sum(n[:g]) for g in range(len(n))]
    return n, starts


def _group_row_map(start, nsteps):
    return lambda i: (jnp.clip(i - start, 0, nsteps - 1), 0)


def _norm_proj_kernel(*refs, layout, group_steps):
    step = pl.program_id(0)
    aux_piece = layout["aux"]
    ngroups = len(group_steps)
    nout = 2 if aux_piece is None else 3
    x_refs = refs[0:ngroups]
    nw_ref, w_ref = refs[ngroups], refs[ngroups + 1]
    out_refs = refs[ngroups + 2:ngroups + 2 + nout * ngroups]
    scr = refs[ngroups + 2 + nout * ngroups:]
    wg_scr, wr_scr = scr[0], scr[1]
    wa_scr = scr[2] if aux_piece is not None else None
    d = w_ref.shape[0]

    def cast_piece(dst, dst_off, src_off, width):
        wpad = _round_up(width, LANE)
        for r in range(0, d, PROJ_CAST_ROWS):
            piece = w_ref[r:r + PROJ_CAST_ROWS, src_off:src_off + width]
            if wpad > width:
                piece = jnp.concatenate([piece, jnp.zeros((PROJ_CAST_ROWS, wpad - width), F32)], axis=1)
            dst[r:r + PROJ_CAST_ROWS, dst_off:dst_off + wpad] = piece.astype(BF16)
        return wpad

    @pl.when(step == 0)
    def _():
        cast_piece(wg_scr, 0, *layout["gate"])
        off = 0
        for src_off, width in layout["rest"]:
            off += cast_piece(wr_scr, off, src_off, width)
        if aux_piece is not None:
            cast_piece(wa_scr, 0, *aux_piece)

    def project(x_ref, outs):
        x = x_ref[...]
        ms = jnp.mean(x * x, axis=-1, keepdims=True)
        h = (x * lax.rsqrt(ms + NORM_EPS) * nw_ref[...]).astype(BF16)
        outs[0][...] = jnp.dot(h, wg_scr[...], preferred_element_type=F32).astype(outs[0].dtype)
        outs[1][...] = jnp.dot(h, wr_scr[...], preferred_element_type=F32).astype(outs[1].dtype)
        if aux_piece is not None:
            outs[2][...] = jnp.dot(h, wa_scr[...], preferred_element_type=F32)

    for g, (start, nsteps) in enumerate(group_steps):
        pl.when((step >= start) & (step < start + nsteps))(
            functools.partial(project, x_refs[g], out_refs[g * nout:(g + 1) * nout]))


def _proj_layout(kind):
    kd, di = GLA_KEY_DIM, D_INNER
    if kind == "gla":
        return dict(gate=(2 * kd + di, di),
                    rest=[(2 * kd, di), (0, kd), (kd, kd), (2 * kd + 2 * di, GLA_GATE_RANK)], aux=None)
    if kind == "ssd":
        return dict(gate=(0, di), rest=[(di, SSD_CONV_DIM)], aux=(di + SSD_CONV_DIM, SSD_HEADS))
    return dict(gate=(di + 2 * SWA_KV_DIM, di), rest=[(0, di + 2 * SWA_KV_DIM)], aux=None)


def _norm_proj(kind, xs, nw, w_in):
    d = xs[0].shape[1]
    layout = _proj_layout(kind)
    ng = layout["gate"][1]
    nr = sum(_round_up(w, LANE) for _, w in layout["rest"])
    has_aux = layout["aux"] is not None
    nsteps, starts = _group_steps([x.shape[0] for x in xs], PROJ_ROWS_PER_STEP)
    fix = lambda i: (0, 0)
    in_specs, out_specs, out_shape = [], [], []
    for x, n, s in zip(xs, nsteps, starts):
        t = x.shape[0]
        tm = t // n
        row = _group_row_map(s, n)
        in_specs.append(pl.BlockSpec((tm, d), row))
        out_specs += [pl.BlockSpec((tm, ng), row), pl.BlockSpec((tm, nr), row)]
        out_shape += [jax.ShapeDtypeStruct((t, ng), BF16), jax.ShapeDtypeStruct((t, nr), BF16)]
        if has_aux:
            out_specs.append(pl.BlockSpec((tm, LANE), row))
            out_shape.append(jax.ShapeDtypeStruct((t, LANE), F32))
    in_specs += [pl.BlockSpec((1, d), fix), pl.BlockSpec(w_in.shape, fix, pipeline_mode=pl.Buffered(1))]
    scratch = [pltpu.VMEM((d, ng), BF16), pltpu.VMEM((d, nr), BF16)]
    if has_aux:
        scratch.append(pltpu.VMEM((d, LANE), BF16))
    outs = pl.pallas_call(
        functools.partial(_norm_proj_kernel, layout=layout, group_steps=tuple(zip(starts, nsteps))),
        grid=(sum(nsteps),),
        in_specs=in_specs,
        out_specs=out_specs,
        out_shape=out_shape,
        scratch_shapes=scratch,
        compiler_params=_cparams(("arbitrary",)),
        name="norm_proj_" + kind,
    )(*xs, nw.reshape(1, d), w_in)
    nout = 3 if has_aux else 2
    return [outs[g * nout:(g + 1) * nout] for g in range(len(xs))]


def _seg_rms(y, seg):
    parts = []
    for s in range(y.shape[1] // seg):
        p = y[:, s * seg:(s + 1) * seg]
        ms = jnp.mean(p * p, axis=-1, keepdims=True)
        parts.append(p * lax.rsqrt(ms + NORM_EPS))
    return jnp.concatenate(parts, axis=1)


def _out_proj_kernel(*refs, mode, final, group_steps):
    ngroups = len(group_steps)
    nw_ref, w_ref, fw_ref = refs[3 * ngroups:3 * ngroups + 3]
    out_refs = refs[3 * ngroups + 3:4 * ngroups + 3]
    w_scr = refs[4 * ngroups + 3]
    step = pl.program_id(0)

    @pl.when(step == 0)
    def _():
        for r in range(0, w_ref.shape[0], PROJ_CAST_ROWS):
            w_scr[r:r + PROJ_CAST_ROWS, :] = w_ref[r:r + PROJ_CAST_ROWS, :].astype(BF16)

    def project(o_ref, gate_ref, x_ref, out_ref):
        o = o_ref[...].astype(F32)
        gt = gate_ref[...].astype(F32)
        act = gt * _sigmoid(gt)
        if mode == "gla":
            y = _seg_rms(o, GLA_HEAD_V) * nw_ref[...] * act
        elif mode == "ssd":
            y = _seg_rms(o * act, SSD_GROUP_W) * nw_ref[...]
        else:
            y = o * act
        out = x_ref[...] + jnp.dot(y.astype(BF16), w_scr[...], preferred_element_type=F32)
        if final:
            ms = jnp.mean(out * out, axis=-1, keepdims=True)
            out = out * lax.rsqrt(ms + NORM_EPS) * fw_ref[...]
        out_ref[...] = out

    for g, (start, nsteps) in enumerate(group_steps):
        pl.when((step >= start) & (step < start + nsteps))(
            functools.partial(project, refs[3 * g], refs[3 * g + 1], refs[3 * g + 2], out_refs[g]))


def _out_proj(mode, os_, gates, nw, w_out, xs, fw, final):
    di = os_[0].shape[1]
    d = xs[0].shape[1]
    nsteps, starts = _group_steps([x.shape[0] for x in xs], PROJ_ROWS_PER_STEP)
    fix = lambda i: (0, 0)
    in_specs, out_specs, out_shape, operands = [], [], [], []
    for o2, g2, x2, n, s in zip(os_, gates, xs, nsteps, starts):
        t = x2.shape[0]
        tm = t // n
        row = _group_row_map(s, n)
        in_specs += [pl.BlockSpec((tm, di), row), pl.BlockSpec((tm, di), row), pl.BlockSpec((tm, d), row)]
        operands += [o2, g2, x2]
        out_specs.append(pl.BlockSpec((tm, d), row))
        out_shape.append(jax.ShapeDtypeStruct((t, d), F32))
    in_specs += [pl.BlockSpec((1, di), fix), pl.BlockSpec((di, d), fix), pl.BlockSpec((1, d), fix)]
    return pl.pallas_call(
        functools.partial(_out_proj_kernel, mode=mode, final=final, group_steps=tuple(zip(starts, nsteps))),
        grid=(sum(nsteps),),
        in_specs=in_specs,
        out_specs=out_specs,
        out_shape=out_shape,
        scratch_shapes=[pltpu.VMEM((di, d), BF16)],
        compiler_params=_cparams(("arbitrary",)),
        name="out_proj_" + mode,
    )(*operands, nw.reshape(1, di), w_out, fw.reshape(1, d))


GLA_CHUNKS_PER_STEP = 4
GLA_LONG_CHUNK = 128
GLA_SINGLE_ANCHOR_MAX_DECAY = 60.0


def _gla_levels(seglen):
    return tuple(seglen >> (i + 1) for i in range(int(math.log2(seglen))))


def _gla_consts(seglen, c):
    levels = _gla_levels(seglen)
    u = np.arange(c)[:, None]
    j = np.arange(c)[None, :]
    same = (u // seglen) == (j // seglen)
    blocks = [same & (j <= u), same & (j > u)]
    lev = np.full((c, c), len(levels) + 1, np.int32)
    for li, h in enumerate(levels):
        b = (u // (2 * h)) * (2 * h) + h - 1
        blocks.append((j > np.minimum(u, b)) & (j <= np.maximum(u, b)))
        sib = (u // (2 * h) == j // (2 * h)) & (u % (2 * h) >= h) & (j % (2 * h) < h)
        lev[sib] = li
    lev[np.eye(c, dtype=bool)] = len(levels)
    m = np.concatenate(blocks, axis=0).astype(np.float32)
    m3 = np.concatenate([m, m, m], axis=1)
    if 2 * c == LANE:
        lev = np.concatenate([lev, lev], axis=1)
    return jnp.asarray(m3, BF16), jnp.asarray(lev)


def _gla_scores(qx, kx, e_q, e_k, lev, nt):
    att = None
    for li, (eq, ek) in enumerate(zip(e_q, e_k)):
        a = nt(qx if eq is None else qx * eq, kx if ek is None else kx * ek)
        att = jnp.where(lev == li, a, 0.0 if att is None else att)
    return att


def _nt_pair(qe, ke):
    kb = ke.astype(BF16)
    z = jnp.zeros((kb.shape[0], GLA_HEAD_K), BF16)
    rhs = jnp.concatenate([jnp.concatenate([kb[:, 0:GLA_HEAD_K], z], axis=1),
                           jnp.concatenate([z, kb[:, GLA_HEAD_K:]], axis=1)], axis=0)
    return lax.dot_general(qe.astype(BF16), rhs, (((1,), (1,)), ((), ())), preferred_element_type=F32)


def _gla_prep(xc, wgk, bgk):
    v = xc[:, 0:D_INNER]
    q = xc[:, D_INNER:D_INNER + GLA_KEY_DIM].astype(F32) * (GLA_HEAD_K ** -0.5)
    k = xc[:, D_INNER + GLA_KEY_DIM:D_INNER + 2 * GLA_KEY_DIM].astype(F32)
    lr = xc[:, D_INNER + 2 * GLA_KEY_DIM:]
    z = _dot(lr, wgk) + bgk
    g = (jnp.minimum(z, 0.0) - jnp.log(1.0 + jnp.exp(-jnp.abs(z)))) * (1.0 / GLA_GATE_NORMALIZER)
    return v, q, k, g


def _gla_chunk(v, q, k, g, state_io, m3, lev, nl, single_anchor):
    c = g.shape[0]
    g3 = jnp.concatenate(_split3(g), axis=0)
    if single_anchor:
        rs = jnp.dot(m3[0:2 * c], g3, preferred_element_type=F32)
        cum = rs[0:c]
        e_in, e_out = jnp.exp(cum), jnp.exp(rs[c:2 * c])
        e_q, e_k = [e_in], [jnp.exp(-cum)]
        lev = jnp.where(lev <= nl, 0, 1)
    else:
        e_all = jnp.exp(jnp.dot(m3, g3, preferred_element_type=F32))
        e_in, e_out = e_all[0:c], e_all[c:2 * c]
        e_q = [e_all[(2 + li) * c:(3 + li) * c] for li in range(nl)] + [None]
        e_k = e_q
    qe_in = q * e_in
    ke_out = k * e_out
    outs = []
    if c == LANE:
        for h in range(GLA_HEADS):
            ks = slice(h * GLA_HEAD_K, (h + 1) * GLA_HEAD_K)
            cut = lambda es: [None if e is None else e[:, ks] for e in es]
            att = _gla_scores(q[:, ks], k[:, ks], cut(e_q), cut(e_k), lev, _dot_nt)
            v_h = v[:, h * GLA_HEAD_V:(h + 1) * GLA_HEAD_V]
            xt = jnp.concatenate([ke_out[:, ks], e_in[:, ks]], axis=0).T
            outs.append(_dot(att, v_h) + state_io(h, qe_in[:, ks], xt, v_h))
        return jnp.concatenate(outs, axis=1)
    lane_lo = lax.broadcasted_iota(jnp.int32, (c, LANE), 1) < c
    for pr in range(GLA_HEADS // 2):
        ls = slice(2 * pr * GLA_HEAD_K, (2 * pr + 2) * GLA_HEAD_K)
        cut = lambda es: [None if e is None else e[:, ls] for e in es]
        att = _gla_scores(q[:, ls], k[:, ls], cut(e_q), cut(e_k), lev, _nt_pair)
        vpair = v[:, 2 * pr * GLA_HEAD_V:(2 * pr + 2) * GLA_HEAD_V]
        vcat = jnp.concatenate([vpair[:, 0:GLA_HEAD_V], vpair[:, GLA_HEAD_V:]], axis=0)
        for half, att_h in enumerate((jnp.where(lane_lo, att, 0.0), jnp.where(lane_lo, 0.0, att))):
            h = 2 * pr + half
            ks = slice(h * GLA_HEAD_K, (h + 1) * GLA_HEAD_K)
            xt = jnp.concatenate([ke_out[:, ks], e_in[:, ks]], axis=0).T
            o_inter = state_io(h, qe_in[:, ks], xt, v[:, h * GLA_HEAD_V:(h + 1) * GLA_HEAD_V])
            outs.append(_dot(att_h, vcat) + o_inter)
    return jnp.concatenate(outs, axis=1)


def _gla_dispatch(preps, run):
    low = None
    for _, _, _, g in preps:
        tot = jnp.min(jnp.sum(g, axis=0, keepdims=True))
        low = tot if low is None else jnp.minimum(low, tot)
    single_anchor_ok = low >= -GLA_SINGLE_ANCHOR_MAX_DECAY
    pl.when(single_anchor_ok)(functools.partial(run, True))
    pl.when(jnp.logical_not(single_anchor_ok))(functools.partial(run, False))


def _gla_carry_kernel(rest_ref, s0_ref, wgk_ref, bgk_ref, m_ref, lev_ref, o_ref, s_ref, *, c, nch, nl):
    @pl.when(pl.program_id(1) == 0)
    def _():
        s_ref[...] = s0_ref[...]

    def state_io(h, qe_in_h, xt, v_h):
        s_h = s_ref[0, h]
        s_ref[0, h] = s_h * xt[:, 2 * c - 1:2 * c] + _dot(xt[:, 0:c], v_h)
        return _dot(qe_in_h, s_h)

    wgk, bgk = wgk_ref[...], bgk_ref[...]
    preps = [_gla_prep(rest_ref[0, ci * c:(ci + 1) * c, :], wgk, bgk) for ci in range(nch)]

    def run(single_anchor):
        for ci, (v, q, k, g) in enumerate(preps):
            o = _gla_chunk(v, q, k, g, state_io, m_ref[...], lev_ref[...], nl, single_anchor)
            o_ref[0, ci * c:(ci + 1) * c, :] = o.astype(o_ref.dtype)

    _gla_dispatch(preps, run)


def _gla_packed_kernel(rest_ref, s0_ref, wgk_ref, bgk_ref, m_ref, lev_ref, o_ref, s_ref, *, seglen, nl):
    c = CHUNK
    nseg = c // seglen
    seg_of_lane = lax.broadcasted_iota(jnp.int32, (GLA_HEAD_K, c), 1) // seglen

    def state_io(h, qe_in_h, xt, v_h):
        ke_t = xt[:, 0:c]
        outs = []
        for sg in range(nseg):
            s_h = s0_ref[sg, h]
            outs.append(_dot(qe_in_h[sg * seglen:(sg + 1) * seglen], s_h))
            end = c + (sg + 1) * seglen - 1
            s_ref[sg, h] = s_h * xt[:, end:end + 1] + _dot(jnp.where(seg_of_lane == sg, ke_t, 0.0), v_h)
        return jnp.concatenate(outs, axis=0)

    preps = [_gla_prep(rest_ref[...], wgk_ref[...], bgk_ref[...])]

    def run(single_anchor):
        v, q, k, g = preps[0]
        o = _gla_chunk(v, q, k, g, state_io, m_ref[...], lev_ref[...], nl, single_anchor)
        o_ref[...] = o.astype(o_ref.dtype)

    _gla_dispatch(preps, run)


def _gla_scan(rest3, s0, w_gk2, b_gk):
    b, l, nr = rest3.shape
    packed = l < CHUNK
    c = CHUNK if packed or l % GLA_LONG_CHUNK else GLA_LONG_CHUNK
    seglen = l if packed else c
    assert c % seglen == 0 and seglen % 8 == 0 and (l % c == 0 or packed)
    nl = len(_gla_levels(seglen))
    m01, lev = _gla_consts(seglen, c)
    wgk = jnp.zeros((LANE, GLA_KEY_DIM), F32).at[:GLA_GATE_RANK].set(w_gk2).astype(BF16)
    consts = (wgk, b_gk.reshape(1, GLA_KEY_DIM), m01, lev)
    if packed:
        nseq = c // seglen
        assert b % nseq == 0
        sblk = (nseq, GLA_HEADS, GLA_HEAD_K, GLA_HEAD_V)
        o, s_fin = pl.pallas_call(
            functools.partial(_gla_packed_kernel, seglen=seglen, nl=nl),
            grid=(b // nseq,),
            in_specs=[pl.BlockSpec((c, nr), lambda i: (i, 0)),
                      pl.BlockSpec(sblk, lambda i: (i, 0, 0, 0))]
                     + [pl.BlockSpec(a.shape, lambda i: (0, 0)) for a in consts],
            out_specs=[pl.BlockSpec((c, D_INNER), lambda i: (i, 0)),
                       pl.BlockSpec(sblk, lambda i: (i, 0, 0, 0))],
            out_shape=[jax.ShapeDtypeStruct((b * l, D_INNER), BF16), jax.ShapeDtypeStruct(s0.shape, F32)],
            compiler_params=_cparams(("parallel",)),
            name="gla_scan_packed",
        )(rest3.reshape(b * l, nr), s0, *consts)
        return o.reshape(b, l, D_INNER), s_fin
    nch = math.gcd(l // c, GLA_CHUNKS_PER_STEP)
    sblk = (1, GLA_HEADS, GLA_HEAD_K, GLA_HEAD_V)
    return pl.pallas_call(
        functools.partial(_gla_carry_kernel, c=c, nch=nch, nl=nl),
        grid=(b, l // (c * nch)),
        in_specs=[pl.BlockSpec((1, c * nch, nr), lambda i, t: (i, t, 0)),
                  pl.BlockSpec(sblk, lambda i, t: (i, 0, 0, 0))]
                 + [pl.BlockSpec(a.shape, lambda i, t: (0, 0)) for a in consts],
        out_specs=[pl.BlockSpec((1, c * nch, D_INNER), lambda i, t: (i, t, 0)),
                   pl.BlockSpec(sblk, lambda i, t: (i, 0, 0, 0))],
        out_shape=[jax.ShapeDtypeStruct((b, l, D_INNER), BF16), jax.ShapeDtypeStruct(s0.shape, F32)],
        compiler_params=_cparams(("parallel", "arbitrary")),
        name="gla_scan",
    )(rest3, s0, *consts)


SSD_CHUNKS_PER_STEP = 4


def _ssd2_consts(seglen):
    c = CHUNK
    u = np.arange(c)
    same = (u[:, None] // seglen) == (u[None, :] // seglen)
    tl = same & (u[None, :] <= u[:, None])
    m = np.concatenate([tl, same], axis=0).astype(np.float32)
    m3 = np.concatenate([m, m, m], axis=1)
    expand = (np.arange(LANE)[:, None] == (np.arange(D_INNER) // SSD_HEAD_DIM)[None, :]).astype(np.float32)
    ex3 = np.concatenate([expand] * 3, axis=0)
    s_of = np.arange(LANE) % c
    tri = np.where(same[:, s_of] & (s_of[None, :] <= u[:, None]), 0.0, NEG_BIG).astype(np.float32)
    bd = ((np.arange(2 * c)[:, None] // c) == (np.arange(LANE)[None, :] // SSD_HEAD_DIM)).astype(np.float32)
    return jnp.asarray(m3, BF16), jnp.asarray(ex3, BF16), jnp.asarray(tri), jnp.asarray(bd)


def _ssd2_conv(ext_ref, rows, cw_ref, cb_ref):
    e = ext_ref[0:8 + rows, :]
    acc = cw_ref[SSD_CONV - 1:SSD_CONV, :] * e[8:]
    for k in range(1, SSD_CONV):
        acc = acc + cw_ref[SSD_CONV - 1 - k:SSD_CONV - k, :] * pltpu.roll(e, k, axis=0)[8:]
    return cb_ref[...] + acc


def _ssd2_chunk(dt_raw, conv, dtb, a_neg, dsk, m3, ex3, tri, bd, state_io):
    c = CHUNK
    xbc = conv * _sigmoid(conv)
    xs = xbc[:, 0:D_INNER]
    bm = xbc[:, D_INNER:D_INNER + SSD_GROUPS * SSD_STATE]
    cm = xbc[:, D_INNER + SSD_GROUPS * SSD_STATE:]
    dt = _softplus(dt_raw + dtb)
    la = dt * a_neg
    rs = jnp.dot(m3, jnp.concatenate(_split3(la), axis=0), preferred_element_type=F32)
    cum, tot = rs[0:c], rs[c:2 * c]
    ecum = jnp.exp(cum)
    stack = jnp.concatenate([dt, dt * jnp.exp(tot - cum), cum, ecum], axis=0)
    rep = jnp.dot(jnp.concatenate(_split3(stack), axis=1), ex3, preferred_element_type=F32)
    dt_rep, dtw_rep, cum_rep, ecum_rep = (rep[i * c:(i + 1) * c] for i in range(4))
    u = xs * dt_rep
    uw = xs * dtw_rep
    at = jnp.concatenate([cum, cum], axis=0).T
    lane_lo = lax.broadcasted_iota(jnp.int32, (1, LANE), 1) < c
    ys = []
    for g in range(SSD_GROUPS):
        gs = slice(g * SSD_GROUP_W, (g + 1) * SSD_GROUP_W)
        ns = slice(g * SSD_STATE, (g + 1) * SSD_STATE)
        cg, bg = cm[:, ns], bm[:, ns]
        cb_rep = _dot_nt(cg, jnp.concatenate([bg] * (SSD_GROUP_W // c), axis=0))
        parts = []
        for p in range(SSD_GROUP_W // LANE):
            col = g * (SSD_GROUP_W // LANE) + p
            ps = slice(col * LANE, (col + 1) * LANE)
            cum_s = jnp.where(lane_lo, at[2 * col:2 * col + 1, :], at[2 * col + 1:2 * col + 2, :])
            w = jnp.exp(cum_rep[:, ps] - cum_s + tri) * cb_rep[:, p * LANE:(p + 1) * LANE]
            up = u[:, ps]
            parts.append(_dot(w, jnp.concatenate([up, up], axis=0) * bd))
        y_inter = state_io(g, cg, bg, uw[:, gs], ecum_rep[:, gs])
        ys.append(jnp.concatenate(parts, axis=1) + y_inter + xs[:, gs] * dsk[:, gs])
    return jnp.concatenate(ys, axis=1)


def _ssd2_carry_kernel(rest_ref, dtr_ref, s0_ref, conv0_ref, cw_ref, cb_ref, dtb_ref, alog_ref, dsk_ref,
                       m3_ref, ex3_ref, tri_ref, bd_ref, y_ref, sfin_ref, ext_ref, st_ref, *, nch):
    c = CHUNK
    t = pl.program_id(1)

    @pl.when(t == 0)
    def _():
        ext_ref[0:8, :] = conv0_ref[0]
        for g in range(SSD_GROUPS):
            st_ref[g] = s0_ref[0, g].T

    a_neg = -jnp.exp(alog_ref[...])

    def state_io(g, cg, bg, uw_g, ecum_g):
        s_g = st_ref[g]
        y_inter = _dot(cg, s_g) * ecum_g
        bg_t = jnp.concatenate([bg, jnp.zeros_like(bg)], axis=0).T[:, 0:c]
        st_ref[g] = s_g * ecum_g[c - 1:c, :] + _dot(bg_t, uw_g)
        return y_inter

    for ci in range(nch):
        ext_ref[8:8 + c, :] = rest_ref[0, ci * c:(ci + 1) * c, :].astype(F32)
        conv = _ssd2_conv(ext_ref, c, cw_ref, cb_ref)
        ext_ref[0:8, :] = ext_ref[c:c + 8, :]
        y = _ssd2_chunk(dtr_ref[0, ci * c:(ci + 1) * c, :], conv, dtb_ref[...], a_neg, dsk_ref[...],
                        m3_ref[...], ex3_ref[...], tri_ref[...], bd_ref[...], state_io)
        y_ref[0, ci * c:(ci + 1) * c, :] = y.astype(y_ref.dtype)

    @pl.when(t == pl.num_programs(1) - 1)
    def _():
        for g in range(SSD_GROUPS):
            sfin_ref[0, g] = st_ref[g].T


def _ssd2_packed_kernel(rest_ref, dtr_ref, s0_ref, conv0_ref, cw_ref, cb_ref, dtb_ref, alog_ref, dsk_ref,
                        m3_ref, ex3_ref, tri_ref, bd_ref, y_ref, sfin_ref, ext_ref, *, seglen):
    c = CHUNK
    nseg = c // seglen
    ext_ref[0:8, :] = jnp.zeros((8, SSD_CONV_DIM), F32)
    xbc_raw = rest_ref[...].astype(F32)
    for sg in range(nseg):
        ext_ref[8 + 16 * sg:16 + 16 * sg, :] = conv0_ref[sg]
        ext_ref[16 + 16 * sg:24 + 16 * sg, :] = xbc_raw[sg * seglen:(sg + 1) * seglen]
    conv2 = _ssd2_conv(ext_ref, 2 * c, cw_ref, cb_ref)
    conv = jnp.concatenate([conv2[16 * sg + 8:16 * sg + 16] for sg in range(nseg)], axis=0)
    a_neg = -jnp.exp(alog_ref[...])
    seg_of_row = lax.broadcasted_iota(jnp.int32, (c, SSD_STATE), 0) // seglen

    def state_io(g, cg, bg, uw_g, ecum_g):
        ends = jnp.concatenate([ecum_g[(sg + 1) * seglen - 1:(sg + 1) * seglen, :] for sg in range(nseg)], axis=0)
        fill = jnp.zeros((LANE - c - nseg, SSD_GROUP_W), F32)
        tg = jnp.concatenate([uw_g, ends, fill], axis=0).T
        uw_t = tg[:, 0:c]
        y_parts = []
        for sg in range(nseg):
            s_sg = s0_ref[sg, g]
            y_parts.append(_dot_nt(cg[sg * seglen:(sg + 1) * seglen], s_sg))
            b_sg = jnp.where(seg_of_row == sg, bg, 0.0)
            sfin_ref[sg, g] = s_sg * tg[:, c + sg:c + sg + 1] + _dot(uw_t, b_sg)
        return jnp.concatenate(y_parts, axis=0) * ecum_g

    y = _ssd2_chunk(dtr_ref[...], conv, dtb_ref[...], a_neg, dsk_ref[...], m3_ref[...], ex3_ref[...],
                    tri_ref[...], bd_ref[...], state_io)
    y_ref[...] = y.astype(y_ref.dtype)


def _ssd2_scan(rest3, dt_raw3, s0, conv0, conv_w, conv_b, dt_bias, a_log, d_skip):
    b, l, nr = rest3.shape
    c = CHUNK
    packed = l < c
    seglen = l if packed else c
    assert c % seglen == 0 and seglen % 8 == 0 and (l % c == 0 or packed)
    m3, ex3, tri, bd = _ssd2_consts(seglen)
    s0g = s0.reshape(b, SSD_GROUPS, SSD_GROUP_W, SSD_STATE)
    conv0p = jnp.concatenate([jnp.zeros((b, 8 - (SSD_CONV - 1), SSD_CONV_DIM), F32), conv0], axis=1)
    pad = lambda a: jnp.zeros((1, LANE), F32).at[0, :SSD_HEADS].set(a)
    dsk = jnp.repeat(d_skip, SSD_HEAD_DIM).reshape(1, D_INNER)
    consts = (conv_w, conv_b.reshape(1, -1), pad(dt_bias), pad(a_log), dsk, m3, ex3, tri, bd)
    if packed:
        nseq = c // seglen
        assert b % nseq == 0
        fix = lambda i: (0, 0)
        const_specs = [pl.BlockSpec(a.shape, fix) for a in consts]
        sblk = (nseq, SSD_GROUPS, SSD_GROUP_W, SSD_STATE)
        y, sfin = pl.pallas_call(
            functools.partial(_ssd2_packed_kernel, seglen=seglen),
            grid=(b // nseq,),
            in_specs=[pl.BlockSpec((c, nr), lambda i: (i, 0)),
                      pl.BlockSpec((c, LANE), lambda i: (i, 0)),
                      pl.BlockSpec(sblk, lambda i: (i, 0, 0, 0)),
                      pl.BlockSpec((nseq, 8, SSD_CONV_DIM), lambda i: (i, 0, 0))] + const_specs,
            out_specs=[pl.BlockSpec((c, D_INNER), lambda i: (i, 0)),
                       pl.BlockSpec(sblk, lambda i: (i, 0, 0, 0))],
            out_shape=[jax.ShapeDtypeStruct((b * l, D_INNER), BF16), jax.ShapeDtypeStruct(s0g.shape, F32)],
            scratch_shapes=[pltpu.VMEM((2 * c + 8, SSD_CONV_DIM), F32)],
            compiler_params=_cparams(("parallel",)),
            name="ssd_scan_packed",
        )(rest3.reshape(b * l, nr), dt_raw3.reshape(b * l, LANE), s0g, conv0p, *consts)
        return y.reshape(b, l, D_INNER), sfin.reshape(s0.shape)
    nch = math.gcd(l // c, SSD_CHUNKS_PER_STEP)
    fix2 = lambda i, t: (0, 0)
    const_specs = [pl.BlockSpec(a.shape, fix2) for a in consts]
    sblk = (1, SSD_GROUPS, SSD_GROUP_W, SSD_STATE)
    y, sfin = pl.pallas_call(
        functools.partial(_ssd2_carry_kernel, nch=nch),
        grid=(b, l // (c * nch)),
        in_specs=[pl.BlockSpec((1, c * nch, nr), lambda i, t: (i, t, 0)),
                  pl.BlockSpec((1, c * nch, LANE), lambda i, t: (i, t, 0)),
                  pl.BlockSpec(sblk, lambda i, t: (i, 0, 0, 0)),
                  pl.BlockSpec((1, 8, SSD_CONV_DIM), lambda i, t: (i, 0, 0))] + const_specs,
        out_specs=[pl.BlockSpec((1, c * nch, D_INNER), lambda i, t: (i, t, 0)),
                   pl.BlockSpec(sblk, lambda i, t: (i, 0, 0, 0))],
        out_shape=[jax.ShapeDtypeStruct((b, l, D_INNER), BF16), jax.ShapeDtypeStruct(s0g.shape, F32)],
        scratch_shapes=[pltpu.VMEM((c + 8, SSD_CONV_DIM), F32),
                        pltpu.VMEM((SSD_GROUPS, SSD_STATE, SSD_GROUP_W), F32)],
        compiler_params=_cparams(("parallel", "arbitrary")),
        name="ssd_scan",
    )(rest3, dt_raw3, s0g, conv0p, *consts)
    return y, sfin.reshape(s0.shape)


SWA_BLOCKS_PER_STEP = 2


def _swa_mask(bq, first_block_has_no_past):
    rows = 4 * bq
    tq = (np.arange(rows) % bq)[:, None]
    s = (np.arange(4 * SWA_WINDOW) % (2 * SWA_WINDOW))[None, :]
    ok = (s > tq) & (s <= tq + SWA_WINDOW)
    if first_block_has_no_past:
        ok = ok & (s >= SWA_WINDOW)
    return np.where(ok, 0.0, NEG_BIG).astype(np.float32)


def _block_diag_pair(col, rolled, odd):
    lane = lax.broadcasted_iota(jnp.int32, col.shape, 1)
    lo = lane < SWA_HEAD_DIM
    if odd:
        top = jnp.where(lo, rolled, 0.0)
        bot = jnp.where(lo, 0.0, col)
    else:
        top = jnp.where(lo, col, 0.0)
        bot = jnp.where(lo, 0.0, rolled)
    return jnp.concatenate([top, bot], axis=0)


def _swa_kernel(sink_ref, x_ref, kprev_ref, vprev_ref, mask_ref, ones_ref, o_ref, *cache_refs, bq, items, nsub):
    w = SWA_WINDOW
    n = pl.program_id(1)
    npair = (SWA_Q_HEADS // SWA_KV_HEADS) // 2
    rows = npair * bq
    lane_lo = lax.broadcasted_iota(jnp.int32, (rows, LANE), 1) < SWA_HEAD_DIM
    sinks = []
    for j in range(SWA_KV_HEADS):
        sinks.append([jnp.concatenate(
            [jnp.full((bq, 1), sink_ref[j * 2 * npair + 2 * p + half], F32) for p in range(npair)], axis=0)
            for half in range(2)])

    kcols = slice(D_INNER, D_INNER + SWA_KV_DIM)
    vcols = slice(D_INNER + SWA_KV_DIM, D_INNER + 2 * SWA_KV_DIM)
    for i in range(items * nsub):
        if nsub == 1:
            x = x_ref[i]
            kprev, vprev = kprev_ref[i], vprev_ref[i]
            amask = mask_ref[jnp.where(n == 0, 0, 1)]
            out_rows = (i, slice(None))
        else:
            x = x_ref[0, i * bq:(i + 1) * bq, :]
            if i == 0:
                kprev, vprev = kprev_ref[0], vprev_ref[0]
                amask = mask_ref[jnp.where(n == 0, 0, 1)]
            else:
                kprev = x_ref[0, (i - 1) * bq:i * bq, kcols]
                vprev = x_ref[0, (i - 1) * bq:i * bq, vcols]
                amask = mask_ref[1]
            out_rows = (0, slice(i * bq, (i + 1) * bq))
        fill = [] if bq == w else [jnp.zeros((w - bq, SWA_KV_DIM), F32)]
        kall = jnp.concatenate([kprev, x[:, kcols]] + fill, axis=0)
        vall = jnp.concatenate([vprev, x[:, vcols]] + fill, axis=0)
        if cache_refs:
            cache_refs[0][i] = kall[bq:bq + w]
            cache_refs[1][i] = vall[bq:bq + w]
        for j in range(SWA_KV_HEADS):
            cs = slice((j // 2) * LANE, (j // 2 + 1) * LANE)
            kcol, vcol = kall[:, cs], vall[:, cs]
            swap = lambda a: jnp.concatenate([a[:, SWA_HEAD_DIM:], a[:, 0:SWA_HEAD_DIM]], axis=1)
            k2 = _block_diag_pair(kcol, swap(kcol), j % 2)
            v2 = _block_diag_pair(vcol, swap(vcol), j % 2)
            v2e = jnp.concatenate([v2.astype(BF16), ones_ref[...]], axis=1)
            qbase = j * npair * LANE
            qs = jnp.concatenate([x[:, qbase + p * LANE:qbase + (p + 1) * LANE] for p in range(npair)],
                                 axis=0) * (SWA_HEAD_DIM ** -0.5)
            sc = _dot_nt(qs, k2) + amask
            pes, ms = [], []
            for half in range(2):
                sh = sc[:, half * 2 * w:(half + 1) * 2 * w]
                m = jnp.maximum(jnp.max(sh, axis=-1, keepdims=True), sinks[j][half])
                pes.append(jnp.exp(sh - m).astype(BF16))
                ms.append(m)
            o = jnp.dot(jnp.concatenate(pes, axis=1), v2e, preferred_element_type=F32)
            esink = jnp.exp(jnp.where(lane_lo, sinks[j][0] - ms[0], sinks[j][1] - ms[1]))
            res = o[:, 0:LANE] / (o[:, LANE:] + esink)
            for p in range(npair):
                o_ref[out_rows[0], out_rows[1], qbase + p * LANE:qbase + (p + 1) * LANE] = (
                    res[p * bq:(p + 1) * bq].astype(o_ref.dtype))


def _swa_attn(rest3, kprev, vprev, sinks, has_past):
    b, l, nr = rest3.shape
    w = SWA_WINDOW
    bq = math.gcd(l, w)
    nb = l // bq
    assert nb == 1 or bq == w
    m_later = _swa_mask(bq, False)
    m_first = m_later if has_past else _swa_mask(bq, True)
    masks = jnp.asarray(np.stack([m_first, m_later]))
    ones2 = jnp.asarray((np.arange(4 * w)[:, None] // (2 * w)) == (np.arange(LANE)[None, :] // SWA_HEAD_DIM), BF16)
    if nb == 1:
        rest3 = rest3.astype(F32)
        items, nsub = math.gcd(b, 8), 1
        kspec = pl.BlockSpec((items, w, SWA_KV_DIM), lambda i, n: (i, 0, 0))
        vspec = kspec
        kin, vin = kprev, vprev
    else:
        assert not has_past
        items, nsub = 1, math.gcd(nb, SWA_BLOCKS_PER_STEP)
        kcol = D_INNER // SWA_KV_DIM
        kspec = pl.BlockSpec((1, w, SWA_KV_DIM), lambda i, n: (i, jnp.maximum(n * nsub - 1, 0), kcol))
        vspec = pl.BlockSpec((1, w, SWA_KV_DIM), lambda i, n: (i, jnp.maximum(n * nsub - 1, 0), kcol + 1))
        kin, vin = rest3, rest3
    out_specs = [pl.BlockSpec((items, bq * nsub, D_INNER), lambda i, n: (i, n, 0))]
    out_shape = [jax.ShapeDtypeStruct((b, l, D_INNER), _mixer_out_dtype(bq))]
    if nb == 1:
        out_specs += [pl.BlockSpec((items, w, SWA_KV_DIM), lambda i, n: (i, 0, 0))] * 2
        out_shape += [jax.ShapeDtypeStruct((b, w, SWA_KV_DIM), F32)] * 2
    return pl.pallas_call(
        functools.partial(_swa_kernel, bq=bq, items=items, nsub=nsub),
        grid=(b // items, nb // nsub),
        in_specs=[
            pl.BlockSpec(memory_space=pltpu.SMEM),
            pl.BlockSpec((items, bq * nsub, nr), lambda i, n: (i, n, 0)),
            kspec,
            vspec,
            pl.BlockSpec(masks.shape, lambda i, n: (0, 0, 0)),
            pl.BlockSpec(ones2.shape, lambda i, n: (0, 0)),
        ],
        out_specs=out_specs,
        out_shape=out_shape,
        compiler_params=_cparams(("parallel", "parallel")),
        name="swa_attn",
    )(sinks, rest3, kin, vin, masks, ones2)


def _trunk(xs, sts, p, has_pasts):
    shapes = [x.shape for x in xs]
    x2s = [x.reshape(-1, x.shape[-1]) for x in xs]
    ones_inner = jnp.ones((D_INNER,), F32)
    news = [{} for _ in xs]
    flat = lambda a: a.reshape(-1, a.shape[-1])

    def gla(i, x2s, key):
        proj = _norm_proj("gla", x2s, p[f"l{i}_norm"], p[f"l{i}_w_in"])
        os_ = []
        for g, ((b, l, _), (gate, rest)) in enumerate(zip(shapes, proj)):
            o, news[g][key] = _gla_scan(rest.reshape(b, l, -1), sts[g][key], p[f"l{i}_w_gk2"], p[f"l{i}_b_gk"])
            os_.append(flat(o))
        hn = jnp.tile(p[f"l{i}_head_norm"], GLA_HEADS)
        return _out_proj("gla", os_, [pr[0] for pr in proj], hn, p[f"l{i}_w_out"], x2s, p["final_norm"], i == 3)

    x2s = gla(0, x2s, "gla0")

    proj = _norm_proj("ssd", x2s, p["l1_norm"], p["l1_w_in"])
    os_ = []
    for g, ((b, l, _), (gate, rest, dt_raw)) in enumerate(zip(shapes, proj)):
        st = sts[g]
        rest3 = rest.reshape(b, l, -1)
        y, news[g]["ssm"] = _ssd2_scan(rest3, dt_raw.reshape(b, l, -1), st["ssm"], st["conv"], p["l1_conv_w"],
                                       p["l1_conv_b"], p["l1_dt_bias"], p["l1_a_log"], p["l1_d_skip"])
        ext_tail = jnp.concatenate([st["conv"], rest3[:, max(l - (SSD_CONV - 1), 0):, :].astype(F32)], axis=1)
        news[g]["conv"] = ext_tail[:, -(SSD_CONV - 1):]
        os_.append(flat(y))
    x2s = _out_proj("ssd", os_, [pr[0] for pr in proj], p["l1_gate_norm"], p["l1_w_out"], x2s,
                    p["final_norm"], False)

    proj = _norm_proj("swa", x2s, p["l2_norm"], p["l2_w_in"])
    os_ = []
    for g, ((b, l, _), (gate, rest)) in enumerate(zip(shapes, proj)):
        st = sts[g]
        rest3 = rest.reshape(b, l, -1)
        kprev = st["swa_k"].reshape(b, SWA_WINDOW, SWA_KV_DIM)
        vprev = st["swa_v"].reshape(b, SWA_WINDOW, SWA_KV_DIM)
        res = _swa_attn(rest3, kprev, vprev, p["l2_sinks"], has_pasts[g])
        if len(res) == 3:
            k_win, v_win = res[1], res[2]
        else:
            k_win = rest3[:, l - SWA_WINDOW:, D_INNER:D_INNER + SWA_KV_DIM].astype(F32)
            v_win = rest3[:, l - SWA_WINDOW:, D_INNER + SWA_KV_DIM:].astype(F32)
        news[g]["swa_k"] = k_win.reshape(st["swa_k"].shape)
        news[g]["swa_v"] = v_win.reshape(st["swa_v"].shape)
        os_.append(flat(res[0]))
    x2s = _out_proj("swa", os_, [pr[0] for pr in proj], ones_inner, p["l2_w_out"], x2s, p["final_norm"], False)

    x2s = gla(3, x2s, "gla3")
    return [x2.reshape(s) for x2, s in zip(x2s, shapes)], news


def kernel(x_prompt, x_sample, state_gla_0, state_ssm_1, state_conv_1, cache_swa_k_2, cache_swa_v_2, state_gla_3, l0_norm, l0_w_in, l0_w_gk2, l0_b_gk, l0_head_norm, l0_w_out, l1_norm, l1_w_in, l1_conv_w, l1_conv_b, l1_dt_bias, l1_a_log, l1_d_skip, l1_gate_norm, l1_w_out, l2_norm, l2_w_in, l2_sinks, l2_w_out, l3_norm, l3_w_in, l3_w_gk2, l3_b_gk, l3_head_norm, l3_w_out, final_norm):
    p = dict(l0_norm=l0_norm, l0_w_in=l0_w_in, l0_w_gk2=l0_w_gk2, l0_b_gk=l0_b_gk,
             l0_head_norm=l0_head_norm, l0_w_out=l0_w_out,
             l1_norm=l1_norm, l1_w_in=l1_w_in, l1_conv_w=l1_conv_w, l1_conv_b=l1_conv_b,
             l1_dt_bias=l1_dt_bias, l1_a_log=l1_a_log, l1_d_skip=l1_d_skip,
             l1_gate_norm=l1_gate_norm, l1_w_out=l1_w_out,
             l2_norm=l2_norm, l2_w_in=l2_w_in, l2_sinks=l2_sinks, l2_w_out=l2_w_out,
             l3_norm=l3_norm, l3_w_in=l3_w_in, l3_w_gk2=l3_w_gk2, l3_b_gk=l3_b_gk,
             l3_head_norm=l3_head_norm, l3_w_out=l3_w_out, final_norm=final_norm)

    bp = x_prompt.shape[0]
    z = lambda a: jnp.zeros((bp,) + a.shape[1:], a.dtype)
    st_p = dict(gla0=z(state_gla_0), ssm=z(state_ssm_1), conv=z(state_conv_1),
                swa_k=z(cache_swa_k_2), swa_v=z(cache_swa_v_2), gla3=z(state_gla_3))
    st_s = dict(gla0=state_gla_0, ssm=state_ssm_1, conv=state_conv_1,
                swa_k=cache_swa_k_2, swa_v=cache_swa_v_2, gla3=state_gla_3)
    (y_p,), (n_p,) = _trunk([x_prompt], [st_p], p, [False])
    (y_s,), (n_s,) = _trunk([x_sample], [st_s], p, [True])
    return (y_p, y_s,
            n_p["gla0"], n_s["gla0"],
            n_p["ssm"], n_s["ssm"],
            n_p["conv"], n_s["conv"],
            n_p["swa_k"], n_s["swa_k"],
            n_p["swa_v"], n_s["swa_v"],
            n_p["gla3"], n_s["gla3"])
```

```python
import functools
import math

import numpy as np
import jax
import jax.numpy as jnp
from jax import lax
from jax.experimental import pallas as pl
from jax.experimental.pallas import tpu as pltpu

F32 = jnp.float32
BF16 = jnp.bfloat16

NORM_EPS = 1e-6
D_MODEL = 1024
D_INNER = 2048
CHUNK = 64
LANE = 128
BF16_SUBLANES = 16
NEG_BIG = -1e30

GLA_HEADS = 4
GLA_HEAD_K = 128
GLA_HEAD_V = 512
GLA_KEY_DIM = 512
GLA_GATE_RANK = 16
GLA_GATE_NORMALIZER = 16.0

SSD_GROUPS = 4
SSD_HEADS = 32
SSD_HEAD_DIM = 64
SSD_STATE = 128
SSD_CONV = 4
SSD_CONV_DIM = 3072
SSD_GROUP_W = D_INNER // SSD_GROUPS

SWA_WINDOW = 128
SWA_KV_HEADS = 4
SWA_HEAD_DIM = 64
SWA_Q_HEADS = 32
SWA_KV_DIM = 256

VMEM_LIMIT = 52 * 1024 * 1024


def _cparams(sem):
    return pltpu.CompilerParams(dimension_semantics=sem, vmem_limit_bytes=VMEM_LIMIT)


def _dot(a, b):
    return jnp.dot(a.astype(BF16), b.astype(BF16), preferred_element_type=F32)


def _dot_nt(a, b):
    return lax.dot_general(a.astype(BF16), b.astype(BF16), (((1,), (1,)), ((), ())),
                           preferred_element_type=F32)


def _split3(x):
    hi = x.astype(BF16)
    r1 = x - hi.astype(F32)
    mid = r1.astype(BF16)
    lo = (r1 - mid.astype(F32)).astype(BF16)
    return hi, mid, lo


def _sigmoid(x):
    return 1.0 / (1.0 + jnp.exp2(x * (-math.log2(math.e))))


def _softplus(x):
    return jnp.maximum(x, 0.0) + jnp.log(1.0 + jnp.exp(-jnp.abs(x)))


def _mixer_out_dtype(block_rows):
    return BF16 if block_rows % BF16_SUBLANES == 0 else F32


PROJ_ROWS_PER_STEP = 512
PROJ_CAST_ROWS = 256


def _round_up(n, m):
    return -(-n // m) * m


def _group_steps(row_counts, tm):
    n = [t // min(tm, t) for t in row_counts]
    starts = [sum(n[:g]) for g in range(len(n))]
    return n, starts


def _group_row_map(start, nsteps):
    return lambda i: (jnp.clip(i - start, 0, nsteps - 1), 0)


def _norm_proj_kernel(*refs, layout, group_steps, w_transposed):
    step = pl.program_id(0)
    aux_piece = layout["aux"]
    ngroups = len(group_steps)
    nout = 2 if aux_piece is None else 3
    x_refs = refs[0:ngroups]
    nw_ref, w_ref = refs[ngroups], refs[ngroups + 1]
    out_refs = refs[ngroups + 2:ngroups + 2 + nout * ngroups]
    scr = refs[ngroups + 2 + nout * ngroups:]
    wg_scr, wr_scr = scr[0], scr[1]
    wa_scr = scr[2] if aux_piece is not None else None
    d = w_ref.shape[1] if w_transposed else w_ref.shape[0]

    def cast_piece(dst, dst_off, src_off, width):
        wpad = _round_up(width, LANE)
        if w_transposed:
            for r in range(0, width, PROJ_CAST_ROWS):
                n = min(PROJ_CAST_ROWS, width - r)
                dst[dst_off + r:dst_off + r + n, :] = w_ref[src_off + r:src_off + r + n, :].astype(BF16)
            if wpad > width:
                dst[dst_off + width:dst_off + wpad, :] = jnp.zeros((wpad - width, d), BF16)
            return wpad
        for r in range(0, d, PROJ_CAST_ROWS):
            piece = w_ref[r:r + PROJ_CAST_ROWS, src_off:src_off + width]
            if wpad > width:
                piece = jnp.concatenate([piece, jnp.zeros((PROJ_CAST_ROWS, wpad - width), F32)], axis=1)
            dst[r:r + PROJ_CAST_ROWS, dst_off:dst_off + wpad] = piece.astype(BF16)
        return wpad

    def matmul(h, w_scr):
        if w_transposed:
            return lax.dot_general(h, w_scr[...], (((1,), (1,)), ((), ())), preferred_element_type=F32)
        return jnp.dot(h, w_scr[...], preferred_element_type=F32)

    @pl.when(step == 0)
    def _():
        cast_piece(wg_scr, 0, *layout["gate"])
        off = 0
        for src_off, width in layout["rest"]:
            off += cast_piece(wr_scr, off, src_off, width)
        if aux_piece is not None:
            cast_piece(wa_scr, 0, *aux_piece)

    def project(x_ref, outs):
        x = x_ref[...]
        ms = jnp.mean(x * x, axis=-1, keepdims=True)
        h = (x * lax.rsqrt(ms + NORM_EPS) * nw_ref[...]).astype(BF16)
        outs[0][...] = matmul(h, wg_scr).astype(outs[0].dtype)
        outs[1][...] = matmul(h, wr_scr).astype(outs[1].dtype)
        if aux_piece is not None:
            outs[2][...] = matmul(h, wa_scr)

    for g, (start, nsteps) in enumerate(group_steps):
        pl.when((step >= start) & (step < start + nsteps))(
            functools.partial(project, x_refs[g], out_refs[g * nout:(g + 1) * nout]))


def _proj_layout(kind):
    kd, di = GLA_KEY_DIM, D_INNER
    if kind == "gla":
        return dict(gate=(2 * kd + di, di),
                    rest=[(2 * kd, di), (0, kd), (kd, kd), (2 * kd + 2 * di, GLA_GATE_RANK)], aux=None)
    if kind == "ssd":
        return dict(gate=(0, di), rest=[(di, SSD_CONV_DIM)], aux=(di + SSD_CONV_DIM, SSD_HEADS))
    return dict(gate=(di + 2 * SWA_KV_DIM, di), rest=[(0, di + 2 * SWA_KV_DIM)], aux=None)


def _norm_proj(kind, xs, nw, w_in):
    d = xs[0].shape[1]
    layout = _proj_layout(kind)
    ng = layout["gate"][1]
    nr = sum(_round_up(w, LANE) for _, w in layout["rest"])
    has_aux = layout["aux"] is not None
    nsteps, starts = _group_steps([x.shape[0] for x in xs], PROJ_ROWS_PER_STEP)
    fix = lambda i: (0, 0)
    in_specs, out_specs, out_shape = [], [], []
    for x, n, s in zip(xs, nsteps, starts):
        t = x.shape[0]
        tm = t // n
        row = _group_row_map(s, n)
        in_specs.append(pl.BlockSpec((tm, d), row))
        out_specs += [pl.BlockSpec((tm, ng), row), pl.BlockSpec((tm, nr), row)]
        out_shape += [jax.ShapeDtypeStruct((t, ng), BF16), jax.ShapeDtypeStruct((t, nr), BF16)]
        if has_aux:
            out_specs.append(pl.BlockSpec((tm, LANE), row))
            out_shape.append(jax.ShapeDtypeStruct((t, LANE), F32))
    w_transposed = w_in.shape[1] % LANE != 0
    w_op = w_in.T if w_transposed else w_in
    swap = (lambda s: s[::-1]) if w_transposed else (lambda s: s)
    in_specs += [pl.BlockSpec((1, d), fix), pl.BlockSpec(w_op.shape, fix, pipeline_mode=pl.Buffered(1))]
    scratch = [pltpu.VMEM(swap((d, ng)), BF16), pltpu.VMEM(swap((d, nr)), BF16)]
    if has_aux:
        scratch.append(pltpu.VMEM(swap((d, LANE)), BF16))
    outs = pl.pallas_call(
        functools.partial(_norm_proj_kernel, layout=layout, group_steps=tuple(zip(starts, nsteps)),
                          w_transposed=w_transposed),
        grid=(sum(nsteps),),
        in_specs=in_specs,
        out_specs=out_specs,
        out_shape=out_shape,
        scratch_shapes=scratch,
        compiler_params=_cparams(("arbitrary",)),
        name="norm_proj_" + kind,
    )(*xs, nw.reshape(1, d), w_op)
    nout = 3 if has_aux else 2
    return [outs[g * nout:(g + 1) * nout] for g in range(len(xs))]


def _seg_rms(y, seg):
    parts = []
    for s in range(y.shape[1] // seg):
        p = y[:, s * seg:(s + 1) * seg]
        ms = jnp.mean(p * p, axis=-1, keepdims=True)
        parts.append(p * lax.rsqrt(ms + NORM_EPS))
    return jnp.concatenate(parts, axis=1)


def _out_proj_kernel(*refs, mode, final, group_steps):
    ngroups = len(group_steps)
    nw_ref, w_ref, fw_ref = refs[3 * ngroups:3 * ngroups + 3]
    out_refs = refs[3 * ngroups + 3:4 * ngroups + 3]
    w_scr = refs[4 * ngroups + 3]
    step = pl.program_id(0)

    @pl.when(step == 0)
    def _():
        for r in range(0, w_ref.shape[0], PROJ_CAST_ROWS):
            w_scr[r:r + PROJ_CAST_ROWS, :] = w_ref[r:r + PROJ_CAST_ROWS, :].astype(BF16)

    def project(o_ref, gate_ref, x_ref, out_ref):
        o = o_ref[...].astype(F32)
        gt = gate_ref[...].astype(F32)
        act = gt * _sigmoid(gt)
        if mode == "gla":
            y = _seg_rms(o, GLA_HEAD_V) * nw_ref[...] * act
        elif mode == "ssd":
            y = _seg_rms(o * act, SSD_GROUP_W) * nw_ref[...]
        else:
            y = o * act
        out = x_ref[...] + jnp.dot(y.astype(BF16), w_scr[...], preferred_element_type=F32)
        if final:
            ms = jnp.mean(out * out, axis=-1, keepdims=True)
            out = out * lax.rsqrt(ms + NORM_EPS) * fw_ref[...]
        out_ref[...] = out

    for g, (start, nsteps) in enumerate(group_steps):
        pl.when((step >= start) & (step < start + nsteps))(
            functools.partial(project, refs[3 * g], refs[3 * g + 1], refs[3 * g + 2], out_refs[g]))


def _out_proj(mode, os_, gates, nw, w_out, xs, fw, final):
    di = os_[0].shape[1]
    d = xs[0].shape[1]
    nsteps, starts = _group_steps([x.shape[0] for x in xs], PROJ_ROWS_PER_STEP)
    fix = lambda i: (0, 0)
    in_specs, out_specs, out_shape, operands = [], [], [], []
    for o2, g2, x2, n, s in zip(os_, gates, xs, nsteps, starts):
        t = x2.shape[0]
        tm = t // n
        row = _group_row_map(s, n)
        in_specs += [pl.BlockSpec((tm, di), row), pl.BlockSpec((tm, di), row), pl.BlockSpec((tm, d), row)]
        operands += [o2, g2, x2]
        out_specs.append(pl.BlockSpec((tm, d), row))
        out_shape.append(jax.ShapeDtypeStruct((t, d), F32))
    in_specs += [pl.BlockSpec((1, di), fix), pl.BlockSpec((di, d), fix), pl.BlockSpec((1, d), fix)]
    return pl.pallas_call(
        functools.partial(_out_proj_kernel, mode=mode, final=final, group_steps=tuple(zip(starts, nsteps))),
        grid=(sum(nsteps),),
        in_specs=in_specs,
        out_specs=out_specs,
        out_shape=out_shape,
        scratch_shapes=[pltpu.VMEM((di, d), BF16)],
        compiler_params=_cparams(("arbitrary",)),
        name="out_proj_" + mode,
    )(*operands, nw.reshape(1, di), w_out, fw.reshape(1, d))


GLA_CHUNKS_PER_STEP = 4
GLA_LONG_CHUNK = 128
GLA_SINGLE_ANCHOR_MAX_DECAY = 60.0


def _gla_levels(seglen):
    return tuple(seglen >> (i + 1) for i in range(int(math.log2(seglen))))


def _gla_consts(seglen, c):
    levels = _gla_levels(seglen)
    u = np.arange(c)[:, None]
    j = np.arange(c)[None, :]
    same = (u // seglen) == (j // seglen)
    blocks = [same & (j <= u), same & (j > u)]
    lev = np.full((c, c), len(levels) + 1, np.int32)
    for li, h in enumerate(levels):
        b = (u // (2 * h)) * (2 * h) + h - 1
        blocks.append((j > np.minimum(u, b)) & (j <= np.maximum(u, b)))
        sib = (u // (2 * h) == j // (2 * h)) & (u % (2 * h) >= h) & (j % (2 * h) < h)
        lev[sib] = li
    lev[np.eye(c, dtype=bool)] = len(levels)
    m = np.concatenate(blocks, axis=0).astype(np.float32)
    m3 = np.concatenate([m, m, m], axis=1)
    if 2 * c == LANE:
        lev = np.concatenate([lev, lev], axis=1)
    return jnp.asarray(m3, BF16), jnp.asarray(lev)


def _gla_scores(qx, kx, e_q, e_k, lev, nt):
    att = None
    for li, (eq, ek) in enumerate(zip(e_q, e_k)):
        a = nt(qx if eq is None else qx * eq, kx if ek is None else kx * ek)
        att = jnp.where(lev == li, a, 0.0 if att is None else att)
    return att


def _nt_pair(qe, ke):
    kb = ke.astype(BF16)
    z = jnp.zeros((kb.shape[0], GLA_HEAD_K), BF16)
    rhs = jnp.concatenate([jnp.concatenate([kb[:, 0:GLA_HEAD_K], z], axis=1),
                           jnp.concatenate([z, kb[:, GLA_HEAD_K:]], axis=1)], axis=0)
    return lax.dot_general(qe.astype(BF16), rhs, (((1,), (1,)), ((), ())), preferred_element_type=F32)


def _gla_prep(xc, wgk, bgk):
    v = xc[:, 0:D_INNER]
    q = xc[:, D_INNER:D_INNER + GLA_KEY_DIM].astype(F32) * (GLA_HEAD_K ** -0.5)
    k = xc[:, D_INNER + GLA_KEY_DIM:D_INNER + 2 * GLA_KEY_DIM].astype(F32)
    lr = xc[:, D_INNER + 2 * GLA_KEY_DIM:]
    z = _dot(lr, wgk) + bgk
    g = (jnp.minimum(z, 0.0) - jnp.log(1.0 + jnp.exp(-jnp.abs(z)))) * (1.0 / GLA_GATE_NORMALIZER)
    return v, q, k, g


def _gla_chunk(v, q, k, g, state_io, m3, lev, nl, single_anchor):
    c = g.shape[0]
    g3 = jnp.concatenate(_split3(g), axis=0)
    if single_anchor:
        rs = jnp.dot(m3[0:2 * c], g3, preferred_element_type=F32)
        cum = rs[0:c]
        e_in, e_out = jnp.exp(cum), jnp.exp(rs[c:2 * c])
        e_q, e_k = [e_in], [jnp.exp(-cum)]
        lev = jnp.where(lev <= nl, 0, 1)
    else:
        e_all = jnp.exp(jnp.dot(m3, g3, preferred_element_type=F32))
        e_in, e_out = e_all[0:c], e_all[c:2 * c]
        e_q = [e_all[(2 + li) * c:(3 + li) * c] for li in range(nl)] + [None]
        e_k = e_q
    qe_in = q * e_in
    ke_out = k * e_out
    outs = []
    if c == LANE:
        for h in range(GLA_HEADS):
            ks = slice(h * GLA_HEAD_K, (h + 1) * GLA_HEAD_K)
            cut = lambda es: [None if e is None else e[:, ks] for e in es]
            att = _gla_scores(q[:, ks], k[:, ks], cut(e_q), cut(e_k), lev, _dot_nt)
            v_h = v[:, h * GLA_HEAD_V:(h + 1) * GLA_HEAD_V]
            xt = jnp.concatenate([ke_out[:, ks], e_in[:, ks]], axis=0).T
            outs.append(_dot(att, v_h) + state_io(h, qe_in[:, ks], xt, v_h))
        return jnp.concatenate(outs, axis=1)
    lane_lo = lax.broadcasted_iota(jnp.int32, (c, LANE), 1) < c
    for pr in range(GLA_HEADS // 2):
        ls = slice(2 * pr * GLA_HEAD_K, (2 * pr + 2) * GLA_HEAD_K)
        cut = lambda es: [None if e is None else e[:, ls] for e in es]
        att = _gla_scores(q[:, ls], k[:, ls], cut(e_q), cut(e_k), lev, _nt_pair)
        vpair = v[:, 2 * pr * GLA_HEAD_V:(2 * pr + 2) * GLA_HEAD_V]
        vcat = jnp.concatenate([vpair[:, 0:GLA_HEAD_V], vpair[:, GLA_HEAD_V:]], axis=0)
        for half, att_h in enumerate((jnp.where(lane_lo, att, 0.0), jnp.where(lane_lo, 0.0, att))):
            h = 2 * pr + half
            ks = slice(h * GLA_HEAD_K, (h + 1) * GLA_HEAD_K)
            xt = jnp.concatenate([ke_out[:, ks], e_in[:, ks]], axis=0).T
            o_inter = state_io(h, qe_in[:, ks], xt, v[:, h * GLA_HEAD_V:(h + 1) * GLA_HEAD_V])
            outs.append(_dot(att_h, vcat) + o_inter)
    return jnp.concatenate(outs, axis=1)


def _gla_dispatch(preps, run):
    low = None
    for _, _, _, g in preps:
        tot = jnp.min(jnp.sum(g, axis=0, keepdims=True))
        low = tot if low is None else jnp.minimum(low, tot)
    single_anchor_ok = low >= -GLA_SINGLE_ANCHOR_MAX_DECAY
    pl.when(single_anchor_ok)(functools.partial(run, True))
    pl.when(jnp.logical_not(single_anchor_ok))(functools.partial(run, False))


def _gla_carry_kernel(rest_ref, s0_ref, wgk_ref, bgk_ref, m_ref, lev_ref, o_ref, s_ref, *, c, nch, nl):
    @pl.when(pl.program_id(1) == 0)
    def _():
        s_ref[...] = s0_ref[...]

    def state_io(h, qe_in_h, xt, v_h):
        s_h = s_ref[0, h]
        s_ref[0, h] = s_h * xt[:, 2 * c - 1:2 * c] + _dot(xt[:, 0:c], v_h)
        return _dot(qe_in_h, s_h)

    wgk, bgk = wgk_ref[...], bgk_ref[...]
    preps = [_gla_prep(rest_ref[0, ci * c:(ci + 1) * c, :], wgk, bgk) for ci in range(nch)]

    def run(single_anchor):
        for ci, (v, q, k, g) in enumerate(preps):
            o = _gla_chunk(v, q, k, g, state_io, m_ref[...], lev_ref[...], nl, single_anchor)
            o_ref[0, ci * c:(ci + 1) * c, :] = o.astype(o_ref.dtype)

    _gla_dispatch(preps, run)


def _gla_packed_kernel(rest_ref, s0_ref, wgk_ref, bgk_ref, m_ref, lev_ref, o_ref, s_ref, *, seglen, nl):
    c = CHUNK
    nseg = c // seglen
    seg_of_lane = lax.broadcasted_iota(jnp.int32, (GLA_HEAD_K, c), 1) // seglen

    def state_io(h, qe_in_h, xt, v_h):
        ke_t = xt[:, 0:c]
        outs = []
        for sg in range(nseg):
            s_h = s0_ref[sg, h]
            outs.append(_dot(qe_in_h[sg * seglen:(sg + 1) * seglen], s_h))
            end = c + (sg + 1) * seglen - 1
            s_ref[sg, h] = s_h * xt[:, end:end + 1] + _dot(jnp.where(seg_of_lane == sg, ke_t, 0.0), v_h)
        return jnp.concatenate(outs, axis=0)

    preps = [_gla_prep(rest_ref[...], wgk_ref[...], bgk_ref[...])]

    def run(single_anchor):
        v, q, k, g = preps[0]
        o = _gla_chunk(v, q, k, g, state_io, m_ref[...], lev_ref[...], nl, single_anchor)
        o_ref[...] = o.astype(o_ref.dtype)

    _gla_dispatch(preps, run)


def _gla_scan(rest3, s0, w_gk2, b_gk):
    b, l, nr = rest3.shape
    packed = l < CHUNK
    c = CHUNK if packed or l % GLA_LONG_CHUNK else GLA_LONG_CHUNK
    seglen = l if packed else c
    assert c % seglen == 0 and seglen % 8 == 0 and (l % c == 0 or packed)
    nl = len(_gla_levels(seglen))
    m01, lev = _gla_consts(seglen, c)
    wgk = jnp.zeros((LANE, GLA_KEY_DIM), F32).at[:GLA_GATE_RANK].set(w_gk2).astype(BF16)
    consts = (wgk, b_gk.reshape(1, GLA_KEY_DIM), m01, lev)
    if packed:
        nseq = c // seglen
        assert b % nseq == 0
        sblk = (nseq, GLA_HEADS, GLA_HEAD_K, GLA_HEAD_V)
        o, s_fin = pl.pallas_call(
            functools.partial(_gla_packed_kernel, seglen=seglen, nl=nl),
            grid=(b // nseq,),
            in_specs=[pl.BlockSpec((c, nr), lambda i: (i, 0)),
                      pl.BlockSpec(sblk, lambda i: (i, 0, 0, 0))]
                     + [pl.BlockSpec(a.shape, lambda i: (0, 0)) for a in consts],
            out_specs=[pl.BlockSpec((c, D_INNER), lambda i: (i, 0)),
                       pl.BlockSpec(sblk, lambda i: (i, 0, 0, 0))],
            out_shape=[jax.ShapeDtypeStruct((b * l, D_INNER), BF16), jax.ShapeDtypeStruct(s0.shape, F32)],
            compiler_params=_cparams(("parallel",)),
            name="gla_scan_packed",
        )(rest3.reshape(b * l, nr), s0, *consts)
        return o.reshape(b, l, D_INNER), s_fin
    nch = math.gcd(l // c, GLA_CHUNKS_PER_STEP)
    sblk = (1, GLA_HEADS, GLA_HEAD_K, GLA_HEAD_V)
    return pl.pallas_call(
        functools.partial(_gla_carry_kernel, c=c, nch=nch, nl=nl),
        grid=(b, l // (c * nch)),
        in_specs=[pl.BlockSpec((1, c * nch, nr), lambda i, t: (i, t, 0)),
                  pl.BlockSpec(sblk, lambda i, t: (i, 0, 0, 0))]
                 + [pl.BlockSpec(a.shape, lambda i, t: (0, 0)) for a in consts],
        out_specs=[pl.BlockSpec((1, c * nch, D_INNER), lambda i, t: (i, t, 0)),
                   pl.BlockSpec(sblk, lambda i, t: (i, 0, 0, 0))],
        out_shape=[jax.ShapeDtypeStruct((b, l, D_INNER), BF16), jax.ShapeDtypeStruct(s0.shape, F32)],
        compiler_params=_cparams(("parallel", "arbitrary")),
        name="gla_scan",
    )(rest3, s0, *consts)


SSD_CHUNKS_PER_STEP = 4


def _ssd2_consts(seglen):
    c = CHUNK
    u = np.arange(c)
    same = (u[:, None] // seglen) == (u[None, :] // seglen)
    tl = same & (u[None, :] <= u[:, None])
    m = np.concatenate([tl, same], axis=0).astype(np.float32)
    m3 = np.concatenate([m, m, m], axis=1)
    expand = (np.arange(LANE)[:, None] == (np.arange(D_INNER) // SSD_HEAD_DIM)[None, :]).astype(np.float32)
    ex3 = np.concatenate([expand] * 3, axis=0)
    s_of = np.arange(LANE) % c
    tri = np.where(same[:, s_of] & (s_of[None, :] <= u[:, None]), 0.0, NEG_BIG).astype(np.float32)
    bd = ((np.arange(2 * c)[:, None] // c) == (np.arange(LANE)[None, :] // SSD_HEAD_DIM)).astype(np.float32)
    return jnp.asarray(m3, BF16), jnp.asarray(ex3, BF16), jnp.asarray(tri), jnp.asarray(bd)


def _ssd2_conv(ext_ref, rows, cw_ref, cb_ref):
    e = ext_ref[0:8 + rows, :]
    acc = cw_ref[SSD_CONV - 1:SSD_CONV, :] * e[8:]
    for k in range(1, SSD_CONV):
        acc = acc + cw_ref[SSD_CONV - 1 - k:SSD_CONV - k, :] * pltpu.roll(e, k, axis=0)[8:]
    return cb_ref[...] + acc


def _ssd2_chunk(dt_raw, conv, dtb, a_neg, dsk, m3, ex3, tri, bd, state_io):
    c = CHUNK
    xbc = conv * _sigmoid(conv)
    xs = xbc[:, 0:D_INNER]
    bm = xbc[:, D_INNER:D_INNER + SSD_GROUPS * SSD_STATE]
    cm = xbc[:, D_INNER + SSD_GROUPS * SSD_STATE:]
    dt = _softplus(dt_raw + dtb)
    la = dt * a_neg
    rs = jnp.dot(m3, jnp.concatenate(_split3(la), axis=0), preferred_element_type=F32)
    cum, tot = rs[0:c], rs[c:2 * c]
    ecum = jnp.exp(cum)
    stack = jnp.concatenate([dt, dt * jnp.exp(tot - cum), cum, ecum], axis=0)
    rep = jnp.dot(jnp.concatenate(_split3(stack), axis=1), ex3, preferred_element_type=F32)
    dt_rep, dtw_rep, cum_rep, ecum_rep = (rep[i * c:(i + 1) * c] for i in range(4))
    u = xs * dt_rep
    uw = xs * dtw_rep
    at = jnp.concatenate([cum, cum], axis=0).T
    lane_lo = lax.broadcasted_iota(jnp.int32, (1, LANE), 1) < c
    ys = []
    for g in range(SSD_GROUPS):
        gs = slice(g * SSD_GROUP_W, (g + 1) * SSD_GROUP_W)
        ns = slice(g * SSD_STATE, (g + 1) * SSD_STATE)
        cg, bg = cm[:, ns], bm[:, ns]
        cb_rep = _dot_nt(cg, jnp.concatenate([bg] * (SSD_GROUP_W // c), axis=0))
        parts = []
        for p in range(SSD_GROUP_W // LANE):
            col = g * (SSD_GROUP_W // LANE) + p
            ps = slice(col * LANE, (col + 1) * LANE)
            cum_s = jnp.where(lane_lo, at[2 * col:2 * col + 1, :], at[2 * col + 1:2 * col + 2, :])
            w = jnp.exp(cum_rep[:, ps] - cum_s + tri) * cb_rep[:, p * LANE:(p + 1) * LANE]
            up = u[:, ps]
            parts.append(_dot(w, jnp.concatenate([up, up], axis=0) * bd))
        y_inter = state_io(g, cg, bg, uw[:, gs], ecum_rep[:, gs])
        ys.append(jnp.concatenate(parts, axis=1) + y_inter + xs[:, gs] * dsk[:, gs])
    return jnp.concatenate(ys, axis=1)


def _ssd2_carry_kernel(rest_ref, dtr_ref, s0_ref, conv0_ref, cw_ref, cb_ref, dtb_ref, alog_ref, dsk_ref,
                       m3_ref, ex3_ref, tri_ref, bd_ref, y_ref, sfin_ref, ext_ref, st_ref, *, nch):
    c = CHUNK
    t = pl.program_id(1)

    @pl.when(t == 0)
    def _():
        ext_ref[0:8, :] = conv0_ref[0]
        for g in range(SSD_GROUPS):
            st_ref[g] = s0_ref[0, g].T

    a_neg = -jnp.exp(alog_ref[...])

    def state_io(g, cg, bg, uw_g, ecum_g):
        s_g = st_ref[g]
        y_inter = _dot(cg, s_g) * ecum_g
        bg_t = jnp.concatenate([bg, jnp.zeros_like(bg)], axis=0).T[:, 0:c]
        st_ref[g] = s_g * ecum_g[c - 1:c, :] + _dot(bg_t, uw_g)
        return y_inter

    for ci in range(nch):
        ext_ref[8:8 + c, :] = rest_ref[0, ci * c:(ci + 1) * c, :].astype(F32)
        conv = _ssd2_conv(ext_ref, c, cw_ref, cb_ref)
        ext_ref[0:8, :] = ext_ref[c:c + 8, :]
        y = _ssd2_chunk(dtr_ref[0, ci * c:(ci + 1) * c, :], conv, dtb_ref[...], a_neg, dsk_ref[...],
                        m3_ref[...], ex3_ref[...], tri_ref[...], bd_ref[...], state_io)
        y_ref[0, ci * c:(ci + 1) * c, :] = y.astype(y_ref.dtype)

    @pl.when(t == pl.num_programs(1) - 1)
    def _():
        for g in range(SSD_GROUPS):
            sfin_ref[0, g] = st_ref[g].T


def _ssd2_packed_kernel(rest_ref, dtr_ref, s0_ref, conv0_ref, cw_ref, cb_ref, dtb_ref, alog_ref, dsk_ref,
                        m3_ref, ex3_ref, tri_ref, bd_ref, y_ref, sfin_ref, ext_ref, *, seglen):
    c = CHUNK
    nseg = c // seglen
    ext_ref[0:8, :] = jnp.zeros((8, SSD_CONV_DIM), F32)
    xbc_raw = rest_ref[...].astype(F32)
    for sg in range(nseg):
        ext_ref[8 + 16 * sg:16 + 16 * sg, :] = conv0_ref[sg]
        ext_ref[16 + 16 * sg:24 + 16 * sg, :] = xbc_raw[sg * seglen:(sg + 1) * seglen]
    conv2 = _ssd2_conv(ext_ref, 2 * c, cw_ref, cb_ref)
    conv = jnp.concatenate([conv2[16 * sg + 8:16 * sg + 16] for sg in range(nseg)], axis=0)
    a_neg = -jnp.exp(alog_ref[...])
    seg_of_row = lax.broadcasted_iota(jnp.int32, (c, SSD_STATE), 0) // seglen

    def state_io(g, cg, bg, uw_g, ecum_g):
        ends = jnp.concatenate([ecum_g[(sg + 1) * seglen - 1:(sg + 1) * seglen, :] for sg in range(nseg)], axis=0)
        fill = jnp.zeros((LANE - c - nseg, SSD_GROUP_W), F32)
        tg = jnp.concatenate([uw_g, ends, fill], axis=0).T
        uw_t = tg[:, 0:c]
        y_parts = []
        for sg in range(nseg):
            s_sg = s0_ref[sg, g]
            y_parts.append(_dot_nt(cg[sg * seglen:(sg + 1) * seglen], s_sg))
            b_sg = jnp.where(seg_of_row == sg, bg, 0.0)
            sfin_ref[sg, g] = s_sg * tg[:, c + sg:c + sg + 1] + _dot(uw_t, b_sg)
        return jnp.concatenate(y_parts, axis=0) * ecum_g

    y = _ssd2_chunk(dtr_ref[...], conv, dtb_ref[...], a_neg, dsk_ref[...], m3_ref[...], ex3_ref[...],
                    tri_ref[...], bd_ref[...], state_io)
    y_ref[...] = y.astype(y_ref.dtype)


def _ssd2_scan(rest3, dt_raw3, s0, conv0, conv_w, conv_b, dt_bias, a_log, d_skip):
    b, l, nr = rest3.shape
    c = CHUNK
    packed = l < c
    seglen = l if packed else c
    assert c % seglen == 0 and seglen % 8 == 0 and (l % c == 0 or packed)
    m3, ex3, tri, bd = _ssd2_consts(seglen)
    s0g = s0.reshape(b, SSD_GROUPS, SSD_GROUP_W, SSD_STATE)
    conv0p = jnp.concatenate([jnp.zeros((b, 8 - (SSD_CONV - 1), SSD_CONV_DIM), F32), conv0], axis=1)
    pad = lambda a: jnp.zeros((1, LANE), F32).at[0, :SSD_HEADS].set(a)
    dsk = jnp.repeat(d_skip, SSD_HEAD_DIM).reshape(1, D_INNER)
    consts = (conv_w, conv_b.reshape(1, -1), pad(dt_bias), pad(a_log), dsk, m3, ex3, tri, bd)
    if packed:
        nseq = c // seglen
        assert b % nseq == 0
        fix = lambda i: (0, 0)
        const_specs = [pl.BlockSpec(a.shape, fix) for a in consts]
        sblk = (nseq, SSD_GROUPS, SSD_GROUP_W, SSD_STATE)
        y, sfin = pl.pallas_call(
            functools.partial(_ssd2_packed_kernel, seglen=seglen),
            grid=(b // nseq,),
            in_specs=[pl.BlockSpec((c, nr), lambda i: (i, 0)),
                      pl.BlockSpec((c, LANE), lambda i: (i, 0)),
                      pl.BlockSpec(sblk, lambda i: (i, 0, 0, 0)),
                      pl.BlockSpec((nseq, 8, SSD_CONV_DIM), lambda i: (i, 0, 0))] + const_specs,
            out_specs=[pl.BlockSpec((c, D_INNER), lambda i: (i, 0)),
                       pl.BlockSpec(sblk, lambda i: (i, 0, 0, 0))],
            out_shape=[jax.ShapeDtypeStruct((b * l, D_INNER), BF16), jax.ShapeDtypeStruct(s0g.shape, F32)],
            scratch_shapes=[pltpu.VMEM((2 * c + 8, SSD_CONV_DIM), F32)],
            compiler_params=_cparams(("parallel",)),
            name="ssd_scan_packed",
        )(rest3.reshape(b * l, nr), dt_raw3.reshape(b * l, LANE), s0g, conv0p, *consts)
        return y.reshape(b, l, D_INNER), sfin.reshape(s0.shape)
    nch = math.gcd(l // c, SSD_CHUNKS_PER_STEP)
    fix2 = lambda i, t: (0, 0)
    const_specs = [pl.BlockSpec(a.shape, fix2) for a in consts]
    sblk = (1, SSD_GROUPS, SSD_GROUP_W, SSD_STATE)
    y, sfin = pl.pallas_call(
        functools.partial(_ssd2_carry_kernel, nch=nch),
        grid=(b, l // (c * nch)),
        in_specs=[pl.BlockSpec((1, c * nch, nr), lambda i, t: (i, t, 0)),
                  pl.BlockSpec((1, c * nch, LANE), lambda i, t: (i, t, 0)),
                  pl.BlockSpec(sblk, lambda i, t: (i, 0, 0, 0)),
                  pl.BlockSpec((1, 8, SSD_CONV_DIM), lambda i, t: (i, 0, 0))] + const_specs,
        out_specs=[pl.BlockSpec((1, c * nch, D_INNER), lambda i, t: (i, t, 0)),
                   pl.BlockSpec(sblk, lambda i, t: (i, 0, 0, 0))],
        out_shape=[jax.ShapeDtypeStruct((b, l, D_INNER), BF16), jax.ShapeDtypeStruct(s0g.shape, F32)],
        scratch_shapes=[pltpu.VMEM((c + 8, SSD_CONV_DIM), F32),
                        pltpu.VMEM((SSD_GROUPS, SSD_STATE, SSD_GROUP_W), F32)],
        compiler_params=_cparams(("parallel", "arbitrary")),
        name="ssd_scan",
    )(rest3, dt_raw3, s0g, conv0p, *consts)
    return y, sfin.reshape(s0.shape)


SWA_BLOCKS_PER_STEP = 2


def _swa_mask(bq, first_block_has_no_past):
    rows = 4 * bq
    tq = (np.arange(rows) % bq)[:, None]
    s = (np.arange(4 * SWA_WINDOW) % (2 * SWA_WINDOW))[None, :]
    ok = (s > tq) & (s <= tq + SWA_WINDOW)
    if first_block_has_no_past:
        ok = ok & (s >= SWA_WINDOW)
    return np.where(ok, 0.0, NEG_BIG).astype(np.float32)


def _block_diag_pair(col, rolled, odd):
    lane = lax.broadcasted_iota(jnp.int32, col.shape, 1)
    lo = lane < SWA_HEAD_DIM
    if odd:
        top = jnp.where(lo, rolled, 0.0)
        bot = jnp.where(lo, 0.0, col)
    else:
        top = jnp.where(lo, col, 0.0)
        bot = jnp.where(lo, 0.0, rolled)
    return jnp.concatenate([top, bot], axis=0)


def _swa_kernel(sink_ref, x_ref, kprev_ref, vprev_ref, mask_ref, ones_ref, o_ref, *cache_refs, bq, items, nsub):
    w = SWA_WINDOW
    n = pl.program_id(1)
    npair = (SWA_Q_HEADS // SWA_KV_HEADS) // 2
    rows = npair * bq
    lane_lo = lax.broadcasted_iota(jnp.int32, (rows, LANE), 1) < SWA_HEAD_DIM
    sinks = []
    for j in range(SWA_KV_HEADS):
        sinks.append([jnp.concatenate(
            [jnp.full((bq, 1), sink_ref[j * 2 * npair + 2 * p + half], F32) for p in range(npair)], axis=0)
            for half in range(2)])

    kcols = slice(D_INNER, D_INNER + SWA_KV_DIM)
    vcols = slice(D_INNER + SWA_KV_DIM, D_INNER + 2 * SWA_KV_DIM)
    for i in range(items * nsub):
        if nsub == 1:
            x = x_ref[i]
            kprev, vprev = kprev_ref[i], vprev_ref[i]
            amask = mask_ref[jnp.where(n == 0, 0, 1)]
            out_rows = (i, slice(None))
        else:
            x = x_ref[0, i * bq:(i + 1) * bq, :]
            if i == 0:
                kprev, vprev = kprev_ref[0], vprev_ref[0]
                amask = mask_ref[jnp.where(n == 0, 0, 1)]
            else:
                kprev = x_ref[0, (i - 1) * bq:i * bq, kcols]
                vprev = x_ref[0, (i - 1) * bq:i * bq, vcols]
                amask = mask_ref[1]
            out_rows = (0, slice(i * bq, (i + 1) * bq))
        fill = [] if bq == w else [jnp.zeros((w - bq, SWA_KV_DIM), F32)]
        kall = jnp.concatenate([kprev, x[:, kcols]] + fill, axis=0)
        vall = jnp.concatenate([vprev, x[:, vcols]] + fill, axis=0)
        if cache_refs:
            cache_refs[0][i] = kall[bq:bq + w]
            cache_refs[1][i] = vall[bq:bq + w]
        for j in range(SWA_KV_HEADS):
            cs = slice((j // 2) * LANE, (j // 2 + 1) * LANE)
            kcol, vcol = kall[:, cs], vall[:, cs]
            swap = lambda a: jnp.concatenate([a[:, SWA_HEAD_DIM:], a[:, 0:SWA_HEAD_DIM]], axis=1)
            k2 = _block_diag_pair(kcol, swap(kcol), j % 2)
            v2 = _block_diag_pair(vcol, swap(vcol), j % 2)
            v2e = jnp.concatenate([v2.astype(BF16), ones_ref[...]], axis=1)
            qbase = j * npair * LANE
            qs = jnp.concatenate([x[:, qbase + p * LANE:qbase + (p + 1) * LANE] for p in range(npair)],
                                 axis=0) * (SWA_HEAD_DIM ** -0.5)
            sc = _dot_nt(qs, k2) + amask
            pes, ms = [], []
            for half in range(2):
                sh = sc[:, half * 2 * w:(half + 1) * 2 * w]
                m = jnp.maximum(jnp.max(sh, axis=-1, keepdims=True), sinks[j][half])
                pes.append(jnp.exp(sh - m).astype(BF16))
                ms.append(m)
            o = jnp.dot(jnp.concatenate(pes, axis=1), v2e, preferred_element_type=F32)
            esink = jnp.exp(jnp.where(lane_lo, sinks[j][0] - ms[0], sinks[j][1] - ms[1]))
            res = o[:, 0:LANE] / (o[:, LANE:] + esink)
            for p in range(npair):
                o_ref[out_rows[0], out_rows[1], qbase + p * LANE:qbase + (p + 1) * LANE] = (
                    res[p * bq:(p + 1) * bq].astype(o_ref.dtype))


def _swa_attn(rest3, kprev, vprev, sinks, has_past):
    b, l, nr = rest3.shape
    w = SWA_WINDOW
    bq = math.gcd(l, w)
    nb = l // bq
    assert nb == 1 or bq == w
    m_later = _swa_mask(bq, False)
    m_first = m_later if has_past else _swa_mask(bq, True)
    masks = jnp.asarray(np.stack([m_first, m_later]))
    ones2 = jnp.asarray((np.arange(4 * w)[:, None] // (2 * w)) == (np.arange(LANE)[None, :] // SWA_HEAD_DIM), BF16)
    if nb == 1:
        rest3 = rest3.astype(F32)
        items, nsub = math.gcd(b, 8), 1
        kspec = pl.BlockSpec((items, w, SWA_KV_DIM), lambda i, n: (i, 0, 0))
        vspec = kspec
        kin, vin = kprev, vprev
    else:
        assert not has_past
        items, nsub = 1, math.gcd(nb, SWA_BLOCKS_PER_STEP)
        kcol = D_INNER // SWA_KV_DIM
        kspec = pl.BlockSpec((1, w, SWA_KV_DIM), lambda i, n: (i, jnp.maximum(n * nsub - 1, 0), kcol))
        vspec = pl.BlockSpec((1, w, SWA_KV_DIM), lambda i, n: (i, jnp.maximum(n * nsub - 1, 0), kcol + 1))
        kin, vin = rest3, rest3
    out_specs = [pl.BlockSpec((items, bq * nsub, D_INNER), lambda i, n: (i, n, 0))]
    out_shape = [jax.ShapeDtypeStruct((b, l, D_INNER), _mixer_out_dtype(bq))]
    if nb == 1:
        out_specs += [pl.BlockSpec((items, w, SWA_KV_DIM), lambda i, n: (i, 0, 0))] * 2
        out_shape += [jax.ShapeDtypeStruct((b, w, SWA_KV_DIM), F32)] * 2
    return pl.pallas_call(
        functools.partial(_swa_kernel, bq=bq, items=items, nsub=nsub),
        grid=(b // items, nb // nsub),
        in_specs=[
            pl.BlockSpec(memory_space=pltpu.SMEM),
            pl.BlockSpec((items, bq * nsub, nr), lambda i, n: (i, n, 0)),
            kspec,
            vspec,
            pl.BlockSpec(masks.shape, lambda i, n: (0, 0, 0)),
            pl.BlockSpec(ones2.shape, lambda i, n: (0, 0)),
        ],
        out_specs=out_specs,
        out_shape=out_shape,
        compiler_params=_cparams(("parallel", "parallel")),
        name="swa_attn",
    )(sinks, rest3, kin, vin, masks, ones2)


def _trunk(xs, sts, p, has_pasts):
    shapes = [x.shape for x in xs]
    x2s = [x.reshape(-1, x.shape[-1]) for x in xs]
    ones_inner = jnp.ones((D_INNER,), F32)
    news = [{} for _ in xs]
    flat = lambda a: a.reshape(-1, a.shape[-1])

    def gla(i, x2s, key):
        proj = _norm_proj("gla", x2s, p[f"l{i}_norm"], p[f"l{i}_w_in"])
        os_ = []
        for g, ((b, l, _), (gate, rest)) in enumerate(zip(shapes, proj)):
            o, news[g][key] = _gla_scan(rest.reshape(b, l, -1), sts[g][key], p[f"l{i}_w_gk2"], p[f"l{i}_b_gk"])
            os_.append(flat(o))
        hn = jnp.tile(p[f"l{i}_head_norm"], GLA_HEADS)
        return _out_proj("gla", os_, [pr[0] for pr in proj], hn, p[f"l{i}_w_out"], x2s, p["final_norm"], i == 3)

    x2s = gla(0, x2s, "gla0")

    proj = _norm_proj("ssd", x2s, p["l1_norm"], p["l1_w_in"])
    os_ = []
    for g, ((b, l, _), (gate, rest, dt_raw)) in enumerate(zip(shapes, proj)):
        st = sts[g]
        rest3 = rest.reshape(b, l, -1)
        y, news[g]["ssm"] = _ssd2_scan(rest3, dt_raw.reshape(b, l, -1), st["ssm"], st["conv"], p["l1_conv_w"],
                                       p["l1_conv_b"], p["l1_dt_bias"], p["l1_a_log"], p["l1_d_skip"])
        ext_tail = jnp.concatenate([st["conv"], rest3[:, max(l - (SSD_CONV - 1), 0):, :].astype(F32)], axis=1)
        news[g]["conv"] = ext_tail[:, -(SSD_CONV - 1):]
        os_.append(flat(y))
    x2s = _out_proj("ssd", os_, [pr[0] for pr in proj], p["l1_gate_norm"], p["l1_w_out"], x2s,
                    p["final_norm"], False)

    proj = _norm_proj("swa", x2s, p["l2_norm"], p["l2_w_in"])
    os_ = []
    for g, ((b, l, _), (gate, rest)) in enumerate(zip(shapes, proj)):
        st = sts[g]
        rest3 = rest.reshape(b, l, -1)
        kprev = st["swa_k"].reshape(b, SWA_WINDOW, SWA_KV_DIM)
        vprev = st["swa_v"].reshape(b, SWA_WINDOW, SWA_KV_DIM)
        res = _swa_attn(rest3, kprev, vprev, p["l2_sinks"], has_pasts[g])
        if len(res) == 3:
            k_win, v_win = res[1], res[2]
        else:
            k_win = rest3[:, l - SWA_WINDOW:, D_INNER:D_INNER + SWA_KV_DIM].astype(F32)
            v_win = rest3[:, l - SWA_WINDOW:, D_INNER + SWA_KV_DIM:].astype(F32)
        news[g]["swa_k"] = k_win.reshape(st["swa_k"].shape)
        news[g]["swa_v"] = v_win.reshape(st["swa_v"].shape)
        os_.append(flat(res[0]))
    x2s = _out_proj("swa", os_, [pr[0] for pr in proj], ones_inner, p["l2_w_out"], x2s, p["final_norm"], False)

    x2s = gla(3, x2s, "gla3")
    return [x2.reshape(s) for x2, s in zip(x2s, shapes)], news


def kernel(x_prompt, x_sample, state_gla_0, state_ssm_1, state_conv_1, cache_swa_k_2, cache_swa_v_2, state_gla_3, l0_norm, l0_w_in, l0_w_gk2, l0_b_gk, l0_head_norm, l0_w_out, l1_norm, l1_w_in, l1_conv_w, l1_conv_b, l1_dt_bias, l1_a_log, l1_d_skip, l1_gate_norm, l1_w_out, l2_norm, l2_w_in, l2_sinks, l2_w_out, l3_norm, l3_w_in, l3_w_gk2, l3_b_gk, l3_head_norm, l3_w_out, final_norm):
    p = dict(l0_norm=l0_norm, l0_w_in=l0_w_in, l0_w_gk2=l0_w_gk2, l0_b_gk=l0_b_gk,
             l0_head_norm=l0_head_norm, l0_w_out=l0_w_out,
             l1_norm=l1_norm, l1_w_in=l1_w_in, l1_conv_w=l1_conv_w, l1_conv_b=l1_conv_b,
             l1_dt_bias=l1_dt_bias, l1_a_log=l1_a_log, l1_d_skip=l1_d_skip,
             l1_gate_norm=l1_gate_norm, l1_w_out=l1_w_out,
             l2_norm=l2_norm, l2_w_in=l2_w_in, l2_sinks=l2_sinks, l2_w_out=l2_w_out,
             l3_norm=l3_norm, l3_w_in=l3_w_in, l3_w_gk2=l3_w_gk2, l3_b_gk=l3_b_gk,
             l3_head_norm=l3_head_norm, l3_w_out=l3_w_out, final_norm=final_norm)

    bp = x_prompt.shape[0]
    z = lambda a: jnp.zeros((bp,) + a.shape[1:], a.dtype)
    st_p = dict(gla0=z(state_gla_0), ssm=z(state_ssm_1), conv=z(state_conv_1),
                swa_k=z(cache_swa_k_2), swa_v=z(cache_swa_v_2), gla3=z(state_gla_3))
    st_s = dict(gla0=state_gla_0, ssm=state_ssm_1, conv=state_conv_1,
                swa_k=cache_swa_k_2, swa_v=cache_swa_v_2, gla3=state_gla_3)
    (y_p,), (n_p,) = _trunk([x_prompt], [st_p], p, [False])
    (y_s,), (n_s,) = _trunk([x_sample], [st_s], p, [True])
    return (y_p, y_s,
            n_p["gla0"], n_s["gla0"],
            n_p["ssm"], n_s["ssm"],
            n_p["conv"], n_s["conv"],
            n_p["swa_k"], n_s["swa_k"],
            n_p["swa_v"], n_s["swa_v"],
            n_p["gla3"], n_s["gla3"])
```

```python
import functools
import math

import numpy as np
import jax
import jax.numpy as jnp
from jax import lax
from jax.experimental import pallas as pl
from jax.experimental.pallas import tpu as pltpu

F32 = jnp.float32
BF16 = jnp.bfloat16

NORM_EPS = 1e-6
D_MODEL = 1024
D_INNER = 2048
CHUNK = 64
LANE = 128
BF16_SUBLANES = 16
NEG_BIG = -1e30

GLA_HEADS = 4
GLA_HEAD_K = 128
GLA_HEAD_V = 512
GLA_KEY_DIM = 512
GLA_GATE_RANK = 16
GLA_GATE_NORMALIZER = 16.0

SSD_GROUPS = 4
SSD_HEADS = 32
SSD_HEAD_DIM = 64
SSD_STATE = 128
SSD_CONV = 4
SSD_CONV_DIM = 3072
SSD_GROUP_W = D_INNER // SSD_GROUPS

SWA_WINDOW = 128
SWA_KV_HEADS = 4
SWA_HEAD_DIM = 64
SWA_Q_HEADS = 32
SWA_KV_DIM = 256

VMEM_LIMIT = 52 * 1024 * 1024


def _cparams(sem):
    return pltpu.CompilerParams(dimension_semantics=sem, vmem_limit_bytes=VMEM_LIMIT)


def _dot(a, b):
    return jnp.dot(a.astype(BF16), b.astype(BF16), preferred_element_type=F32)


def _dot_nt(a, b):
    return lax.dot_general(a.astype(BF16), b.astype(BF16), (((1,), (1,)), ((), ())),
                           preferred_element_type=F32)


def _split3(x):
    hi = x.astype(BF16)
    r1 = x - hi.astype(F32)
    mid = r1.astype(BF16)
    lo = (r1 - mid.astype(F32)).astype(BF16)
    return hi, mid, lo


def _sigmoid(x):
    return 1.0 / (1.0 + jnp.exp2(x * (-math.log2(math.e))))


def _softplus(x):
    return jnp.maximum(x, 0.0) + jnp.log(1.0 + jnp.exp(-jnp.abs(x)))


def _mixer_out_dtype(block_rows):
    return BF16 if block_rows % BF16_SUBLANES == 0 else F32


PROJ_ROWS_PER_STEP = 512
PROJ_CAST_ROWS = 256


def _round_up(n, m):
    return -(-n // m) * m


def _group_steps(row_counts, tm):
    n = [t // min(tm, t) for t in row_counts]
    starts = [sum(n[:g]) for g in range(len(n))]
    return n, starts


def _group_row_map(start, nsteps):
    return lambda i: (jnp.clip(i - start, 0, nsteps - 1), 0)


def _norm_proj_kernel(*refs, layout, group_steps, w_transposed):
    step = pl.program_id(0)
    aux_piece = layout["aux"]
    ngroups = len(group_steps)
    nout = 2 if aux_piece is None else 3
    x_refs = refs[0:ngroups]
    nw_ref, w_ref = refs[ngroups], refs[ngroups + 1]
    out_refs = refs[ngroups + 2:ngroups + 2 + nout * ngroups]
    scr = refs[ngroups + 2 + nout * ngroups:]
    wg_scr, wr_scr = scr[0], scr[1]
    wa_scr = scr[2] if aux_piece is not None else None
    d = w_ref.shape[1] if w_transposed else w_ref.shape[0]

    def cast_piece(dst, dst_off, src_off, width):
        wpad = _round_up(width, LANE)
        if w_transposed:
            for r in range(0, width, PROJ_CAST_ROWS):
                n = min(PROJ_CAST_ROWS, width - r)
                dst[dst_off + r:dst_off + r + n, :] = w_ref[src_off + r:src_off + r + n, :].astype(BF16)
            if wpad > width:
                dst[dst_off + width:dst_off + wpad, :] = jnp.zeros((wpad - width, d), BF16)
            return wpad
        for r in range(0, d, PROJ_CAST_ROWS):
            piece = w_ref[r:r + PROJ_CAST_ROWS, src_off:src_off + width]
            if wpad > width:
                piece = jnp.concatenate([piece, jnp.zeros((PROJ_CAST_ROWS, wpad - width), F32)], axis=1)
            dst[r:r + PROJ_CAST_ROWS, dst_off:dst_off + wpad] = piece.astype(BF16)
        return wpad

    def matmul(h, w_scr):
        if w_transposed:
            return lax.dot_general(h, w_scr[...], (((1,), (1,)), ((), ())), preferred_element_type=F32)
        return jnp.dot(h, w_scr[...], preferred_element_type=F32)

    @pl.when(step == 0)
    def _():
        cast_piece(wg_scr, 0, *layout["gate"])
        off = 0
        for src_off, width in layout["rest"]:
            off += cast_piece(wr_scr, off, src_off, width)
        if aux_piece is not None:
            cast_piece(wa_scr, 0, *aux_piece)

    def project(x_ref, outs):
        x = x_ref[...]
        ms = jnp.mean(x * x, axis=-1, keepdims=True)
        h = (x * lax.rsqrt(ms + NORM_EPS) * nw_ref[...]).astype(BF16)
        outs[0][...] = matmul(h, wg_scr).astype(outs[0].dtype)
        outs[1][...] = matmul(h, wr_scr).astype(outs[1].dtype)
        if aux_piece is not None:
            outs[2][...] = matmul(h, wa_scr)

    for g, (start, nsteps) in enumerate(group_steps):
        pl.when((step >= start) & (step < start + nsteps))(
            functools.partial(project, x_refs[g], out_refs[g * nout:(g + 1) * nout]))


def _proj_layout(kind):
    kd, di = GLA_KEY_DIM, D_INNER
    if kind == "gla":
        return dict(gate=(2 * kd + di, di),
                    rest=[(2 * kd, di), (0, kd), (kd, kd), (2 * kd + 2 * di, GLA_GATE_RANK)], aux=None)
    if kind == "ssd":
        return dict(gate=(0, di), rest=[(di, SSD_CONV_DIM)], aux=(di + SSD_CONV_DIM, SSD_HEADS))
    return dict(gate=(di + 2 * SWA_KV_DIM, di), rest=[(0, di + 2 * SWA_KV_DIM)], aux=None)


def _norm_proj(kind, xs, nw, w_in):
    d = xs[0].shape[1]
    layout = _proj_layout(kind)
    ng = layout["gate"][1]
    nr = sum(_round_up(w, LANE) for _, w in layout["rest"])
    has_aux = layout["aux"] is not None
    nsteps, starts = _group_steps([x.shape[0] for x in xs], PROJ_ROWS_PER_STEP)
    fix = lambda i: (0, 0)
    in_specs, out_specs, out_shape = [], [], []
    for x, n, s in zip(xs, nsteps, starts):
        t = x.shape[0]
        tm = t // n
        row = _group_row_map(s, n)
        in_specs.append(pl.BlockSpec((tm, d), row))
        out_specs += [pl.BlockSpec((tm, ng), row), pl.BlockSpec((tm, nr), row)]
        out_shape += [jax.ShapeDtypeStruct((t, ng), BF16), jax.ShapeDtypeStruct((t, nr), BF16)]
        if has_aux:
            out_specs.append(pl.BlockSpec((tm, LANE), row))
            out_shape.append(jax.ShapeDtypeStruct((t, LANE), F32))
    w_transposed = w_in.shape[1] % LANE != 0
    w_op = w_in.T if w_transposed else w_in
    swap = (lambda s: s[::-1]) if w_transposed else (lambda s: s)
    in_specs += [pl.BlockSpec((1, d), fix), pl.BlockSpec(w_op.shape, fix, pipeline_mode=pl.Buffered(1))]
    scratch = [pltpu.VMEM(swap((d, ng)), BF16), pltpu.VMEM(swap((d, nr)), BF16)]
    if has_aux:
        scratch.append(pltpu.VMEM(swap((d, LANE)), BF16))
    outs = pl.pallas_call(
        functools.partial(_norm_proj_kernel, layout=layout, group_steps=tuple(zip(starts, nsteps)),
                          w_transposed=w_transposed),
        grid=(sum(nsteps),),
        in_specs=in_specs,
        out_specs=out_specs,
        out_shape=out_shape,
        scratch_shapes=scratch,
        compiler_params=_cparams(("arbitrary",)),
        name="norm_proj_" + kind,
    )(*xs, nw.reshape(1, d), w_op)
    nout = 3 if has_aux else 2
    return [outs[g * nout:(g + 1) * nout] for g in range(len(xs))]


def _seg_rms(y, seg):
    parts = []
    for s in range(y.shape[1] // seg):
        p = y[:, s * seg:(s + 1) * seg]
        ms = jnp.mean(p * p, axis=-1, keepdims=True)
        parts.append(p * lax.rsqrt(ms + NORM_EPS))
    return jnp.concatenate(parts, axis=1)


def _out_proj_kernel(*refs, mode, final, group_steps):
    ngroups = len(group_steps)
    nw_ref, w_ref, fw_ref = refs[3 * ngroups:3 * ngroups + 3]
    out_refs = refs[3 * ngroups + 3:4 * ngroups + 3]
    w_scr = refs[4 * ngroups + 3]
    step = pl.program_id(0)

    @pl.when(step == 0)
    def _():
        for r in range(0, w_ref.shape[0], PROJ_CAST_ROWS):
            w_scr[r:r + PROJ_CAST_ROWS, :] = w_ref[r:r + PROJ_CAST_ROWS, :].astype(BF16)

    def project(o_ref, gate_ref, x_ref, out_ref):
        o = o_ref[...].astype(F32)
        gt = gate_ref[...].astype(F32)
        act = gt * _sigmoid(gt)
        if mode == "gla":
            y = _seg_rms(o, GLA_HEAD_V) * nw_ref[...] * act
        elif mode == "ssd":
            y = _seg_rms(o * act, SSD_GROUP_W) * nw_ref[...]
        else:
            y = o * act
        out = x_ref[...] + jnp.dot(y.astype(BF16), w_scr[...], preferred_element_type=F32)
        if final:
            ms = jnp.mean(out * out, axis=-1, keepdims=True)
            out = out * lax.rsqrt(ms + NORM_EPS) * fw_ref[...]
        out_ref[...] = out

    for g, (start, nsteps) in enumerate(group_steps):
        pl.when((step >= start) & (step < start + nsteps))(
            functools.partial(project, refs[3 * g], refs[3 * g + 1], refs[3 * g + 2], out_refs[g]))


def _out_proj(mode, os_, gates, nw, w_out, xs, fw, final):
    di = os_[0].shape[1]
    d = xs[0].shape[1]
    nsteps, starts = _group_steps([x.shape[0] for x in xs], PROJ_ROWS_PER_STEP)
    fix = lambda i: (0, 0)
    in_specs, out_specs, out_shape, operands = [], [], [], []
    for o2, g2, x2, n, s in zip(os_, gates, xs, nsteps, starts):
        t = x2.shape[0]
        tm = t // n
        row = _group_row_map(s, n)
        in_specs += [pl.BlockSpec((tm, di), row), pl.BlockSpec((tm, di), row), pl.BlockSpec((tm, d), row)]
        operands += [o2, g2, x2]
        out_specs.append(pl.BlockSpec((tm, d), row))
        out_shape.append(jax.ShapeDtypeStruct((t, d), F32))
    in_specs += [pl.BlockSpec((1, di), fix), pl.BlockSpec((di, d), fix), pl.BlockSpec((1, d), fix)]
    return pl.pallas_call(
        functools.partial(_out_proj_kernel, mode=mode, final=final, group_steps=tuple(zip(starts, nsteps))),
        grid=(sum(nsteps),),
        in_specs=in_specs,
        out_specs=out_specs,
        out_shape=out_shape,
        scratch_shapes=[pltpu.VMEM((di, d), BF16)],
        compiler_params=_cparams(("arbitrary",)),
        name="out_proj_" + mode,
    )(*operands, nw.reshape(1, di), w_out, fw.reshape(1, d))


GLA_CHUNKS_PER_STEP = 4
GLA_LONG_CHUNK = 128
GLA_SINGLE_ANCHOR_MAX_DECAY = 60.0


def _gla_levels(seglen):
    return tuple(seglen >> (i + 1) for i in range(int(math.log2(seglen))))


def _gla_consts(seglen, c):
    levels = _gla_levels(seglen)
    u = np.arange(c)[:, None]
    j = np.arange(c)[None, :]
    same = (u // seglen) == (j // seglen)
    blocks = [same & (j <= u), same & (j > u)]
    lev = np.full((c, c), len(levels) + 1, np.int32)
    for li, h in enumerate(levels):
        b = (u // (2 * h)) * (2 * h) + h - 1
        blocks.append((j > np.minimum(u, b)) & (j <= np.maximum(u, b)))
        sib = (u // (2 * h) == j // (2 * h)) & (u % (2 * h) >= h) & (j % (2 * h) < h)
        lev[sib] = li
    lev[np.eye(c, dtype=bool)] = len(levels)
    m = np.concatenate(blocks, axis=0).astype(np.float32)
    m3 = np.concatenate([m, m, m], axis=1)
    if 2 * c == LANE:
        lev = np.concatenate([lev, lev], axis=1)
    return jnp.asarray(m3, BF16), jnp.asarray(lev)


def _gla_scores(qx, kx, e_q, e_k, lev, nt):
    att = None
    for li, (eq, ek) in enumerate(zip(e_q, e_k)):
        a = nt(qx if eq is None else qx * eq, kx if ek is None else kx * ek)
        att = jnp.where(lev == li, a, 0.0 if att is None else att)
    return att


def _nt_pair(qe, ke):
    kb = ke.astype(BF16)
    z = jnp.zeros((kb.shape[0], GLA_HEAD_K), BF16)
    rhs = jnp.concatenate([jnp.concatenate([kb[:, 0:GLA_HEAD_K], z], axis=1),
                           jnp.concatenate([z, kb[:, GLA_HEAD_K:]], axis=1)], axis=0)
    return lax.dot_general(qe.astype(BF16), rhs, (((1,), (1,)), ((), ())), preferred_element_type=F32)


def _gla_prep(xc, wgk, bgk):
    v = xc[:, 0:D_INNER]
    q = xc[:, D_INNER:D_INNER + GLA_KEY_DIM].astype(F32) * (GLA_HEAD_K ** -0.5)
    k = xc[:, D_INNER + GLA_KEY_DIM:D_INNER + 2 * GLA_KEY_DIM].astype(F32)
    lr = xc[:, D_INNER + 2 * GLA_KEY_DIM:]
    z = _dot(lr, wgk) + bgk
    g = (jnp.minimum(z, 0.0) - jnp.log(1.0 + jnp.exp(-jnp.abs(z)))) * (1.0 / GLA_GATE_NORMALIZER)
    return v, q, k, g


def _gla_chunk(v, q, k, g, state_io, m3, lev, nl, single_anchor):
    c = g.shape[0]
    g3 = jnp.concatenate(_split3(g), axis=0)
    if single_anchor:
        rs = jnp.dot(m3[0:2 * c], g3, preferred_element_type=F32)
        cum = rs[0:c]
        e_in, e_out = jnp.exp(cum), jnp.exp(rs[c:2 * c])
        e_q, e_k = [e_in], [jnp.exp(-cum)]
        lev = jnp.where(lev <= nl, 0, 1)
    else:
        e_all = jnp.exp(jnp.dot(m3, g3, preferred_element_type=F32))
        e_in, e_out = e_all[0:c], e_all[c:2 * c]
        e_q = [e_all[(2 + li) * c:(3 + li) * c] for li in range(nl)] + [None]
        e_k = e_q
    qe_in = q * e_in
    ke_out = k * e_out
    outs = []
    if c == LANE:
        for h in range(GLA_HEADS):
            ks = slice(h * GLA_HEAD_K, (h + 1) * GLA_HEAD_K)
            cut = lambda es: [None if e is None else e[:, ks] for e in es]
            att = _gla_scores(q[:, ks], k[:, ks], cut(e_q), cut(e_k), lev, _dot_nt)
            v_h = v[:, h * GLA_HEAD_V:(h + 1) * GLA_HEAD_V]
            xt = jnp.concatenate([ke_out[:, ks], e_in[:, ks]], axis=0).T
            outs.append(_dot(att, v_h) + state_io(h, qe_in[:, ks], xt, v_h))
        return jnp.concatenate(outs, axis=1)
    lane_lo = lax.broadcasted_iota(jnp.int32, (c, LANE), 1) < c
    for pr in range(GLA_HEADS // 2):
        ls = slice(2 * pr * GLA_HEAD_K, (2 * pr + 2) * GLA_HEAD_K)
        cut = lambda es: [None if e is None else e[:, ls] for e in es]
        att = _gla_scores(q[:, ls], k[:, ls], cut(e_q), cut(e_k), lev, _nt_pair)
        vpair = v[:, 2 * pr * GLA_HEAD_V:(2 * pr + 2) * GLA_HEAD_V]
        vcat = jnp.concatenate([vpair[:, 0:GLA_HEAD_V], vpair[:, GLA_HEAD_V:]], axis=0)
        for half, att_h in enumerate((jnp.where(lane_lo, att, 0.0), jnp.where(lane_lo, 0.0, att))):
            h = 2 * pr + half
            ks = slice(h * GLA_HEAD_K, (h + 1) * GLA_HEAD_K)
            xt = jnp.concatenate([ke_out[:, ks], e_in[:, ks]], axis=0).T
            o_inter = state_io(h, qe_in[:, ks], xt, v[:, h * GLA_HEAD_V:(h + 1) * GLA_HEAD_V])
            outs.append(_dot(att_h, vcat) + o_inter)
    return jnp.concatenate(outs, axis=1)


def _gla_dispatch(preps, run):
    low = None
    for _, _, _, g in preps:
        tot = jnp.min(jnp.sum(g, axis=0, keepdims=True))
        low = tot if low is None else jnp.minimum(low, tot)
    single_anchor_ok = low >= -GLA_SINGLE_ANCHOR_MAX_DECAY
    pl.when(single_anchor_ok)(functools.partial(run, True))
    pl.when(jnp.logical_not(single_anchor_ok))(functools.partial(run, False))


def _gla_carry_kernel(rest_ref, s0_ref, wgk_ref, bgk_ref, m_ref, lev_ref, o_ref, s_ref, *, c, nch, nl):
    @pl.when(pl.program_id(1) == 0)
    def _():
        s_ref[...] = s0_ref[...]

    def state_io(h, qe_in_h, xt, v_h):
        s_h = s_ref[0, h]
        s_ref[0, h] = s_h * xt[:, 2 * c - 1:2 * c] + _dot(xt[:, 0:c], v_h)
        return _dot(qe_in_h, s_h)

    wgk, bgk = wgk_ref[...], bgk_ref[...]
    preps = [_gla_prep(rest_ref[0, ci * c:(ci + 1) * c, :], wgk, bgk) for ci in range(nch)]

    def run(single_anchor):
        for ci, (v, q, k, g) in enumerate(preps):
            o = _gla_chunk(v, q, k, g, state_io, m_ref[...], lev_ref[...], nl, single_anchor)
            o_ref[0, ci * c:(ci + 1) * c, :] = o.astype(o_ref.dtype)

    _gla_dispatch(preps, run)


def _gla_packed_kernel(rest_ref, s0_ref, wgk_ref, bgk_ref, m_ref, lev_ref, o_ref, s_ref, *, seglen, nl):
    c = CHUNK
    nseg = c // seglen
    seg_of_lane = lax.broadcasted_iota(jnp.int32, (GLA_HEAD_K, c), 1) // seglen

    def state_io(h, qe_in_h, xt, v_h):
        ke_t = xt[:, 0:c]
        outs = []
        for sg in range(nseg):
            s_h = s0_ref[sg, h]
            outs.append(_dot(qe_in_h[sg * seglen:(sg + 1) * seglen], s_h))
            end = c + (sg + 1) * seglen - 1
            s_ref[sg, h] = s_h * xt[:, end:end + 1] + _dot(jnp.where(seg_of_lane == sg, ke_t, 0.0), v_h)
        return jnp.concatenate(outs, axis=0)

    preps = [_gla_prep(rest_ref[...], wgk_ref[...], bgk_ref[...])]

    def run(single_anchor):
        v, q, k, g = preps[0]
        o = _gla_chunk(v, q, k, g, state_io, m_ref[...], lev_ref[...], nl, single_anchor)
        o_ref[...] = o.astype(o_ref.dtype)

    _gla_dispatch(preps, run)


def _gla_scan(rest3, s0, w_gk2, b_gk):
    b, l, nr = rest3.shape
    packed = l < CHUNK
    c = CHUNK if packed or l % GLA_LONG_CHUNK else GLA_LONG_CHUNK
    seglen = l if packed else c
    assert c % seglen == 0 and seglen % 8 == 0 and (l % c == 0 or packed)
    nl = len(_gla_levels(seglen))
    m01, lev = _gla_consts(seglen, c)
    wgk = jnp.zeros((LANE, GLA_KEY_DIM), F32).at[:GLA_GATE_RANK].set(w_gk2).astype(BF16)
    consts = (wgk, b_gk.reshape(1, GLA_KEY_DIM), m01, lev)
    if packed:
        nseq = c // seglen
        assert b % nseq == 0
        sblk = (nseq, GLA_HEADS, GLA_HEAD_K, GLA_HEAD_V)
        o, s_fin = pl.pallas_call(
            functools.partial(_gla_packed_kernel, seglen=seglen, nl=nl),
            grid=(b // nseq,),
            in_specs=[pl.BlockSpec((c, nr), lambda i: (i, 0)),
                      pl.BlockSpec(sblk, lambda i: (i, 0, 0, 0))]
                     + [pl.BlockSpec(a.shape, lambda i: (0, 0)) for a in consts],
            out_specs=[pl.BlockSpec((c, D_INNER), lambda i: (i, 0)),
                       pl.BlockSpec(sblk, lambda i: (i, 0, 0, 0))],
            out_shape=[jax.ShapeDtypeStruct((b * l, D_INNER), BF16), jax.ShapeDtypeStruct(s0.shape, F32)],
            compiler_params=_cparams(("parallel",)),
            name="gla_scan_packed",
        )(rest3.reshape(b * l, nr), s0, *consts)
        return o.reshape(b, l, D_INNER), s_fin
    nch = math.gcd(l // c, GLA_CHUNKS_PER_STEP)
    sblk = (1, GLA_HEADS, GLA_HEAD_K, GLA_HEAD_V)
    return pl.pallas_call(
        functools.partial(_gla_carry_kernel, c=c, nch=nch, nl=nl),
        grid=(b, l // (c * nch)),
        in_specs=[pl.BlockSpec((1, c * nch, nr), lambda i, t: (i, t, 0)),
                  pl.BlockSpec(sblk, lambda i, t: (i, 0, 0, 0))]
                 + [pl.BlockSpec(a.shape, lambda i, t: (0, 0)) for a in consts],
        out_specs=[pl.BlockSpec((1, c * nch, D_INNER), lambda i, t: (i, t, 0)),
                   pl.BlockSpec(sblk, lambda i, t: (i, 0, 0, 0))],
        out_shape=[jax.ShapeDtypeStruct((b, l, D_INNER), BF16), jax.ShapeDtypeStruct(s0.shape, F32)],
        compiler_params=_cparams(("parallel", "arbitrary")),
        name="gla_scan",
    )(rest3, s0, *consts)


SSD_CHUNKS_PER_STEP = 4


def _ssd2_consts(seglen):
    c = CHUNK
    u = np.arange(c)
    same = (u[:, None] // seglen) == (u[None, :] // seglen)
    tl = same & (u[None, :] <= u[:, None])
    m = np.concatenate([tl, same], axis=0).astype(np.float32)
    m3 = np.concatenate([m, m, m], axis=1)
    expand = (np.arange(LANE)[:, None] == (np.arange(D_INNER) // SSD_HEAD_DIM)[None, :]).astype(np.float32)
    ex3 = np.concatenate([expand] * 3, axis=0)
    s_of = np.arange(LANE) % c
    tri = np.where(same[:, s_of] & (s_of[None, :] <= u[:, None]), 0.0, NEG_BIG).astype(np.float32)
    bd = ((np.arange(2 * c)[:, None] // c) == (np.arange(LANE)[None, :] // SSD_HEAD_DIM)).astype(np.float32)
    return jnp.asarray(m3, BF16), jnp.asarray(ex3, BF16), jnp.asarray(tri), jnp.asarray(bd)


def _ssd2_conv(ext_ref, rows, cw_ref, cb_ref):
    e = ext_ref[0:8 + rows, :]
    acc = cw_ref[SSD_CONV - 1:SSD_CONV, :] * e[8:]
    for k in range(1, SSD_CONV):
        acc = acc + cw_ref[SSD_CONV - 1 - k:SSD_CONV - k, :] * pltpu.roll(e, k, axis=0)[8:]
    return cb_ref[...] + acc


def _ssd2_chunk(dt_raw, conv, dtb, a_neg, dsk, m3, ex3, tri, bd, state_io):
    c = CHUNK
    xbc = conv * _sigmoid(conv)
    xs = xbc[:, 0:D_INNER]
    bm = xbc[:, D_INNER:D_INNER + SSD_GROUPS * SSD_STATE]
    cm = xbc[:, D_INNER + SSD_GROUPS * SSD_STATE:]
    dt = _softplus(dt_raw + dtb)
    la = dt * a_neg
    rs = jnp.dot(m3, jnp.concatenate(_split3(la), axis=0), preferred_element_type=F32)
    cum, tot = rs[0:c], rs[c:2 * c]
    stack = jnp.concatenate([dt * jnp.exp(tot - cum), cum], axis=0)
    rep = jnp.dot(jnp.concatenate(_split3(stack), axis=1), ex3, preferred_element_type=F32)
    dtw_rep, cum_rep = rep[0:c], rep[c:2 * c]
    ecum_rep = jnp.exp(cum_rep)
    uw = xs * dtw_rep
    at = jnp.concatenate([cum, cum], axis=0).T
    adt = jnp.concatenate([dt, dt], axis=0).T
    lane_lo = lax.broadcasted_iota(jnp.int32, (1, LANE), 1) < c
    ys = []
    for g in range(SSD_GROUPS):
        gs = slice(g * SSD_GROUP_W, (g + 1) * SSD_GROUP_W)
        ns = slice(g * SSD_STATE, (g + 1) * SSD_STATE)
        cg, bg = cm[:, ns], bm[:, ns]
        cb_rep = _dot_nt(cg, jnp.concatenate([bg] * (SSD_GROUP_W // c), axis=0))
        parts = []
        for p in range(SSD_GROUP_W // LANE):
            col = g * (SSD_GROUP_W // LANE) + p
            ps = slice(col * LANE, (col + 1) * LANE)
            cum_s = jnp.where(lane_lo, at[2 * col:2 * col + 1, :], at[2 * col + 1:2 * col + 2, :])
            dt_s = jnp.where(lane_lo, adt[2 * col:2 * col + 1, :], adt[2 * col + 1:2 * col + 2, :])
            w = jnp.exp(cum_rep[:, ps] - cum_s + tri) * (cb_rep[:, p * LANE:(p + 1) * LANE] * dt_s)
            up = xs[:, ps]
            parts.append(_dot(w, jnp.concatenate([up, up], axis=0) * bd))
        y_inter = state_io(g, cg, bg, uw[:, gs], ecum_rep[:, gs])
        ys.append(jnp.concatenate(parts, axis=1) + y_inter + xs[:, gs] * dsk[:, gs])
    return jnp.concatenate(ys, axis=1)


def _ssd2_carry_kernel(rest_ref, dtr_ref, s0_ref, conv0_ref, cw_ref, cb_ref, dtb_ref, alog_ref, dsk_ref,
                       m3_ref, ex3_ref, tri_ref, bd_ref, y_ref, sfin_ref, ext_ref, st_ref, *, nch):
    c = CHUNK
    t = pl.program_id(1)

    @pl.when(t == 0)
    def _():
        ext_ref[0:8, :] = conv0_ref[0]
        for g in range(SSD_GROUPS):
            st_ref[g] = s0_ref[0, g].T

    a_neg = -jnp.exp(alog_ref[...])

    def state_io(g, cg, bg, uw_g, ecum_g):
        s_g = st_ref[g]
        y_inter = _dot(cg, s_g) * ecum_g
        bg_t = jnp.concatenate([bg, jnp.zeros_like(bg)], axis=0).T[:, 0:c]
        st_ref[g] = s_g * ecum_g[c - 1:c, :] + _dot(bg_t, uw_g)
        return y_inter

    for ci in range(nch):
        ext_ref[8:8 + c, :] = rest_ref[0, ci * c:(ci + 1) * c, :].astype(F32)
        conv = _ssd2_conv(ext_ref, c, cw_ref, cb_ref)
        ext_ref[0:8, :] = ext_ref[c:c + 8, :]
        y = _ssd2_chunk(dtr_ref[0, ci * c:(ci + 1) * c, :], conv, dtb_ref[...], a_neg, dsk_ref[...],
                        m3_ref[...], ex3_ref[...], tri_ref[...], bd_ref[...], state_io)
        y_ref[0, ci * c:(ci + 1) * c, :] = y.astype(y_ref.dtype)

    @pl.when(t == pl.num_programs(1) - 1)
    def _():
        for g in range(SSD_GROUPS):
            sfin_ref[0, g] = st_ref[g].T


def _ssd2_packed_kernel(rest_ref, dtr_ref, s0_ref, conv0_ref, cw_ref, cb_ref, dtb_ref, alog_ref, dsk_ref,
                        m3_ref, ex3_ref, tri_ref, bd_ref, y_ref, sfin_ref, ext_ref, *, seglen):
    c = CHUNK
    nseg = c // seglen
    ext_ref[0:8, :] = jnp.zeros((8, SSD_CONV_DIM), F32)
    xbc_raw = rest_ref[...].astype(F32)
    for sg in range(nseg):
        ext_ref[8 + 16 * sg:16 + 16 * sg, :] = conv0_ref[sg]
        ext_ref[16 + 16 * sg:24 + 16 * sg, :] = xbc_raw[sg * seglen:(sg + 1) * seglen]
    conv2 = _ssd2_conv(ext_ref, 2 * c, cw_ref, cb_ref)
    conv = jnp.concatenate([conv2[16 * sg + 8:16 * sg + 16] for sg in range(nseg)], axis=0)
    a_neg = -jnp.exp(alog_ref[...])
    seg_of_row = lax.broadcasted_iota(jnp.int32, (c, SSD_STATE), 0) // seglen

    def state_io(g, cg, bg, uw_g, ecum_g):
        ends = jnp.concatenate([ecum_g[(sg + 1) * seglen - 1:(sg + 1) * seglen, :] for sg in range(nseg)], axis=0)
        fill = jnp.zeros((LANE - c - nseg, SSD_GROUP_W), F32)
        tg = jnp.concatenate([uw_g, ends, fill], axis=0).T
        uw_t = tg[:, 0:c]
        y_parts = []
        for sg in range(nseg):
            s_sg = s0_ref[sg, g]
            y_parts.append(_dot_nt(cg[sg * seglen:(sg + 1) * seglen], s_sg))
            b_sg = jnp.where(seg_of_row == sg, bg, 0.0)
            sfin_ref[sg, g] = s_sg * tg[:, c + sg:c + sg + 1] + _dot(uw_t, b_sg)
        return jnp.concatenate(y_parts, axis=0) * ecum_g

    y = _ssd2_chunk(dtr_ref[...], conv, dtb_ref[...], a_neg, dsk_ref[...], m3_ref[...], ex3_ref[...],
                    tri_ref[...], bd_ref[...], state_io)
    y_ref[...] = y.astype(y_ref.dtype)


def _ssd2_scan(rest3, dt_raw3, s0, conv0, conv_w, conv_b, dt_bias, a_log, d_skip):
    b, l, nr = rest3.shape
    c = CHUNK
    packed = l < c
    seglen = l if packed else c
    assert c % seglen == 0 and seglen % 8 == 0 and (l % c == 0 or packed)
    m3, ex3, tri, bd = _ssd2_consts(seglen)
    s0g = s0.reshape(b, SSD_GROUPS, SSD_GROUP_W, SSD_STATE)
    conv0p = jnp.concatenate([jnp.zeros((b, 8 - (SSD_CONV - 1), SSD_CONV_DIM), F32), conv0], axis=1)
    pad = lambda a: jnp.zeros((1, LANE), F32).at[0, :SSD_HEADS].set(a)
    dsk = jnp.repeat(d_skip, SSD_HEAD_DIM).reshape(1, D_INNER)
    consts = (conv_w, conv_b.reshape(1, -1), pad(dt_bias), pad(a_log), dsk, m3, ex3, tri, bd)
    if packed:
        nseq = c // seglen
        assert b % nseq == 0
        fix = lambda i: (0, 0)
        const_specs = [pl.BlockSpec(a.shape, fix) for a in consts]
        sblk = (nseq, SSD_GROUPS, SSD_GROUP_W, SSD_STATE)
        y, sfin = pl.pallas_call(
            functools.partial(_ssd2_packed_kernel, seglen=seglen),
            grid=(b // nseq,),
            in_specs=[pl.BlockSpec((c, nr), lambda i: (i, 0)),
                      pl.BlockSpec((c, LANE), lambda i: (i, 0)),
                      pl.BlockSpec(sblk, lambda i: (i, 0, 0, 0)),
                      pl.BlockSpec((nseq, 8, SSD_CONV_DIM), lambda i: (i, 0, 0))] + const_specs,
            out_specs=[pl.BlockSpec((c, D_INNER), lambda i: (i, 0)),
                       pl.BlockSpec(sblk, lambda i: (i, 0, 0, 0))],
            out_shape=[jax.ShapeDtypeStruct((b * l, D_INNER), BF16), jax.ShapeDtypeStruct(s0g.shape, F32)],
            scratch_shapes=[pltpu.VMEM((2 * c + 8, SSD_CONV_DIM), F32)],
            compiler_params=_cparams(("parallel",)),
            name="ssd_scan_packed",
        )(rest3.reshape(b * l, nr), dt_raw3.reshape(b * l, LANE), s0g, conv0p, *consts)
        return y.reshape(b, l, D_INNER), sfin.reshape(s0.shape)
    nch = math.gcd(l // c, SSD_CHUNKS_PER_STEP)
    fix2 = lambda i, t: (0, 0)
    const_specs = [pl.BlockSpec(a.shape, fix2) for a in consts]
    sblk = (1, SSD_GROUPS, SSD_GROUP_W, SSD_STATE)
    y, sfin = pl.pallas_call(
        functools.partial(_ssd2_carry_kernel, nch=nch),
        grid=(b, l // (c * nch)),
        in_specs=[pl.BlockSpec((1, c * nch, nr), lambda i, t: (i, t, 0)),
                  pl.BlockSpec((1, c * nch, LANE), lambda i, t: (i, t, 0)),
                  pl.BlockSpec(sblk, lambda i, t: (i, 0, 0, 0)),
                  pl.BlockSpec((1, 8, SSD_CONV_DIM), lambda i, t: (i, 0, 0))] + const_specs,
        out_specs=[pl.BlockSpec((1, c * nch, D_INNER), lambda i, t: (i, t, 0)),
                   pl.BlockSpec(sblk, lambda i, t: (i, 0, 0, 0))],
        out_shape=[jax.ShapeDtypeStruct((b, l, D_INNER), BF16), jax.ShapeDtypeStruct(s0g.shape, F32)],
        scratch_shapes=[pltpu.VMEM((c + 8, SSD_CONV_DIM), F32),
                        pltpu.VMEM((SSD_GROUPS, SSD_STATE, SSD_GROUP_W), F32)],
        compiler_params=_cparams(("parallel", "arbitrary")),
        name="ssd_scan",
    )(rest3, dt_raw3, s0g, conv0p, *consts)
    return y, sfin.reshape(s0.shape)


SWA_BLOCKS_PER_STEP = 2


def _swa_mask(bq, first_block_has_no_past):
    rows = 4 * bq
    tq = (np.arange(rows) % bq)[:, None]
    s = (np.arange(4 * SWA_WINDOW) % (2 * SWA_WINDOW))[None, :]
    ok = (s > tq) & (s <= tq + SWA_WINDOW)
    if first_block_has_no_past:
        ok = ok & (s >= SWA_WINDOW)
    return np.where(ok, 0.0, NEG_BIG).astype(np.float32)


def _block_diag_pair(col, rolled, odd):
    lane = lax.broadcasted_iota(jnp.int32, col.shape, 1)
    lo = lane < SWA_HEAD_DIM
    if odd:
        top = jnp.where(lo, rolled, 0.0)
        bot = jnp.where(lo, 0.0, col)
    else:
        top = jnp.where(lo, col, 0.0)
        bot = jnp.where(lo, 0.0, rolled)
    return jnp.concatenate([top, bot], axis=0)


def _swa_kernel(sink_ref, x_ref, kprev_ref, vprev_ref, mask_ref, ones_ref, o_ref, *cache_refs, bq, items, nsub):
    w = SWA_WINDOW
    n = pl.program_id(1)
    npair = (SWA_Q_HEADS // SWA_KV_HEADS) // 2
    rows = npair * bq
    lane_lo = lax.broadcasted_iota(jnp.int32, (rows, LANE), 1) < SWA_HEAD_DIM
    sinks = []
    for j in range(SWA_KV_HEADS):
        sinks.append([jnp.concatenate(
            [jnp.full((bq, 1), sink_ref[j * 2 * npair + 2 * p + half], F32) for p in range(npair)], axis=0)
            for half in range(2)])

    kcols = slice(D_INNER, D_INNER + SWA_KV_DIM)
    vcols = slice(D_INNER + SWA_KV_DIM, D_INNER + 2 * SWA_KV_DIM)
    work = []
    for i in range(items * nsub):
        if nsub == 1:
            x = x_ref[i]
            kprev, vprev = kprev_ref[i], vprev_ref[i]
            amask = mask_ref[jnp.where(n == 0, 0, 1)]
            out_rows = (i, slice(None))
        else:
            x = x_ref[0, i * bq:(i + 1) * bq, :]
            if i == 0:
                kprev, vprev = kprev_ref[0], vprev_ref[0]
                amask = mask_ref[jnp.where(n == 0, 0, 1)]
            else:
                kprev = x_ref[0, (i - 1) * bq:i * bq, kcols]
                vprev = x_ref[0, (i - 1) * bq:i * bq, vcols]
                amask = mask_ref[1]
            out_rows = (0, slice(i * bq, (i + 1) * bq))
        fill = [] if bq == w else [jnp.zeros((w - bq, SWA_KV_DIM), F32)]
        kall = jnp.concatenate([kprev, x[:, kcols]] + fill, axis=0)
        vall = jnp.concatenate([vprev, x[:, vcols]] + fill, axis=0)
        if cache_refs:
            cache_refs[0][i] = kall[bq:bq + w]
            cache_refs[1][i] = vall[bq:bq + w]
        work.append((x, kall, vall, amask, out_rows))

    nwork = len(work)
    tile_rows = lambda a: jnp.concatenate([a] * nwork, axis=0)
    swap = lambda a: jnp.concatenate([a[:, SWA_HEAD_DIM:], a[:, 0:SWA_HEAD_DIM]], axis=1)
    lane_lo_all = tile_rows(lane_lo)
    for j in range(SWA_KV_HEADS):
        cs = slice((j // 2) * LANE, (j // 2 + 1) * LANE)
        qbase = j * npair * LANE
        scs, v2es = [], []
        for x, kall, vall, amask, _ in work:
            kcol, vcol = kall[:, cs], vall[:, cs]
            k2 = _block_diag_pair(kcol, swap(kcol), j % 2)
            v2 = _block_diag_pair(vcol, swap(vcol), j % 2)
            v2es.append(jnp.concatenate([v2.astype(BF16), ones_ref[...]], axis=1))
            qs = jnp.concatenate([x[:, qbase + p * LANE:qbase + (p + 1) * LANE] for p in range(npair)],
                                 axis=0) * (SWA_HEAD_DIM ** -0.5)
            scs.append(_dot_nt(qs, k2) + amask)
        sc = jnp.concatenate(scs, axis=0)
        pes, ms = [], []
        for half in range(2):
            sh = sc[:, half * 2 * w:(half + 1) * 2 * w]
            sink = tile_rows(sinks[j][half])
            m = jnp.maximum(jnp.max(sh, axis=-1, keepdims=True), sink)
            pes.append(jnp.exp(sh - m).astype(BF16))
            ms.append(sink - m)
        pe = jnp.concatenate(pes, axis=1)
        o = jnp.concatenate([jnp.dot(pe[i * rows:(i + 1) * rows], v2es[i], preferred_element_type=F32)
                             for i in range(nwork)], axis=0)
        esink = jnp.exp(jnp.where(lane_lo_all, ms[0], ms[1]))
        res = o[:, 0:LANE] / (o[:, LANE:] + esink)
        for i, (_, _, _, _, out_rows) in enumerate(work):
            for p in range(npair):
                o_ref[out_rows[0], out_rows[1], qbase + p * LANE:qbase + (p + 1) * LANE] = (
                    res[i * rows + p * bq:i * rows + (p + 1) * bq].astype(o_ref.dtype))


def _swa_attn(rest3, kprev, vprev, sinks, has_past):
    b, l, nr = rest3.shape
    w = SWA_WINDOW
    bq = math.gcd(l, w)
    nb = l // bq
    assert nb == 1 or bq == w
    m_later = _swa_mask(bq, False)
    m_first = m_later if has_past else _swa_mask(bq, True)
    masks = jnp.asarray(np.stack([m_first, m_later]))
    ones2 = jnp.asarray((np.arange(4 * w)[:, None] // (2 * w)) == (np.arange(LANE)[None, :] // SWA_HEAD_DIM), BF16)
    if nb == 1:
        rest3 = rest3.astype(F32)
        items, nsub = math.gcd(b, 8), 1
        kspec = pl.BlockSpec((items, w, SWA_KV_DIM), lambda i, n: (i, 0, 0))
        vspec = kspec
        kin, vin = kprev, vprev
    else:
        assert not has_past
        items, nsub = 1, math.gcd(nb, SWA_BLOCKS_PER_STEP)
        kcol = D_INNER // SWA_KV_DIM
        kspec = pl.BlockSpec((1, w, SWA_KV_DIM), lambda i, n: (i, jnp.maximum(n * nsub - 1, 0), kcol))
        vspec = pl.BlockSpec((1, w, SWA_KV_DIM), lambda i, n: (i, jnp.maximum(n * nsub - 1, 0), kcol + 1))
        kin, vin = rest3, rest3
    out_specs = [pl.BlockSpec((items, bq * nsub, D_INNER), lambda i, n: (i, n, 0))]
    out_shape = [jax.ShapeDtypeStruct((b, l, D_INNER), _mixer_out_dtype(bq))]
    if nb == 1:
        out_specs += [pl.BlockSpec((items, w, SWA_KV_DIM), lambda i, n: (i, 0, 0))] * 2
        out_shape += [jax.ShapeDtypeStruct((b, w, SWA_KV_DIM), F32)] * 2
    return pl.pallas_call(
        functools.partial(_swa_kernel, bq=bq, items=items, nsub=nsub),
        grid=(b // items, nb // nsub),
        in_specs=[
            pl.BlockSpec(memory_space=pltpu.SMEM),
            pl.BlockSpec((items, bq * nsub, nr), lambda i, n: (i, n, 0)),
            kspec,
            vspec,
            pl.BlockSpec(masks.shape, lambda i, n: (0, 0, 0)),
            pl.BlockSpec(ones2.shape, lambda i, n: (0, 0)),
        ],
        out_specs=out_specs,
        out_shape=out_shape,
        compiler_params=_cparams(("parallel", "parallel")),
        name="swa_attn",
    )(sinks, rest3, kin, vin, masks, ones2)


def _trunk(xs, sts, p, has_pasts):
    shapes = [x.shape for x in xs]
    x2s = [x.reshape(-1, x.shape[-1]) for x in xs]
    ones_inner = jnp.ones((D_INNER,), F32)
    news = [{} for _ in xs]
    flat = lambda a: a.reshape(-1, a.shape[-1])

    def gla(i, x2s, key):
        proj = _norm_proj("gla", x2s, p[f"l{i}_norm"], p[f"l{i}_w_in"])
        os_ = []
        for g, ((b, l, _), (gate, rest)) in enumerate(zip(shapes, proj)):
            o, news[g][key] = _gla_scan(rest.reshape(b, l, -1), sts[g][key], p[f"l{i}_w_gk2"], p[f"l{i}_b_gk"])
            os_.append(flat(o))
        hn = jnp.tile(p[f"l{i}_head_norm"], GLA_HEADS)
        return _out_proj("gla", os_, [pr[0] for pr in proj], hn, p[f"l{i}_w_out"], x2s, p["final_norm"], i == 3)

    x2s = gla(0, x2s, "gla0")

    proj = _norm_proj("ssd", x2s, p["l1_norm"], p["l1_w_in"])
    os_ = []
    for g, ((b, l, _), (gate, rest, dt_raw)) in enumerate(zip(shapes, proj)):
        st = sts[g]
        rest3 = rest.reshape(b, l, -1)
        y, news[g]["ssm"] = _ssd2_scan(rest3, dt_raw.reshape(b, l, -1), st["ssm"], st["conv"], p["l1_conv_w"],
                                       p["l1_conv_b"], p["l1_dt_bias"], p["l1_a_log"], p["l1_d_skip"])
        ext_tail = jnp.concatenate([st["conv"], rest3[:, max(l - (SSD_CONV - 1), 0):, :].astype(F32)], axis=1)
        news[g]["conv"] = ext_tail[:, -(SSD_CONV - 1):]
        os_.append(flat(y))
    x2s = _out_proj("ssd", os_, [pr[0] for pr in proj], p["l1_gate_norm"], p["l1_w_out"], x2s,
                    p["final_norm"], False)

    proj = _norm_proj("swa", x2s, p["l2_norm"], p["l2_w_in"])
    os_ = []
    for g, ((b, l, _), (gate, rest)) in enumerate(zip(shapes, proj)):
        st = sts[g]
        rest3 = rest.reshape(b, l, -1)
        kprev = st["swa_k"].reshape(b, SWA_WINDOW, SWA_KV_DIM)
        vprev = st["swa_v"].reshape(b, SWA_WINDOW, SWA_KV_DIM)
        res = _swa_attn(rest3, kprev, vprev, p["l2_sinks"], has_pasts[g])
        if len(res) == 3:
            k_win, v_win = res[1], res[2]
        else:
            k_win = rest3[:, l - SWA_WINDOW:, D_INNER:D_INNER + SWA_KV_DIM].astype(F32)
            v_win = rest3[:, l - SWA_WINDOW:, D_INNER + SWA_KV_DIM:].astype(F32)
        news[g]["swa_k"] = k_win.reshape(st["swa_k"].shape)
        news[g]["swa_v"] = v_win.reshape(st["swa_v"].shape)
        os_.append(flat(res[0]))
    x2s = _out_proj("swa", os_, [pr[0] for pr in proj], ones_inner, p["l2_w_out"], x2s, p["final_norm"], False)

    x2s = gla(3, x2s, "gla3")
    return [x2.reshape(s) for x2, s in zip(x2s, shapes)], news


def kernel(x_prompt, x_sample, state_gla_0, state_ssm_1, state_conv_1, cache_swa_k_2, cache_swa_v_2, state_gla_3, l0_norm, l0_w_in, l0_w_gk2, l0_b_gk, l0_head_norm, l0_w_out, l1_norm, l1_w_in, l1_conv_w, l1_conv_b, l1_dt_bias, l1_a_log, l1_d_skip, l1_gate_norm, l1_w_out, l2_norm, l2_w_in, l2_sinks, l2_w_out, l3_norm, l3_w_in, l3_w_gk2, l3_b_gk, l3_head_norm, l3_w_out, final_norm):
    p = dict(l0_norm=l0_norm, l0_w_in=l0_w_in, l0_w_gk2=l0_w_gk2, l0_b_gk=l0_b_gk,
             l0_head_norm=l0_head_norm, l0_w_out=l0_w_out,
             l1_norm=l1_norm, l1_w_in=l1_w_in, l1_conv_w=l1_conv_w, l1_conv_b=l1_conv_b,
             l1_dt_bias=l1_dt_bias, l1_a_log=l1_a_log, l1_d_skip=l1_d_skip,
             l1_gate_norm=l1_gate_norm, l1_w_out=l1_w_out,
             l2_norm=l2_norm, l2_w_in=l2_w_in, l2_sinks=l2_sinks, l2_w_out=l2_w_out,
             l3_norm=l3_norm, l3_w_in=l3_w_in, l3_w_gk2=l3_w_gk2, l3_b_gk=l3_b_gk,
             l3_head_norm=l3_head_norm, l3_w_out=l3_w_out, final_norm=final_norm)

    bp = x_prompt.shape[0]
    z = lambda a: jnp.zeros((bp,) + a.shape[1:], a.dtype)
    st_p = dict(gla0=z(state_gla_0), ssm=z(state_ssm_1), conv=z(state_conv_1),
                swa_k=z(cache_swa_k_2), swa_v=z(cache_swa_v_2), gla3=z(state_gla_3))
    st_s = dict(gla0=state_gla_0, ssm=state_ssm_1, conv=state_conv_1,
                swa_k=cache_swa_k_2, swa_v=cache_swa_v_2, gla3=state_gla_3)
    (y_p,), (n_p,) = _trunk([x_prompt], [st_p], p, [False])
    (y_s,), (n_s,) = _trunk([x_sample], [st_s], p, [True])
    return (y_p, y_s,
            n_p["gla0"], n_s["gla0"],
            n_p["ssm"], n_s["ssm"],
            n_p["conv"], n_s["conv"],
            n_p["swa_k"], n_s["swa_k"],
            n_p["swa_v"], n_s["swa_v"],
            n_p["gla3"], n_s["gla3"])
```

```python
import functools
import math

import numpy as np
import jax
import jax.numpy as jnp
from jax import lax
from jax.experimental import pallas as pl
from jax.experimental.pallas import tpu as pltpu

F32 = jnp.float32
BF16 = jnp.bfloat16

NORM_EPS = 1e-6
D_MODEL = 1024
D_INNER = 2048
CHUNK = 64
LANE = 128
BF16_SUBLANES = 16
NEG_BIG = -1e30

GLA_HEADS = 4
GLA_HEAD_K = 128
GLA_HEAD_V = 512
GLA_KEY_DIM = 512
GLA_GATE_RANK = 16
GLA_GATE_NORMALIZER = 16.0

SSD_GROUPS = 4
SSD_HEADS = 32
SSD_HEAD_DIM = 64
SSD_STATE = 128
SSD_CONV = 4
SSD_CONV_DIM = 3072
SSD_GROUP_W = D_INNER // SSD_GROUPS

SWA_WINDOW = 128
SWA_KV_HEADS = 4
SWA_HEAD_DIM = 64
SWA_Q_HEADS = 32
SWA_KV_DIM = 256

VMEM_LIMIT = 52 * 1024 * 1024


def _cparams(sem):
    return pltpu.CompilerParams(dimension_semantics=sem, vmem_limit_bytes=VMEM_LIMIT)


def _dot(a, b):
    return jnp.dot(a.astype(BF16), b.astype(BF16), preferred_element_type=F32)


def _dot_nt(a, b):
    return lax.dot_general(a.astype(BF16), b.astype(BF16), (((1,), (1,)), ((), ())),
                           preferred_element_type=F32)


def _split3(x):
    hi = x.astype(BF16)
    r1 = x - hi.astype(F32)
    mid = r1.astype(BF16)
    lo = (r1 - mid.astype(F32)).astype(BF16)
    return hi, mid, lo


def _sigmoid(x):
    return 1.0 / (1.0 + jnp.exp2(x * (-math.log2(math.e))))


def _softplus(x):
    return jnp.maximum(x, 0.0) + jnp.log(1.0 + jnp.exp(-jnp.abs(x)))


def _mixer_out_dtype(block_rows):
    return BF16 if block_rows % BF16_SUBLANES == 0 else F32


PROJ_ROWS_PER_STEP = 512
PROJ_CAST_ROWS = 256


def _round_up(n, m):
    return -(-n // m) * m


def _group_steps(row_counts, tm):
    n = [t // min(tm, t) for t in row_counts]
    starts = [sum(n[:g]) for g in range(len(n))]
    return n, starts


def _group_row_map(start, nsteps):
    return lambda i: (jnp.clip(i - start, 0, nsteps - 1), 0)


def _norm_proj_kernel(*refs, layout, group_steps, w_transposed):
    step = pl.program_id(0)
    aux_piece = layout["aux"]
    ngroups = len(group_steps)
    nout = 2 if aux_piece is None else 3
    x_refs = refs[0:ngroups]
    nw_ref, w_ref = refs[ngroups], refs[ngroups + 1]
    out_refs = refs[ngroups + 2:ngroups + 2 + nout * ngroups]
    scr = refs[ngroups + 2 + nout * ngroups:]
    wg_scr, wr_scr = scr[0], scr[1]
    wa_scr = scr[2] if aux_piece is not None else None
    d = w_ref.shape[1] if w_transposed else w_ref.shape[0]

    def cast_piece(dst, dst_off, src_off, width):
        wpad = _round_up(width, LANE)
        if w_transposed:
            for r in range(0, width, PROJ_CAST_ROWS):
                n = min(PROJ_CAST_ROWS, width - r)
                dst[dst_off + r:dst_off + r + n, :] = w_ref[src_off + r:src_off + r + n, :].astype(BF16)
            if wpad > width:
                dst[dst_off + width:dst_off + wpad, :] = jnp.zeros((wpad - width, d), BF16)
            return wpad
        for r in range(0, d, PROJ_CAST_ROWS):
            piece = w_ref[r:r + PROJ_CAST_ROWS, src_off:src_off + width]
            if wpad > width:
                piece = jnp.concatenate([piece, jnp.zeros((PROJ_CAST_ROWS, wpad - width), F32)], axis=1)
            dst[r:r + PROJ_CAST_ROWS, dst_off:dst_off + wpad] = piece.astype(BF16)
        return wpad

    def matmul(h, w_scr):
        if w_transposed:
            return lax.dot_general(h, w_scr[...], (((1,), (1,)), ((), ())), preferred_element_type=F32)
        return jnp.dot(h, w_scr[...], preferred_element_type=F32)

    @pl.when(step == 0)
    def _():
        cast_piece(wg_scr, 0, *layout["gate"])
        off = 0
        for src_off, width in layout["rest"]:
            off += cast_piece(wr_scr, off, src_off, width)
        if aux_piece is not None:
            cast_piece(wa_scr, 0, *aux_piece)

    def project(x_ref, outs):
        x = x_ref[...]
        ms = jnp.mean(x * x, axis=-1, keepdims=True)
        h = (x * lax.rsqrt(ms + NORM_EPS) * nw_ref[...]).astype(BF16)
        outs[0][...] = matmul(h, wg_scr).astype(outs[0].dtype)
        outs[1][...] = matmul(h, wr_scr).astype(outs[1].dtype)
        if aux_piece is not None:
            outs[2][...] = matmul(h, wa_scr)

    for g, (start, nsteps) in enumerate(group_steps):
        pl.when((step >= start) & (step < start + nsteps))(
            functools.partial(project, x_refs[g], out_refs[g * nout:(g + 1) * nout]))


def _proj_layout(kind):
    kd, di = GLA_KEY_DIM, D_INNER
    if kind == "gla":
        return dict(gate=(2 * kd + di, di),
                    rest=[(2 * kd, di), (0, kd), (kd, kd), (2 * kd + 2 * di, GLA_GATE_RANK)], aux=None)
    if kind == "ssd":
        return dict(gate=(0, di), rest=[(di, SSD_CONV_DIM)], aux=(di + SSD_CONV_DIM, SSD_HEADS))
    return dict(gate=(di + 2 * SWA_KV_DIM, di), rest=[(0, di + 2 * SWA_KV_DIM)], aux=None)


def _norm_proj(kind, xs, nw, w_in):
    d = xs[0].shape[1]
    layout = _proj_layout(kind)
    ng = layout["gate"][1]
    nr = sum(_round_up(w, LANE) for _, w in layout["rest"])
    has_aux = layout["aux"] is not None
    nsteps, starts = _group_steps([x.shape[0] for x in xs], PROJ_ROWS_PER_STEP)
    fix = lambda i: (0, 0)
    in_specs, out_specs, out_shape = [], [], []
    for x, n, s in zip(xs, nsteps, starts):
        t = x.shape[0]
        tm = t // n
        row = _group_row_map(s, n)
        in_specs.append(pl.BlockSpec((tm, d), row))
        out_specs += [pl.BlockSpec((tm, ng), row), pl.BlockSpec((tm, nr), row)]
        out_shape += [jax.ShapeDtypeStruct((t, ng), BF16), jax.ShapeDtypeStruct((t, nr), BF16)]
        if has_aux:
            out_specs.append(pl.BlockSpec((tm, LANE), row))
            out_shape.append(jax.ShapeDtypeStruct((t, LANE), F32))
    w_transposed = w_in.shape[1] % LANE != 0
    w_op = w_in.T if w_transposed else w_in
    swap = (lambda s: s[::-1]) if w_transposed else (lambda s: s)
    in_specs += [pl.BlockSpec((1, d), fix), pl.BlockSpec(w_op.shape, fix, pipeline_mode=pl.Buffered(1))]
    scratch = [pltpu.VMEM(swap((d, ng)), BF16), pltpu.VMEM(swap((d, nr)), BF16)]
    if has_aux:
        scratch.append(pltpu.VMEM(swap((d, LANE)), BF16))
    outs = pl.pallas_call(
        functools.partial(_norm_proj_kernel, layout=layout, group_steps=tuple(zip(starts, nsteps)),
                          w_transposed=w_transposed),
        grid=(sum(nsteps),),
        in_specs=in_specs,
        out_specs=out_specs,
        out_shape=out_shape,
        scratch_shapes=scratch,
        compiler_params=_cparams(("arbitrary",)),
        name="norm_proj_" + kind,
    )(*xs, nw.reshape(1, d), w_op)
    nout = 3 if has_aux else 2
    return [outs[g * nout:(g + 1) * nout] for g in range(len(xs))]


def _seg_rms(y, seg):
    parts = []
    for s in range(y.shape[1] // seg):
        p = y[:, s * seg:(s + 1) * seg]
        ms = jnp.mean(p * p, axis=-1, keepdims=True)
        parts.append(p * lax.rsqrt(ms + NORM_EPS))
    return jnp.concatenate(parts, axis=1)


def _out_proj_kernel(*refs, mode, final, group_steps):
    ngroups = len(group_steps)
    nw_ref, w_ref, fw_ref = refs[3 * ngroups:3 * ngroups + 3]
    out_refs = refs[3 * ngroups + 3:4 * ngroups + 3]
    w_scr = refs[4 * ngroups + 3]
    step = pl.program_id(0)

    @pl.when(step == 0)
    def _():
        for r in range(0, w_ref.shape[0], PROJ_CAST_ROWS):
            w_scr[r:r + PROJ_CAST_ROWS, :] = w_ref[r:r + PROJ_CAST_ROWS, :].astype(BF16)

    def project(o_ref, gate_ref, x_ref, out_ref):
        o = o_ref[...].astype(F32)
        gt = gate_ref[...].astype(F32)
        act = gt * _sigmoid(gt)
        if mode == "gla":
            y = _seg_rms(o, GLA_HEAD_V) * nw_ref[...] * act
        elif mode == "ssd":
            y = _seg_rms(o * act, SSD_GROUP_W) * nw_ref[...]
        else:
            y = o * act
        out = x_ref[...] + jnp.dot(y.astype(BF16), w_scr[...], preferred_element_type=F32)
        if final:
            ms = jnp.mean(out * out, axis=-1, keepdims=True)
            out = out * lax.rsqrt(ms + NORM_EPS) * fw_ref[...]
        out_ref[...] = out

    for g, (start, nsteps) in enumerate(group_steps):
        pl.when((step >= start) & (step < start + nsteps))(
            functools.partial(project, refs[3 * g], refs[3 * g + 1], refs[3 * g + 2], out_refs[g]))


def _out_proj(mode, os_, gates, nw, w_out, xs, fw, final):
    di = os_[0].shape[1]
    d = xs[0].shape[1]
    nsteps, starts = _group_steps([x.shape[0] for x in xs], PROJ_ROWS_PER_STEP)
    fix = lambda i: (0, 0)
    in_specs, out_specs, out_shape, operands = [], [], [], []
    for o2, g2, x2, n, s in zip(os_, gates, xs, nsteps, starts):
        t = x2.shape[0]
        tm = t // n
        row = _group_row_map(s, n)
        in_specs += [pl.BlockSpec((tm, di), row), pl.BlockSpec((tm, di), row), pl.BlockSpec((tm, d), row)]
        operands += [o2, g2, x2]
        out_specs.append(pl.BlockSpec((tm, d), row))
        out_shape.append(jax.ShapeDtypeStruct((t, d), F32))
    in_specs += [pl.BlockSpec((1, di), fix), pl.BlockSpec((di, d), fix), pl.BlockSpec((1, d), fix)]
    return pl.pallas_call(
        functools.partial(_out_proj_kernel, mode=mode, final=final, group_steps=tuple(zip(starts, nsteps))),
        grid=(sum(nsteps),),
        in_specs=in_specs,
        out_specs=out_specs,
        out_shape=out_shape,
        scratch_shapes=[pltpu.VMEM((di, d), BF16)],
        compiler_params=_cparams(("arbitrary",)),
        name="out_proj_" + mode,
    )(*operands, nw.reshape(1, di), w_out, fw.reshape(1, d))


GLA_CHUNKS_PER_STEP = 4
GLA_LONG_CHUNK = 128
GLA_SINGLE_ANCHOR_MAX_DECAY = 60.0


def _gla_levels(seglen):
    return tuple(seglen >> (i + 1) for i in range(int(math.log2(seglen))))


def _gla_consts(seglen, c):
    levels = _gla_levels(seglen)
    u = np.arange(c)[:, None]
    j = np.arange(c)[None, :]
    same = (u // seglen) == (j // seglen)
    blocks = [same & (j <= u), same & (j > u)]
    lev = np.full((c, c), len(levels) + 1, np.int32)
    for li, h in enumerate(levels):
        b = (u // (2 * h)) * (2 * h) + h - 1
        blocks.append((j > np.minimum(u, b)) & (j <= np.maximum(u, b)))
        sib = (u // (2 * h) == j // (2 * h)) & (u % (2 * h) >= h) & (j % (2 * h) < h)
        lev[sib] = li
    lev[np.eye(c, dtype=bool)] = len(levels)
    m = np.concatenate(blocks, axis=0).astype(np.float32)
    m3 = np.concatenate([m, m, m], axis=1)
    if 2 * c == LANE:
        lev = np.concatenate([lev, lev], axis=1)
    return jnp.asarray(m3, BF16), jnp.asarray(lev)


def _gla_scores(qx, kx, e_q, e_k, lev, nt):
    att = None
    for li, (eq, ek) in enumerate(zip(e_q, e_k)):
        a = nt(qx if eq is None else qx * eq, kx if ek is None else kx * ek)
        att = jnp.where(lev == li, a, 0.0 if att is None else att)
    return att


def _nt_pair(qe, ke):
    kb = ke.astype(BF16)
    z = jnp.zeros((kb.shape[0], GLA_HEAD_K), BF16)
    rhs = jnp.concatenate([jnp.concatenate([kb[:, 0:GLA_HEAD_K], z], axis=1),
                           jnp.concatenate([z, kb[:, GLA_HEAD_K:]], axis=1)], axis=0)
    return lax.dot_general(qe.astype(BF16), rhs, (((1,), (1,)), ((), ())), preferred_element_type=F32)


def _gla_prep(xc, wgk, bgk):
    v = xc[:, 0:D_INNER]
    q = xc[:, D_INNER:D_INNER + GLA_KEY_DIM].astype(F32) * (GLA_HEAD_K ** -0.5)
    k = xc[:, D_INNER + GLA_KEY_DIM:D_INNER + 2 * GLA_KEY_DIM].astype(F32)
    lr = xc[:, D_INNER + 2 * GLA_KEY_DIM:]
    z = _dot(lr, wgk) + bgk
    g = (jnp.minimum(z, 0.0) - jnp.log(1.0 + jnp.exp(-jnp.abs(z)))) * (1.0 / GLA_GATE_NORMALIZER)
    return v, q, k, g


def _gla_chunk(v, q, k, g, m3, lev, nl, single_anchor):
    c = g.shape[0]
    g3 = jnp.concatenate(_split3(g), axis=0)
    if single_anchor:
        rs = jnp.dot(m3[0:2 * c], g3, preferred_element_type=F32)
        cum = rs[0:c]
        e_in, e_out = jnp.exp(cum), jnp.exp(rs[c:2 * c])
        e_q, e_k = [e_in], [jnp.exp(-cum)]
        lev = jnp.where(lev <= nl, 0, 1)
    else:
        e_all = jnp.exp(jnp.dot(m3, g3, preferred_element_type=F32))
        e_in, e_out = e_all[0:c], e_all[c:2 * c]
        e_q = [e_all[(2 + li) * c:(3 + li) * c] for li in range(nl)] + [None]
        e_k = e_q
    qe_in = q * e_in
    ke_out = k * e_out
    outs, pending = [], []
    if c == LANE:
        for h in range(GLA_HEADS):
            ks = slice(h * GLA_HEAD_K, (h + 1) * GLA_HEAD_K)
            cut = lambda es: [None if e is None else e[:, ks] for e in es]
            att = _gla_scores(q[:, ks], k[:, ks], cut(e_q), cut(e_k), lev, _dot_nt)
            v_h = v[:, h * GLA_HEAD_V:(h + 1) * GLA_HEAD_V]
            xt = jnp.concatenate([ke_out[:, ks], e_in[:, ks]], axis=0).T
            outs.append(_dot(att, v_h))
            pending.append((h, qe_in[:, ks], xt, v_h))
        return outs, pending
    lane_lo = lax.broadcasted_iota(jnp.int32, (c, LANE), 1) < c
    for pr in range(GLA_HEADS // 2):
        ls = slice(2 * pr * GLA_HEAD_K, (2 * pr + 2) * GLA_HEAD_K)
        cut = lambda es: [None if e is None else e[:, ls] for e in es]
        att = _gla_scores(q[:, ls], k[:, ls], cut(e_q), cut(e_k), lev, _nt_pair)
        vpair = v[:, 2 * pr * GLA_HEAD_V:(2 * pr + 2) * GLA_HEAD_V]
        vcat = jnp.concatenate([vpair[:, 0:GLA_HEAD_V], vpair[:, GLA_HEAD_V:]], axis=0)
        for half, att_h in enumerate((jnp.where(lane_lo, att, 0.0), jnp.where(lane_lo, 0.0, att))):
            h = 2 * pr + half
            ks = slice(h * GLA_HEAD_K, (h + 1) * GLA_HEAD_K)
            xt = jnp.concatenate([ke_out[:, ks], e_in[:, ks]], axis=0).T
            outs.append(_dot(att_h, vcat))
            pending.append((h, qe_in[:, ks], xt, v[:, h * GLA_HEAD_V:(h + 1) * GLA_HEAD_V]))
    return outs, pending


def _gla_finish(outs, pending, state_io):
    return jnp.concatenate([o + state_io(*p) for o, p in zip(outs, pending)], axis=1)


def _gla_dispatch(preps, run):
    low = None
    for _, _, _, g in preps:
        tot = jnp.min(jnp.sum(g, axis=0, keepdims=True))
        low = tot if low is None else jnp.minimum(low, tot)
    single_anchor_ok = low >= -GLA_SINGLE_ANCHOR_MAX_DECAY
    pl.when(single_anchor_ok)(functools.partial(run, True))
    pl.when(jnp.logical_not(single_anchor_ok))(functools.partial(run, False))


def _gla_carry_kernel(rest_ref, s0_ref, wgk_ref, bgk_ref, m_ref, lev_ref, o_ref, s_ref, *, c, nch, nl):
    @pl.when(pl.program_id(1) == 0)
    def _():
        s_ref[...] = s0_ref[...]

    def state_io(h, qe_in_h, xt, v_h):
        s_h = s_ref[0, h]
        s_ref[0, h] = s_h * xt[:, 2 * c - 1:2 * c] + _dot(xt[:, 0:c], v_h)
        return _dot(qe_in_h, s_h)

    wgk, bgk = wgk_ref[...], bgk_ref[...]
    preps = [_gla_prep(rest_ref[0, ci * c:(ci + 1) * c, :], wgk, bgk) for ci in range(nch)]

    def run(single_anchor):
        staged = [_gla_chunk(v, q, k, g, m_ref[...], lev_ref[...], nl, single_anchor) for v, q, k, g in preps]
        for ci, (outs, pending) in enumerate(staged):
            o_ref[0, ci * c:(ci + 1) * c, :] = _gla_finish(outs, pending, state_io).astype(o_ref.dtype)

    _gla_dispatch(preps, run)


def _gla_packed_kernel(rest_ref, s0_ref, wgk_ref, bgk_ref, m_ref, lev_ref, o_ref, s_ref, *, seglen, nl):
    c = CHUNK
    nseg = c // seglen
    seg_of_lane = lax.broadcasted_iota(jnp.int32, (GLA_HEAD_K, c), 1) // seglen

    def state_io(h, qe_in_h, xt, v_h):
        ke_t = xt[:, 0:c]
        outs = []
        for sg in range(nseg):
            s_h = s0_ref[sg, h]
            outs.append(_dot(qe_in_h[sg * seglen:(sg + 1) * seglen], s_h))
            end = c + (sg + 1) * seglen - 1
            s_ref[sg, h] = s_h * xt[:, end:end + 1] + _dot(jnp.where(seg_of_lane == sg, ke_t, 0.0), v_h)
        return jnp.concatenate(outs, axis=0)

    preps = [_gla_prep(rest_ref[...], wgk_ref[...], bgk_ref[...])]

    def run(single_anchor):
        v, q, k, g = preps[0]
        outs, pending = _gla_chunk(v, q, k, g, m_ref[...], lev_ref[...], nl, single_anchor)
        o_ref[...] = _gla_finish(outs, pending, state_io).astype(o_ref.dtype)

    _gla_dispatch(preps, run)


def _gla_scan(rest3, s0, w_gk2, b_gk):
    b, l, nr = rest3.shape
    packed = l < CHUNK
    c = CHUNK if packed or l % GLA_LONG_CHUNK else GLA_LONG_CHUNK
    seglen = l if packed else c
    assert c % seglen == 0 and seglen % 8 == 0 and (l % c == 0 or packed)
    nl = len(_gla_levels(seglen))
    m01, lev = _gla_consts(seglen, c)
    wgk = jnp.zeros((LANE, GLA_KEY_DIM), F32).at[:GLA_GATE_RANK].set(w_gk2).astype(BF16)
    consts = (wgk, b_gk.reshape(1, GLA_KEY_DIM), m01, lev)
    if packed:
        nseq = c // seglen
        assert b % nseq == 0
        sblk = (nseq, GLA_HEADS, GLA_HEAD_K, GLA_HEAD_V)
        o, s_fin = pl.pallas_call(
            functools.partial(_gla_packed_kernel, seglen=seglen, nl=nl),
            grid=(b // nseq,),
            in_specs=[pl.BlockSpec((c, nr), lambda i: (i, 0)),
                      pl.BlockSpec(sblk, lambda i: (i, 0, 0, 0))]
                     + [pl.BlockSpec(a.shape, lambda i: (0, 0)) for a in consts],
            out_specs=[pl.BlockSpec((c, D_INNER), lambda i: (i, 0)),
                       pl.BlockSpec(sblk, lambda i: (i, 0, 0, 0))],
            out_shape=[jax.ShapeDtypeStruct((b * l, D_INNER), BF16), jax.ShapeDtypeStruct(s0.shape, F32)],
            compiler_params=_cparams(("parallel",)),
            name="gla_scan_packed",
        )(rest3.reshape(b * l, nr), s0, *consts)
        return o.reshape(b, l, D_INNER), s_fin
    nch = math.gcd(l // c, GLA_CHUNKS_PER_STEP)
    sblk = (1, GLA_HEADS, GLA_HEAD_K, GLA_HEAD_V)
    return pl.pallas_call(
        functools.partial(_gla_carry_kernel, c=c, nch=nch, nl=nl),
        grid=(b, l // (c * nch)),
        in_specs=[pl.BlockSpec((1, c * nch, nr), lambda i, t: (i, t, 0)),
                  pl.BlockSpec(sblk, lambda i, t: (i, 0, 0, 0))]
                 + [pl.BlockSpec(a.shape, lambda i, t: (0, 0)) for a in consts],
        out_specs=[pl.BlockSpec((1, c * nch, D_INNER), lambda i, t: (i, t, 0)),
                   pl.BlockSpec(sblk, lambda i, t: (i, 0, 0, 0))],
        out_shape=[jax.ShapeDtypeStruct((b, l, D_INNER), BF16), jax.ShapeDtypeStruct(s0.shape, F32)],
        compiler_params=_cparams(("parallel", "arbitrary")),
        name="gla_scan",
    )(rest3, s0, *consts)


SSD_CHUNKS_PER_STEP = 4


def _ssd2_consts(seglen):
    c = CHUNK
    u = np.arange(c)
    same = (u[:, None] // seglen) == (u[None, :] // seglen)
    tl = same & (u[None, :] <= u[:, None])
    m = np.concatenate([tl, same], axis=0).astype(np.float32)
    m3 = np.concatenate([m, m, m], axis=1)
    expand = (np.arange(LANE)[:, None] == (np.arange(D_INNER) // SSD_HEAD_DIM)[None, :]).astype(np.float32)
    ex3 = np.concatenate([expand] * 3, axis=0)
    s_of = np.arange(LANE) % c
    tri = np.where(same[:, s_of] & (s_of[None, :] <= u[:, None]), 0.0, NEG_BIG).astype(np.float32)
    bd = ((np.arange(2 * c)[:, None] // c) == (np.arange(LANE)[None, :] // SSD_HEAD_DIM)).astype(np.float32)
    return jnp.asarray(m3, BF16), jnp.asarray(ex3, BF16), jnp.asarray(tri), jnp.asarray(bd)


def _ssd2_conv(ext_ref, rows, cw_ref, cb_ref):
    e = ext_ref[0:8 + rows, :]
    acc = cw_ref[SSD_CONV - 1:SSD_CONV, :] * e[8:]
    for k in range(1, SSD_CONV):
        acc = acc + cw_ref[SSD_CONV - 1 - k:SSD_CONV - k, :] * pltpu.roll(e, k, axis=0)[8:]
    return cb_ref[...] + acc


def _ssd2_chunk(dt_raw, conv, dtb, a_neg, dsk, m3, ex3, tri, bd, state_io):
    c = CHUNK
    xbc = conv * _sigmoid(conv)
    xs = xbc[:, 0:D_INNER]
    bm = xbc[:, D_INNER:D_INNER + SSD_GROUPS * SSD_STATE]
    cm = xbc[:, D_INNER + SSD_GROUPS * SSD_STATE:]
    dt = _softplus(dt_raw + dtb)
    la = dt * a_neg
    rs = jnp.dot(m3, jnp.concatenate(_split3(la), axis=0), preferred_element_type=F32)
    cum, tot = rs[0:c], rs[c:2 * c]
    stack = jnp.concatenate([dt * jnp.exp(tot - cum), cum], axis=0)
    rep = jnp.dot(jnp.concatenate(_split3(stack), axis=1), ex3, preferred_element_type=F32)
    dtw_rep, cum_rep = rep[0:c], rep[c:2 * c]
    ecum_rep = jnp.exp(cum_rep)
    uw = xs * dtw_rep
    at = jnp.concatenate([cum, cum], axis=0).T
    adt = jnp.concatenate([dt, dt], axis=0).T
    lane_lo = lax.broadcasted_iota(jnp.int32, (1, LANE), 1) < c
    ys = []
    for g in range(SSD_GROUPS):
        gs = slice(g * SSD_GROUP_W, (g + 1) * SSD_GROUP_W)
        ns = slice(g * SSD_STATE, (g + 1) * SSD_STATE)
        cg, bg = cm[:, ns], bm[:, ns]
        cb_rep = _dot_nt(cg, jnp.concatenate([bg] * (SSD_GROUP_W // c), axis=0))
        parts = []
        for p in range(SSD_GROUP_W // LANE):
            col = g * (SSD_GROUP_W // LANE) + p
            ps = slice(col * LANE, (col + 1) * LANE)
            cum_s = jnp.where(lane_lo, at[2 * col:2 * col + 1, :], at[2 * col + 1:2 * col + 2, :])
            dt_s = jnp.where(lane_lo, adt[2 * col:2 * col + 1, :], adt[2 * col + 1:2 * col + 2, :])
            w = jnp.exp(cum_rep[:, ps] - cum_s + tri) * (cb_rep[:, p * LANE:(p + 1) * LANE] * dt_s)
            up = xs[:, ps]
            parts.append(_dot(w, jnp.concatenate([up, up], axis=0) * bd))
        y_inter = state_io(g, cg, bg, uw[:, gs], ecum_rep[:, gs])
        ys.append(jnp.concatenate(parts, axis=1) + y_inter + xs[:, gs] * dsk[:, gs])
    return jnp.concatenate(ys, axis=1)


def _ssd2_carry_kernel(rest_ref, dtr_ref, s0_ref, conv0_ref, cw_ref, cb_ref, dtb_ref, alog_ref, dsk_ref,
                       m3_ref, ex3_ref, tri_ref, bd_ref, y_ref, sfin_ref, ext_ref, st_ref, *, nch):
    c = CHUNK
    t = pl.program_id(1)

    @pl.when(t == 0)
    def _():
        ext_ref[0:8, :] = conv0_ref[0]
        for g in range(SSD_GROUPS):
            st_ref[g] = s0_ref[0, g].T

    a_neg = -jnp.exp(alog_ref[...])

    def state_io(g, cg, bg, uw_g, ecum_g):
        s_g = st_ref[g]
        y_inter = _dot(cg, s_g) * ecum_g
        bg_t = jnp.concatenate([bg, jnp.zeros_like(bg)], axis=0).T[:, 0:c]
        st_ref[g] = s_g * ecum_g[c - 1:c, :] + _dot(bg_t, uw_g)
        return y_inter

    for ci in range(nch):
        ext_ref[8:8 + c, :] = rest_ref[0, ci * c:(ci + 1) * c, :].astype(F32)
        conv = _ssd2_conv(ext_ref, c, cw_ref, cb_ref)
        ext_ref[0:8, :] = ext_ref[c:c + 8, :]
        y = _ssd2_chunk(dtr_ref[0, ci * c:(ci + 1) * c, :], conv, dtb_ref[...], a_neg, dsk_ref[...],
                        m3_ref[...], ex3_ref[...], tri_ref[...], bd_ref[...], state_io)
        y_ref[0, ci * c:(ci + 1) * c, :] = y.astype(y_ref.dtype)

    @pl.when(t == pl.num_programs(1) - 1)
    def _():
        for g in range(SSD_GROUPS):
            sfin_ref[0, g] = st_ref[g].T


def _ssd2_packed_kernel(rest_ref, dtr_ref, s0_ref, conv0_ref, cw_ref, cb_ref, dtb_ref, alog_ref, dsk_ref,
                        m3_ref, ex3_ref, tri_ref, bd_ref, y_ref, sfin_ref, ext_ref, *, seglen):
    c = CHUNK
    nseg = c // seglen
    ext_ref[0:8, :] = jnp.zeros((8, SSD_CONV_DIM), F32)
    xbc_raw = rest_ref[...].astype(F32)
    for sg in range(nseg):
        ext_ref[8 + 16 * sg:16 + 16 * sg, :] = conv0_ref[sg]
        ext_ref[16 + 16 * sg:24 + 16 * sg, :] = xbc_raw[sg * seglen:(sg + 1) * seglen]
    conv2 = _ssd2_conv(ext_ref, 2 * c, cw_ref, cb_ref)
    conv = jnp.concatenate([conv2[16 * sg + 8:16 * sg + 16] for sg in range(nseg)], axis=0)
    a_neg = -jnp.exp(alog_ref[...])
    seg_of_row = lax.broadcasted_iota(jnp.int32, (c, SSD_STATE), 0) // seglen

    def state_io(g, cg, bg, uw_g, ecum_g):
        ends = jnp.concatenate([ecum_g[(sg + 1) * seglen - 1:(sg + 1) * seglen, :] for sg in range(nseg)], axis=0)
        fill = jnp.zeros((LANE - c - nseg, SSD_GROUP_W), F32)
        tg = jnp.concatenate([uw_g, ends, fill], axis=0).T
        uw_t = tg[:, 0:c]
        y_parts = []
        for sg in range(nseg):
            s_sg = s0_ref[sg, g]
            y_parts.append(_dot_nt(cg[sg * seglen:(sg + 1) * seglen], s_sg))
            b_sg = jnp.where(seg_of_row == sg, bg, 0.0)
            sfin_ref[sg, g] = s_sg * tg[:, c + sg:c + sg + 1] + _dot(uw_t, b_sg)
        return jnp.concatenate(y_parts, axis=0) * ecum_g

    y = _ssd2_chunk(dtr_ref[...], conv, dtb_ref[...], a_neg, dsk_ref[...], m3_ref[...], ex3_ref[...],
                    tri_ref[...], bd_ref[...], state_io)
    y_ref[...] = y.astype(y_ref.dtype)


def _ssd2_scan(rest3, dt_raw3, s0, conv0, conv_w, conv_b, dt_bias, a_log, d_skip):
    b, l, nr = rest3.shape
    c = CHUNK
    packed = l < c
    seglen = l if packed else c
    assert c % seglen == 0 and seglen % 8 == 0 and (l % c == 0 or packed)
    m3, ex3, tri, bd = _ssd2_consts(seglen)
    s0g = s0.reshape(b, SSD_GROUPS, SSD_GROUP_W, SSD_STATE)
    conv0p = jnp.concatenate([jnp.zeros((b, 8 - (SSD_CONV - 1), SSD_CONV_DIM), F32), conv0], axis=1)
    pad = lambda a: jnp.zeros((1, LANE), F32).at[0, :SSD_HEADS].set(a)
    dsk = jnp.repeat(d_skip, SSD_HEAD_DIM).reshape(1, D_INNER)
    consts = (conv_w, conv_b.reshape(1, -1), pad(dt_bias), pad(a_log), dsk, m3, ex3, tri, bd)
    if packed:
        nseq = c // seglen
        assert b % nseq == 0
        fix = lambda i: (0, 0)
        const_specs = [pl.BlockSpec(a.shape, fix) for a in consts]
        sblk = (nseq, SSD_GROUPS, SSD_GROUP_W, SSD_STATE)
        y, sfin = pl.pallas_call(
            functools.partial(_ssd2_packed_kernel, seglen=seglen),
            grid=(b // nseq,),
            in_specs=[pl.BlockSpec((c, nr), lambda i: (i, 0)),
                      pl.BlockSpec((c, LANE), lambda i: (i, 0)),
                      pl.BlockSpec(sblk, lambda i: (i, 0, 0, 0)),
                      pl.BlockSpec((nseq, 8, SSD_CONV_DIM), lambda i: (i, 0, 0))] + const_specs,
            out_specs=[pl.BlockSpec((c, D_INNER), lambda i: (i, 0)),
                       pl.BlockSpec(sblk, lambda i: (i, 0, 0, 0))],
            out_shape=[jax.ShapeDtypeStruct((b * l, D_INNER), BF16), jax.ShapeDtypeStruct(s0g.shape, F32)],
            scratch_shapes=[pltpu.VMEM((2 * c + 8, SSD_CONV_DIM), F32)],
            compiler_params=_cparams(("parallel",)),
            name="ssd_scan_packed",
        )(rest3.reshape(b * l, nr), dt_raw3.reshape(b * l, LANE), s0g, conv0p, *consts)
        return y.reshape(b, l, D_INNER), sfin.reshape(s0.shape)
    nch = math.gcd(l // c, SSD_CHUNKS_PER_STEP)
    fix2 = lambda i, t: (0, 0)
    const_specs = [pl.BlockSpec(a.shape, fix2) for a in consts]
    sblk = (1, SSD_GROUPS, SSD_GROUP_W, SSD_STATE)
    y, sfin = pl.pallas_call(
        functools.partial(_ssd2_carry_kernel, nch=nch),
        grid=(b, l // (c * nch)),
        in_specs=[pl.BlockSpec((1, c * nch, nr), lambda i, t: (i, t, 0)),
                  pl.BlockSpec((1, c * nch, LANE), lambda i, t: (i, t, 0)),
                  pl.BlockSpec(sblk, lambda i, t: (i, 0, 0, 0)),
                  pl.BlockSpec((1, 8, SSD_CONV_DIM), lambda i, t: (i, 0, 0))] + const_specs,
        out_specs=[pl.BlockSpec((1, c * nch, D_INNER), lambda i, t: (i, t, 0)),
                   pl.BlockSpec(sblk, lambda i, t: (i, 0, 0, 0))],
        out_shape=[jax.ShapeDtypeStruct((b, l, D_INNER), BF16), jax.ShapeDtypeStruct(s0g.shape, F32)],
        scratch_shapes=[pltpu.VMEM((c + 8, SSD_CONV_DIM), F32),
                        pltpu.VMEM((SSD_GROUPS, SSD_STATE, SSD_GROUP_W), F32)],
        compiler_params=_cparams(("parallel", "arbitrary")),
        name="ssd_scan",
    )(rest3, dt_raw3, s0g, conv0p, *consts)
    return y, sfin.reshape(s0.shape)


SWA_SEQS_PER_STEP = 16
SWA_BLOCKS_PER_STEP = 8


def _swa_mask(bq, first_block_has_no_past):
    rows = 4 * bq
    tq = (np.arange(rows) % bq)[:, None]
    s = (np.arange(4 * SWA_WINDOW) % (2 * SWA_WINDOW))[None, :]
    ok = (s > tq) & (s <= tq + SWA_WINDOW)
    if first_block_has_no_past:
        ok = ok & (s >= SWA_WINDOW)
    return np.where(ok, 0.0, NEG_BIG).astype(np.float32)


def _block_diag_pair(col, rolled, odd):
    lane = lax.broadcasted_iota(jnp.int32, col.shape, 1)
    lo = lane < SWA_HEAD_DIM
    if odd:
        top = jnp.where(lo, rolled, 0.0)
        bot = jnp.where(lo, 0.0, col)
    else:
        top = jnp.where(lo, col, 0.0)
        bot = jnp.where(lo, 0.0, rolled)
    return jnp.concatenate([top, bot], axis=0)


def _swa_kernel(sink_ref, x_ref, kprev_ref, vprev_ref, mask_ref, ones_ref, o_ref, *cache_refs, bq, items, nsub):
    w = SWA_WINDOW
    n = pl.program_id(1)
    npair = (SWA_Q_HEADS // SWA_KV_HEADS) // 2
    rows = npair * bq
    lane_lo = lax.broadcasted_iota(jnp.int32, (rows, LANE), 1) < SWA_HEAD_DIM
    sinks = []
    for j in range(SWA_KV_HEADS):
        sinks.append([jnp.concatenate(
            [jnp.full((bq, 1), sink_ref[j * 2 * npair + 2 * p + half], F32) for p in range(npair)], axis=0)
            for half in range(2)])

    kcols = slice(D_INNER, D_INNER + SWA_KV_DIM)
    vcols = slice(D_INNER + SWA_KV_DIM, D_INNER + 2 * SWA_KV_DIM)
    work = []
    for i in range(items * nsub):
        if nsub == 1:
            x = x_ref[i]
            kprev, vprev = kprev_ref[i], vprev_ref[i]
            amask = mask_ref[jnp.where(n == 0, 0, 1)]
            out_rows = (i, slice(None))
        else:
            x = x_ref[0, i * bq:(i + 1) * bq, :]
            if i == 0:
                kprev, vprev = kprev_ref[0], vprev_ref[0]
                amask = mask_ref[jnp.where(n == 0, 0, 1)]
            else:
                kprev = x_ref[0, (i - 1) * bq:i * bq, kcols]
                vprev = x_ref[0, (i - 1) * bq:i * bq, vcols]
                amask = mask_ref[1]
            out_rows = (0, slice(i * bq, (i + 1) * bq))
        fill = [] if bq == w else [jnp.zeros((w - bq, SWA_KV_DIM), F32)]
        kall = jnp.concatenate([kprev, x[:, kcols]] + fill, axis=0)
        vall = jnp.concatenate([vprev, x[:, vcols]] + fill, axis=0)
        if cache_refs:
            cache_refs[0][i] = kall[bq:bq + w]
            cache_refs[1][i] = vall[bq:bq + w]
        work.append((x, kall, vall, amask, out_rows))

    nwork = len(work)
    tile_rows = lambda a: jnp.concatenate([a] * nwork, axis=0)
    swap = lambda a: jnp.concatenate([a[:, SWA_HEAD_DIM:], a[:, 0:SWA_HEAD_DIM]], axis=1)
    lane_lo_all = tile_rows(lane_lo)
    for j in range(SWA_KV_HEADS):
        cs = slice((j // 2) * LANE, (j // 2 + 1) * LANE)
        qbase = j * npair * LANE
        scs, v2es = [], []
        for x, kall, vall, amask, _ in work:
            kcol, vcol = kall[:, cs], vall[:, cs]
            k2 = _block_diag_pair(kcol, swap(kcol), j % 2)
            v2 = _block_diag_pair(vcol, swap(vcol), j % 2)
            v2es.append(jnp.concatenate([v2.astype(BF16), ones_ref[...]], axis=1))
            qs = jnp.concatenate([x[:, qbase + p * LANE:qbase + (p + 1) * LANE] for p in range(npair)],
                                 axis=0) * (SWA_HEAD_DIM ** -0.5)
            scs.append(_dot_nt(qs, k2) + amask)
        sc = jnp.concatenate(scs, axis=0)
        pes, ms = [], []
        for half in range(2):
            sh = sc[:, half * 2 * w:(half + 1) * 2 * w]
            sink = tile_rows(sinks[j][half])
            m = jnp.maximum(jnp.max(sh, axis=-1, keepdims=True), sink)
            pes.append(jnp.exp(sh - m).astype(BF16))
            ms.append(sink - m)
        pe = jnp.concatenate(pes, axis=1)
        o = jnp.concatenate([jnp.dot(pe[i * rows:(i + 1) * rows], v2es[i], preferred_element_type=F32)
                             for i in range(nwork)], axis=0)
        esink = jnp.exp(jnp.where(lane_lo_all, ms[0], ms[1]))
        res = o[:, 0:LANE] / (o[:, LANE:] + esink)
        for i, (_, _, _, _, out_rows) in enumerate(work):
            for p in range(npair):
                o_ref[out_rows[0], out_rows[1], qbase + p * LANE:qbase + (p + 1) * LANE] = (
                    res[i * rows + p * bq:i * rows + (p + 1) * bq].astype(o_ref.dtype))


def _swa_attn(rest3, kprev, vprev, sinks, has_past):
    b, l, nr = rest3.shape
    w = SWA_WINDOW
    bq = math.gcd(l, w)
    nb = l // bq
    assert nb == 1 or bq == w
    m_later = _swa_mask(bq, False)
    m_first = m_later if has_past else _swa_mask(bq, True)
    masks = jnp.asarray(np.stack([m_first, m_later]))
    ones2 = jnp.asarray((np.arange(4 * w)[:, None] // (2 * w)) == (np.arange(LANE)[None, :] // SWA_HEAD_DIM), BF16)
    if nb == 1:
        rest3 = rest3.astype(F32)
        items, nsub = math.gcd(b, SWA_SEQS_PER_STEP), 1
        kspec = pl.BlockSpec((items, w, SWA_KV_DIM), lambda i, n: (i, 0, 0))
        vspec = kspec
        kin, vin = kprev, vprev
    else:
        assert not has_past
        items, nsub = 1, math.gcd(nb, SWA_BLOCKS_PER_STEP)
        kcol = D_INNER // SWA_KV_DIM
        kspec = pl.BlockSpec((1, w, SWA_KV_DIM), lambda i, n: (i, jnp.maximum(n * nsub - 1, 0), kcol))
        vspec = pl.BlockSpec((1, w, SWA_KV_DIM), lambda i, n: (i, jnp.maximum(n * nsub - 1, 0), kcol + 1))
        kin, vin = rest3, rest3
    out_specs = [pl.BlockSpec((items, bq * nsub, D_INNER), lambda i, n: (i, n, 0))]
    out_shape = [jax.ShapeDtypeStruct((b, l, D_INNER), _mixer_out_dtype(bq))]
    if nb == 1:
        out_specs += [pl.BlockSpec((items, w, SWA_KV_DIM), lambda i, n: (i, 0, 0))] * 2
        out_shape += [jax.ShapeDtypeStruct((b, w, SWA_KV_DIM), F32)] * 2
    return pl.pallas_call(
        functools.partial(_swa_kernel, bq=bq, items=items, nsub=nsub),
        grid=(b // items, nb // nsub),
        in_specs=[
            pl.BlockSpec(memory_space=pltpu.SMEM),
            pl.BlockSpec((items, bq * nsub, nr), lambda i, n: (i, n, 0)),
            kspec,
            vspec,
            pl.BlockSpec(masks.shape, lambda i, n: (0, 0, 0)),
            pl.BlockSpec(ones2.shape, lambda i, n: (0, 0)),
        ],
        out_specs=out_specs,
        out_shape=out_shape,
        compiler_params=_cparams(("parallel", "parallel")),
        name="swa_attn",
    )(sinks, rest3, kin, vin, masks, ones2)


def _trunk(xs, sts, p, has_pasts):
    shapes = [x.shape for x in xs]
    x2s = [x.reshape(-1, x.shape[-1]) for x in xs]
    ones_inner = jnp.ones((D_INNER,), F32)
    news = [{} for _ in xs]
    flat = lambda a: a.reshape(-1, a.shape[-1])

    def gla(i, x2s, key):
        proj = _norm_proj("gla", x2s, p[f"l{i}_norm"], p[f"l{i}_w_in"])
        os_ = []
        for g, ((b, l, _), (gate, rest)) in enumerate(zip(shapes, proj)):
            o, news[g][key] = _gla_scan(rest.reshape(b, l, -1), sts[g][key], p[f"l{i}_w_gk2"], p[f"l{i}_b_gk"])
            os_.append(flat(o))
        hn = jnp.tile(p[f"l{i}_head_norm"], GLA_HEADS)
        return _out_proj("gla", os_, [pr[0] for pr in proj], hn, p[f"l{i}_w_out"], x2s, p["final_norm"], i == 3)

    x2s = gla(0, x2s, "gla0")

    proj = _norm_proj("ssd", x2s, p["l1_norm"], p["l1_w_in"])
    os_ = []
    for g, ((b, l, _), (gate, rest, dt_raw)) in enumerate(zip(shapes, proj)):
        st = sts[g]
        rest3 = rest.reshape(b, l, -1)
        y, news[g]["ssm"] = _ssd2_scan(rest3, dt_raw.reshape(b, l, -1), st["ssm"], st["conv"], p["l1_conv_w"],
                                       p["l1_conv_b"], p["l1_dt_bias"], p["l1_a_log"], p["l1_d_skip"])
        ext_tail = jnp.concatenate([st["conv"], rest3[:, max(l - (SSD_CONV - 1), 0):, :].astype(F32)], axis=1)
        news[g]["conv"] = ext_tail[:, -(SSD_CONV - 1):]
        os_.append(flat(y))
    x2s = _out_proj("ssd", os_, [pr[0] for pr in proj], p["l1_gate_norm"], p["l1_w_out"], x2s,
                    p["final_norm"], False)

    proj = _norm_proj("swa", x2s, p["l2_norm"], p["l2_w_in"])
    os_ = []
    for g, ((b, l, _), (gate, rest)) in enumerate(zip(shapes, proj)):
        st = sts[g]
        rest3 = rest.reshape(b, l, -1)
        kprev = st["swa_k"].reshape(b, SWA_WINDOW, SWA_KV_DIM)
        vprev = st["swa_v"].reshape(b, SWA_WINDOW, SWA_KV_DIM)
        res = _swa_attn(rest3, kprev, vprev, p["l2_sinks"], has_pasts[g])
        if len(res) == 3:
            k_win, v_win = res[1], res[2]
        else:
            k_win = rest3[:, l - SWA_WINDOW:, D_INNER:D_INNER + SWA_KV_DIM].astype(F32)
            v_win = rest3[:, l - SWA_WINDOW:, D_INNER + SWA_KV_DIM:].astype(F32)
        news[g]["swa_k"] = k_win.reshape(st["swa_k"].shape)
        news[g]["swa_v"] = v_win.reshape(st["swa_v"].shape)
        os_.append(flat(res[0]))
    x2s = _out_proj("swa", os_, [pr[0] for pr in proj], ones_inner, p["l2_w_out"], x2s, p["final_norm"], False)

    x2s = gla(3, x2s, "gla3")
    return [x2.reshape(s) for x2, s in zip(x2s, shapes)], news


def kernel(x_prompt, x_sample, state_gla_0, state_ssm_1, state_conv_1, cache_swa_k_2, cache_swa_v_2, state_gla_3, l0_norm, l0_w_in, l0_w_gk2, l0_b_gk, l0_head_norm, l0_w_out, l1_norm, l1_w_in, l1_conv_w, l1_conv_b, l1_dt_bias, l1_a_log, l1_d_skip, l1_gate_norm, l1_w_out, l2_norm, l2_w_in, l2_sinks, l2_w_out, l3_norm, l3_w_in, l3_w_gk2, l3_b_gk, l3_head_norm, l3_w_out, final_norm):
    p = dict(l0_norm=l0_norm, l0_w_in=l0_w_in, l0_w_gk2=l0_w_gk2, l0_b_gk=l0_b_gk,
             l0_head_norm=l0_head_norm, l0_w_out=l0_w_out,
             l1_norm=l1_norm, l1_w_in=l1_w_in, l1_conv_w=l1_conv_w, l1_conv_b=l1_conv_b,
             l1_dt_bias=l1_dt_bias, l1_a_log=l1_a_log, l1_d_skip=l1_d_skip,
             l1_gate_norm=l1_gate_norm, l1_w_out=l1_w_out,
             l2_norm=l2_norm, l2_w_in=l2_w_in, l2_sinks=l2_sinks, l2_w_out=l2_w_out,
             l3_norm=l3_norm, l3_w_in=l3_w_in, l3_w_gk2=l3_w_gk2, l3_b_gk=l3_b_gk,
             l3_head_norm=l3_head_norm, l3_w_out=l3_w_out, final_norm=final_norm)

    bp = x_prompt.shape[0]
    z = lambda a: jnp.zeros((bp,) + a.shape[1:], a.dtype)
    st_p = dict(gla0=z(state_gla_0), ssm=z(state_ssm_1), conv=z(state_conv_1),
                swa_k=z(cache_swa_k_2), swa_v=z(cache_swa_v_2), gla3=z(state_gla_3))
    st_s = dict(gla0=state_gla_0, ssm=state_ssm_1, conv=state_conv_1,
                swa_k=cache_swa_k_2, swa_v=cache_swa_v_2, gla3=state_gla_3)
    (y_p,), (n_p,) = _trunk([x_prompt], [st_p], p, [False])
    (y_s,), (n_s,) = _trunk([x_sample], [st_s], p, [True])
    return (y_p, y_s,
            n_p["gla0"], n_s["gla0"],
            n_p["ssm"], n_s["ssm"],
            n_p["conv"], n_s["conv"],
            n_p["swa_k"], n_s["swa_k"],
            n_p["swa_v"], n_s["swa_v"],
            n_p["gla3"], n_s["gla3"])
```

```python
import functools
import math

import numpy as np
import jax
import jax.numpy as jnp
from jax import lax
from jax.experimental import pallas as pl
from jax.experimental.pallas import tpu as pltpu

F32 = jnp.float32
BF16 = jnp.bfloat16

NORM_EPS = 1e-6
D_MODEL = 1024
D_INNER = 2048
CHUNK = 64
LANE = 128
BF16_SUBLANES = 16
NEG_BIG = -1e30

GLA_HEADS = 4
GLA_HEAD_K = 128
GLA_HEAD_V = 512
GLA_KEY_DIM = 512
GLA_GATE_RANK = 16
GLA_GATE_NORMALIZER = 16.0

SSD_GROUPS = 4
SSD_HEADS = 32
SSD_HEAD_DIM = 64
SSD_STATE = 128
SSD_CONV = 4
SSD_CONV_DIM = 3072
SSD_GROUP_W = D_INNER // SSD_GROUPS

SWA_WINDOW = 128
SWA_KV_HEADS = 4
SWA_HEAD_DIM = 64
SWA_Q_HEADS = 32
SWA_KV_DIM = 256

VMEM_LIMIT = 52 * 1024 * 1024


def _cparams(sem):
    return pltpu.CompilerParams(dimension_semantics=sem, vmem_limit_bytes=VMEM_LIMIT)


def _dot(a, b):
    return jnp.dot(a.astype(BF16), b.astype(BF16), preferred_element_type=F32)


def _dot_nt(a, b):
    return lax.dot_general(a.astype(BF16), b.astype(BF16), (((1,), (1,)), ((), ())),
                           preferred_element_type=F32)


def _split3(x):
    hi = x.astype(BF16)
    r1 = x - hi.astype(F32)
    mid = r1.astype(BF16)
    lo = (r1 - mid.astype(F32)).astype(BF16)
    return hi, mid, lo


def _sigmoid(x):
    return 1.0 / (1.0 + jnp.exp2(x * (-math.log2(math.e))))


def _softplus(x):
    return jnp.maximum(x, 0.0) + jnp.log(1.0 + jnp.exp(-jnp.abs(x)))


def _mixer_out_dtype(block_rows):
    return BF16 if block_rows % BF16_SUBLANES == 0 else F32


PROJ_ROWS_PER_STEP = 512
PROJ_CAST_ROWS = 256


def _round_up(n, m):
    return -(-n // m) * m


def _group_steps(row_counts, tm):
    n = [t // min(tm, t) for t in row_counts]
    starts = [sum(n[:g]) for g in range(len(n))]
    return n, starts


def _group_row_map(start, nsteps):
    return lambda i: (jnp.clip(i - start, 0, nsteps - 1), 0)


def _norm_proj_kernel(*refs, layout, group_steps, w_transposed):
    step = pl.program_id(0)
    aux_piece = layout["aux"]
    ngroups = len(group_steps)
    nout = 2 if aux_piece is None else 3
    x_refs = refs[0:ngroups]
    nw_ref, w_ref = refs[ngroups], refs[ngroups + 1]
    out_refs = refs[ngroups + 2:ngroups + 2 + nout * ngroups]
    scr = refs[ngroups + 2 + nout * ngroups:]
    wg_scr, wr_scr = scr[0], scr[1]
    wa_scr = scr[2] if aux_piece is not None else None
    d = w_ref.shape[1] if w_transposed else w_ref.shape[0]

    def cast_piece(dst, dst_off, src_off, width):
        wpad = _round_up(width, LANE)
        if w_transposed:
            for r in range(0, width, PROJ_CAST_ROWS):
                n = min(PROJ_CAST_ROWS, width - r)
                dst[dst_off + r:dst_off + r + n, :] = w_ref[src_off + r:src_off + r + n, :].astype(BF16)
            if wpad > width:
                dst[dst_off + width:dst_off + wpad, :] = jnp.zeros((wpad - width, d), BF16)
            return wpad
        for r in range(0, d, PROJ_CAST_ROWS):
            piece = w_ref[r:r + PROJ_CAST_ROWS, src_off:src_off + width]
            if wpad > width:
                piece = jnp.concatenate([piece, jnp.zeros((PROJ_CAST_ROWS, wpad - width), F32)], axis=1)
            dst[r:r + PROJ_CAST_ROWS, dst_off:dst_off + wpad] = piece.astype(BF16)
        return wpad

    def matmul(h, w_scr):
        if w_transposed:
            return lax.dot_general(h, w_scr[...], (((1,), (1,)), ((), ())), preferred_element_type=F32)
        return jnp.dot(h, w_scr[...], preferred_element_type=F32)

    @pl.when(step == 0)
    def _():
        cast_piece(wg_scr, 0, *layout["gate"])
        off = 0
        for src_off, width in layout["rest"]:
            off += cast_piece(wr_scr, off, src_off, width)
        if aux_piece is not None:
            cast_piece(wa_scr, 0, *aux_piece)

    def project(x_ref, outs):
        x = x_ref[...]
        ms = jnp.mean(x * x, axis=-1, keepdims=True)
        h = (x * lax.rsqrt(ms + NORM_EPS) * nw_ref[...]).astype(BF16)
        outs[0][...] = matmul(h, wg_scr).astype(outs[0].dtype)
        outs[1][...] = matmul(h, wr_scr).astype(outs[1].dtype)
        if aux_piece is not None:
            outs[2][...] = matmul(h, wa_scr)

    for g, (start, nsteps) in enumerate(group_steps):
        pl.when((step >= start) & (step < start + nsteps))(
            functools.partial(project, x_refs[g], out_refs[g * nout:(g + 1) * nout]))


def _proj_layout(kind):
    kd, di = GLA_KEY_DIM, D_INNER
    if kind == "gla":
        return dict(gate=(2 * kd + di, di),
                    rest=[(2 * kd, di), (0, kd), (kd, kd), (2 * kd + 2 * di, GLA_GATE_RANK)], aux=None)
    if kind == "ssd":
        return dict(gate=(0, di), rest=[(di, SSD_CONV_DIM)], aux=(di + SSD_CONV_DIM, SSD_HEADS))
    return dict(gate=(di + 2 * SWA_KV_DIM, di), rest=[(0, di + 2 * SWA_KV_DIM)], aux=None)


def _norm_proj(kind, xs, nw, w_in):
    d = xs[0].shape[1]
    layout = _proj_layout(kind)
    ng = layout["gate"][1]
    nr = sum(_round_up(w, LANE) for _, w in layout["rest"])
    has_aux = layout["aux"] is not None
    nsteps, starts = _group_steps([x.shape[0] for x in xs], PROJ_ROWS_PER_STEP)
    fix = lambda i: (0, 0)
    in_specs, out_specs, out_shape = [], [], []
    for x, n, s in zip(xs, nsteps, starts):
        t = x.shape[0]
        tm = t // n
        row = _group_row_map(s, n)
        in_specs.append(pl.BlockSpec((tm, d), row))
        out_specs += [pl.BlockSpec((tm, ng), row), pl.BlockSpec((tm, nr), row)]
        out_shape += [jax.ShapeDtypeStruct((t, ng), BF16), jax.ShapeDtypeStruct((t, nr), BF16)]
        if has_aux:
            out_specs.append(pl.BlockSpec((tm, LANE), row))
            out_shape.append(jax.ShapeDtypeStruct((t, LANE), F32))
    w_transposed = w_in.shape[1] % LANE != 0
    w_op = w_in.T if w_transposed else w_in
    swap = (lambda s: s[::-1]) if w_transposed else (lambda s: s)
    in_specs += [pl.BlockSpec((1, d), fix), pl.BlockSpec(w_op.shape, fix, pipeline_mode=pl.Buffered(1))]
    scratch = [pltpu.VMEM(swap((d, ng)), BF16), pltpu.VMEM(swap((d, nr)), BF16)]
    if has_aux:
        scratch.append(pltpu.VMEM(swap((d, LANE)), BF16))
    outs = pl.pallas_call(
        functools.partial(_norm_proj_kernel, layout=layout, group_steps=tuple(zip(starts, nsteps)),
                          w_transposed=w_transposed),
        grid=(sum(nsteps),),
        in_specs=in_specs,
        out_specs=out_specs,
        out_shape=out_shape,
        scratch_shapes=scratch,
        compiler_params=_cparams(("arbitrary",)),
        name="norm_proj_" + kind,
    )(*xs, nw.reshape(1, d), w_op)
    nout = 3 if has_aux else 2
    return [outs[g * nout:(g + 1) * nout] for g in range(len(xs))]


def _seg_rms(y, seg):
    parts = []
    for s in range(y.shape[1] // seg):
        p = y[:, s * seg:(s + 1) * seg]
        ms = jnp.mean(p * p, axis=-1, keepdims=True)
        parts.append(p * lax.rsqrt(ms + NORM_EPS))
    return jnp.concatenate(parts, axis=1)


def _out_proj_kernel(*refs, mode, final, group_steps):
    ngroups = len(group_steps)
    nw_ref, w_ref, fw_ref = refs[3 * ngroups:3 * ngroups + 3]
    out_refs = refs[3 * ngroups + 3:4 * ngroups + 3]
    w_scr = refs[4 * ngroups + 3]
    step = pl.program_id(0)

    @pl.when(step == 0)
    def _():
        for r in range(0, w_ref.shape[0], PROJ_CAST_ROWS):
            w_scr[r:r + PROJ_CAST_ROWS, :] = w_ref[r:r + PROJ_CAST_ROWS, :].astype(BF16)

    def project(o_ref, gate_ref, x_ref, out_ref):
        o = o_ref[...].astype(F32)
        gt = gate_ref[...].astype(F32)
        act = gt * _sigmoid(gt)
        if mode == "gla":
            y = _seg_rms(o, GLA_HEAD_V) * nw_ref[...] * act
        elif mode == "ssd":
            y = _seg_rms(o * act, SSD_GROUP_W) * nw_ref[...]
        else:
            y = o * act
        out = x_ref[...] + jnp.dot(y.astype(BF16), w_scr[...], preferred_element_type=F32)
        if final:
            ms = jnp.mean(out * out, axis=-1, keepdims=True)
            out = out * lax.rsqrt(ms + NORM_EPS) * fw_ref[...]
        out_ref[...] = out

    for g, (start, nsteps) in enumerate(group_steps):
        pl.when((step >= start) & (step < start + nsteps))(
            functools.partial(project, refs[3 * g], refs[3 * g + 1], refs[3 * g + 2], out_refs[g]))


def _out_proj(mode, os_, gates, nw, w_out, xs, fw, final):
    di = os_[0].shape[1]
    d = xs[0].shape[1]
    nsteps, starts = _group_steps([x.shape[0] for x in xs], PROJ_ROWS_PER_STEP)
    fix = lambda i: (0, 0)
    in_specs, out_specs, out_shape, operands = [], [], [], []
    for o2, g2, x2, n, s in zip(os_, gates, xs, nsteps, starts):
        t = x2.shape[0]
        tm = t // n
        row = _group_row_map(s, n)
        in_specs += [pl.BlockSpec((tm, di), row), pl.BlockSpec((tm, di), row), pl.BlockSpec((tm, d), row)]
        operands += [o2, g2, x2]
        out_specs.append(pl.BlockSpec((tm, d), row))
        out_shape.append(jax.ShapeDtypeStruct((t, d), F32))
    in_specs += [pl.BlockSpec((1, di), fix), pl.BlockSpec((di, d), fix), pl.BlockSpec((1, d), fix)]
    return pl.pallas_call(
        functools.partial(_out_proj_kernel, mode=mode, final=final, group_steps=tuple(zip(starts, nsteps))),
        grid=(sum(nsteps),),
        in_specs=in_specs,
        out_specs=out_specs,
        out_shape=out_shape,
        scratch_shapes=[pltpu.VMEM((di, d), BF16)],
        compiler_params=_cparams(("arbitrary",)),
        name="out_proj_" + mode,
    )(*operands, nw.reshape(1, di), w_out, fw.reshape(1, d))


GLA_CHUNKS_PER_STEP = 8
GLA_LONG_CHUNK = 128
GLA_SINGLE_ANCHOR_MAX_DECAY = 60.0


def _gla_levels(seglen):
    return tuple(seglen >> (i + 1) for i in range(int(math.log2(seglen))))


def _gla_consts(seglen, c):
    levels = _gla_levels(seglen)
    u = np.arange(c)[:, None]
    j = np.arange(c)[None, :]
    same = (u // seglen) == (j // seglen)
    blocks = [same & (j <= u), same & (j > u)]
    lev = np.full((c, c), len(levels) + 1, np.int32)
    for li, h in enumerate(levels):
        b = (u // (2 * h)) * (2 * h) + h - 1
        blocks.append((j > np.minimum(u, b)) & (j <= np.maximum(u, b)))
        sib = (u // (2 * h) == j // (2 * h)) & (u % (2 * h) >= h) & (j % (2 * h) < h)
        lev[sib] = li
    lev[np.eye(c, dtype=bool)] = len(levels)
    m = np.concatenate(blocks, axis=0).astype(np.float32)
    m3 = np.concatenate([m, m, m], axis=1)
    if 2 * c == LANE:
        lev = np.concatenate([lev, lev], axis=1)
    return jnp.asarray(m3, BF16), jnp.asarray(lev)


def _gla_scores(qx, kx, e_q, e_k, lev, nt):
    att = None
    for li, (eq, ek) in enumerate(zip(e_q, e_k)):
        a = nt(qx if eq is None else qx * eq, kx if ek is None else kx * ek)
        att = jnp.where(lev == li, a, 0.0 if att is None else att)
    return att


def _nt_pair(qe, ke):
    kb = ke.astype(BF16)
    z = jnp.zeros((kb.shape[0], GLA_HEAD_K), BF16)
    rhs = jnp.concatenate([jnp.concatenate([kb[:, 0:GLA_HEAD_K], z], axis=1),
                           jnp.concatenate([z, kb[:, GLA_HEAD_K:]], axis=1)], axis=0)
    return lax.dot_general(qe.astype(BF16), rhs, (((1,), (1,)), ((), ())), preferred_element_type=F32)


def _gla_prep(xc, wgk, bgk):
    v = xc[:, 0:D_INNER]
    q = xc[:, D_INNER:D_INNER + GLA_KEY_DIM].astype(F32) * (GLA_HEAD_K ** -0.5)
    k = xc[:, D_INNER + GLA_KEY_DIM:D_INNER + 2 * GLA_KEY_DIM].astype(F32)
    lr = xc[:, D_INNER + 2 * GLA_KEY_DIM:]
    z = _dot(lr, wgk) + bgk
    g = (jnp.minimum(z, 0.0) - jnp.log(1.0 + jnp.exp(-jnp.abs(z)))) * (1.0 / GLA_GATE_NORMALIZER)
    return v, q, k, g


def _gla_chunk(v, q, k, g, m3, lev, nl, single_anchor):
    c = g.shape[0]
    g3 = jnp.concatenate(_split3(g), axis=0)
    if single_anchor:
        rs = jnp.dot(m3[0:2 * c], g3, preferred_element_type=F32)
        cum = rs[0:c]
        e_in, e_out = jnp.exp(cum), jnp.exp(rs[c:2 * c])
        e_q, e_k = [e_in], [jnp.exp(-cum)]
        lev = jnp.where(lev <= nl, 0, 1)
    else:
        e_all = jnp.exp(jnp.dot(m3, g3, preferred_element_type=F32))
        e_in, e_out = e_all[0:c], e_all[c:2 * c]
        e_q = [e_all[(2 + li) * c:(3 + li) * c] for li in range(nl)] + [None]
        e_k = e_q
    qe_in = q * e_in
    ke_out = k * e_out
    outs, pending = [], []
    if c == LANE:
        for h in range(GLA_HEADS):
            ks = slice(h * GLA_HEAD_K, (h + 1) * GLA_HEAD_K)
            cut = lambda es: [None if e is None else e[:, ks] for e in es]
            att = _gla_scores(q[:, ks], k[:, ks], cut(e_q), cut(e_k), lev, _dot_nt)
            v_h = v[:, h * GLA_HEAD_V:(h + 1) * GLA_HEAD_V]
            xt = jnp.concatenate([ke_out[:, ks], e_in[:, ks]], axis=0).T
            outs.append(_dot(att, v_h))
            pending.append((h, qe_in[:, ks], xt, v_h))
        return outs, pending
    lane_lo = lax.broadcasted_iota(jnp.int32, (c, LANE), 1) < c
    for pr in range(GLA_HEADS // 2):
        ls = slice(2 * pr * GLA_HEAD_K, (2 * pr + 2) * GLA_HEAD_K)
        cut = lambda es: [None if e is None else e[:, ls] for e in es]
        att = _gla_scores(q[:, ls], k[:, ls], cut(e_q), cut(e_k), lev, _nt_pair)
        vpair = v[:, 2 * pr * GLA_HEAD_V:(2 * pr + 2) * GLA_HEAD_V]
        vcat = jnp.concatenate([vpair[:, 0:GLA_HEAD_V], vpair[:, GLA_HEAD_V:]], axis=0)
        for half, att_h in enumerate((jnp.where(lane_lo, att, 0.0), jnp.where(lane_lo, 0.0, att))):
            h = 2 * pr + half
            ks = slice(h * GLA_HEAD_K, (h + 1) * GLA_HEAD_K)
            xt = jnp.concatenate([ke_out[:, ks], e_in[:, ks]], axis=0).T
            outs.append(_dot(att_h, vcat))
            pending.append((h, qe_in[:, ks], xt, v[:, h * GLA_HEAD_V:(h + 1) * GLA_HEAD_V]))
    return outs, pending


def _gla_finish(outs, pending, state_io):
    return jnp.concatenate([o + state_io(*p) for o, p in zip(outs, pending)], axis=1)


def _gla_dispatch(preps, run):
    low = None
    for _, _, _, g in preps:
        tot = jnp.min(jnp.sum(g, axis=0, keepdims=True))
        low = tot if low is None else jnp.minimum(low, tot)
    single_anchor_ok = low >= -GLA_SINGLE_ANCHOR_MAX_DECAY
    pl.when(single_anchor_ok)(functools.partial(run, True))
    pl.when(jnp.logical_not(single_anchor_ok))(functools.partial(run, False))


def _gla_carry_kernel(rest_ref, s0_ref, wgk_ref, bgk_ref, m_ref, lev_ref, o_ref, s_ref, *, c, nch, nl):
    @pl.when(pl.program_id(1) == 0)
    def _():
        s_ref[...] = s0_ref[...]

    def state_io(h, qe_in_h, xt, v_h):
        s_h = s_ref[0, h]
        s_ref[0, h] = s_h * xt[:, 2 * c - 1:2 * c] + _dot(xt[:, 0:c], v_h)
        return _dot(qe_in_h, s_h)

    wgk, bgk = wgk_ref[...], bgk_ref[...]
    preps = [_gla_prep(rest_ref[0, ci * c:(ci + 1) * c, :], wgk, bgk) for ci in range(nch)]

    def run(single_anchor):
        staged = [_gla_chunk(v, q, k, g, m_ref[...], lev_ref[...], nl, single_anchor) for v, q, k, g in preps]
        for ci, (outs, pending) in enumerate(staged):
            o_ref[0, ci * c:(ci + 1) * c, :] = _gla_finish(outs, pending, state_io).astype(o_ref.dtype)

    _gla_dispatch(preps, run)


def _gla_packed_kernel(rest_ref, s0_ref, wgk_ref, bgk_ref, m_ref, lev_ref, o_ref, s_ref, *, seglen, nl):
    c = CHUNK
    nseg = c // seglen
    seg_of_lane = lax.broadcasted_iota(jnp.int32, (GLA_HEAD_K, c), 1) // seglen

    def state_io(h, qe_in_h, xt, v_h):
        ke_t = xt[:, 0:c]
        outs = []
        for sg in range(nseg):
            s_h = s0_ref[sg, h]
            outs.append(_dot(qe_in_h[sg * seglen:(sg + 1) * seglen], s_h))
            end = c + (sg + 1) * seglen - 1
            s_ref[sg, h] = s_h * xt[:, end:end + 1] + _dot(jnp.where(seg_of_lane == sg, ke_t, 0.0), v_h)
        return jnp.concatenate(outs, axis=0)

    preps = [_gla_prep(rest_ref[...], wgk_ref[...], bgk_ref[...])]

    def run(single_anchor):
        v, q, k, g = preps[0]
        outs, pending = _gla_chunk(v, q, k, g, m_ref[...], lev_ref[...], nl, single_anchor)
        o_ref[...] = _gla_finish(outs, pending, state_io).astype(o_ref.dtype)

    _gla_dispatch(preps, run)


def _gla_scan(rest3, s0, w_gk2, b_gk):
    b, l, nr = rest3.shape
    packed = l < CHUNK
    c = CHUNK if packed or l % GLA_LONG_CHUNK else GLA_LONG_CHUNK
    seglen = l if packed else c
    assert c % seglen == 0 and seglen % 8 == 0 and (l % c == 0 or packed)
    nl = len(_gla_levels(seglen))
    m01, lev = _gla_consts(seglen, c)
    wgk = jnp.zeros((LANE, GLA_KEY_DIM), F32).at[:GLA_GATE_RANK].set(w_gk2).astype(BF16)
    consts = (wgk, b_gk.reshape(1, GLA_KEY_DIM), m01, lev)
    if packed:
        nseq = c // seglen
        assert b % nseq == 0
        sblk = (nseq, GLA_HEADS, GLA_HEAD_K, GLA_HEAD_V)
        o, s_fin = pl.pallas_call(
            functools.partial(_gla_packed_kernel, seglen=seglen, nl=nl),
            grid=(b // nseq,),
            in_specs=[pl.BlockSpec((c, nr), lambda i: (i, 0)),
                      pl.BlockSpec(sblk, lambda i: (i, 0, 0, 0))]
                     + [pl.BlockSpec(a.shape, lambda i: (0, 0)) for a in consts],
            out_specs=[pl.BlockSpec((c, D_INNER), lambda i: (i, 0)),
                       pl.BlockSpec(sblk, lambda i: (i, 0, 0, 0))],
            out_shape=[jax.ShapeDtypeStruct((b * l, D_INNER), BF16), jax.ShapeDtypeStruct(s0.shape, F32)],
            compiler_params=_cparams(("parallel",)),
            name="gla_scan_packed",
        )(rest3.reshape(b * l, nr), s0, *consts)
        return o.reshape(b, l, D_INNER), s_fin
    nch = math.gcd(l // c, GLA_CHUNKS_PER_STEP)
    sblk = (1, GLA_HEADS, GLA_HEAD_K, GLA_HEAD_V)
    return pl.pallas_call(
        functools.partial(_gla_carry_kernel, c=c, nch=nch, nl=nl),
        grid=(b, l // (c * nch)),
        in_specs=[pl.BlockSpec((1, c * nch, nr), lambda i, t: (i, t, 0)),
                  pl.BlockSpec(sblk, lambda i, t: (i, 0, 0, 0))]
                 + [pl.BlockSpec(a.shape, lambda i, t: (0, 0)) for a in consts],
        out_specs=[pl.BlockSpec((1, c * nch, D_INNER), lambda i, t: (i, t, 0)),
                   pl.BlockSpec(sblk, lambda i, t: (i, 0, 0, 0))],
        out_shape=[jax.ShapeDtypeStruct((b, l, D_INNER), BF16), jax.ShapeDtypeStruct(s0.shape, F32)],
        compiler_params=_cparams(("parallel", "arbitrary")),
        name="gla_scan",
    )(rest3, s0, *consts)


SSD_CHUNKS_PER_STEP = 4


def _ssd2_consts(seglen):
    c = CHUNK
    u = np.arange(c)
    same = (u[:, None] // seglen) == (u[None, :] // seglen)
    tl = same & (u[None, :] <= u[:, None])
    m = np.concatenate([tl, same], axis=0).astype(np.float32)
    m3 = np.concatenate([m, m, m], axis=1)
    expand = (np.arange(LANE)[:, None] == (np.arange(D_INNER) // SSD_HEAD_DIM)[None, :]).astype(np.float32)
    ex3 = np.concatenate([expand] * 3, axis=0)
    s_of = np.arange(LANE) % c
    tri = np.where(same[:, s_of] & (s_of[None, :] <= u[:, None]), 0.0, NEG_BIG).astype(np.float32)
    bd = ((np.arange(2 * c)[:, None] // c) == (np.arange(LANE)[None, :] // SSD_HEAD_DIM)).astype(np.float32)
    return jnp.asarray(m3, BF16), jnp.asarray(ex3, BF16), jnp.asarray(tri), jnp.asarray(bd)


def _ssd2_conv(ext_ref, rows, cw_ref, cb_ref):
    e = ext_ref[0:8 + rows, :]
    acc = cw_ref[SSD_CONV - 1:SSD_CONV, :] * e[8:]
    for k in range(1, SSD_CONV):
        acc = acc + cw_ref[SSD_CONV - 1 - k:SSD_CONV - k, :] * pltpu.roll(e, k, axis=0)[8:]
    return cb_ref[...] + acc


def _ssd2_chunk(dt_raw, conv, dtb, a_neg, dsk, m3, ex3, tri, bd, state_io):
    c = CHUNK
    xbc = conv * _sigmoid(conv)
    xs = xbc[:, 0:D_INNER]
    bm = xbc[:, D_INNER:D_INNER + SSD_GROUPS * SSD_STATE]
    cm = xbc[:, D_INNER + SSD_GROUPS * SSD_STATE:]
    dt = _softplus(dt_raw + dtb)
    la = dt * a_neg
    rs = jnp.dot(m3, jnp.concatenate(_split3(la), axis=0), preferred_element_type=F32)
    cum, tot = rs[0:c], rs[c:2 * c]
    stack = jnp.concatenate([dt * jnp.exp(tot - cum), cum], axis=0)
    rep = jnp.dot(jnp.concatenate(_split3(stack), axis=1), ex3, preferred_element_type=F32)
    dtw_rep, cum_rep = rep[0:c], rep[c:2 * c]
    ecum_rep = jnp.exp(cum_rep)
    uw = xs * dtw_rep
    at = jnp.concatenate([cum, cum], axis=0).T
    adt = jnp.concatenate([dt, dt], axis=0).T
    lane_lo = lax.broadcasted_iota(jnp.int32, (1, LANE), 1) < c
    ys = []
    for g in range(SSD_GROUPS):
        gs = slice(g * SSD_GROUP_W, (g + 1) * SSD_GROUP_W)
        ns = slice(g * SSD_STATE, (g + 1) * SSD_STATE)
        cg, bg = cm[:, ns], bm[:, ns]
        cb_rep = _dot_nt(cg, jnp.concatenate([bg] * (SSD_GROUP_W // c), axis=0))
        parts = []
        for p in range(SSD_GROUP_W // LANE):
            col = g * (SSD_GROUP_W // LANE) + p
            ps = slice(col * LANE, (col + 1) * LANE)
            cum_s = jnp.where(lane_lo, at[2 * col:2 * col + 1, :], at[2 * col + 1:2 * col + 2, :])
            dt_s = jnp.where(lane_lo, adt[2 * col:2 * col + 1, :], adt[2 * col + 1:2 * col + 2, :])
            w = jnp.exp(cum_rep[:, ps] - cum_s + tri) * (cb_rep[:, p * LANE:(p + 1) * LANE] * dt_s)
            up = xs[:, ps]
            parts.append(_dot(w, jnp.concatenate([up, up], axis=0) * bd))
        y_inter = state_io(g, cg, bg, uw[:, gs], ecum_rep[:, gs])
        ys.append(jnp.concatenate(parts, axis=1) + y_inter + xs[:, gs] * dsk[:, gs])
    return jnp.concatenate(ys, axis=1)


def _ssd2_carry_kernel(rest_ref, dtr_ref, s0_ref, conv0_ref, cw_ref, cb_ref, dtb_ref, alog_ref, dsk_ref,
                       m3_ref, ex3_ref, tri_ref, bd_ref, y_ref, sfin_ref, ext_ref, st_ref, *, nch):
    c = CHUNK
    t = pl.program_id(1)

    @pl.when(t == 0)
    def _():
        ext_ref[0:8, :] = conv0_ref[0]
        for g in range(SSD_GROUPS):
            st_ref[g] = s0_ref[0, g].T

    a_neg = -jnp.exp(alog_ref[...])

    def state_io(g, cg, bg, uw_g, ecum_g):
        s_g = st_ref[g]
        y_inter = _dot(cg, s_g) * ecum_g
        bg_t = jnp.concatenate([bg, jnp.zeros_like(bg)], axis=0).T[:, 0:c]
        st_ref[g] = s_g * ecum_g[c - 1:c, :] + _dot(bg_t, uw_g)
        return y_inter

    for ci in range(nch):
        ext_ref[8:8 + c, :] = rest_ref[0, ci * c:(ci + 1) * c, :].astype(F32)
        conv = _ssd2_conv(ext_ref, c, cw_ref, cb_ref)
        ext_ref[0:8, :] = ext_ref[c:c + 8, :]
        y = _ssd2_chunk(dtr_ref[0, ci * c:(ci + 1) * c, :], conv, dtb_ref[...], a_neg, dsk_ref[...],
                        m3_ref[...], ex3_ref[...], tri_ref[...], bd_ref[...], state_io)
        y_ref[0, ci * c:(ci + 1) * c, :] = y.astype(y_ref.dtype)

    @pl.when(t == pl.num_programs(1) - 1)
    def _():
        for g in range(SSD_GROUPS):
            sfin_ref[0, g] = st_ref[g].T


def _ssd2_packed_kernel(rest_ref, dtr_ref, s0_ref, conv0_ref, cw_ref, cb_ref, dtb_ref, alog_ref, dsk_ref,
                        m3_ref, ex3_ref, tri_ref, bd_ref, y_ref, sfin_ref, ext_ref, *, seglen):
    c = CHUNK
    nseg = c // seglen
    ext_ref[0:8, :] = jnp.zeros((8, SSD_CONV_DIM), F32)
    xbc_raw = rest_ref[...].astype(F32)
    for sg in range(nseg):
        ext_ref[8 + 16 * sg:16 + 16 * sg, :] = conv0_ref[sg]
        ext_ref[16 + 16 * sg:24 + 16 * sg, :] = xbc_raw[sg * seglen:(sg + 1) * seglen]
    conv2 = _ssd2_conv(ext_ref, 2 * c, cw_ref, cb_ref)
    conv = jnp.concatenate([conv2[16 * sg + 8:16 * sg + 16] for sg in range(nseg)], axis=0)
    a_neg = -jnp.exp(alog_ref[...])
    seg_of_row = lax.broadcasted_iota(jnp.int32, (c, SSD_STATE), 0) // seglen

    def state_io(g, cg, bg, uw_g, ecum_g):
        ends = jnp.concatenate([ecum_g[(sg + 1) * seglen - 1:(sg + 1) * seglen, :] for sg in range(nseg)], axis=0)
        fill = jnp.zeros((LANE - c - nseg, SSD_GROUP_W), F32)
        tg = jnp.concatenate([uw_g, ends, fill], axis=0).T
        uw_t = tg[:, 0:c]
        y_parts = []
        for sg in range(nseg):
            s_sg = s0_ref[sg, g]
            y_parts.append(_dot_nt(cg[sg * seglen:(sg + 1) * seglen], s_sg))
            b_sg = jnp.where(seg_of_row == sg, bg, 0.0)
            sfin_ref[sg, g] = s_sg * tg[:, c + sg:c + sg + 1] + _dot(uw_t, b_sg)
        return jnp.concatenate(y_parts, axis=0) * ecum_g

    y = _ssd2_chunk(dtr_ref[...], conv, dtb_ref[...], a_neg, dsk_ref[...], m3_ref[...], ex3_ref[...],
                    tri_ref[...], bd_ref[...], state_io)
    y_ref[...] = y.astype(y_ref.dtype)


def _ssd2_scan(rest3, dt_raw3, s0, conv0, conv_w, conv_b, dt_bias, a_log, d_skip):
    b, l, nr = rest3.shape
    c = CHUNK
    packed = l < c
    seglen = l if packed else c
    assert c % seglen == 0 and seglen % 8 == 0 and (l % c == 0 or packed)
    m3, ex3, tri, bd = _ssd2_consts(seglen)
    s0g = s0.reshape(b, SSD_GROUPS, SSD_GROUP_W, SSD_STATE)
    conv0p = jnp.concatenate([jnp.zeros((b, 8 - (SSD_CONV - 1), SSD_CONV_DIM), F32), conv0], axis=1)
    pad = lambda a: jnp.zeros((1, LANE), F32).at[0, :SSD_HEADS].set(a)
    dsk = jnp.repeat(d_skip, SSD_HEAD_DIM).reshape(1, D_INNER)
    consts = (conv_w, conv_b.reshape(1, -1), pad(dt_bias), pad(a_log), dsk, m3, ex3, tri, bd)
    if packed:
        nseq = c // seglen
        assert b % nseq == 0
        fix = lambda i: (0, 0)
        const_specs = [pl.BlockSpec(a.shape, fix) for a in consts]
        sblk = (nseq, SSD_GROUPS, SSD_GROUP_W, SSD_STATE)
        y, sfin = pl.pallas_call(
            functools.partial(_ssd2_packed_kernel, seglen=seglen),
            grid=(b // nseq,),
            in_specs=[pl.BlockSpec((c, nr), lambda i: (i, 0)),
                      pl.BlockSpec((c, LANE), lambda i: (i, 0)),
                      pl.BlockSpec(sblk, lambda i: (i, 0, 0, 0)),
                      pl.BlockSpec((nseq, 8, SSD_CONV_DIM), lambda i: (i, 0, 0))] + const_specs,
            out_specs=[pl.BlockSpec((c, D_INNER), lambda i: (i, 0)),
                       pl.BlockSpec(sblk, lambda i: (i, 0, 0, 0))],
            out_shape=[jax.ShapeDtypeStruct((b * l, D_INNER), BF16), jax.ShapeDtypeStruct(s0g.shape, F32)],
            scratch_shapes=[pltpu.VMEM((2 * c + 8, SSD_CONV_DIM), F32)],
            compiler_params=_cparams(("parallel",)),
            name="ssd_scan_packed",
        )(rest3.reshape(b * l, nr), dt_raw3.reshape(b * l, LANE), s0g, conv0p, *consts)
        return y.reshape(b, l, D_INNER), sfin.reshape(s0.shape)
    nch = math.gcd(l // c, SSD_CHUNKS_PER_STEP)
    fix2 = lambda i, t: (0, 0)
    const_specs = [pl.BlockSpec(a.shape, fix2) for a in consts]
    sblk = (1, SSD_GROUPS, SSD_GROUP_W, SSD_STATE)
    y, sfin = pl.pallas_call(
        functools.partial(_ssd2_carry_kernel, nch=nch),
        grid=(b, l // (c * nch)),
        in_specs=[pl.BlockSpec((1, c * nch, nr), lambda i, t: (i, t, 0)),
                  pl.BlockSpec((1, c * nch, LANE), lambda i, t: (i, t, 0)),
                  pl.BlockSpec(sblk, lambda i, t: (i, 0, 0, 0)),
                  pl.BlockSpec((1, 8, SSD_CONV_DIM), lambda i, t: (i, 0, 0))] + const_specs,
        out_specs=[pl.BlockSpec((1, c * nch, D_INNER), lambda i, t: (i, t, 0)),
                   pl.BlockSpec(sblk, lambda i, t: (i, 0, 0, 0))],
        out_shape=[jax.ShapeDtypeStruct((b, l, D_INNER), BF16), jax.ShapeDtypeStruct(s0g.shape, F32)],
        scratch_shapes=[pltpu.VMEM((c + 8, SSD_CONV_DIM), F32),
                        pltpu.VMEM((SSD_GROUPS, SSD_STATE, SSD_GROUP_W), F32)],
        compiler_params=_cparams(("parallel", "arbitrary")),
        name="ssd_scan",
    )(rest3, dt_raw3, s0g, conv0p, *consts)
    return y, sfin.reshape(s0.shape)


SWA_SEQS_PER_STEP = 16
SWA_BLOCKS_PER_STEP = 8


def _swa_mask(bq, first_block_has_no_past):
    rows = 4 * bq
    tq = (np.arange(rows) % bq)[:, None]
    s = (np.arange(4 * SWA_WINDOW) % (2 * SWA_WINDOW))[None, :]
    ok = (s > tq) & (s <= tq + SWA_WINDOW)
    if first_block_has_no_past:
        ok = ok & (s >= SWA_WINDOW)
    return np.where(ok, 0.0, NEG_BIG).astype(np.float32)


def _block_diag_pair(col, rolled, odd):
    lane = lax.broadcasted_iota(jnp.int32, col.shape, 1)
    lo = lane < SWA_HEAD_DIM
    if odd:
        top = jnp.where(lo, rolled, 0.0)
        bot = jnp.where(lo, 0.0, col)
    else:
        top = jnp.where(lo, col, 0.0)
        bot = jnp.where(lo, 0.0, rolled)
    return jnp.concatenate([top, bot], axis=0)


def _swa_kernel(sink_ref, x_ref, kprev_ref, vprev_ref, mask_ref, ones_ref, o_ref, *cache_refs, bq, items, nsub):
    w = SWA_WINDOW
    n = pl.program_id(1)
    npair = (SWA_Q_HEADS // SWA_KV_HEADS) // 2
    rows = npair * bq
    lane_lo = lax.broadcasted_iota(jnp.int32, (rows, LANE), 1) < SWA_HEAD_DIM
    sinks = []
    for j in range(SWA_KV_HEADS):
        sinks.append([jnp.concatenate(
            [jnp.full((bq, 1), sink_ref[j * 2 * npair + 2 * p + half], F32) for p in range(npair)], axis=0)
            for half in range(2)])

    kcols = slice(D_INNER, D_INNER + SWA_KV_DIM)
    vcols = slice(D_INNER + SWA_KV_DIM, D_INNER + 2 * SWA_KV_DIM)
    work = []
    for i in range(items * nsub):
        if nsub == 1:
            x = x_ref[i]
            kprev, vprev = kprev_ref[i], vprev_ref[i]
            amask = mask_ref[jnp.where(n == 0, 0, 1)]
            out_rows = (i, slice(None))
        else:
            x = x_ref[0, i * bq:(i + 1) * bq, :]
            if i == 0:
                kprev, vprev = kprev_ref[0], vprev_ref[0]
                amask = mask_ref[jnp.where(n == 0, 0, 1)]
            else:
                kprev = x_ref[0, (i - 1) * bq:i * bq, kcols]
                vprev = x_ref[0, (i - 1) * bq:i * bq, vcols]
                amask = mask_ref[1]
            out_rows = (0, slice(i * bq, (i + 1) * bq))
        fill = [] if bq == w else [jnp.zeros((w - bq, SWA_KV_DIM), F32)]
        kall = jnp.concatenate([kprev, x[:, kcols]] + fill, axis=0)
        vall = jnp.concatenate([vprev, x[:, vcols]] + fill, axis=0)
        if cache_refs:
            cache_refs[0][i] = kall[bq:bq + w]
            cache_refs[1][i] = vall[bq:bq + w]
        work.append((x, kall, vall, amask, out_rows))

    nwork = len(work)
    tile_rows = lambda a: jnp.concatenate([a] * nwork, axis=0)
    swap = lambda a: jnp.concatenate([a[:, SWA_HEAD_DIM:], a[:, 0:SWA_HEAD_DIM]], axis=1)
    lane_lo_all = tile_rows(lane_lo)
    for j in range(SWA_KV_HEADS):
        cs = slice((j // 2) * LANE, (j // 2 + 1) * LANE)
        qbase = j * npair * LANE
        scs, v2es = [], []
        for x, kall, vall, amask, _ in work:
            kcol, vcol = kall[:, cs], vall[:, cs]
            k2 = _block_diag_pair(kcol, swap(kcol), j % 2)
            v2 = _block_diag_pair(vcol, swap(vcol), j % 2)
            v2es.append(jnp.concatenate([v2.astype(BF16), ones_ref[...]], axis=1))
            qs = jnp.concatenate([x[:, qbase + p * LANE:qbase + (p + 1) * LANE] for p in range(npair)],
                                 axis=0) * (SWA_HEAD_DIM ** -0.5)
            scs.append(_dot_nt(qs, k2) + amask)
        sc = jnp.concatenate(scs, axis=0)
        pes, ms = [], []
        for half in range(2):
            sh = sc[:, half * 2 * w:(half + 1) * 2 * w]
            sink = tile_rows(sinks[j][half])
            m = jnp.maximum(jnp.max(sh, axis=-1, keepdims=True), sink)
            pes.append(jnp.exp(sh - m).astype(BF16))
            ms.append(sink - m)
        pe = jnp.concatenate(pes, axis=1)
        o = jnp.concatenate([jnp.dot(pe[i * rows:(i + 1) * rows], v2es[i], preferred_element_type=F32)
                             for i in range(nwork)], axis=0)
        esink = jnp.exp(jnp.where(lane_lo_all, ms[0], ms[1]))
        res = o[:, 0:LANE] / (o[:, LANE:] + esink)
        for i, (_, _, _, _, out_rows) in enumerate(work):
            for p in range(npair):
                o_ref[out_rows[0], out_rows[1], qbase + p * LANE:qbase + (p + 1) * LANE] = (
                    res[i * rows + p * bq:i * rows + (p + 1) * bq].astype(o_ref.dtype))


def _swa_attn(rest3, kprev, vprev, sinks, has_past):
    b, l, nr = rest3.shape
    w = SWA_WINDOW
    bq = math.gcd(l, w)
    nb = l // bq
    assert nb == 1 or bq == w
    m_later = _swa_mask(bq, False)
    m_first = m_later if has_past else _swa_mask(bq, True)
    masks = jnp.asarray(np.stack([m_first, m_later]))
    ones2 = jnp.asarray((np.arange(4 * w)[:, None] // (2 * w)) == (np.arange(LANE)[None, :] // SWA_HEAD_DIM), BF16)
    if nb == 1:
        rest3 = rest3.astype(F32)
        items, nsub = math.gcd(b, SWA_SEQS_PER_STEP), 1
        kspec = pl.BlockSpec((items, w, SWA_KV_DIM), lambda i, n: (i, 0, 0))
        vspec = kspec
        kin, vin = kprev, vprev
    else:
        assert not has_past
        items, nsub = 1, math.gcd(nb, SWA_BLOCKS_PER_STEP)
        kcol = D_INNER // SWA_KV_DIM
        kspec = pl.BlockSpec((1, w, SWA_KV_DIM), lambda i, n: (i, jnp.maximum(n * nsub - 1, 0), kcol))
        vspec = pl.BlockSpec((1, w, SWA_KV_DIM), lambda i, n: (i, jnp.maximum(n * nsub - 1, 0), kcol + 1))
        kin, vin = rest3, rest3
    out_specs = [pl.BlockSpec((items, bq * nsub, D_INNER), lambda i, n: (i, n, 0))]
    out_shape = [jax.ShapeDtypeStruct((b, l, D_INNER), _mixer_out_dtype(bq))]
    if nb == 1:
        out_specs += [pl.BlockSpec((items, w, SWA_KV_DIM), lambda i, n: (i, 0, 0))] * 2
        out_shape += [jax.ShapeDtypeStruct((b, w, SWA_KV_DIM), F32)] * 2
    return pl.pallas_call(
        functools.partial(_swa_kernel, bq=bq, items=items, nsub=nsub),
        grid=(b // items, nb // nsub),
        in_specs=[
            pl.BlockSpec(memory_space=pltpu.SMEM),
            pl.BlockSpec((items, bq * nsub, nr), lambda i, n: (i, n, 0)),
            kspec,
            vspec,
            pl.BlockSpec(masks.shape, lambda i, n: (0, 0, 0)),
            pl.BlockSpec(ones2.shape, lambda i, n: (0, 0)),
        ],
        out_specs=out_specs,
        out_shape=out_shape,
        compiler_params=_cparams(("parallel", "parallel")),
        name="swa_attn",
    )(sinks, rest3, kin, vin, masks, ones2)


def _trunk(xs, sts, p, has_pasts):
    shapes = [x.shape for x in xs]
    x2s = [x.reshape(-1, x.shape[-1]) for x in xs]
    ones_inner = jnp.ones((D_INNER,), F32)
    news = [{} for _ in xs]
    flat = lambda a: a.reshape(-1, a.shape[-1])

    def gla(i, x2s, key):
        proj = _norm_proj("gla", x2s, p[f"l{i}_norm"], p[f"l{i}_w_in"])
        os_ = []
        for g, ((b, l, _), (gate, rest)) in enumerate(zip(shapes, proj)):
            o, news[g][key] = _gla_scan(rest.reshape(b, l, -1), sts[g][key], p[f"l{i}_w_gk2"], p[f"l{i}_b_gk"])
            os_.append(flat(o))
        hn = jnp.tile(p[f"l{i}_head_norm"], GLA_HEADS)
        return _out_proj("gla", os_, [pr[0] for pr in proj], hn, p[f"l{i}_w_out"], x2s, p["final_norm"], i == 3)

    x2s = gla(0, x2s, "gla0")

    proj = _norm_proj("ssd", x2s, p["l1_norm"], p["l1_w_in"])
    os_ = []
    for g, ((b, l, _), (gate, rest, dt_raw)) in enumerate(zip(shapes, proj)):
        st = sts[g]
        rest3 = rest.reshape(b, l, -1)
        y, news[g]["ssm"] = _ssd2_scan(rest3, dt_raw.reshape(b, l, -1), st["ssm"], st["conv"], p["l1_conv_w"],
                                       p["l1_conv_b"], p["l1_dt_bias"], p["l1_a_log"], p["l1_d_skip"])
        ext_tail = jnp.concatenate([st["conv"], rest3[:, max(l - (SSD_CONV - 1), 0):, :].astype(F32)], axis=1)
        news[g]["conv"] = ext_tail[:, -(SSD_CONV - 1):]
        os_.append(flat(y))
    x2s = _out_proj("ssd", os_, [pr[0] for pr in proj], p["l1_gate_norm"], p["l1_w_out"], x2s,
                    p["final_norm"], False)

    proj = _norm_proj("swa", x2s, p["l2_norm"], p["l2_w_in"])
    os_ = []
    for g, ((b, l, _), (gate, rest)) in enumerate(zip(shapes, proj)):
        st = sts[g]
        rest3 = rest.reshape(b, l, -1)
        kprev = st["swa_k"].reshape(b, SWA_WINDOW, SWA_KV_DIM)
        vprev = st["swa_v"].reshape(b, SWA_WINDOW, SWA_KV_DIM)
        res = _swa_attn(rest3, kprev, vprev, p["l2_sinks"], has_pasts[g])
        if len(res) == 3:
            k_win, v_win = res[1], res[2]
        else:
            k_win = rest3[:, l - SWA_WINDOW:, D_INNER:D_INNER + SWA_KV_DIM].astype(F32)
            v_win = rest3[:, l - SWA_WINDOW:, D_INNER + SWA_KV_DIM:].astype(F32)
        news[g]["swa_k"] = k_win.reshape(st["swa_k"].shape)
        news[g]["swa_v"] = v_win.reshape(st["swa_v"].shape)
        os_.append(flat(res[0]))
    x2s = _out_proj("swa", os_, [pr[0] for pr in proj], ones_inner, p["l2_w_out"], x2s, p["final_norm"], False)

    x2s = gla(3, x2s, "gla3")
    return [x2.reshape(s) for x2, s in zip(x2s, shapes)], news


def kernel(x_prompt, x_sample, state_gla_0, state_ssm_1, state_conv_1, cache_swa_k_2, cache_swa_v_2, state_gla_3, l0_norm, l0_w_in, l0_w_gk2, l0_b_gk, l0_head_norm, l0_w_out, l1_norm, l1_w_in, l1_conv_w, l1_conv_b, l1_dt_bias, l1_a_log, l1_d_skip, l1_gate_norm, l1_w_out, l2_norm, l2_w_in, l2_sinks, l2_w_out, l3_norm, l3_w_in, l3_w_gk2, l3_b_gk, l3_head_norm, l3_w_out, final_norm):
    p = dict(l0_norm=l0_norm, l0_w_in=l0_w_in, l0_w_gk2=l0_w_gk2, l0_b_gk=l0_b_gk,
             l0_head_norm=l0_head_norm, l0_w_out=l0_w_out,
             l1_norm=l1_norm, l1_w_in=l1_w_in, l1_conv_w=l1_conv_w, l1_conv_b=l1_conv_b,
             l1_dt_bias=l1_dt_bias, l1_a_log=l1_a_log, l1_d_skip=l1_d_skip,
             l1_gate_norm=l1_gate_norm, l1_w_out=l1_w_out,
             l2_norm=l2_norm, l2_w_in=l2_w_in, l2_sinks=l2_sinks, l2_w_out=l2_w_out,
             l3_norm=l3_norm, l3_w_in=l3_w_in, l3_w_gk2=l3_w_gk2, l3_b_gk=l3_b_gk,
             l3_head_norm=l3_head_norm, l3_w_out=l3_w_out, final_norm=final_norm)

    bp = x_prompt.shape[0]
    z = lambda a: jnp.zeros((bp,) + a.shape[1:], a.dtype)
    st_p = dict(gla0=z(state_gla_0), ssm=z(state_ssm_1), conv=z(state_conv_1),
                swa_k=z(cache_swa_k_2), swa_v=z(cache_swa_v_2), gla3=z(state_gla_3))
    st_s = dict(gla0=state_gla_0, ssm=state_ssm_1, conv=state_conv_1,
                swa_k=cache_swa_k_2, swa_v=cache_swa_v_2, gla3=state_gla_3)
    (y_p,), (n_p,) = _trunk([x_prompt], [st_p], p, [False])
    (y_s,), (n_s,) = _trunk([x_sample], [st_s], p, [True])
    return (y_p, y_s,
            n_p["gla0"], n_s["gla0"],
            n_p["ssm"], n_s["ssm"],
            n_p["conv"], n_s["conv"],
            n_p["swa_k"], n_s["swa_k"],
            n_p["swa_v"], n_s["swa_v"],
            n_p["gla3"], n_s["gla3"])
```

```python
import functools
import math

import numpy as np
import jax
import jax.numpy as jnp
from jax import lax
from jax.experimental import pallas as pl
from jax.experimental.pallas import tpu as pltpu

F32 = jnp.float32
BF16 = jnp.bfloat16

NORM_EPS = 1e-6
D_MODEL = 1024
D_INNER = 2048
CHUNK = 64
LANE = 128
BF16_SUBLANES = 16
NEG_BIG = -1e30

GLA_HEADS = 4
GLA_HEAD_K = 128
GLA_HEAD_V = 512
GLA_KEY_DIM = 512
GLA_GATE_RANK = 16
GLA_GATE_NORMALIZER = 16.0

SSD_GROUPS = 4
SSD_HEADS = 32
SSD_HEAD_DIM = 64
SSD_STATE = 128
SSD_CONV = 4
SSD_CONV_DIM = 3072
SSD_GROUP_W = D_INNER // SSD_GROUPS

SWA_WINDOW = 128
SWA_KV_HEADS = 4
SWA_HEAD_DIM = 64
SWA_Q_HEADS = 32
SWA_KV_DIM = 256

VMEM_LIMIT = 52 * 1024 * 1024


def _cparams(sem):
    return pltpu.CompilerParams(dimension_semantics=sem, vmem_limit_bytes=VMEM_LIMIT)


def _dot(a, b):
    return jnp.dot(a.astype(BF16), b.astype(BF16), preferred_element_type=F32)


def _dot_nt(a, b):
    return lax.dot_general(a.astype(BF16), b.astype(BF16), (((1,), (1,)), ((), ())),
                           preferred_element_type=F32)


def _split3(x):
    hi = x.astype(BF16)
    r1 = x - hi.astype(F32)
    mid = r1.astype(BF16)
    lo = (r1 - mid.astype(F32)).astype(BF16)
    return hi, mid, lo


def _sigmoid(x):
    return 1.0 / (1.0 + jnp.exp2(x * (-math.log2(math.e))))


def _softplus(x):
    return jnp.maximum(x, 0.0) + jnp.log(1.0 + jnp.exp(-jnp.abs(x)))


def _mixer_out_dtype(block_rows):
    return BF16 if block_rows % BF16_SUBLANES == 0 else F32


PROJ_ROWS_PER_STEP = 512
OUT_PROJ_ROWS_PER_STEP = 1024
PROJ_CAST_ROWS = 256


def _round_up(n, m):
    return -(-n // m) * m


def _group_steps(row_counts, tm):
    n = [t // min(tm, t) for t in row_counts]
    starts = [sum(n[:g]) for g in range(len(n))]
    return n, starts


def _group_row_map(start, nsteps):
    return lambda i: (jnp.clip(i - start, 0, nsteps - 1), 0)


def _norm_proj_kernel(*refs, layout, group_steps, w_transposed):
    step = pl.program_id(0)
    aux_piece = layout["aux"]
    ngroups = len(group_steps)
    nout = 2 if aux_piece is None else 3
    x_refs = refs[0:ngroups]
    nw_ref, w_ref = refs[ngroups], refs[ngroups + 1]
    out_refs = refs[ngroups + 2:ngroups + 2 + nout * ngroups]
    scr = refs[ngroups + 2 + nout * ngroups:]
    wg_scr, wr_scr = scr[0], scr[1]
    wa_scr = scr[2] if aux_piece is not None else None
    d = w_ref.shape[1] if w_transposed else w_ref.shape[0]

    def cast_piece(dst, dst_off, src_off, width):
        wpad = _round_up(width, LANE)
        if w_transposed:
            for r in range(0, width, PROJ_CAST_ROWS):
                n = min(PROJ_CAST_ROWS, width - r)
                dst[dst_off + r:dst_off + r + n, :] = w_ref[src_off + r:src_off + r + n, :].astype(BF16)
            if wpad > width:
                dst[dst_off + width:dst_off + wpad, :] = jnp.zeros((wpad - width, d), BF16)
            return wpad
        for r in range(0, d, PROJ_CAST_ROWS):
            piece = w_ref[r:r + PROJ_CAST_ROWS, src_off:src_off + width]
            if wpad > width:
                piece = jnp.concatenate([piece, jnp.zeros((PROJ_CAST_ROWS, wpad - width), F32)], axis=1)
            dst[r:r + PROJ_CAST_ROWS, dst_off:dst_off + wpad] = piece.astype(BF16)
        return wpad

    def matmul(h, w_scr):
        if w_transposed:
            return lax.dot_general(h, w_scr[...], (((1,), (1,)), ((), ())), preferred_element_type=F32)
        return jnp.dot(h, w_scr[...], preferred_element_type=F32)

    @pl.when(step == 0)
    def _():
        cast_piece(wg_scr, 0, *layout["gate"])
        off = 0
        for src_off, width in layout["rest"]:
            off += cast_piece(wr_scr, off, src_off, width)
        if aux_piece is not None:
            cast_piece(wa_scr, 0, *aux_piece)

    def project(x_ref, outs):
        x = x_ref[...]
        ms = jnp.mean(x * x, axis=-1, keepdims=True)
        h = (x * lax.rsqrt(ms + NORM_EPS) * nw_ref[...]).astype(BF16)
        outs[0][...] = matmul(h, wg_scr).astype(outs[0].dtype)
        outs[1][...] = matmul(h, wr_scr).astype(outs[1].dtype)
        if aux_piece is not None:
            outs[2][...] = matmul(h, wa_scr)

    for g, (start, nsteps) in enumerate(group_steps):
        pl.when((step >= start) & (step < start + nsteps))(
            functools.partial(project, x_refs[g], out_refs[g * nout:(g + 1) * nout]))


def _proj_layout(kind):
    kd, di = GLA_KEY_DIM, D_INNER
    if kind == "gla":
        return dict(gate=(2 * kd + di, di),
                    rest=[(2 * kd, di), (0, kd), (kd, kd), (2 * kd + 2 * di, GLA_GATE_RANK)], aux=None)
    if kind == "ssd":
        return dict(gate=(0, di), rest=[(di, SSD_CONV_DIM)], aux=(di + SSD_CONV_DIM, SSD_HEADS))
    return dict(gate=(di + 2 * SWA_KV_DIM, di), rest=[(0, di + 2 * SWA_KV_DIM)], aux=None)


def _norm_proj(kind, xs, nw, w_in):
    d = xs[0].shape[1]
    layout = _proj_layout(kind)
    ng = layout["gate"][1]
    nr = sum(_round_up(w, LANE) for _, w in layout["rest"])
    has_aux = layout["aux"] is not None
    nsteps, starts = _group_steps([x.shape[0] for x in xs], PROJ_ROWS_PER_STEP)
    fix = lambda i: (0, 0)
    in_specs, out_specs, out_shape = [], [], []
    for x, n, s in zip(xs, nsteps, starts):
        t = x.shape[0]
        tm = t // n
        row = _group_row_map(s, n)
        in_specs.append(pl.BlockSpec((tm, d), row))
        out_specs += [pl.BlockSpec((tm, ng), row), pl.BlockSpec((tm, nr), row)]
        out_shape += [jax.ShapeDtypeStruct((t, ng), BF16), jax.ShapeDtypeStruct((t, nr), BF16)]
        if has_aux:
            out_specs.append(pl.BlockSpec((tm, LANE), row))
            out_shape.append(jax.ShapeDtypeStruct((t, LANE), F32))
    w_transposed = w_in.shape[1] % LANE != 0
    w_op = w_in.T if w_transposed else w_in
    swap = (lambda s: s[::-1]) if w_transposed else (lambda s: s)
    in_specs += [pl.BlockSpec((1, d), fix), pl.BlockSpec(w_op.shape, fix, pipeline_mode=pl.Buffered(1))]
    scratch = [pltpu.VMEM(swap((d, ng)), BF16), pltpu.VMEM(swap((d, nr)), BF16)]
    if has_aux:
        scratch.append(pltpu.VMEM(swap((d, LANE)), BF16))
    outs = pl.pallas_call(
        functools.partial(_norm_proj_kernel, layout=layout, group_steps=tuple(zip(starts, nsteps)),
                          w_transposed=w_transposed),
        grid=(sum(nsteps),),
        in_specs=in_specs,
        out_specs=out_specs,
        out_shape=out_shape,
        scratch_shapes=scratch,
        compiler_params=_cparams(("arbitrary",)),
        name="norm_proj_" + kind,
    )(*xs, nw.reshape(1, d), w_op)
    nout = 3 if has_aux else 2
    return [outs[g * nout:(g + 1) * nout] for g in range(len(xs))]


def _seg_rms(y, seg):
    parts = []
    for s in range(y.shape[1] // seg):
        p = y[:, s * seg:(s + 1) * seg]
        ms = jnp.mean(p * p, axis=-1, keepdims=True)
        parts.append(p * lax.rsqrt(ms + NORM_EPS))
    return jnp.concatenate(parts, axis=1)


def _out_proj_kernel(*refs, mode, final, group_steps):
    ngroups = len(group_steps)
    nw_ref, w_ref, fw_ref = refs[3 * ngroups:3 * ngroups + 3]
    out_refs = refs[3 * ngroups + 3:4 * ngroups + 3]
    w_scr = refs[4 * ngroups + 3]
    step = pl.program_id(0)

    @pl.when(step == 0)
    def _():
        for r in range(0, w_ref.shape[0], PROJ_CAST_ROWS):
            w_scr[r:r + PROJ_CAST_ROWS, :] = w_ref[r:r + PROJ_CAST_ROWS, :].astype(BF16)

    def project(o_ref, gate_ref, x_ref, out_ref):
        o = o_ref[...].astype(F32)
        gt = gate_ref[...].astype(F32)
        act = gt * _sigmoid(gt)
        if mode == "gla":
            y = _seg_rms(o, GLA_HEAD_V) * nw_ref[...] * act
        elif mode == "ssd":
            y = _seg_rms(o * act, SSD_GROUP_W) * nw_ref[...]
        else:
            y = o * act
        out = x_ref[...] + jnp.dot(y.astype(BF16), w_scr[...], preferred_element_type=F32)
        if final:
            ms = jnp.mean(out * out, axis=-1, keepdims=True)
            out = out * lax.rsqrt(ms + NORM_EPS) * fw_ref[...]
        out_ref[...] = out

    for g, (start, nsteps) in enumerate(group_steps):
        pl.when((step >= start) & (step < start + nsteps))(
            functools.partial(project, refs[3 * g], refs[3 * g + 1], refs[3 * g + 2], out_refs[g]))


def _out_proj(mode, os_, gates, nw, w_out, xs, fw, final):
    di = os_[0].shape[1]
    d = xs[0].shape[1]
    nsteps, starts = _group_steps([x.shape[0] for x in xs], OUT_PROJ_ROWS_PER_STEP)
    fix = lambda i: (0, 0)
    in_specs, out_specs, out_shape, operands = [], [], [], []
    for o2, g2, x2, n, s in zip(os_, gates, xs, nsteps, starts):
        t = x2.shape[0]
        tm = t // n
        row = _group_row_map(s, n)
        in_specs += [pl.BlockSpec((tm, di), row), pl.BlockSpec((tm, di), row), pl.BlockSpec((tm, d), row)]
        operands += [o2, g2, x2]
        out_specs.append(pl.BlockSpec((tm, d), row))
        out_shape.append(jax.ShapeDtypeStruct((t, d), F32))
    in_specs += [pl.BlockSpec((1, di), fix), pl.BlockSpec((di, d), fix, pipeline_mode=pl.Buffered(1)),
                 pl.BlockSpec((1, d), fix)]
    return pl.pallas_call(
        functools.partial(_out_proj_kernel, mode=mode, final=final, group_steps=tuple(zip(starts, nsteps))),
        grid=(sum(nsteps),),
        in_specs=in_specs,
        out_specs=out_specs,
        out_shape=out_shape,
        scratch_shapes=[pltpu.VMEM((di, d), BF16)],
        compiler_params=_cparams(("arbitrary",)),
        name="out_proj_" + mode,
    )(*operands, nw.reshape(1, di), w_out, fw.reshape(1, d))


GLA_CHUNKS_PER_STEP = 8
GLA_LONG_CHUNK = 128
GLA_SINGLE_ANCHOR_MAX_DECAY = 60.0


def _gla_levels(seglen):
    return tuple(seglen >> (i + 1) for i in range(int(math.log2(seglen))))


def _gla_consts(seglen, c):
    levels = _gla_levels(seglen)
    u = np.arange(c)[:, None]
    j = np.arange(c)[None, :]
    same = (u // seglen) == (j // seglen)
    blocks = [same & (j <= u), same & (j > u)]
    lev = np.full((c, c), len(levels) + 1, np.int32)
    for li, h in enumerate(levels):
        b = (u // (2 * h)) * (2 * h) + h - 1
        blocks.append((j > np.minimum(u, b)) & (j <= np.maximum(u, b)))
        sib = (u // (2 * h) == j // (2 * h)) & (u % (2 * h) >= h) & (j % (2 * h) < h)
        lev[sib] = li
    lev[np.eye(c, dtype=bool)] = len(levels)
    m = np.concatenate(blocks, axis=0).astype(np.float32)
    m3 = np.concatenate([m, m, m], axis=1)
    if 2 * c == LANE:
        lev = np.concatenate([lev, lev], axis=1)
    return jnp.asarray(m3, BF16), jnp.asarray(lev)


def _gla_scores(qx, kx, e_q, e_k, lev, nt):
    att = None
    for li, (eq, ek) in enumerate(zip(e_q, e_k)):
        a = nt(qx if eq is None else qx * eq, kx if ek is None else kx * ek)
        att = jnp.where(lev == li, a, 0.0 if att is None else att)
    return att


def _nt_pair(qe, ke):
    kb = ke.astype(BF16)
    z = jnp.zeros((kb.shape[0], GLA_HEAD_K), BF16)
    rhs = jnp.concatenate([jnp.concatenate([kb[:, 0:GLA_HEAD_K], z], axis=1),
                           jnp.concatenate([z, kb[:, GLA_HEAD_K:]], axis=1)], axis=0)
    return lax.dot_general(qe.astype(BF16), rhs, (((1,), (1,)), ((), ())), preferred_element_type=F32)


def _gla_prep(xc, wgk, bgk):
    v = xc[:, 0:D_INNER]
    q = xc[:, D_INNER:D_INNER + GLA_KEY_DIM].astype(F32) * (GLA_HEAD_K ** -0.5)
    k = xc[:, D_INNER + GLA_KEY_DIM:D_INNER + 2 * GLA_KEY_DIM].astype(F32)
    lr = xc[:, D_INNER + 2 * GLA_KEY_DIM:]
    z = _dot(lr, wgk) + bgk
    g = (jnp.minimum(z, 0.0) - jnp.log(1.0 + jnp.exp(-jnp.abs(z)))) * (1.0 / GLA_GATE_NORMALIZER)
    return v, q, k, g


def _gla_chunk(v, q, k, g, m3, lev, nl, single_anchor):
    c = g.shape[0]
    g3 = jnp.concatenate(_split3(g), axis=0)
    if single_anchor:
        rs = jnp.dot(m3[0:2 * c], g3, preferred_element_type=F32)
        cum = rs[0:c]
        e_in, e_out = jnp.exp(cum), jnp.exp(rs[c:2 * c])
        e_q, e_k = [e_in], [jnp.exp(-cum)]
        lev = jnp.where(lev <= nl, 0, 1)
    else:
        e_all = jnp.exp(jnp.dot(m3, g3, preferred_element_type=F32))
        e_in, e_out = e_all[0:c], e_all[c:2 * c]
        e_q = [e_all[(2 + li) * c:(3 + li) * c] for li in range(nl)] + [None]
        e_k = e_q
    qe_in = q * e_in
    ke_out = k * e_out
    outs, pending = [], []
    if c == LANE:
        for h in range(GLA_HEADS):
            ks = slice(h * GLA_HEAD_K, (h + 1) * GLA_HEAD_K)
            cut = lambda es: [None if e is None else e[:, ks] for e in es]
            att = _gla_scores(q[:, ks], k[:, ks], cut(e_q), cut(e_k), lev, _dot_nt)
            v_h = v[:, h * GLA_HEAD_V:(h + 1) * GLA_HEAD_V]
            xt = jnp.concatenate([ke_out[:, ks], e_in[:, ks]], axis=0).T
            outs.append(_dot(att, v_h))
            pending.append((h, qe_in[:, ks], xt, v_h))
        return outs, pending
    lane_lo = lax.broadcasted_iota(jnp.int32, (c, LANE), 1) < c
    for pr in range(GLA_HEADS // 2):
        ls = slice(2 * pr * GLA_HEAD_K, (2 * pr + 2) * GLA_HEAD_K)
        cut = lambda es: [None if e is None else e[:, ls] for e in es]
        att = _gla_scores(q[:, ls], k[:, ls], cut(e_q), cut(e_k), lev, _nt_pair)
        vpair = v[:, 2 * pr * GLA_HEAD_V:(2 * pr + 2) * GLA_HEAD_V]
        vcat = jnp.concatenate([vpair[:, 0:GLA_HEAD_V], vpair[:, GLA_HEAD_V:]], axis=0)
        for half, att_h in enumerate((jnp.where(lane_lo, att, 0.0), jnp.where(lane_lo, 0.0, att))):
            h = 2 * pr + half
            ks = slice(h * GLA_HEAD_K, (h + 1) * GLA_HEAD_K)
            xt = jnp.concatenate([ke_out[:, ks], e_in[:, ks]], axis=0).T
            outs.append(_dot(att_h, vcat))
            pending.append((h, qe_in[:, ks], xt, v[:, h * GLA_HEAD_V:(h + 1) * GLA_HEAD_V]))
    return outs, pending


def _gla_finish(outs, pending, state_io):
    return jnp.concatenate([o + state_io(*p) for o, p in zip(outs, pending)], axis=1)


def _gla_dispatch(preps, run):
    low = None
    for _, _, _, g in preps:
        tot = jnp.min(jnp.sum(g, axis=0, keepdims=True))
        low = tot if low is None else jnp.minimum(low, tot)
    single_anchor_ok = low >= -GLA_SINGLE_ANCHOR_MAX_DECAY
    pl.when(single_anchor_ok)(functools.partial(run, True))
    pl.when(jnp.logical_not(single_anchor_ok))(functools.partial(run, False))


def _gla_carry_kernel(rest_ref, s0_ref, wgk_ref, bgk_ref, m_ref, lev_ref, o_ref, s_ref, *, c, nch, nl):
    @pl.when(pl.program_id(1) == 0)
    def _():
        s_ref[...] = s0_ref[...]

    def state_io(h, qe_in_h, xt, v_h):
        s_h = s_ref[0, h]
        s_ref[0, h] = s_h * xt[:, 2 * c - 1:2 * c] + _dot(xt[:, 0:c], v_h)
        return _dot(qe_in_h, s_h)

    wgk, bgk = wgk_ref[...], bgk_ref[...]
    preps = [_gla_prep(rest_ref[0, ci * c:(ci + 1) * c, :], wgk, bgk) for ci in range(nch)]

    def run(single_anchor):
        staged = [_gla_chunk(v, q, k, g, m_ref[...], lev_ref[...], nl, single_anchor) for v, q, k, g in preps]
        for ci, (outs, pending) in enumerate(staged):
            o_ref[0, ci * c:(ci + 1) * c, :] = _gla_finish(outs, pending, state_io).astype(o_ref.dtype)

    _gla_dispatch(preps, run)


def _gla_packed_kernel(rest_ref, s0_ref, wgk_ref, bgk_ref, m_ref, lev_ref, o_ref, s_ref, *, seglen, nl):
    c = CHUNK
    nseg = c // seglen
    seg_of_lane = lax.broadcasted_iota(jnp.int32, (GLA_HEAD_K, c), 1) // seglen

    def state_io(h, qe_in_h, xt, v_h):
        ke_t = xt[:, 0:c]
        outs = []
        for sg in range(nseg):
            s_h = s0_ref[sg, h]
            outs.append(_dot(qe_in_h[sg * seglen:(sg + 1) * seglen], s_h))
            end = c + (sg + 1) * seglen - 1
            s_ref[sg, h] = s_h * xt[:, end:end + 1] + _dot(jnp.where(seg_of_lane == sg, ke_t, 0.0), v_h)
        return jnp.concatenate(outs, axis=0)

    preps = [_gla_prep(rest_ref[...], wgk_ref[...], bgk_ref[...])]

    def run(single_anchor):
        v, q, k, g = preps[0]
        outs, pending = _gla_chunk(v, q, k, g, m_ref[...], lev_ref[...], nl, single_anchor)
        o_ref[...] = _gla_finish(outs, pending, state_io).astype(o_ref.dtype)

    _gla_dispatch(preps, run)


def _gla_scan(rest3, s0, w_gk2, b_gk):
    b, l, nr = rest3.shape
    packed = l < CHUNK
    c = CHUNK if packed or l % GLA_LONG_CHUNK else GLA_LONG_CHUNK
    seglen = l if packed else c
    assert c % seglen == 0 and seglen % 8 == 0 and (l % c == 0 or packed)
    nl = len(_gla_levels(seglen))
    m01, lev = _gla_consts(seglen, c)
    wgk = jnp.zeros((LANE, GLA_KEY_DIM), F32).at[:GLA_GATE_RANK].set(w_gk2).astype(BF16)
    consts = (wgk, b_gk.reshape(1, GLA_KEY_DIM), m01, lev)
    if packed:
        nseq = c // seglen
        assert b % nseq == 0
        sblk = (nseq, GLA_HEADS, GLA_HEAD_K, GLA_HEAD_V)
        o, s_fin = pl.pallas_call(
            functools.partial(_gla_packed_kernel, seglen=seglen, nl=nl),
            grid=(b // nseq,),
            in_specs=[pl.BlockSpec((c, nr), lambda i: (i, 0)),
                      pl.BlockSpec(sblk, lambda i: (i, 0, 0, 0))]
                     + [pl.BlockSpec(a.shape, lambda i: (0, 0)) for a in consts],
            out_specs=[pl.BlockSpec((c, D_INNER), lambda i: (i, 0)),
                       pl.BlockSpec(sblk, lambda i: (i, 0, 0, 0))],
            out_shape=[jax.ShapeDtypeStruct((b * l, D_INNER), BF16), jax.ShapeDtypeStruct(s0.shape, F32)],
            compiler_params=_cparams(("parallel",)),
            name="gla_scan_packed",
        )(rest3.reshape(b * l, nr), s0, *consts)
        return o.reshape(b, l, D_INNER), s_fin
    nch = math.gcd(l // c, GLA_CHUNKS_PER_STEP)
    sblk = (1, GLA_HEADS, GLA_HEAD_K, GLA_HEAD_V)
    return pl.pallas_call(
        functools.partial(_gla_carry_kernel, c=c, nch=nch, nl=nl),
        grid=(b, l // (c * nch)),
        in_specs=[pl.BlockSpec((1, c * nch, nr), lambda i, t: (i, t, 0)),
                  pl.BlockSpec(sblk, lambda i, t: (i, 0, 0, 0))]
                 + [pl.BlockSpec(a.shape, lambda i, t: (0, 0)) for a in consts],
        out_specs=[pl.BlockSpec((1, c * nch, D_INNER), lambda i, t: (i, t, 0)),
                   pl.BlockSpec(sblk, lambda i, t: (i, 0, 0, 0))],
        out_shape=[jax.ShapeDtypeStruct((b, l, D_INNER), BF16), jax.ShapeDtypeStruct(s0.shape, F32)],
        compiler_params=_cparams(("parallel", "arbitrary")),
        name="gla_scan",
    )(rest3, s0, *consts)


SSD_CHUNKS_PER_STEP = 4


def _ssd2_consts(seglen):
    c = CHUNK
    u = np.arange(c)
    same = (u[:, None] // seglen) == (u[None, :] // seglen)
    tl = same & (u[None, :] <= u[:, None])
    m = np.concatenate([tl, same], axis=0).astype(np.float32)
    m3 = np.concatenate([m, m, m], axis=1)
    expand = (np.arange(LANE)[:, None] == (np.arange(D_INNER) // SSD_HEAD_DIM)[None, :]).astype(np.float32)
    ex3 = np.concatenate([expand] * 3, axis=0)
    s_of = np.arange(LANE) % c
    tri = np.where(same[:, s_of] & (s_of[None, :] <= u[:, None]), 0.0, NEG_BIG).astype(np.float32)
    bd = ((np.arange(2 * c)[:, None] // c) == (np.arange(LANE)[None, :] // SSD_HEAD_DIM)).astype(np.float32)
    return jnp.asarray(m3, BF16), jnp.asarray(ex3, BF16), jnp.asarray(tri), jnp.asarray(bd)


def _ssd2_conv(ext_ref, rows, cw_ref, cb_ref):
    e = ext_ref[0:8 + rows, :]
    acc = cw_ref[SSD_CONV - 1:SSD_CONV, :] * e[8:]
    for k in range(1, SSD_CONV):
        acc = acc + cw_ref[SSD_CONV - 1 - k:SSD_CONV - k, :] * pltpu.roll(e, k, axis=0)[8:]
    return cb_ref[...] + acc


def _ssd2_chunk(dt_raw, conv, dtb, a_neg, dsk, m3, ex3, tri, bd, state_io):
    c = CHUNK
    xbc = conv * _sigmoid(conv)
    xs = xbc[:, 0:D_INNER]
    bm = xbc[:, D_INNER:D_INNER + SSD_GROUPS * SSD_STATE]
    cm = xbc[:, D_INNER + SSD_GROUPS * SSD_STATE:]
    dt = _softplus(dt_raw + dtb)
    la = dt * a_neg
    rs = jnp.dot(m3, jnp.concatenate(_split3(la), axis=0), preferred_element_type=F32)
    cum, tot = rs[0:c], rs[c:2 * c]
    stack = jnp.concatenate([dt * jnp.exp(tot - cum), cum], axis=0)
    rep = jnp.dot(jnp.concatenate(_split3(stack), axis=1), ex3, preferred_element_type=F32)
    dtw_rep, cum_rep = rep[0:c], rep[c:2 * c]
    ecum_rep = jnp.exp(cum_rep)
    uw = xs * dtw_rep
    at = jnp.concatenate([cum, cum], axis=0).T
    adt = jnp.concatenate([dt, dt], axis=0).T
    lane_lo = lax.broadcasted_iota(jnp.int32, (1, LANE), 1) < c
    ys = []
    for g in range(SSD_GROUPS):
        gs = slice(g * SSD_GROUP_W, (g + 1) * SSD_GROUP_W)
        ns = slice(g * SSD_STATE, (g + 1) * SSD_STATE)
        cg, bg = cm[:, ns], bm[:, ns]
        cb_rep = _dot_nt(cg, jnp.concatenate([bg] * (SSD_GROUP_W // c), axis=0))
        parts = []
        for p in range(SSD_GROUP_W // LANE):
            col = g * (SSD_GROUP_W // LANE) + p
            ps = slice(col * LANE, (col + 1) * LANE)
            cum_s = jnp.where(lane_lo, at[2 * col:2 * col + 1, :], at[2 * col + 1:2 * col + 2, :])
            dt_s = jnp.where(lane_lo, adt[2 * col:2 * col + 1, :], adt[2 * col + 1:2 * col + 2, :])
            w = jnp.exp(cum_rep[:, ps] - cum_s + tri) * (cb_rep[:, p * LANE:(p + 1) * LANE] * dt_s)
            up = xs[:, ps]
            parts.append(_dot(w, jnp.concatenate([up, up], axis=0) * bd))
        y_inter = state_io(g, cg, bg, uw[:, gs], ecum_rep[:, gs])
        ys.append(jnp.concatenate(parts, axis=1) + y_inter + xs[:, gs] * dsk[:, gs])
    return jnp.concatenate(ys, axis=1)


def _ssd2_carry_kernel(rest_ref, dtr_ref, s0_ref, conv0_ref, cw_ref, cb_ref, dtb_ref, alog_ref, dsk_ref,
                       m3_ref, ex3_ref, tri_ref, bd_ref, y_ref, sfin_ref, ext_ref, st_ref, *, nch):
    c = CHUNK
    t = pl.program_id(1)

    @pl.when(t == 0)
    def _():
        ext_ref[0:8, :] = conv0_ref[0]
        for g in range(SSD_GROUPS):
            st_ref[g] = s0_ref[0, g].T

    a_neg = -jnp.exp(alog_ref[...])

    def state_io(g, cg, bg, uw_g, ecum_g):
        s_g = st_ref[g]
        y_inter = _dot(cg, s_g) * ecum_g
        bg_t = jnp.concatenate([bg, jnp.zeros_like(bg)], axis=0).T[:, 0:c]
        st_ref[g] = s_g * ecum_g[c - 1:c, :] + _dot(bg_t, uw_g)
        return y_inter

    for ci in range(nch):
        ext_ref[8:8 + c, :] = rest_ref[0, ci * c:(ci + 1) * c, :].astype(F32)
        conv = _ssd2_conv(ext_ref, c, cw_ref, cb_ref)
        ext_ref[0:8, :] = ext_ref[c:c + 8, :]
        y = _ssd2_chunk(dtr_ref[0, ci * c:(ci + 1) * c, :], conv, dtb_ref[...], a_neg, dsk_ref[...],
                        m3_ref[...], ex3_ref[...], tri_ref[...], bd_ref[...], state_io)
        y_ref[0, ci * c:(ci + 1) * c, :] = y.astype(y_ref.dtype)

    @pl.when(t == pl.num_programs(1) - 1)
    def _():
        for g in range(SSD_GROUPS):
            sfin_ref[0, g] = st_ref[g].T


def _ssd2_packed_kernel(rest_ref, dtr_ref, s0_ref, conv0_ref, cw_ref, cb_ref, dtb_ref, alog_ref, dsk_ref,
                        m3_ref, ex3_ref, tri_ref, bd_ref, y_ref, sfin_ref, ext_ref, *, seglen):
    c = CHUNK
    nseg = c // seglen
    ext_ref[0:8, :] = jnp.zeros((8, SSD_CONV_DIM), F32)
    xbc_raw = rest_ref[...].astype(F32)
    for sg in range(nseg):
        ext_ref[8 + 16 * sg:16 + 16 * sg, :] = conv0_ref[sg]
        ext_ref[16 + 16 * sg:24 + 16 * sg, :] = xbc_raw[sg * seglen:(sg + 1) * seglen]
    conv2 = _ssd2_conv(ext_ref, 2 * c, cw_ref, cb_ref)
    conv = jnp.concatenate([conv2[16 * sg + 8:16 * sg + 16] for sg in range(nseg)], axis=0)
    a_neg = -jnp.exp(alog_ref[...])
    seg_of_row = lax.broadcasted_iota(jnp.int32, (c, SSD_STATE), 0) // seglen

    def state_io(g, cg, bg, uw_g, ecum_g):
        ends = jnp.concatenate([ecum_g[(sg + 1) * seglen - 1:(sg + 1) * seglen, :] for sg in range(nseg)], axis=0)
        fill = jnp.zeros((LANE - c - nseg, SSD_GROUP_W), F32)
        tg = jnp.concatenate([uw_g, ends, fill], axis=0).T
        uw_t = tg[:, 0:c]
        y_parts = []
        for sg in range(nseg):
            s_sg = s0_ref[sg, g]
            y_parts.append(_dot_nt(cg[sg * seglen:(sg + 1) * seglen], s_sg))
            b_sg = jnp.where(seg_of_row == sg, bg, 0.0)
            sfin_ref[sg, g] = s_sg * tg[:, c + sg:c + sg + 1] + _dot(uw_t, b_sg)
        return jnp.concatenate(y_parts, axis=0) * ecum_g

    y = _ssd2_chunk(dtr_ref[...], conv, dtb_ref[...], a_neg, dsk_ref[...], m3_ref[...], ex3_ref[...],
                    tri_ref[...], bd_ref[...], state_io)
    y_ref[...] = y.astype(y_ref.dtype)


def _ssd2_scan(rest3, dt_raw3, s0, conv0, conv_w, conv_b, dt_bias, a_log, d_skip):
    b, l, nr = rest3.shape
    c = CHUNK
    packed = l < c
    seglen = l if packed else c
    assert c % seglen == 0 and seglen % 8 == 0 and (l % c == 0 or packed)
    m3, ex3, tri, bd = _ssd2_consts(seglen)
    s0g = s0.reshape(b, SSD_GROUPS, SSD_GROUP_W, SSD_STATE)
    conv0p = jnp.concatenate([jnp.zeros((b, 8 - (SSD_CONV - 1), SSD_CONV_DIM), F32), conv0], axis=1)
    pad = lambda a: jnp.zeros((1, LANE), F32).at[0, :SSD_HEADS].set(a)
    dsk = jnp.repeat(d_skip, SSD_HEAD_DIM).reshape(1, D_INNER)
    consts = (conv_w, conv_b.reshape(1, -1), pad(dt_bias), pad(a_log), dsk, m3, ex3, tri, bd)
    if packed:
        nseq = c // seglen
        assert b % nseq == 0
        fix = lambda i: (0, 0)
        const_specs = [pl.BlockSpec(a.shape, fix) for a in consts]
        sblk = (nseq, SSD_GROUPS, SSD_GROUP_W, SSD_STATE)
        y, sfin = pl.pallas_call(
            functools.partial(_ssd2_packed_kernel, seglen=seglen),
            grid=(b // nseq,),
            in_specs=[pl.BlockSpec((c, nr), lambda i: (i, 0)),
                      pl.BlockSpec((c, LANE), lambda i: (i, 0)),
                      pl.BlockSpec(sblk, lambda i: (i, 0, 0, 0)),
                      pl.BlockSpec((nseq, 8, SSD_CONV_DIM), lambda i: (i, 0, 0))] + const_specs,
            out_specs=[pl.BlockSpec((c, D_INNER), lambda i: (i, 0)),
                       pl.BlockSpec(sblk, lambda i: (i, 0, 0, 0))],
            out_shape=[jax.ShapeDtypeStruct((b * l, D_INNER), BF16), jax.ShapeDtypeStruct(s0g.shape, F32)],
            scratch_shapes=[pltpu.VMEM((2 * c + 8, SSD_CONV_DIM), F32)],
            compiler_params=_cparams(("parallel",)),
            name="ssd_scan_packed",
        )(rest3.reshape(b * l, nr), dt_raw3.reshape(b * l, LANE), s0g, conv0p, *consts)
        return y.reshape(b, l, D_INNER), sfin.reshape(s0.shape)
    nch = math.gcd(l // c, SSD_CHUNKS_PER_STEP)
    fix2 = lambda i, t: (0, 0)
    const_specs = [pl.BlockSpec(a.shape, fix2) for a in consts]
    sblk = (1, SSD_GROUPS, SSD_GROUP_W, SSD_STATE)
    y, sfin = pl.pallas_call(
        functools.partial(_ssd2_carry_kernel, nch=nch),
        grid=(b, l // (c * nch)),
        in_specs=[pl.BlockSpec((1, c * nch, nr), lambda i, t: (i, t, 0)),
                  pl.BlockSpec((1, c * nch, LANE), lambda i, t: (i, t, 0)),
                  pl.BlockSpec(sblk, lambda i, t: (i, 0, 0, 0)),
                  pl.BlockSpec((1, 8, SSD_CONV_DIM), lambda i, t: (i, 0, 0))] + const_specs,
        out_specs=[pl.BlockSpec((1, c * nch, D_INNER), lambda i, t: (i, t, 0)),
                   pl.BlockSpec(sblk, lambda i, t: (i, 0, 0, 0))],
        out_shape=[jax.ShapeDtypeStruct((b, l, D_INNER), BF16), jax.ShapeDtypeStruct(s0g.shape, F32)],
        scratch_shapes=[pltpu.VMEM((c + 8, SSD_CONV_DIM), F32),
                        pltpu.VMEM((SSD_GROUPS, SSD_STATE, SSD_GROUP_W), F32)],
        compiler_params=_cparams(("parallel", "arbitrary")),
        name="ssd_scan",
    )(rest3, dt_raw3, s0g, conv0p, *consts)
    return y, sfin.reshape(s0.shape)


SWA_SEQS_PER_STEP = 16
SWA_BLOCKS_PER_STEP = 8


def _swa_mask(bq, first_block_has_no_past):
    rows = 4 * bq
    tq = (np.arange(rows) % bq)[:, None]
    s = (np.arange(4 * SWA_WINDOW) % (2 * SWA_WINDOW))[None, :]
    ok = (s > tq) & (s <= tq + SWA_WINDOW)
    if first_block_has_no_past:
        ok = ok & (s >= SWA_WINDOW)
    return np.where(ok, 0.0, NEG_BIG).astype(np.float32)


def _block_diag_pair(col, rolled, odd):
    lane = lax.broadcasted_iota(jnp.int32, col.shape, 1)
    lo = lane < SWA_HEAD_DIM
    if odd:
        top = jnp.where(lo, rolled, 0.0)
        bot = jnp.where(lo, 0.0, col)
    else:
        top = jnp.where(lo, col, 0.0)
        bot = jnp.where(lo, 0.0, rolled)
    return jnp.concatenate([top, bot], axis=0)


def _swa_kernel(sink_ref, x_ref, kprev_ref, vprev_ref, mask_ref, ones_ref, o_ref, *cache_refs, bq, items, nsub):
    w = SWA_WINDOW
    n = pl.program_id(1)
    npair = (SWA_Q_HEADS // SWA_KV_HEADS) // 2
    rows = npair * bq
    lane_lo = lax.broadcasted_iota(jnp.int32, (rows, LANE), 1) < SWA_HEAD_DIM
    sinks = []
    for j in range(SWA_KV_HEADS):
        sinks.append([jnp.concatenate(
            [jnp.full((bq, 1), sink_ref[j * 2 * npair + 2 * p + half], F32) for p in range(npair)], axis=0)
            for half in range(2)])

    kcols = slice(D_INNER, D_INNER + SWA_KV_DIM)
    vcols = slice(D_INNER + SWA_KV_DIM, D_INNER + 2 * SWA_KV_DIM)
    work = []
    for i in range(items * nsub):
        if nsub == 1:
            x = x_ref[i]
            kprev, vprev = kprev_ref[i], vprev_ref[i]
            amask = mask_ref[jnp.where(n == 0, 0, 1)]
            out_rows = (i, slice(None))
        else:
            x = x_ref[0, i * bq:(i + 1) * bq, :]
            if i == 0:
                kprev, vprev = kprev_ref[0], vprev_ref[0]
                amask = mask_ref[jnp.where(n == 0, 0, 1)]
            else:
                kprev = x_ref[0, (i - 1) * bq:i * bq, kcols]
                vprev = x_ref[0, (i - 1) * bq:i * bq, vcols]
                amask = mask_ref[1]
            out_rows = (0, slice(i * bq, (i + 1) * bq))
        fill = [] if bq == w else [jnp.zeros((w - bq, SWA_KV_DIM), F32)]
        kall = jnp.concatenate([kprev, x[:, kcols]] + fill, axis=0)
        vall = jnp.concatenate([vprev, x[:, vcols]] + fill, axis=0)
        if cache_refs:
            cache_refs[0][i] = kall[bq:bq + w]
            cache_refs[1][i] = vall[bq:bq + w]
        work.append((x, kall, vall, amask, out_rows))

    nwork = len(work)
    tile_rows = lambda a: jnp.concatenate([a] * nwork, axis=0)
    swap = lambda a: jnp.concatenate([a[:, SWA_HEAD_DIM:], a[:, 0:SWA_HEAD_DIM]], axis=1)
    lane_lo_all = tile_rows(lane_lo)
    for j in range(SWA_KV_HEADS):
        cs = slice((j // 2) * LANE, (j // 2 + 1) * LANE)
        qbase = j * npair * LANE
        scs, v2es = [], []
        for x, kall, vall, amask, _ in work:
            kcol, vcol = kall[:, cs], vall[:, cs]
            k2 = _block_diag_pair(kcol, swap(kcol), j % 2)
            v2 = _block_diag_pair(vcol, swap(vcol), j % 2)
            v2es.append(jnp.concatenate([v2.astype(BF16), ones_ref[...]], axis=1))
            qs = jnp.concatenate([x[:, qbase + p * LANE:qbase + (p + 1) * LANE] for p in range(npair)],
                                 axis=0) * (SWA_HEAD_DIM ** -0.5)
            scs.append(_dot_nt(qs, k2) + amask)
        sc = jnp.concatenate(scs, axis=0)
        pes, ms = [], []
        for half in range(2):
            sh = sc[:, half * 2 * w:(half + 1) * 2 * w]
            sink = tile_rows(sinks[j][half])
            m = jnp.maximum(jnp.max(sh, axis=-1, keepdims=True), sink)
            pes.append(jnp.exp(sh - m).astype(BF16))
            ms.append(sink - m)
        pe = jnp.concatenate(pes, axis=1)
        o = jnp.concatenate([jnp.dot(pe[i * rows:(i + 1) * rows], v2es[i], preferred_element_type=F32)
                             for i in range(nwork)], axis=0)
        esink = jnp.exp(jnp.where(lane_lo_all, ms[0], ms[1]))
        res = o[:, 0:LANE] / (o[:, LANE:] + esink)
        for i, (_, _, _, _, out_rows) in enumerate(work):
            for p in range(npair):
                o_ref[out_rows[0], out_rows[1], qbase + p * LANE:qbase + (p + 1) * LANE] = (
                    res[i * rows + p * bq:i * rows + (p + 1) * bq].astype(o_ref.dtype))


def _swa_attn(rest3, kprev, vprev, sinks, has_past):
    b, l, nr = rest3.shape
    w = SWA_WINDOW
    bq = math.gcd(l, w)
    nb = l // bq
    assert nb == 1 or bq == w
    m_later = _swa_mask(bq, False)
    m_first = m_later if has_past else _swa_mask(bq, True)
    masks = jnp.asarray(np.stack([m_first, m_later]))
    ones2 = jnp.asarray((np.arange(4 * w)[:, None] // (2 * w)) == (np.arange(LANE)[None, :] // SWA_HEAD_DIM), BF16)
    if nb == 1:
        rest3 = rest3.astype(F32)
        items, nsub = math.gcd(b, SWA_SEQS_PER_STEP), 1
        kspec = pl.BlockSpec((items, w, SWA_KV_DIM), lambda i, n: (i, 0, 0))
        vspec = kspec
        kin, vin = kprev, vprev
    else:
        assert not has_past
        items, nsub = 1, math.gcd(nb, SWA_BLOCKS_PER_STEP)
        kcol = D_INNER // SWA_KV_DIM
        kspec = pl.BlockSpec((1, w, SWA_KV_DIM), lambda i, n: (i, jnp.maximum(n * nsub - 1, 0), kcol))
        vspec = pl.BlockSpec((1, w, SWA_KV_DIM), lambda i, n: (i, jnp.maximum(n * nsub - 1, 0), kcol + 1))
        kin, vin = rest3, rest3
    out_specs = [pl.BlockSpec((items, bq * nsub, D_INNER), lambda i, n: (i, n, 0))]
    out_shape = [jax.ShapeDtypeStruct((b, l, D_INNER), _mixer_out_dtype(bq))]
    if nb == 1:
        out_specs += [pl.BlockSpec((items, w, SWA_KV_DIM), lambda i, n: (i, 0, 0))] * 2
        out_shape += [jax.ShapeDtypeStruct((b, w, SWA_KV_DIM), F32)] * 2
    return pl.pallas_call(
        functools.partial(_swa_kernel, bq=bq, items=items, nsub=nsub),
        grid=(b // items, nb // nsub),
        in_specs=[
            pl.BlockSpec(memory_space=pltpu.SMEM),
            pl.BlockSpec((items, bq * nsub, nr), lambda i, n: (i, n, 0)),
            kspec,
            vspec,
            pl.BlockSpec(masks.shape, lambda i, n: (0, 0, 0)),
            pl.BlockSpec(ones2.shape, lambda i, n: (0, 0)),
        ],
        out_specs=out_specs,
        out_shape=out_shape,
        compiler_params=_cparams(("parallel", "parallel")),
        name="swa_attn",
    )(sinks, rest3, kin, vin, masks, ones2)


def _trunk(xs, sts, p, has_pasts):
    shapes = [x.shape for x in xs]
    x2s = [x.reshape(-1, x.shape[-1]) for x in xs]
    ones_inner = jnp.ones((D_INNER,), F32)
    news = [{} for _ in xs]
    flat = lambda a: a.reshape(-1, a.shape[-1])

    def gla(i, x2s, key):
        proj = _norm_proj("gla", x2s, p[f"l{i}_norm"], p[f"l{i}_w_in"])
        os_ = []
        for g, ((b, l, _), (gate, rest)) in enumerate(zip(shapes, proj)):
            o, news[g][key] = _gla_scan(rest.reshape(b, l, -1), sts[g][key], p[f"l{i}_w_gk2"], p[f"l{i}_b_gk"])
            os_.append(flat(o))
        hn = jnp.tile(p[f"l{i}_head_norm"], GLA_HEADS)
        return _out_proj("gla", os_, [pr[0] for pr in proj], hn, p[f"l{i}_w_out"], x2s, p["final_norm"], i == 3)

    x2s = gla(0, x2s, "gla0")

    proj = _norm_proj("ssd", x2s, p["l1_norm"], p["l1_w_in"])
    os_ = []
    for g, ((b, l, _), (gate, rest, dt_raw)) in enumerate(zip(shapes, proj)):
        st = sts[g]
        rest3 = rest.reshape(b, l, -1)
        y, news[g]["ssm"] = _ssd2_scan(rest3, dt_raw.reshape(b, l, -1), st["ssm"], st["conv"], p["l1_conv_w"],
                                       p["l1_conv_b"], p["l1_dt_bias"], p["l1_a_log"], p["l1_d_skip"])
        ext_tail = jnp.concatenate([st["conv"], rest3[:, max(l - (SSD_CONV - 1), 0):, :].astype(F32)], axis=1)
        news[g]["conv"] = ext_tail[:, -(SSD_CONV - 1):]
        os_.append(flat(y))
    x2s = _out_proj("ssd", os_, [pr[0] for pr in proj], p["l1_gate_norm"], p["l1_w_out"], x2s,
                    p["final_norm"], False)

    proj = _norm_proj("swa", x2s, p["l2_norm"], p["l2_w_in"])
    os_ = []
    for g, ((b, l, _), (gate, rest)) in enumerate(zip(shapes, proj)):
        st = sts[g]
        rest3 = rest.reshape(b, l, -1)
        kprev = st["swa_k"].reshape(b, SWA_WINDOW, SWA_KV_DIM)
        vprev = st["swa_v"].reshape(b, SWA_WINDOW, SWA_KV_DIM)
        res = _swa_attn(rest3, kprev, vprev, p["l2_sinks"], has_pasts[g])
        if len(res) == 3:
            k_win, v_win = res[1], res[2]
        else:
            k_win = rest3[:, l - SWA_WINDOW:, D_INNER:D_INNER + SWA_KV_DIM].astype(F32)
            v_win = rest3[:, l - SWA_WINDOW:, D_INNER + SWA_KV_DIM:].astype(F32)
        news[g]["swa_k"] = k_win.reshape(st["swa_k"].shape)
        news[g]["swa_v"] = v_win.reshape(st["swa_v"].shape)
        os_.append(flat(res[0]))
    x2s = _out_proj("swa", os_, [pr[0] for pr in proj], ones_inner, p["l2_w_out"], x2s, p["final_norm"], False)

    x2s = gla(3, x2s, "gla3")
    return [x2.reshape(s) for x2, s in zip(x2s, shapes)], news


def kernel(x_prompt, x_sample, state_gla_0, state_ssm_1, state_conv_1, cache_swa_k_2, cache_swa_v_2, state_gla_3, l0_norm, l0_w_in, l0_w_gk2, l0_b_gk, l0_head_norm, l0_w_out, l1_norm, l1_w_in, l1_conv_w, l1_conv_b, l1_dt_bias, l1_a_log, l1_d_skip, l1_gate_norm, l1_w_out, l2_norm, l2_w_in, l2_sinks, l2_w_out, l3_norm, l3_w_in, l3_w_gk2, l3_b_gk, l3_head_norm, l3_w_out, final_norm):
    p = dict(l0_norm=l0_norm, l0_w_in=l0_w_in, l0_w_gk2=l0_w_gk2, l0_b_gk=l0_b_gk,
             l0_head_norm=l0_head_norm, l0_w_out=l0_w_out,
             l1_norm=l1_norm, l1_w_in=l1_w_in, l1_conv_w=l1_conv_w, l1_conv_b=l1_conv_b,
             l1_dt_bias=l1_dt_bias, l1_a_log=l1_a_log, l1_d_skip=l1_d_skip,
             l1_gate_norm=l1_gate_norm, l1_w_out=l1_w_out,
             l2_norm=l2_norm, l2_w_in=l2_w_in, l2_sinks=l2_sinks, l2_w_out=l2_w_out,
             l3_norm=l3_norm, l3_w_in=l3_w_in, l3_w_gk2=l3_w_gk2, l3_b_gk=l3_b_gk,
             l3_head_norm=l3_head_norm, l3_w_out=l3_w_out, final_norm=final_norm)

    bp = x_prompt.shape[0]
    z = lambda a: jnp.zeros((bp,) + a.shape[1:], a.dtype)
    st_p = dict(gla0=z(state_gla_0), ssm=z(state_ssm_1), conv=z(state_conv_1),
                swa_k=z(cache_swa_k_2), swa_v=z(cache_swa_v_2), gla3=z(state_gla_3))
    st_s = dict(gla0=state_gla_0, ssm=state_ssm_1, conv=state_conv_1,
                swa_k=cache_swa_k_2, swa_v=cache_swa_v_2, gla3=state_gla_3)
    (y_p,), (n_p,) = _trunk([x_prompt], [st_p], p, [False])
    (y_s,), (n_s,) = _trunk([x_sample], [st_s], p, [True])
    return (y_p, y_s,
            n_p["gla0"], n_s["gla0"],
            n_p["ssm"], n_s["ssm"],
            n_p["conv"], n_s["conv"],
            n_p["swa_k"], n_s["swa_k"],
            n_p["swa_v"], n_s["swa_v"],
            n_p["gla3"], n_s["gla3"])
```

```python
import functools
import math

import numpy as np
import jax
import jax.numpy as jnp
from jax import lax
from jax.experimental import pallas as pl
from jax.experimental.pallas import tpu as pltpu

F32 = jnp.float32
BF16 = jnp.bfloat16

NORM_EPS = 1e-6
D_MODEL = 1024
D_INNER = 2048
CHUNK = 64
LANE = 128
BF16_SUBLANES = 16
NEG_BIG = -1e30

GLA_HEADS = 4
GLA_HEAD_K = 128
GLA_HEAD_V = 512
GLA_KEY_DIM = 512
GLA_GATE_RANK = 16
GLA_GATE_NORMALIZER = 16.0

SSD_GROUPS = 4
SSD_HEADS = 32
SSD_HEAD_DIM = 64
SSD_STATE = 128
SSD_CONV = 4
SSD_CONV_DIM = 3072
SSD_GROUP_W = D_INNER // SSD_GROUPS

SWA_WINDOW = 128
SWA_KV_HEADS = 4
SWA_HEAD_DIM = 64
SWA_Q_HEADS = 32
SWA_KV_DIM = 256

VMEM_LIMIT = 52 * 1024 * 1024


def _cparams(sem):
    return pltpu.CompilerParams(dimension_semantics=sem, vmem_limit_bytes=VMEM_LIMIT)


def _dot(a, b):
    return jnp.dot(a.astype(BF16), b.astype(BF16), preferred_element_type=F32)


def _dot_nt(a, b):
    return lax.dot_general(a.astype(BF16), b.astype(BF16), (((1,), (1,)), ((), ())),
                           preferred_element_type=F32)


def _split3(x):
    hi = x.astype(BF16)
    r1 = x - hi.astype(F32)
    mid = r1.astype(BF16)
    lo = (r1 - mid.astype(F32)).astype(BF16)
    return hi, mid, lo


def _sigmoid(x):
    return 1.0 / (1.0 + jnp.exp2(x * (-math.log2(math.e))))


def _softplus(x):
    return jnp.maximum(x, 0.0) + jnp.log(1.0 + jnp.exp(-jnp.abs(x)))


def _mixer_out_dtype(block_rows):
    return BF16 if block_rows % BF16_SUBLANES == 0 else F32


PROJ_ROWS_PER_STEP = 512
OUT_PROJ_ROWS_PER_STEP = 1024
PROJ_CAST_ROWS = 256


def _round_up(n, m):
    return -(-n // m) * m


def _group_steps(row_counts, tm):
    n = [t // min(tm, t) for t in row_counts]
    starts = [sum(n[:g]) for g in range(len(n))]
    return n, starts


def _group_row_map(start, nsteps):
    return lambda i: (jnp.clip(i - start, 0, nsteps - 1), 0)


def _norm_proj_kernel(*refs, layout, group_steps, w_transposed):
    step = pl.program_id(0)
    aux_piece = layout["aux"]
    ngroups = len(group_steps)
    nout = 2 if aux_piece is None else 3
    x_refs = refs[0:ngroups]
    nw_ref, w_ref = refs[ngroups], refs[ngroups + 1]
    out_refs = refs[ngroups + 2:ngroups + 2 + nout * ngroups]
    scr = refs[ngroups + 2 + nout * ngroups:]
    wg_scr, wr_scr = scr[0], scr[1]
    wa_scr = scr[2] if aux_piece is not None else None
    d = w_ref.shape[1] if w_transposed else w_ref.shape[0]

    def cast_piece(dst, dst_off, src_off, width):
        wpad = _round_up(width, LANE)
        if w_transposed:
            for r in range(0, width, PROJ_CAST_ROWS):
                n = min(PROJ_CAST_ROWS, width - r)
                dst[dst_off + r:dst_off + r + n, :] = w_ref[src_off + r:src_off + r + n, :].astype(BF16)
            if wpad > width:
                dst[dst_off + width:dst_off + wpad, :] = jnp.zeros((wpad - width, d), BF16)
            return wpad
        for r in range(0, d, PROJ_CAST_ROWS):
            piece = w_ref[r:r + PROJ_CAST_ROWS, src_off:src_off + width]
            if wpad > width:
                piece = jnp.concatenate([piece, jnp.zeros((PROJ_CAST_ROWS, wpad - width), F32)], axis=1)
            dst[r:r + PROJ_CAST_ROWS, dst_off:dst_off + wpad] = piece.astype(BF16)
        return wpad

    def matmul(h, w_scr):
        if w_transposed:
            return lax.dot_general(h, w_scr[...], (((1,), (1,)), ((), ())), preferred_element_type=F32)
        return jnp.dot(h, w_scr[...], preferred_element_type=F32)

    @pl.when(step == 0)
    def _():
        cast_piece(wg_scr, 0, *layout["gate"])
        off = 0
        for src_off, width in layout["rest"]:
            off += cast_piece(wr_scr, off, src_off, width)
        if aux_piece is not None:
            cast_piece(wa_scr, 0, *aux_piece)

    def project(x_ref, outs):
        x = x_ref[...]
        ms = jnp.mean(x * x, axis=-1, keepdims=True)
        h = (x * lax.rsqrt(ms + NORM_EPS) * nw_ref[...]).astype(BF16)
        outs[0][...] = matmul(h, wg_scr).astype(outs[0].dtype)
        outs[1][...] = matmul(h, wr_scr).astype(outs[1].dtype)
        if aux_piece is not None:
            outs[2][...] = matmul(h, wa_scr)

    for g, (start, nsteps) in enumerate(group_steps):
        pl.when((step >= start) & (step < start + nsteps))(
            functools.partial(project, x_refs[g], out_refs[g * nout:(g + 1) * nout]))


def _proj_layout(kind):
    kd, di = GLA_KEY_DIM, D_INNER
    if kind == "gla":
        return dict(gate=(2 * kd + di, di),
                    rest=[(2 * kd, di), (0, kd), (kd, kd), (2 * kd + 2 * di, GLA_GATE_RANK)], aux=None)
    if kind == "ssd":
        return dict(gate=(0, di), rest=[(di, SSD_CONV_DIM)], aux=(di + SSD_CONV_DIM, SSD_HEADS))
    return dict(gate=(di + 2 * SWA_KV_DIM, di), rest=[(0, di + 2 * SWA_KV_DIM)], aux=None)


def _norm_proj(kind, xs, nw, w_in):
    d = xs[0].shape[1]
    layout = _proj_layout(kind)
    ng = layout["gate"][1]
    nr = sum(_round_up(w, LANE) for _, w in layout["rest"])
    has_aux = layout["aux"] is not None
    nsteps, starts = _group_steps([x.shape[0] for x in xs], PROJ_ROWS_PER_STEP)
    fix = lambda i: (0, 0)
    in_specs, out_specs, out_shape = [], [], []
    for x, n, s in zip(xs, nsteps, starts):
        t = x.shape[0]
        tm = t // n
        row = _group_row_map(s, n)
        in_specs.append(pl.BlockSpec((tm, d), row))
        out_specs += [pl.BlockSpec((tm, ng), row), pl.BlockSpec((tm, nr), row)]
        out_shape += [jax.ShapeDtypeStruct((t, ng), BF16), jax.ShapeDtypeStruct((t, nr), BF16)]
        if has_aux:
            out_specs.append(pl.BlockSpec((tm, LANE), row))
            out_shape.append(jax.ShapeDtypeStruct((t, LANE), F32))
    w_transposed = w_in.shape[1] % LANE != 0
    w_op = w_in.T if w_transposed else w_in
    swap = (lambda s: s[::-1]) if w_transposed else (lambda s: s)
    in_specs += [pl.BlockSpec((1, d), fix), pl.BlockSpec(w_op.shape, fix, pipeline_mode=pl.Buffered(1))]
    scratch = [pltpu.VMEM(swap((d, ng)), BF16), pltpu.VMEM(swap((d, nr)), BF16)]
    if has_aux:
        scratch.append(pltpu.VMEM(swap((d, LANE)), BF16))
    outs = pl.pallas_call(
        functools.partial(_norm_proj_kernel, layout=layout, group_steps=tuple(zip(starts, nsteps)),
                          w_transposed=w_transposed),
        grid=(sum(nsteps),),
        in_specs=in_specs,
        out_specs=out_specs,
        out_shape=out_shape,
        scratch_shapes=scratch,
        compiler_params=_cparams(("arbitrary",)),
        name="norm_proj_" + kind,
    )(*xs, nw.reshape(1, d), w_op)
    nout = 3 if has_aux else 2
    return [outs[g * nout:(g + 1) * nout] for g in range(len(xs))]


def _seg_rms(y, seg):
    parts = []
    for s in range(y.shape[1] // seg):
        p = y[:, s * seg:(s + 1) * seg]
        ms = jnp.mean(p * p, axis=-1, keepdims=True)
        parts.append(p * lax.rsqrt(ms + NORM_EPS))
    return jnp.concatenate(parts, axis=1)


def _out_proj_kernel(*refs, mode, final, group_steps):
    ngroups = len(group_steps)
    nw_ref, w_ref, fw_ref = refs[3 * ngroups:3 * ngroups + 3]
    out_refs = refs[3 * ngroups + 3:4 * ngroups + 3]
    w_scr = refs[4 * ngroups + 3]
    step = pl.program_id(0)

    @pl.when(step == 0)
    def _():
        for r in range(0, w_ref.shape[0], PROJ_CAST_ROWS):
            w_scr[r:r + PROJ_CAST_ROWS, :] = w_ref[r:r + PROJ_CAST_ROWS, :].astype(BF16)

    def project(o_ref, gate_ref, x_ref, out_ref):
        o = o_ref[...].astype(F32)
        gt = gate_ref[...].astype(F32)
        act = gt * _sigmoid(gt)
        if mode == "gla":
            y = _seg_rms(o, GLA_HEAD_V) * nw_ref[...] * act
        elif mode == "ssd":
            y = _seg_rms(o * act, SSD_GROUP_W) * nw_ref[...]
        else:
            y = o * act
        out = x_ref[...] + jnp.dot(y.astype(BF16), w_scr[...], preferred_element_type=F32)
        if final:
            ms = jnp.mean(out * out, axis=-1, keepdims=True)
            out = out * lax.rsqrt(ms + NORM_EPS) * fw_ref[...]
        out_ref[...] = out

    for g, (start, nsteps) in enumerate(group_steps):
        pl.when((step >= start) & (step < start + nsteps))(
            functools.partial(project, refs[3 * g], refs[3 * g + 1], refs[3 * g + 2], out_refs[g]))


def _out_proj(mode, os_, gates, nw, w_out, xs, fw, final):
    di = os_[0].shape[1]
    d = xs[0].shape[1]
    nsteps, starts = _group_steps([x.shape[0] for x in xs], OUT_PROJ_ROWS_PER_STEP)
    fix = lambda i: (0, 0)
    in_specs, out_specs, out_shape, operands = [], [], [], []
    for o2, g2, x2, n, s in zip(os_, gates, xs, nsteps, starts):
        t = x2.shape[0]
        tm = t // n
        row = _group_row_map(s, n)
        in_specs += [pl.BlockSpec((tm, di), row), pl.BlockSpec((tm, di), row), pl.BlockSpec((tm, d), row)]
        operands += [o2, g2, x2]
        out_specs.append(pl.BlockSpec((tm, d), row))
        out_shape.append(jax.ShapeDtypeStruct((t, d), F32))
    in_specs += [pl.BlockSpec((1, di), fix), pl.BlockSpec((di, d), fix, pipeline_mode=pl.Buffered(1)),
                 pl.BlockSpec((1, d), fix)]
    return pl.pallas_call(
        functools.partial(_out_proj_kernel, mode=mode, final=final, group_steps=tuple(zip(starts, nsteps))),
        grid=(sum(nsteps),),
        in_specs=in_specs,
        out_specs=out_specs,
        out_shape=out_shape,
        scratch_shapes=[pltpu.VMEM((di, d), BF16)],
        compiler_params=_cparams(("arbitrary",)),
        name="out_proj_" + mode,
    )(*operands, nw.reshape(1, di), w_out, fw.reshape(1, d))


GLA_CHUNKS_PER_STEP = 8
GLA_LONG_CHUNK = 128
GLA_SINGLE_ANCHOR_MAX_DECAY = 60.0


def _gla_levels(seglen):
    return tuple(seglen >> (i + 1) for i in range(int(math.log2(seglen))))


def _gla_consts(seglen, c):
    levels = _gla_levels(seglen)
    u = np.arange(c)[:, None]
    j = np.arange(c)[None, :]
    same = (u // seglen) == (j // seglen)
    blocks = [same & (j <= u), same & (j > u)]
    lev = np.full((c, c), len(levels) + 1, np.int32)
    for li, h in enumerate(levels):
        b = (u // (2 * h)) * (2 * h) + h - 1
        blocks.append((j > np.minimum(u, b)) & (j <= np.maximum(u, b)))
        sib = (u // (2 * h) == j // (2 * h)) & (u % (2 * h) >= h) & (j % (2 * h) < h)
        lev[sib] = li
    lev[np.eye(c, dtype=bool)] = len(levels)
    m = np.concatenate(blocks, axis=0).astype(np.float32)
    m3 = np.concatenate([m, m, m], axis=1)
    if 2 * c == LANE:
        lev = np.concatenate([lev, lev], axis=1)
    return jnp.asarray(m3, BF16), jnp.asarray(lev)


def _gla_scores(qx, kx, e_q, e_k, lev, nt):
    att = None
    for li, (eq, ek) in enumerate(zip(e_q, e_k)):
        a = nt(qx if eq is None else qx * eq, kx if ek is None else kx * ek)
        att = jnp.where(lev == li, a, 0.0 if att is None else att)
    return att


def _nt_pair(qe, ke):
    kb = ke.astype(BF16)
    z = jnp.zeros((kb.shape[0], GLA_HEAD_K), BF16)
    rhs = jnp.concatenate([jnp.concatenate([kb[:, 0:GLA_HEAD_K], z], axis=1),
                           jnp.concatenate([z, kb[:, GLA_HEAD_K:]], axis=1)], axis=0)
    return lax.dot_general(qe.astype(BF16), rhs, (((1,), (1,)), ((), ())), preferred_element_type=F32)


def _gla_prep(xc, wgk, bgk):
    v = xc[:, 0:D_INNER]
    q = xc[:, D_INNER:D_INNER + GLA_KEY_DIM].astype(F32) * (GLA_HEAD_K ** -0.5)
    k = xc[:, D_INNER + GLA_KEY_DIM:D_INNER + 2 * GLA_KEY_DIM].astype(F32)
    lr = xc[:, D_INNER + 2 * GLA_KEY_DIM:]
    z = _dot(lr, wgk) + bgk
    g = (jnp.minimum(z, 0.0) - jnp.log(1.0 + jnp.exp(-jnp.abs(z)))) * (1.0 / GLA_GATE_NORMALIZER)
    return v, q, k, g


def _gla_chunk(v, q, k, g, m3, lev, nl, single_anchor):
    c = g.shape[0]
    g3 = jnp.concatenate(_split3(g), axis=0)
    if single_anchor:
        rs = jnp.dot(m3[0:2 * c], g3, preferred_element_type=F32)
        cum = rs[0:c]
        e_in, e_out = jnp.exp(cum), jnp.exp(rs[c:2 * c])
        e_q, e_k = [e_in], [jnp.exp(-cum)]
        lev = jnp.where(lev <= nl, 0, 1)
    else:
        e_all = jnp.exp(jnp.dot(m3, g3, preferred_element_type=F32))
        e_in, e_out = e_all[0:c], e_all[c:2 * c]
        e_q = [e_all[(2 + li) * c:(3 + li) * c] for li in range(nl)] + [None]
        e_k = e_q
    qe_in = q * e_in
    ke_out = k * e_out
    outs, pending = [], []
    if c == LANE:
        for h in range(GLA_HEADS):
            ks = slice(h * GLA_HEAD_K, (h + 1) * GLA_HEAD_K)
            cut = lambda es: [None if e is None else e[:, ks] for e in es]
            att = _gla_scores(q[:, ks], k[:, ks], cut(e_q), cut(e_k), lev, _dot_nt)
            v_h = v[:, h * GLA_HEAD_V:(h + 1) * GLA_HEAD_V]
            xt = jnp.concatenate([ke_out[:, ks], e_in[:, ks]], axis=0).T
            outs.append(_dot(att, v_h))
            pending.append((h, qe_in[:, ks], xt, v_h))
        return outs, pending
    lane_lo = lax.broadcasted_iota(jnp.int32, (c, LANE), 1) < c
    for pr in range(GLA_HEADS // 2):
        ls = slice(2 * pr * GLA_HEAD_K, (2 * pr + 2) * GLA_HEAD_K)
        cut = lambda es: [None if e is None else e[:, ls] for e in es]
        att = _gla_scores(q[:, ls], k[:, ls], cut(e_q), cut(e_k), lev, _nt_pair)
        vpair = v[:, 2 * pr * GLA_HEAD_V:(2 * pr + 2) * GLA_HEAD_V]
        vcat = jnp.concatenate([vpair[:, 0:GLA_HEAD_V], vpair[:, GLA_HEAD_V:]], axis=0)
        for half, att_h in enumerate((jnp.where(lane_lo, att, 0.0), jnp.where(lane_lo, 0.0, att))):
            h = 2 * pr + half
            ks = slice(h * GLA_HEAD_K, (h + 1) * GLA_HEAD_K)
            xt = jnp.concatenate([ke_out[:, ks], e_in[:, ks]], axis=0).T
            outs.append(_dot(att_h, vcat))
            pending.append((h, qe_in[:, ks], xt, v[:, h * GLA_HEAD_V:(h + 1) * GLA_HEAD_V]))
    return outs, pending


def _gla_finish(outs, pending, state_io):
    return jnp.concatenate([o + state_io(*p) for o, p in zip(outs, pending)], axis=1)


def _gla_dispatch(preps, run):
    low = None
    for _, _, _, g in preps:
        tot = jnp.min(jnp.sum(g, axis=0, keepdims=True))
        low = tot if low is None else jnp.minimum(low, tot)
    single_anchor_ok = low >= -GLA_SINGLE_ANCHOR_MAX_DECAY
    pl.when(single_anchor_ok)(functools.partial(run, True))
    pl.when(jnp.logical_not(single_anchor_ok))(functools.partial(run, False))


def _gla_carry_kernel(rest_ref, s0_ref, wgk_ref, bgk_ref, m_ref, lev_ref, o_ref, s_ref, *, c, nch, nl):
    @pl.when(pl.program_id(1) == 0)
    def _():
        s_ref[...] = s0_ref[...]

    def state_io(h, qe_in_h, xt, v_h):
        s_h = s_ref[0, h]
        s_ref[0, h] = s_h * xt[:, 2 * c - 1:2 * c] + _dot(xt[:, 0:c], v_h)
        return _dot(qe_in_h, s_h)

    wgk, bgk = wgk_ref[...], bgk_ref[...]
    preps = [_gla_prep(rest_ref[0, ci * c:(ci + 1) * c, :], wgk, bgk) for ci in range(nch)]

    def run(single_anchor):
        staged = [_gla_chunk(v, q, k, g, m_ref[...], lev_ref[...], nl, single_anchor) for v, q, k, g in preps]
        for ci, (outs, pending) in enumerate(staged):
            o_ref[0, ci * c:(ci + 1) * c, :] = _gla_finish(outs, pending, state_io).astype(o_ref.dtype)

    _gla_dispatch(preps, run)


def _gla_packed_kernel(rest_ref, s0_ref, wgk_ref, bgk_ref, m_ref, lev_ref, o_ref, s_ref, *, seglen, nl):
    c = CHUNK
    nseg = c // seglen
    seg_of_lane = lax.broadcasted_iota(jnp.int32, (GLA_HEAD_K, c), 1) // seglen

    def state_io(h, qe_in_h, xt, v_h):
        ke_t = xt[:, 0:c]
        outs = []
        for sg in range(nseg):
            s_h = s0_ref[sg, h]
            outs.append(_dot(qe_in_h[sg * seglen:(sg + 1) * seglen], s_h))
            end = c + (sg + 1) * seglen - 1
            s_ref[sg, h] = s_h * xt[:, end:end + 1] + _dot(jnp.where(seg_of_lane == sg, ke_t, 0.0), v_h)
        return jnp.concatenate(outs, axis=0)

    preps = [_gla_prep(rest_ref[...], wgk_ref[...], bgk_ref[...])]

    def run(single_anchor):
        v, q, k, g = preps[0]
        outs, pending = _gla_chunk(v, q, k, g, m_ref[...], lev_ref[...], nl, single_anchor)
        o_ref[...] = _gla_finish(outs, pending, state_io).astype(o_ref.dtype)

    _gla_dispatch(preps, run)


def _gla_scan(rest3, s0, w_gk2, b_gk):
    b, l, nr = rest3.shape
    packed = l < CHUNK
    c = CHUNK if packed or l % GLA_LONG_CHUNK else GLA_LONG_CHUNK
    seglen = l if packed else c
    assert c % seglen == 0 and seglen % 8 == 0 and (l % c == 0 or packed)
    nl = len(_gla_levels(seglen))
    m01, lev = _gla_consts(seglen, c)
    wgk = jnp.zeros((LANE, GLA_KEY_DIM), F32).at[:GLA_GATE_RANK].set(w_gk2).astype(BF16)
    consts = (wgk, b_gk.reshape(1, GLA_KEY_DIM), m01, lev)
    if packed:
        nseq = c // seglen
        assert b % nseq == 0
        sblk = (nseq, GLA_HEADS, GLA_HEAD_K, GLA_HEAD_V)
        o, s_fin = pl.pallas_call(
            functools.partial(_gla_packed_kernel, seglen=seglen, nl=nl),
            grid=(b // nseq,),
            in_specs=[pl.BlockSpec((c, nr), lambda i: (i, 0)),
                      pl.BlockSpec(sblk, lambda i: (i, 0, 0, 0))]
                     + [pl.BlockSpec(a.shape, lambda i: (0, 0)) for a in consts],
            out_specs=[pl.BlockSpec((c, D_INNER), lambda i: (i, 0)),
                       pl.BlockSpec(sblk, lambda i: (i, 0, 0, 0))],
            out_shape=[jax.ShapeDtypeStruct((b * l, D_INNER), BF16), jax.ShapeDtypeStruct(s0.shape, F32)],
            compiler_params=_cparams(("parallel",)),
            name="gla_scan_packed",
        )(rest3.reshape(b * l, nr), s0, *consts)
        return o.reshape(b, l, D_INNER), s_fin
    nch = math.gcd(l // c, GLA_CHUNKS_PER_STEP)
    sblk = (1, GLA_HEADS, GLA_HEAD_K, GLA_HEAD_V)
    return pl.pallas_call(
        functools.partial(_gla_carry_kernel, c=c, nch=nch, nl=nl),
        grid=(b, l // (c * nch)),
        in_specs=[pl.BlockSpec((1, c * nch, nr), lambda i, t: (i, t, 0)),
                  pl.BlockSpec(sblk, lambda i, t: (i, 0, 0, 0))]
                 + [pl.BlockSpec(a.shape, lambda i, t: (0, 0)) for a in consts],
        out_specs=[pl.BlockSpec((1, c * nch, D_INNER), lambda i, t: (i, t, 0)),
                   pl.BlockSpec(sblk, lambda i, t: (i, 0, 0, 0))],
        out_shape=[jax.ShapeDtypeStruct((b, l, D_INNER), BF16), jax.ShapeDtypeStruct(s0.shape, F32)],
        compiler_params=_cparams(("parallel", "arbitrary")),
        name="gla_scan",
    )(rest3, s0, *consts)


SSD_CHUNKS_PER_STEP = 8


def _ssd2_consts(seglen):
    c = CHUNK
    u = np.arange(c)
    same = (u[:, None] // seglen) == (u[None, :] // seglen)
    tl = same & (u[None, :] <= u[:, None])
    m = np.concatenate([tl, same], axis=0).astype(np.float32)
    m3 = np.concatenate([m, m, m], axis=1)
    expand = (np.arange(LANE)[:, None] == (np.arange(D_INNER) // SSD_HEAD_DIM)[None, :]).astype(np.float32)
    ex3 = np.concatenate([expand] * 3, axis=0)
    s_of = np.arange(LANE) % c
    tri = np.where(same[:, s_of] & (s_of[None, :] <= u[:, None]), 0.0, NEG_BIG).astype(np.float32)
    bd = ((np.arange(2 * c)[:, None] // c) == (np.arange(LANE)[None, :] // SSD_HEAD_DIM)).astype(np.float32)
    return jnp.asarray(m3, BF16), jnp.asarray(ex3, BF16), jnp.asarray(tri), jnp.asarray(bd)


def _ssd2_conv(ext_ref, rows, cw_ref, cb_ref):
    e = ext_ref[0:8 + rows, :]
    acc = cw_ref[SSD_CONV - 1:SSD_CONV, :] * e[8:]
    for k in range(1, SSD_CONV):
        acc = acc + cw_ref[SSD_CONV - 1 - k:SSD_CONV - k, :] * pltpu.roll(e, k, axis=0)[8:]
    return cb_ref[...] + acc


def _ssd2_chunk(dt_raw, conv, dtb, a_neg, dsk, m3, ex3, tri, bd, state_io):
    c = CHUNK
    xbc = conv * _sigmoid(conv)
    xs = xbc[:, 0:D_INNER]
    bm = xbc[:, D_INNER:D_INNER + SSD_GROUPS * SSD_STATE]
    cm = xbc[:, D_INNER + SSD_GROUPS * SSD_STATE:]
    dt = _softplus(dt_raw + dtb)
    la = dt * a_neg
    rs = jnp.dot(m3, jnp.concatenate(_split3(la), axis=0), preferred_element_type=F32)
    cum, tot = rs[0:c], rs[c:2 * c]
    stack = jnp.concatenate([dt * jnp.exp(tot - cum), cum], axis=0)
    rep = jnp.dot(jnp.concatenate(_split3(stack), axis=1), ex3, preferred_element_type=F32)
    dtw_rep, cum_rep = rep[0:c], rep[c:2 * c]
    ecum_rep = jnp.exp(cum_rep)
    uw = xs * dtw_rep
    at = jnp.concatenate([cum, cum], axis=0).T
    adt = jnp.concatenate([dt, dt], axis=0).T
    lane_lo = lax.broadcasted_iota(jnp.int32, (1, LANE), 1) < c
    ys = []
    for g in range(SSD_GROUPS):
        gs = slice(g * SSD_GROUP_W, (g + 1) * SSD_GROUP_W)
        ns = slice(g * SSD_STATE, (g + 1) * SSD_STATE)
        cg, bg = cm[:, ns], bm[:, ns]
        cb_rep = _dot_nt(cg, jnp.concatenate([bg] * (SSD_GROUP_W // c), axis=0))
        parts = []
        for p in range(SSD_GROUP_W // LANE):
            col = g * (SSD_GROUP_W // LANE) + p
            ps = slice(col * LANE, (col + 1) * LANE)
            cum_s = jnp.where(lane_lo, at[2 * col:2 * col + 1, :], at[2 * col + 1:2 * col + 2, :])
            dt_s = jnp.where(lane_lo, adt[2 * col:2 * col + 1, :], adt[2 * col + 1:2 * col + 2, :])
            w = jnp.exp(cum_rep[:, ps] - cum_s + tri) * (cb_rep[:, p * LANE:(p + 1) * LANE] * dt_s)
            up = xs[:, ps]
            parts.append(_dot(w, jnp.concatenate([up, up], axis=0) * bd))
        y_inter = state_io(g, cg, bg, uw[:, gs], ecum_rep[:, gs])
        ys.append(jnp.concatenate(parts, axis=1) + y_inter + xs[:, gs] * dsk[:, gs])
    return jnp.concatenate(ys, axis=1)


def _ssd2_carry_kernel(rest_ref, dtr_ref, s0_ref, conv0_ref, cw_ref, cb_ref, dtb_ref, alog_ref, dsk_ref,
                       m3_ref, ex3_ref, tri_ref, bd_ref, y_ref, sfin_ref, ext_ref, st_ref, *, nch):
    c = CHUNK
    t = pl.program_id(1)

    @pl.when(t == 0)
    def _():
        ext_ref[0:8, :] = conv0_ref[0]
        for g in range(SSD_GROUPS):
            st_ref[g] = s0_ref[0, g].T

    a_neg = -jnp.exp(alog_ref[...])

    def state_io(g, cg, bg, uw_g, ecum_g):
        s_g = st_ref[g]
        y_inter = _dot(cg, s_g) * ecum_g
        bg_t = jnp.concatenate([bg, jnp.zeros_like(bg)], axis=0).T[:, 0:c]
        st_ref[g] = s_g * ecum_g[c - 1:c, :] + _dot(bg_t, uw_g)
        return y_inter

    for ci in range(nch):
        ext_ref[8:8 + c, :] = rest_ref[0, ci * c:(ci + 1) * c, :].astype(F32)
        conv = _ssd2_conv(ext_ref, c, cw_ref, cb_ref)
        ext_ref[0:8, :] = ext_ref[c:c + 8, :]
        y = _ssd2_chunk(dtr_ref[0, ci * c:(ci + 1) * c, :], conv, dtb_ref[...], a_neg, dsk_ref[...],
                        m3_ref[...], ex3_ref[...], tri_ref[...], bd_ref[...], state_io)
        y_ref[0, ci * c:(ci + 1) * c, :] = y.astype(y_ref.dtype)

    @pl.when(t == pl.num_programs(1) - 1)
    def _():
        for g in range(SSD_GROUPS):
            sfin_ref[0, g] = st_ref[g].T


def _ssd2_packed_kernel(rest_ref, dtr_ref, s0_ref, conv0_ref, cw_ref, cb_ref, dtb_ref, alog_ref, dsk_ref,
                        m3_ref, ex3_ref, tri_ref, bd_ref, y_ref, sfin_ref, ext_ref, *, seglen):
    c = CHUNK
    nseg = c // seglen
    ext_ref[0:8, :] = jnp.zeros((8, SSD_CONV_DIM), F32)
    xbc_raw = rest_ref[...].astype(F32)
    for sg in range(nseg):
        ext_ref[8 + 16 * sg:16 + 16 * sg, :] = conv0_ref[sg]
        ext_ref[16 + 16 * sg:24 + 16 * sg, :] = xbc_raw[sg * seglen:(sg + 1) * seglen]
    conv2 = _ssd2_conv(ext_ref, 2 * c, cw_ref, cb_ref)
    conv = jnp.concatenate([conv2[16 * sg + 8:16 * sg + 16] for sg in range(nseg)], axis=0)
    a_neg = -jnp.exp(alog_ref[...])
    seg_of_row = lax.broadcasted_iota(jnp.int32, (c, SSD_STATE), 0) // seglen

    def state_io(g, cg, bg, uw_g, ecum_g):
        ends = jnp.concatenate([ecum_g[(sg + 1) * seglen - 1:(sg + 1) * seglen, :] for sg in range(nseg)], axis=0)
        fill = jnp.zeros((LANE - c - nseg, SSD_GROUP_W), F32)
        tg = jnp.concatenate([uw_g, ends, fill], axis=0).T
        uw_t = tg[:, 0:c]
        y_parts = []
        for sg in range(nseg):
            s_sg = s0_ref[sg, g]
            y_parts.append(_dot_nt(cg[sg * seglen:(sg + 1) * seglen], s_sg))
            b_sg = jnp.where(seg_of_row == sg, bg, 0.0)
            sfin_ref[sg, g] = s_sg * tg[:, c + sg:c + sg + 1] + _dot(uw_t, b_sg)
        return jnp.concatenate(y_parts, axis=0) * ecum_g

    y = _ssd2_chunk(dtr_ref[...], conv, dtb_ref[...], a_neg, dsk_ref[...], m3_ref[...], ex3_ref[...],
                    tri_ref[...], bd_ref[...], state_io)
    y_ref[...] = y.astype(y_ref.dtype)


def _ssd2_scan(rest3, dt_raw3, s0, conv0, conv_w, conv_b, dt_bias, a_log, d_skip):
    b, l, nr = rest3.shape
    c = CHUNK
    packed = l < c
    seglen = l if packed else c
    assert c % seglen == 0 and seglen % 8 == 0 and (l % c == 0 or packed)
    m3, ex3, tri, bd = _ssd2_consts(seglen)
    s0g = s0.reshape(b, SSD_GROUPS, SSD_GROUP_W, SSD_STATE)
    conv0p = jnp.concatenate([jnp.zeros((b, 8 - (SSD_CONV - 1), SSD_CONV_DIM), F32), conv0], axis=1)
    pad = lambda a: jnp.zeros((1, LANE), F32).at[0, :SSD_HEADS].set(a)
    dsk = jnp.repeat(d_skip, SSD_HEAD_DIM).reshape(1, D_INNER)
    consts = (conv_w, conv_b.reshape(1, -1), pad(dt_bias), pad(a_log), dsk, m3, ex3, tri, bd)
    if packed:
        nseq = c // seglen
        assert b % nseq == 0
        fix = lambda i: (0, 0)
        const_specs = [pl.BlockSpec(a.shape, fix) for a in consts]
        sblk = (nseq, SSD_GROUPS, SSD_GROUP_W, SSD_STATE)
        y, sfin = pl.pallas_call(
            functools.partial(_ssd2_packed_kernel, seglen=seglen),
            grid=(b // nseq,),
            in_specs=[pl.BlockSpec((c, nr), lambda i: (i, 0)),
                      pl.BlockSpec((c, LANE), lambda i: (i, 0)),
                      pl.BlockSpec(sblk, lambda i: (i, 0, 0, 0)),
                      pl.BlockSpec((nseq, 8, SSD_CONV_DIM), lambda i: (i, 0, 0))] + const_specs,
            out_specs=[pl.BlockSpec((c, D_INNER), lambda i: (i, 0)),
                       pl.BlockSpec(sblk, lambda i: (i, 0, 0, 0))],
            out_shape=[jax.ShapeDtypeStruct((b * l, D_INNER), BF16), jax.ShapeDtypeStruct(s0g.shape, F32)],
            scratch_shapes=[pltpu.VMEM((2 * c + 8, SSD_CONV_DIM), F32)],
            compiler_params=_cparams(("parallel",)),
            name="ssd_scan_packed",
        )(rest3.reshape(b * l, nr), dt_raw3.reshape(b * l, LANE), s0g, conv0p, *consts)
        return y.reshape(b, l, D_INNER), sfin.reshape(s0.shape)
    nch = math.gcd(l // c, SSD_CHUNKS_PER_STEP)
    fix2 = lambda i, t: (0, 0)
    const_specs = [pl.BlockSpec(a.shape, fix2) for a in consts]
    sblk = (1, SSD_GROUPS, SSD_GROUP_W, SSD_STATE)
    y, sfin = pl.pallas_call(
        functools.partial(_ssd2_carry_kernel, nch=nch),
        grid=(b, l // (c * nch)),
        in_specs=[pl.BlockSpec((1, c * nch, nr), lambda i, t: (i, t, 0)),
                  pl.BlockSpec((1, c * nch, LANE), lambda i, t: (i, t, 0)),
                  pl.BlockSpec(sblk, lambda i, t: (i, 0, 0, 0)),
                  pl.BlockSpec((1, 8, SSD_CONV_DIM), lambda i, t: (i, 0, 0))] + const_specs,
        out_specs=[pl.BlockSpec((1, c * nch, D_INNER), lambda i, t: (i, t, 0)),
                   pl.BlockSpec(sblk, lambda i, t: (i, 0, 0, 0))],
        out_shape=[jax.ShapeDtypeStruct((b, l, D_INNER), BF16), jax.ShapeDtypeStruct(s0g.shape, F32)],
        scratch_shapes=[pltpu.VMEM((c + 8, SSD_CONV_DIM), F32),
                        pltpu.VMEM((SSD_GROUPS, SSD_STATE, SSD_GROUP_W), F32)],
        compiler_params=_cparams(("parallel", "arbitrary")),
        name="ssd_scan",
    )(rest3, dt_raw3, s0g, conv0p, *consts)
    return y, sfin.reshape(s0.shape)


SWA_SEQS_PER_STEP = 32
SWA_BLOCKS_PER_STEP = 8


def _swa_mask(bq, first_block_has_no_past):
    rows = 4 * bq
    tq = (np.arange(rows) % bq)[:, None]
    s = (np.arange(4 * SWA_WINDOW) % (2 * SWA_WINDOW))[None, :]
    ok = (s > tq) & (s <= tq + SWA_WINDOW)
    if first_block_has_no_past:
        ok = ok & (s >= SWA_WINDOW)
    return np.where(ok, 0.0, NEG_BIG).astype(np.float32)


def _block_diag_pair(col, rolled, odd):
    lane = lax.broadcasted_iota(jnp.int32, col.shape, 1)
    lo = lane < SWA_HEAD_DIM
    if odd:
        top = jnp.where(lo, rolled, 0.0)
        bot = jnp.where(lo, 0.0, col)
    else:
        top = jnp.where(lo, col, 0.0)
        bot = jnp.where(lo, 0.0, rolled)
    return jnp.concatenate([top, bot], axis=0)


def _swa_kernel(sink_ref, x_ref, kprev_ref, vprev_ref, mask_ref, ones_ref, o_ref, *cache_refs, bq, items, nsub):
    w = SWA_WINDOW
    n = pl.program_id(1)
    npair = (SWA_Q_HEADS // SWA_KV_HEADS) // 2
    rows = npair * bq
    lane_lo = lax.broadcasted_iota(jnp.int32, (rows, LANE), 1) < SWA_HEAD_DIM
    sinks = []
    for j in range(SWA_KV_HEADS):
        sinks.append([jnp.concatenate(
            [jnp.full((bq, 1), sink_ref[j * 2 * npair + 2 * p + half], F32) for p in range(npair)], axis=0)
            for half in range(2)])

    kcols = slice(D_INNER, D_INNER + SWA_KV_DIM)
    vcols = slice(D_INNER + SWA_KV_DIM, D_INNER + 2 * SWA_KV_DIM)
    work = []
    for i in range(items * nsub):
        if nsub == 1:
            x = x_ref[i]
            kprev, vprev = kprev_ref[i], vprev_ref[i]
            amask = mask_ref[jnp.where(n == 0, 0, 1)]
            out_rows = (i, slice(None))
        else:
            x = x_ref[0, i * bq:(i + 1) * bq, :]
            if i == 0:
                kprev, vprev = kprev_ref[0], vprev_ref[0]
                amask = mask_ref[jnp.where(n == 0, 0, 1)]
            else:
                kprev = x_ref[0, (i - 1) * bq:i * bq, kcols]
                vprev = x_ref[0, (i - 1) * bq:i * bq, vcols]
                amask = mask_ref[1]
            out_rows = (0, slice(i * bq, (i + 1) * bq))
        fill = [] if bq == w else [jnp.zeros((w - bq, SWA_KV_DIM), F32)]
        kall = jnp.concatenate([kprev, x[:, kcols]] + fill, axis=0)
        vall = jnp.concatenate([vprev, x[:, vcols]] + fill, axis=0)
        if cache_refs:
            cache_refs[0][i] = kall[bq:bq + w]
            cache_refs[1][i] = vall[bq:bq + w]
        work.append((x, kall, vall, amask, out_rows))

    nwork = len(work)
    tile_rows = lambda a: jnp.concatenate([a] * nwork, axis=0)
    swap = lambda a: jnp.concatenate([a[:, SWA_HEAD_DIM:], a[:, 0:SWA_HEAD_DIM]], axis=1)
    lane_lo_all = tile_rows(lane_lo)
    for j in range(SWA_KV_HEADS):
        cs = slice((j // 2) * LANE, (j // 2 + 1) * LANE)
        qbase = j * npair * LANE
        scs, v2es = [], []
        for x, kall, vall, amask, _ in work:
            kcol, vcol = kall[:, cs], vall[:, cs]
            k2 = _block_diag_pair(kcol, swap(kcol), j % 2)
            v2 = _block_diag_pair(vcol, swap(vcol), j % 2)
            v2es.append(jnp.concatenate([v2.astype(BF16), ones_ref[...]], axis=1))
            qs = jnp.concatenate([x[:, qbase + p * LANE:qbase + (p + 1) * LANE] for p in range(npair)],
                                 axis=0) * (SWA_HEAD_DIM ** -0.5)
            scs.append(_dot_nt(qs, k2) + amask)
        sc = jnp.concatenate(scs, axis=0)
        pes, ms = [], []
        for half in range(2):
            sh = sc[:, half * 2 * w:(half + 1) * 2 * w]
            sink = tile_rows(sinks[j][half])
            m = jnp.maximum(jnp.max(sh, axis=-1, keepdims=True), sink)
            pes.append(jnp.exp(sh - m).astype(BF16))
            ms.append(sink - m)
        pe = jnp.concatenate(pes, axis=1)
        o = jnp.concatenate([jnp.dot(pe[i * rows:(i + 1) * rows], v2es[i], preferred_element_type=F32)
                             for i in range(nwork)], axis=0)
        esink = jnp.exp(jnp.where(lane_lo_all, ms[0], ms[1]))
        res = o[:, 0:LANE] / (o[:, LANE:] + esink)
        for i, (_, _, _, _, out_rows) in enumerate(work):
            for p in range(npair):
                o_ref[out_rows[0], out_rows[1], qbase + p * LANE:qbase + (p + 1) * LANE] = (
                    res[i * rows + p * bq:i * rows + (p + 1) * bq].astype(o_ref.dtype))


def _swa_attn(rest3, kprev, vprev, sinks, has_past):
    b, l, nr = rest3.shape
    w = SWA_WINDOW
    bq = math.gcd(l, w)
    nb = l // bq
    assert nb == 1 or bq == w
    m_later = _swa_mask(bq, False)
    m_first = m_later if has_past else _swa_mask(bq, True)
    masks = jnp.asarray(np.stack([m_first, m_later]))
    ones2 = jnp.asarray((np.arange(4 * w)[:, None] // (2 * w)) == (np.arange(LANE)[None, :] // SWA_HEAD_DIM), BF16)
    if nb == 1:
        rest3 = rest3.astype(F32)
        items, nsub = math.gcd(b, SWA_SEQS_PER_STEP), 1
        kspec = pl.BlockSpec((items, w, SWA_KV_DIM), lambda i, n: (i, 0, 0))
        vspec = kspec
        kin, vin = kprev, vprev
    else:
        assert not has_past
        items, nsub = 1, math.gcd(nb, SWA_BLOCKS_PER_STEP)
        kcol = D_INNER // SWA_KV_DIM
        kspec = pl.BlockSpec((1, w, SWA_KV_DIM), lambda i, n: (i, jnp.maximum(n * nsub - 1, 0), kcol))
        vspec = pl.BlockSpec((1, w, SWA_KV_DIM), lambda i, n: (i, jnp.maximum(n * nsub - 1, 0), kcol + 1))
        kin, vin = rest3, rest3
    out_specs = [pl.BlockSpec((items, bq * nsub, D_INNER), lambda i, n: (i, n, 0))]
    out_shape = [jax.ShapeDtypeStruct((b, l, D_INNER), _mixer_out_dtype(bq))]
    if nb == 1:
        out_specs += [pl.BlockSpec((items, w, SWA_KV_DIM), lambda i, n: (i, 0, 0))] * 2
        out_shape += [jax.ShapeDtypeStruct((b, w, SWA_KV_DIM), F32)] * 2
    return pl.pallas_call(
        functools.partial(_swa_kernel, bq=bq, items=items, nsub=nsub),
        grid=(b // items, nb // nsub),
        in_specs=[
            pl.BlockSpec(memory_space=pltpu.SMEM),
            pl.BlockSpec((items, bq * nsub, nr), lambda i, n: (i, n, 0)),
            kspec,
            vspec,
            pl.BlockSpec(masks.shape, lambda i, n: (0, 0, 0)),
            pl.BlockSpec(ones2.shape, lambda i, n: (0, 0)),
        ],
        out_specs=out_specs,
        out_shape=out_shape,
        compiler_params=_cparams(("parallel", "parallel")),
        name="swa_attn",
    )(sinks, rest3, kin, vin, masks, ones2)


def _trunk(xs, sts, p, has_pasts):
    shapes = [x.shape for x in xs]
    x2s = [x.reshape(-1, x.shape[-1]) for x in xs]
    ones_inner = jnp.ones((D_INNER,), F32)
    news = [{} for _ in xs]
    flat = lambda a: a.reshape(-1, a.shape[-1])

    def gla(i, x2s, key):
        proj = _norm_proj("gla", x2s, p[f"l{i}_norm"], p[f"l{i}_w_in"])
        os_ = []
        for g, ((b, l, _), (gate, rest)) in enumerate(zip(shapes, proj)):
            o, news[g][key] = _gla_scan(rest.reshape(b, l, -1), sts[g][key], p[f"l{i}_w_gk2"], p[f"l{i}_b_gk"])
            os_.append(flat(o))
        hn = jnp.tile(p[f"l{i}_head_norm"], GLA_HEADS)
        return _out_proj("gla", os_, [pr[0] for pr in proj], hn, p[f"l{i}_w_out"], x2s, p["final_norm"], i == 3)

    x2s = gla(0, x2s, "gla0")

    proj = _norm_proj("ssd", x2s, p["l1_norm"], p["l1_w_in"])
    os_ = []
    for g, ((b, l, _), (gate, rest, dt_raw)) in enumerate(zip(shapes, proj)):
        st = sts[g]
        rest3 = rest.reshape(b, l, -1)
        y, news[g]["ssm"] = _ssd2_scan(rest3, dt_raw.reshape(b, l, -1), st["ssm"], st["conv"], p["l1_conv_w"],
                                       p["l1_conv_b"], p["l1_dt_bias"], p["l1_a_log"], p["l1_d_skip"])
        ext_tail = jnp.concatenate([st["conv"], rest3[:, max(l - (SSD_CONV - 1), 0):, :].astype(F32)], axis=1)
        news[g]["conv"] = ext_tail[:, -(SSD_CONV - 1):]
        os_.append(flat(y))
    x2s = _out_proj("ssd", os_, [pr[0] for pr in proj], p["l1_gate_norm"], p["l1_w_out"], x2s,
                    p["final_norm"], False)

    proj = _norm_proj("swa", x2s, p["l2_norm"], p["l2_w_in"])
    os_ = []
    for g, ((b, l, _), (gate, rest)) in enumerate(zip(shapes, proj)):
        st = sts[g]
        rest3 = rest.reshape(b, l, -1)
        kprev = st["swa_k"].reshape(b, SWA_WINDOW, SWA_KV_DIM)
        vprev = st["swa_v"].reshape(b, SWA_WINDOW, SWA_KV_DIM)
        res = _swa_attn(rest3, kprev, vprev, p["l2_sinks"], has_pasts[g])
        if len(res) == 3:
            k_win, v_win = res[1], res[2]
        else:
            k_win = rest3[:, l - SWA_WINDOW:, D_INNER:D_INNER + SWA_KV_DIM].astype(F32)
            v_win = rest3[:, l - SWA_WINDOW:, D_INNER + SWA_KV_DIM:].astype(F32)
        news[g]["swa_k"] = k_win.reshape(st["swa_k"].shape)
        news[g]["swa_v"] = v_win.reshape(st["swa_v"].shape)
        os_.append(flat(res[0]))
    x2s = _out_proj("swa", os_, [pr[0] for pr in proj], ones_inner, p["l2_w_out"], x2s, p["final_norm"], False)

    x2s = gla(3, x2s, "gla3")
    return [x2.reshape(s) for x2, s in zip(x2s, shapes)], news


def kernel(x_prompt, x_sample, state_gla_0, state_ssm_1, state_conv_1, cache_swa_k_2, cache_swa_v_2, state_gla_3, l0_norm, l0_w_in, l0_w_gk2, l0_b_gk, l0_head_norm, l0_w_out, l1_norm, l1_w_in, l1_conv_w, l1_conv_b, l1_dt_bias, l1_a_log, l1_d_skip, l1_gate_norm, l1_w_out, l2_norm, l2_w_in, l2_sinks, l2_w_out, l3_norm, l3_w_in, l3_w_gk2, l3_b_gk, l3_head_norm, l3_w_out, final_norm):
    p = dict(l0_norm=l0_norm, l0_w_in=l0_w_in, l0_w_gk2=l0_w_gk2, l0_b_gk=l0_b_gk,
             l0_head_norm=l0_head_norm, l0_w_out=l0_w_out,
             l1_norm=l1_norm, l1_w_in=l1_w_in, l1_conv_w=l1_conv_w, l1_conv_b=l1_conv_b,
             l1_dt_bias=l1_dt_bias, l1_a_log=l1_a_log, l1_d_skip=l1_d_skip,
             l1_gate_norm=l1_gate_norm, l1_w_out=l1_w_out,
             l2_norm=l2_norm, l2_w_in=l2_w_in, l2_sinks=l2_sinks, l2_w_out=l2_w_out,
             l3_norm=l3_norm, l3_w_in=l3_w_in, l3_w_gk2=l3_w_gk2, l3_b_gk=l3_b_gk,
             l3_head_norm=l3_head_norm, l3_w_out=l3_w_out, final_norm=final_norm)

    bp = x_prompt.shape[0]
    z = lambda a: jnp.zeros((bp,) + a.shape[1:], a.dtype)
    st_p = dict(gla0=z(state_gla_0), ssm=z(state_ssm_1), conv=z(state_conv_1),
                swa_k=z(cache_swa_k_2), swa_v=z(cache_swa_v_2), gla3=z(state_gla_3))
    st_s = dict(gla0=state_gla_0, ssm=state_ssm_1, conv=state_conv_1,
                swa_k=cache_swa_k_2, swa_v=cache_swa_v_2, gla3=state_gla_3)
    (y_p,), (n_p,) = _trunk([x_prompt], [st_p], p, [False])
    (y_s,), (n_s,) = _trunk([x_sample], [st_s], p, [True])
    return (y_p, y_s,
            n_p["gla0"], n_s["gla0"],
            n_p["ssm"], n_s["ssm"],
            n_p["conv"], n_s["conv"],
            n_p["swa_k"], n_s["swa_k"],
            n_p["swa_v"], n_s["swa_v"],
            n_p["gla3"], n_s["gla3"])
```
